```python
import jax, jax.numpy as jnp
from jax import lax
import numpy as np

D_MODEL = 1024
BATCH = 32
SEQ = 256
DEPTH = 2
DEC_BATCH = 4
DEC_SEQ = 2048
PAST_LEN = 256

GRID_W = 64
BLOCK = 128
WINDOW = 128
ROPE_BASE = 10000.0
NORM_EPS = 1e-6
NEG_INF = -1e30

N_BRANCH = 4
BRANCH_W = D_MODEL // N_BRANCH
HEAD_DIM = 64

MLA_HEADS = BRANCH_W // HEAD_DIM
MLA_NOPE = HEAD_DIM
MLA_ROPE = HEAD_DIM // 2
MLA_V = HEAD_DIM
MLA_Q_RANK = BRANCH_W
MLA_KV_RANK = BRANCH_W // 2
MLA_SCALE = (MLA_NOPE + MLA_ROPE) ** -0.5

RET_HEADS = BRANCH_W // HEAD_DIM
RET_DK = HEAD_DIM
RET_DV = HEAD_DIM

WIN_HEADS = BRANCH_W // HEAD_DIM
WIN_KV_HEADS = WIN_HEADS // 2
WIN_GROUP = WIN_HEADS // WIN_KV_HEADS

GQA_HEADS = BRANCH_W // HEAD_DIM
GQA_KV_HEADS = GQA_HEADS // 2
GQA_GROUP = GQA_HEADS // GQA_KV_HEADS

ATT_SCALE = HEAD_DIM ** -0.5
D_FF = 4 * D_MODEL
ALPHA = (2.0 * DEPTH) ** 0.25
BETA = (8.0 * DEPTH) ** -0.25

IN_SIZES = (MLA_Q_RANK, MLA_KV_RANK, MLA_ROPE,
            RET_HEADS * RET_DK, RET_HEADS * RET_DK, RET_HEADS * RET_DV, RET_HEADS * RET_DV,
            WIN_HEADS * HEAD_DIM, WIN_KV_HEADS * HEAD_DIM, WIN_KV_HEADS * HEAD_DIM,
            GQA_HEADS * HEAD_DIM, GQA_KV_HEADS * HEAD_DIM, GQA_KV_HEADS * HEAD_DIM,
            N_BRANCH * D_MODEL)
IN_DIM = sum(IN_SIZES)

kernel_name = 'hybrid_diffusion_parallel_mla_retention_window_qknorm'


def _rmsnorm(x, g):
    xf = x.astype(jnp.float32)
    y = xf * lax.rsqrt(jnp.mean(xf * xf, -1, keepdims=True) + NORM_EPS)
    return (y * g.astype(jnp.float32)).astype(x.dtype)


def _layernorm(x, g, b):
    xf = x.astype(jnp.float32)
    mu = jnp.mean(xf, -1, keepdims=True)
    var = jnp.mean(jnp.square(xf - mu), -1, keepdims=True)
    y = (xf - mu) * lax.rsqrt(var + NORM_EPS) * g.astype(jnp.float32) + b.astype(jnp.float32)
    return y.astype(x.dtype)


def _axial_rope(t, rot_dim):
    rows = t // GRID_W
    row = jnp.repeat(jnp.arange(rows, dtype=jnp.float32), GRID_W)
    col = (jnp.arange(t) % GRID_W).astype(jnp.float32)
    n_freq = rot_dim // 4
    inv = ROPE_BASE ** (-jnp.arange(n_freq, dtype=jnp.float32) / n_freq)
    ang = jnp.concatenate([row[:, None] * inv, col[:, None] * inv], axis=-1)
    return jnp.cos(ang), jnp.sin(ang)


def _apply_rope(x, cos, sin):
    half = x.shape[-1] // 2
    x1, x2 = x[..., :half], x[..., half:]
    c = cos[None, :, None, :].astype(x.dtype)
    s = sin[None, :, None, :].astype(x.dtype)
    return jnp.concatenate([x1 * c - x2 * s, x1 * s + x2 * c], axis=-1)


def _softmax(s, sink):
    if sink is None:
        return jax.nn.softmax(s, axis=-1)
    sink = sink.astype(jnp.float32)
    m = jnp.maximum(jnp.max(s, -1, keepdims=True), sink)
    e = jnp.exp(s - m)
    return e / (jnp.sum(e, -1, keepdims=True) + jnp.exp(sink - m))


def _dense_attention(q, k, v, sink=None):
    b, t, kh, g, dq = q.shape
    dv = v.shape[-1]
    nb = t // BLOCK
    qb = jnp.moveaxis(q.reshape(b, nb, BLOCK, kh, g, dq), 1, 0)
    sink_b = None if sink is None else sink[None, :, :, None, None]

    def one_block(qblk):
        s = jnp.einsum('bqkgd,bskd->bkgqs', qblk, k).astype(jnp.float32)
        p = _softmax(s, sink_b).astype(v.dtype)
        return jnp.einsum('bkgqs,bskd->bqkgd', p, v)

    o = lax.map(one_block, qb)
    return jnp.moveaxis(o, 0, 1).reshape(b, t, kh, g, dv)


def _banded_attention(q, k, v, k_ctx, v_ctx, sink):
    b, t, kh, g, d = q.shape
    nb = t // BLOCK
    pad = ((0, 0), (BLOCK, BLOCK), (0, 0), (0, 0))

    def band(a):
        ap = jnp.pad(a, pad).reshape(b, nb + 2, BLOCK, kh, a.shape[-1])
        return jnp.concatenate([ap[:, :-2], ap[:, 1:-1], ap[:, 2:]], axis=2)

    kb, vb = band(k), band(v)
    qb = q.reshape(b, nb, BLOCK, kh, g, d)
    qpos = jnp.arange(nb)[:, None] * BLOCK + jnp.arange(BLOCK)[None, :]
    kpos = jnp.arange(nb)[:, None] * BLOCK - BLOCK + jnp.arange(3 * BLOCK)[None, :]
    valid = ((jnp.abs(qpos[:, :, None] - kpos[:, None, :]) <= WINDOW)
             & (kpos[:, None, :] >= 0) & (kpos[:, None, :] < t))
    s_loc = jnp.einsum('bnqkgd,bnskd->bnkgqs', qb, kb).astype(jnp.float32)
    s_loc = jnp.where(valid[None, :, None, None], s_loc, NEG_INF)
    s_ctx = jnp.einsum('bnqkgd,bskd->bnkgqs', qb, k_ctx).astype(jnp.float32)
    p = _softmax(jnp.concatenate([s_loc, s_ctx], -1), sink[None, None, :, :, None, None]).astype(v.dtype)
    nl = 3 * BLOCK
    o = (jnp.einsum('bnkgqs,bnskd->bnqkgd', p[..., :nl], vb)
         + jnp.einsum('bnkgqs,bskd->bnqkgd', p[..., nl:], v_ctx))
    return o.reshape(b, t, kh, g, v.shape[-1])


def _retention_dir(q, k, v, log_gamma, s0, strict):
    b, t, h, dk = q.shape
    dv = v.shape[-1]
    nc = t // BLOCK
    lg = log_gamma.astype(jnp.float32)
    idx = jnp.arange(BLOCK, dtype=jnp.float32)
    diff = idx[:, None] - idx[None, :]
    mask = (diff > 0) if strict else (diff >= 0)
    dmat = jnp.where(mask[None], jnp.exp(jnp.maximum(diff, 0.0)[None] * lg[:, None, None]), 0.0)
    q_dec = jnp.exp((idx[:, None] + 1.0) * lg[None, :])
    k_dec = jnp.exp((BLOCK - 1.0 - idx)[:, None] * lg[None, :])
    c_dec = jnp.exp(BLOCK * lg)

    def chunks(a):
        return jnp.moveaxis(a.astype(jnp.float32).reshape(b, nc, BLOCK, h, a.shape[-1]), 1, 0)

    def step(state, inp):
        qc, kc, vc = inp
        att = jnp.einsum('bihd,bjhd->bhij', qc, kc) * dmat
        intra = jnp.einsum('bhij,bjhe->bihe', att, vc)
        inter = jnp.einsum('bihd,bhde->bihe', qc, state) * q_dec[None, :, :, None]
        state = (state * c_dec[None, :, None, None]
                 + jnp.einsum('bjhd,bjhe->bhde', kc * k_dec[None, :, :, None], vc))
        return state, intra + inter

    s_fin, o = lax.scan(step, s0.astype(jnp.float32), (chunks(q), chunks(k), chunks(v)))
    return jnp.moveaxis(o, 0, 1).reshape(b, t, h, dv), s_fin


def _bi_retention(q, k, v, lg_f, lg_b, s0_f, s0_b):
    o_f, s_f = _retention_dir(q, k, v, lg_f, s0_f, False)
    o_b, s_b = _retention_dir(q[:, ::-1], k[:, ::-1], v[:, ::-1], lg_b, s0_b, True)
    return o_f + o_b[:, ::-1], s_f, s_b


def _retention_out(o, gate, gain):
    mu = jnp.mean(o, -1, keepdims=True)
    var = jnp.mean(jnp.square(o - mu), -1, keepdims=True)
    y = ((o - mu) * lax.rsqrt(var + NORM_EPS)).reshape(o.shape[0], o.shape[1], -1) * gain.astype(jnp.float32)
    return jax.nn.silu(gate) * y.astype(gate.dtype)


def _project(h, lp):
    b, t, _ = h.shape
    split_at = np.cumsum(IN_SIZES)[:-1].tolist()
    (q_lat, kv_lat, k_pe, rq, rk, rv, rg, wq, wk, wv, gq, gk, gv, gates) = jnp.split(h @ lp['w_in'], split_at, axis=-1)
    qa = (_rmsnorm(q_lat, lp['mla_q_norm']) @ lp['mla_w_uq']).reshape(b, t, MLA_HEADS, MLA_NOPE + MLA_ROPE)
    return dict(
        a_q_nope=qa[..., :MLA_NOPE], a_q_pe=qa[..., MLA_NOPE:],
        a_ckv=_rmsnorm(kv_lat, lp['mla_kv_norm']), a_k_pe=k_pe,
        b_q=rq.reshape(b, t, RET_HEADS, RET_DK),
        b_k=rk.reshape(b, t, RET_HEADS, RET_DK) * (RET_DK ** -0.5),
        b_v=rv.reshape(b, t, RET_HEADS, RET_DV), b_g=rg,
        c_q=wq.reshape(b, t, WIN_HEADS, HEAD_DIM),
        c_k=wk.reshape(b, t, WIN_KV_HEADS, HEAD_DIM),
        c_v=wv.reshape(b, t, WIN_KV_HEADS, HEAD_DIM),
        d_q=_rmsnorm(gq.reshape(b, t, GQA_HEADS, HEAD_DIM), lp['gqa_q_norm']),
        d_k=_rmsnorm(gk.reshape(b, t, GQA_KV_HEADS, HEAD_DIM), lp['gqa_k_norm']),
        d_v=gv.reshape(b, t, GQA_KV_HEADS, HEAD_DIM),
        gates=gates)


def _mla_keys(ckv, k_pe, lp):
    b, s, _ = ckv.shape
    k_nope = (ckv @ lp['mla_w_uk']).reshape(b, s, MLA_HEADS, MLA_NOPE)
    v = (ckv @ lp['mla_w_uv']).reshape(b, s, MLA_HEADS, MLA_V)
    k = jnp.concatenate([k_nope, jnp.broadcast_to(k_pe[:, :, None, :], (b, s, MLA_HEADS, MLA_ROPE))], -1)
    return k, v


def _merge(outs, gates, lp):
    terms = [jax.nn.sigmoid(gates[..., i * D_MODEL:(i + 1) * D_MODEL]) * (o @ lp['w_branch'][i])
             for i, o in enumerate(outs)]
    return (terms[0] + terms[1] + terms[2] + terms[3]) @ lp['w_o']


def _context_mixer(h, lp):
    p = _project(h, lp)
    b, t, _ = h.shape
    ka, va = _mla_keys(p['a_ckv'], p['a_k_pe'], lp)
    qa = jnp.concatenate([p['a_q_nope'], p['a_q_pe']], -1) * MLA_SCALE
    o_a = _dense_attention(qa[:, :, :, None, :], ka, va)
    zeros = jnp.zeros((b, RET_HEADS, RET_DK, RET_DV), jnp.float32)
    o_b, s_f, s_b = _bi_retention(p['b_q'], p['b_k'], p['b_v'], jax.nn.log_sigmoid(lp['ret_decay_fwd']),
                                  jax.nn.log_sigmoid(lp['ret_decay_bwd']), zeros, zeros)
    o_b = _retention_out(o_b, p['b_g'], lp['ret_gn_gain'])
    sink = lp['win_sink'].reshape(WIN_KV_HEADS, WIN_GROUP)
    qc = p['c_q'].reshape(b, t, WIN_KV_HEADS, WIN_GROUP, HEAD_DIM) * ATT_SCALE
    o_c = _dense_attention(qc, p['c_k'], p['c_v'], sink)
    qd = p['d_q'].reshape(b, t, GQA_KV_HEADS, GQA_GROUP, HEAD_DIM) * ATT_SCALE
    o_d = _dense_attention(qd, p['d_k'], p['d_v'])
    y = _merge([o_a.reshape(b, t, -1), o_b, o_c.reshape(b, t, -1), o_d.reshape(b, t, -1)], p['gates'], lp)
    ctx_state = (p['a_ckv'], p['a_k_pe'], p['c_k'], p['c_v'], p['d_k'], p['d_v'],
                 s_f.astype(h.dtype), s_b.astype(h.dtype))
    return y, ctx_state


def _latent_mixer(h, lp, ckv_c, kpe_c, kc_c, vc_c, kd_c, vd_c, sf_c, sb_c):
    p = _project(h, lp)
    b, t, _ = h.shape
    cos_a, sin_a = _axial_rope(t, MLA_ROPE)
    cos_h, sin_h = _axial_rope(t, HEAD_DIM)
    q_pe = _apply_rope(p['a_q_pe'], cos_a, sin_a)
    k_pe = _apply_rope(p['a_k_pe'][:, :, None, :], cos_a, sin_a)[:, :, 0]
    ka, va = _mla_keys(jnp.concatenate([p['a_ckv'], ckv_c], 1), jnp.concatenate([k_pe, kpe_c], 1), lp)
    qa = jnp.concatenate([p['a_q_nope'], q_pe], -1) * MLA_SCALE
    o_a = _dense_attention(qa[:, :, :, None, :], ka, va)
    o_b, _, _ = _bi_retention(p['b_q'], p['b_k'], p['b_v'], jax.nn.log_sigmoid(lp['ret_decay_fwd']),
                              jax.nn.log_sigmoid(lp['ret_decay_bwd']), sf_c, sb_c)
    o_b = _retention_out(o_b, p['b_g'], lp['ret_gn_gain'])
    sink = lp['win_sink'].reshape(WIN_KV_HEADS, WIN_GROUP)
    qc = _apply_rope(p['c_q'], cos_h, sin_h).reshape(b, t, WIN_KV_HEADS, WIN_GROUP, HEAD_DIM) * ATT_SCALE
    kc = _apply_rope(p['c_k'], cos_h, sin_h)
    o_c = _banded_attention(qc, kc, p['c_v'], kc_c, vc_c, sink)
    qd = _apply_rope(p['d_q'], cos_h, sin_h).reshape(b, t, GQA_KV_HEADS, GQA_GROUP, HEAD_DIM) * ATT_SCALE
    kd = jnp.concatenate([_apply_rope(p['d_k'], cos_h, sin_h), kd_c], 1)
    vd = jnp.concatenate([p['d_v'], vd_c], 1)
    o_d = _dense_attention(qd, kd, vd)
    return _merge([o_a.reshape(b, t, -1), o_b, o_c.reshape(b, t, -1), o_d.reshape(b, t, -1)], p['gates'], lp)


def _layer(x, cond, lp, mixer_fn):
    mod = jax.nn.silu(cond) @ lp['w_ada'] + lp['b_ada']
    sh1, sc1, g1, sh2, sc2, g2 = jnp.split(mod[:, None, :], 6, axis=-1)
    y, extra = mixer_fn(x * (1 + sc1) + sh1)
    x = _layernorm(ALPHA * x + g1 * y, lp['ln1_g'], lp['ln1_b'])
    h = x * (1 + sc2) + sh2
    f = jnp.square(jax.nn.relu(h @ lp['w_up'])) @ lp['w_down']
    x = _layernorm(ALPHA * x + g2 * f, lp['ln2_g'], lp['ln2_b'])
    return x, extra


def setup_inputs(seed: int = 0) -> dict:
    key = jax.random.key(seed)
    ks = iter(jax.random.split(key, 40))

    def nrm(shape, scale):
        return scale * jax.random.normal(next(ks), shape, jnp.float32)

    base_logit = jnp.log(2.0 ** (5.0 + jnp.arange(RET_HEADS, dtype=jnp.float32)) - 1.0)
    return {
        'x_prompt': nrm((BATCH, SEQ, D_MODEL), 1.0),
        'x_sample': nrm((DEC_BATCH, DEC_SEQ, D_MODEL), 1.0),
        'cache_mla_ckv': nrm((DEC_BATCH, DEPTH, PAST_LEN, MLA_KV_RANK), 1.0),
        'cache_mla_kpe': nrm((DEC_BATCH, DEPTH, PAST_LEN, MLA_ROPE), 1.0),
        'cache_win_k': nrm((DEC_BATCH, DEPTH, PAST_LEN, WIN_KV_HEADS, HEAD_DIM), 1.0),
        'cache_win_v': nrm((DEC_BATCH, DEPTH, PAST_LEN, WIN_KV_HEADS, HEAD_DIM), 1.0),
        'cache_gqa_k': nrm((DEC_BATCH, DEPTH, PAST_LEN, GQA_KV_HEADS, HEAD_DIM), 1.0),
        'cache_gqa_v': nrm((DEC_BATCH, DEPTH, PAST_LEN, GQA_KV_HEADS, HEAD_DIM), 1.0),
        'state_ret_fwd': nrm((DEC_BATCH, DEPTH, RET_HEADS, RET_DK, RET_DV), 0.5),
        'state_ret_bwd': nrm((DEC_BATCH, DEPTH, RET_HEADS, RET_DK, RET_DV), 0.5),
        'c': nrm((DEC_BATCH, D_MODEL), 1.0),
        'c_ctx': nrm((D_MODEL,), 1.0),
        'w_ada': nrm((DEPTH, D_MODEL, 6 * D_MODEL), 0.5 * D_MODEL ** -0.5),
        'b_ada': nrm((DEPTH, 6 * D_MODEL), 0.1),
        'w_in': nrm((DEPTH, D_MODEL, IN_DIM), D_MODEL ** -0.5),
        'mla_q_norm': 1.0 + nrm((DEPTH, MLA_Q_RANK), 0.02),
        'mla_w_uq': nrm((DEPTH, MLA_Q_RANK, MLA_HEADS * (MLA_NOPE + MLA_ROPE)), MLA_Q_RANK ** -0.5),
        'mla_kv_norm': 1.0 + nrm((DEPTH, MLA_KV_RANK), 0.02),
        'mla_w_uk': nrm((DEPTH, MLA_KV_RANK, MLA_HEADS * MLA_NOPE), MLA_KV_RANK ** -0.5),
        'mla_w_uv': nrm((DEPTH, MLA_KV_RANK, MLA_HEADS * MLA_V), MLA_KV_RANK ** -0.5),
        'ret_decay_fwd': base_logit + nrm((DEPTH, RET_HEADS), 0.1),
        'ret_decay_bwd': base_logit + nrm((DEPTH, RET_HEADS), 0.1),
        'ret_gn_gain': 1.0 + nrm((DEPTH, RET_HEADS * RET_DV), 0.02),
        'win_sink': nrm((DEPTH, WIN_HEADS), 0.5),
        'gqa_q_norm': 1.0 + nrm((DEPTH, HEAD_DIM), 0.02),
        'gqa_k_norm': 1.0 + nrm((DEPTH, HEAD_DIM), 0.02),
        'w_branch': nrm((DEPTH, N_BRANCH, BRANCH_W, D_MODEL), BETA * BRANCH_W ** -0.5),
        'w_o': nrm((DEPTH, D_MODEL, D_MODEL), BETA * D_MODEL ** -0.5),
        'ln1_g': 1.0 + nrm((DEPTH, D_MODEL), 0.02),
        'ln1_b': nrm((DEPTH, D_MODEL), 0.02),
        'w_up': nrm((DEPTH, D_MODEL, D_FF), D_MODEL ** -0.5),
        'w_down': nrm((DEPTH, D_FF, D_MODEL), BETA * D_FF ** -0.5),
        'ln2_g': 1.0 + nrm((DEPTH, D_MODEL), 0.02),
        'ln2_b': nrm((DEPTH, D_MODEL), 0.02),
    }


def reference(x_prompt, x_sample, cache_mla_ckv, cache_mla_kpe, cache_win_k, cache_win_v, cache_gqa_k,
              cache_gqa_v, state_ret_fwd, state_ret_bwd, c, c_ctx, w_ada, b_ada, w_in, mla_q_norm, mla_w_uq,
              mla_kv_norm, mla_w_uk, mla_w_uv, ret_decay_fwd, ret_decay_bwd, ret_gn_gain, win_sink, gqa_q_norm,
              gqa_k_norm, w_branch, w_o, ln1_g, ln1_b, w_up, w_down, ln2_g, ln2_b):
    layers = [dict(w_ada=w_ada[l], b_ada=b_ada[l], w_in=w_in[l], mla_q_norm=mla_q_norm[l], mla_w_uq=mla_w_uq[l],
                   mla_kv_norm=mla_kv_norm[l], mla_w_uk=mla_w_uk[l], mla_w_uv=mla_w_uv[l],
                   ret_decay_fwd=ret_decay_fwd[l], ret_decay_bwd=ret_decay_bwd[l], ret_gn_gain=ret_gn_gain[l],
                   win_sink=win_sink[l], gqa_q_norm=gqa_q_norm[l], gqa_k_norm=gqa_k_norm[l],
                   w_branch=w_branch[l], w_o=w_o[l], ln1_g=ln1_g[l], ln1_b=ln1_b[l], w_up=w_up[l],
                   w_down=w_down[l], ln2_g=ln2_g[l], ln2_b=ln2_b[l]) for l in range(DEPTH)]

    xp = x_prompt
    ctx_states = []
    for l in range(DEPTH):
        lp = layers[l]
        xp, st = _layer(xp, c_ctx[None, :], lp, lambda hh, lp=lp: _context_mixer(hh, lp))
        ctx_states.append(st)
    y_prompt = xp
    new_mla_ckv = jnp.stack([s[0] for s in ctx_states], axis=1)
    new_mla_kpe = jnp.stack([s[1] for s in ctx_states], axis=1)
    new_win_k = jnp.stack([s[2] for s in ctx_states], axis=1)
    new_win_v = jnp.stack([s[3] for s in ctx_states], axis=1)
    new_gqa_k = jnp.stack([s[4] for s in ctx_states], axis=1)
    new_gqa_v = jnp.stack([s[5] for s in ctx_states], axis=1)
    new_ret_fwd = jnp.stack([s[6] for s in ctx_states], axis=1)
    new_ret_bwd = jnp.stack([s[7] for s in ctx_states], axis=1)

    xs = x_sample
    for l in range(DEPTH):
        lp = layers[l]

        def mix(hh, lp=lp, l=l):
            return _latent_mixer(hh, lp, cache_mla_ckv[:, l], cache_mla_kpe[:, l], cache_win_k[:, l],
                                 cache_win_v[:, l], cache_gqa_k[:, l], cache_gqa_v[:, l],
                                 state_ret_fwd[:, l], state_ret_bwd[:, l]), None

        xs, _ = _layer(xs, c, lp, mix)
    y_sample = xs

    return (y_prompt, y_sample, new_mla_ckv, new_mla_kpe, new_win_k, new_win_v, new_gqa_k, new_gqa_v,
            new_ret_fwd, new_ret_bwd)
```

```python
import functools

import jax
import jax.numpy as jnp
import numpy as np
from jax import lax
from jax.experimental import pallas as pl
from jax.experimental.pallas import tpu as pltpu

D_MODEL = 1024
DEPTH = 2
GRID_W = 64
CHUNK = 128
WINDOW = 128
ROPE_BASE = 10000.0
NORM_EPS = 1e-6
NEG_INF = -1e30
HEAD_DIM = 64
N_HEADS = 4
BRANCH_W = 256
MLA_ROPE = 32
MLA_KV_RANK = 128
MLA_SCALE = (HEAD_DIM + MLA_ROPE) ** -0.5
ATT_SCALE = HEAD_DIM ** -0.5
D_FF = 4 * D_MODEL
ALPHA = (2.0 * DEPTH) ** 0.25
LANES = 128

P_QLAT, P_KVLAT, P_KPE, P_RET, P_WQ, P_WK, P_WV, P_GQ, P_GK, P_GV, P_END = (
    0, 256, 384, 512, 1536, 1792, 1920, 2048, 2304, 2432, 2560)
FQ_W = 1024
FKV_W = 768
VMEM_LIMIT = 56 * 1024 * 1024

F32 = jnp.float32
BF16 = jnp.bfloat16


def _dot(a, b):
    return jnp.dot(a, b, preferred_element_type=F32)


def _dot_nt(a, b):
    return lax.dot_general(a, b, (((1,), (1,)), ((), ())), preferred_element_type=F32)


def _dot_tn(a, b):
    return lax.dot_general(a, b, (((0,), (0,)), ((), ())), preferred_element_type=F32)


def _layernorm(x, g, b):
    mu = jnp.mean(x, -1, keepdims=True)
    d = x - mu
    var = jnp.mean(d * d, -1, keepdims=True)
    return d * lax.rsqrt(var + NORM_EPS) * g + b


def _rmsnorm(x, g):
    return x * lax.rsqrt(jnp.mean(x * x, -1, keepdims=True) + NORM_EPS) * g


def _seg_sum(x, ones_bd):
    hi = x.astype(BF16)
    lo = (x - hi.astype(F32)).astype(BF16)
    return _dot(hi, ones_bd) + _dot(lo, ones_bd)


def _rope_block(x, cos, sin, half, first):
    rot = jnp.where(first, pltpu.roll(x, LANES - half, 1), pltpu.roll(x, half, 1))
    return x * cos + rot * sin


def _ada_kernel(cond_ref, w_ref, b_ref, o_ref):
    cnd = cond_ref[...]
    s = (cnd * jax.nn.sigmoid(cnd)).astype(BF16)
    o_ref[0] = _dot(s, w_ref[0].astype(BF16)) + b_ref[0]


def _ada(cond8, w_ada, b_ada):
    tn = 1024
    n = w_ada.shape[-1]
    return pl.pallas_call(
        _ada_kernel,
        grid=(DEPTH, n // tn),
        in_specs=[pl.BlockSpec((8, D_MODEL), lambda l, j: (0, 0)),
                  pl.BlockSpec((1, D_MODEL, tn), lambda l, j: (l, 0, j)),
                  pl.BlockSpec((1, 1, tn), lambda l, j: (l, 0, j))],
        out_specs=pl.BlockSpec((1, 8, tn), lambda l, j: (l, 0, j)),
        out_shape=jax.ShapeDtypeStruct((DEPTH, 8, n), F32),
        compiler_params=pltpu.CompilerParams(dimension_semantics=("arbitrary", "arbitrary")),
        name="ada_mod",
    )(cond8, w_ada, b_ada.reshape(DEPTH, 1, n))


def _proj_kernel(*refs, latent):
    if latent:
        (x_ref, mod_ref, wmix_ref, qn_ref, wuq_ref, kvn_ref, gqn_ref, gkn_ref, bd_ref,
         cq_ref, sq_ref, ck_ref, sk_ref, ch_ref, sh_ref,
         fq_ref, fkv_ref, fr_ref) = refs
    else:
        (x_ref, mod_ref, wmix_ref, qn_ref, wuq_ref, kvn_ref, gqn_ref, gkn_ref, bd_ref,
         fq_ref, fkv_ref, fr_ref, ockv_ref, okpe_ref, owk_ref, owv_ref, ogk_ref, ogv_ref) = refs

    x = x_ref[0]
    sh1 = mod_ref[0, 0:1, :]
    sc1 = mod_ref[0, 1:2, :]
    h = (x * (1.0 + sc1) + sh1).astype(BF16)
    p = _dot(h, wmix_ref[...])
    tm = p.shape[0]

    lane = lax.broadcasted_iota(jnp.int32, (tm, LANES), 1)
    first_head = (lane % HEAD_DIM) < (HEAD_DIM // 2)

    def rope_heads(v):
        if not latent:
            return v
        cos, sin = ch_ref[...], sh_ref[...]
        blocks = [_rope_block(v[:, j:j + LANES], cos, sin, HEAD_DIM // 2, first_head)
                  for j in range(0, v.shape[1], LANES)]
        return blocks[0] if len(blocks) == 1 else jnp.concatenate(blocks, axis=1)

    qn = _rmsnorm(p[:, P_QLAT:P_KVLAT], qn_ref[...]).astype(BF16)
    qa = _dot(qn, wuq_ref[...])
    if latent:
        cos, sin = cq_ref[...], sq_ref[...]
        first = (lane >= HEAD_DIM) & (lane < HEAD_DIM + MLA_ROPE // 2)
        qa = jnp.concatenate(
            [_rope_block(qa[:, j:j + LANES], cos, sin, MLA_ROPE // 2, first) for j in range(0, 512, LANES)],
            axis=1)
    fq_ref[0, :, 0:512] = (qa * MLA_SCALE).astype(BF16)

    ckv = _rmsnorm(p[:, P_KVLAT:P_KPE], kvn_ref[...])
    kpe = p[:, P_KPE:P_RET]
    if latent:
        first = lane < MLA_ROPE // 2
        kpe_r = _rope_block(kpe, ck_ref[...], sk_ref[...], MLA_ROPE // 2, first)
    else:
        kpe_r = kpe
        ockv_ref[0] = ckv
        okpe_ref[0] = kpe[:, 0:MLA_ROPE]
    fkv_ref[0, :, 0:128] = ckv.astype(BF16)
    fkv_ref[0, :, 128:256] = kpe_r.astype(BF16)

    fr_ref[0, :, 0:256] = p[:, P_RET:P_RET + 256]
    fr_ref[0, :, 256:512] = p[:, P_RET + 256:P_RET + 512] * (HEAD_DIM ** -0.5)
    fr_ref[0, :, 512:1024] = p[:, P_RET + 512:P_WQ]

    fq_ref[0, :, 512:768] = (rope_heads(p[:, P_WQ:P_WK]) * ATT_SCALE).astype(BF16)
    wk = p[:, P_WK:P_WV]
    wv = p[:, P_WV:P_GQ]
    fkv_ref[0, :, 256:384] = rope_heads(wk).astype(BF16)
    fkv_ref[0, :, 384:512] = wv.astype(BF16)

    bd = bd_ref[...]
    gq = p[:, P_GQ:P_GK]
    gqn = gq * lax.rsqrt(_seg_sum(gq * gq, bd) * (1.0 / HEAD_DIM) + NORM_EPS) * gqn_ref[...]
    fq_ref[0, :, 768:1024] = (rope_heads(gqn) * ATT_SCALE).astype(BF16)
    gk = p[:, P_GK:P_GV]
    gkn = gk * lax.rsqrt(_seg_sum(gk * gk, bd[0:128, 0:128]) * (1.0 / HEAD_DIM) + NORM_EPS) * gkn_ref[...]
    gv = p[:, P_GV:P_END]
    fkv_ref[0, :, 512:640] = rope_heads(gkn).astype(BF16)
    fkv_ref[0, :, 640:768] = gv.astype(BF16)
    if not latent:
        owk_ref[0] = wk
        owv_ref[0] = wv
        ogk_ref[0] = gkn
        ogv_ref[0] = gv


def _proj(x, mod, lw, rope, latent, tm, mod_off, mod_stride):
    g, tg, _ = x.shape
    grid = (g, tg // tm)
    const = lambda *s: pl.BlockSpec(s, lambda i, t: (0,) * len(s))
    tok = lambda w: pl.BlockSpec((1, tm, w), lambda i, t: (i, t, 0))
    in_specs = [tok(D_MODEL),
                pl.BlockSpec((1, 6, D_MODEL), lambda i, t: (mod_off + i * mod_stride, 0, 0)),
                const(D_MODEL, P_END), const(1, 256), const(256, 512), const(1, 128),
                const(1, 256), const(1, 128), const(256, 256)]
    args = [x, mod, lw["wmix"], lw["qn"], lw["wuq"], lw["kvn"], lw["gqn"], lw["gkn"], lw["bd"]]
    out_specs = [tok(FQ_W), tok(FKV_W), tok(1024)]
    out_shape = [jax.ShapeDtypeStruct((g, tg, FQ_W), BF16), jax.ShapeDtypeStruct((g, tg, FKV_W), BF16),
                 jax.ShapeDtypeStruct((g, tg, 1024), F32)]
    if latent:
        in_specs += [pl.BlockSpec((tm, LANES), lambda i, t: (t, 0))] * 6
        args += list(rope)
    else:
        out_specs += [tok(128), tok(MLA_ROPE), tok(128), tok(128), tok(128), tok(128)]
        out_shape += [jax.ShapeDtypeStruct((g, tg, w), F32) for w in (128, MLA_ROPE, 128, 128, 128, 128)]
    return pl.pallas_call(
        functools.partial(_proj_kernel, latent=latent),
        grid=grid, in_specs=in_specs, out_specs=out_specs, out_shape=out_shape,
        compiler_params=pltpu.CompilerParams(dimension_semantics=("arbitrary", "arbitrary"),
                                             vmem_limit_bytes=VMEM_LIMIT),
        name="proj_latent" if latent else "proj_ctx",
    )(*args)


def _softmax_parts(parts, sink):
    m = parts[0].max(-1, keepdims=True)
    for s in parts[1:]:
        m = jnp.maximum(m, s.max(-1, keepdims=True))
    if sink is not None:
        m = jnp.maximum(m, sink)
    es = [jnp.exp(s - m) for s in parts]
    den = es[0].sum(-1, keepdims=True)
    for e in es[1:]:
        den = den + e.sum(-1, keepdims=True)
    if sink is not None:
        den = den + jnp.exp(sink - m)
    inv = 1.0 / den
    return [(e * inv).astype(BF16) for e in es]


def _attn_kernel(*refs, t, tq, n_cache):
    latent = n_cache > 0
    s_len = t + n_cache
    if latent:
        (fq_ref, fkv_ref, cckv_ref, ckpe_ref, cwk_ref, cwv_ref, cgk_ref, cgv_ref,
         wka_ref, wuvm_ref, place_ref, sink_ref, o_ref,
         ckpe_s, ka_s, vam_s, kd_s, vd_s, vdm_s, vcm_s, kcc_s, vccm_s) = refs
    else:
        (fq_ref, fkv_ref, wka_ref, wuvm_ref, place_ref, sink_ref, o_ref,
         ckpe_s, ka_s, vam_s, kd_s, vd_s, vdm_s, vcm_s) = refs

    qi = pl.program_id(1)

    @pl.when(qi == 0)
    def _():
        ckpe_s[0:t, :] = fkv_ref[0, :, 0:256]
        kd_s[0:t, :] = fkv_ref[0, :, 512:640]
        vd_s[0:t, :] = fkv_ref[0, :, 640:768]
        if latent:
            ckpe_s[t:s_len, 0:128] = cckv_ref[0, 0].astype(BF16)
            ckpe_s[t:s_len, 128:256] = ckpe_ref[0, 0].astype(BF16)
            kd_s[t:s_len, :] = cgk_ref[0, 0].astype(BF16)
            vd_s[t:s_len, :] = cgv_ref[0, 0].astype(BF16)
            kcc_s[...] = cwk_ref[0, 0].astype(BF16)
        ck = ckpe_s[...]
        vd = vd_s[...]
        vc = fkv_ref[0, :, 384:512]
        for h in range(N_HEADS):
            ka_s[h] = _dot(ck, wka_ref[h]).astype(BF16)
            vam_s[h] = _dot(ck, wuvm_ref[h]).astype(BF16)
            vdm_s[h] = _dot(vd, place_ref[h]).astype(BF16)
            vcm_s[h] = _dot(vc, place_ref[h]).astype(BF16)
            if latent:
                vccm_s[h] = _dot(cwv_ref[0, 0].astype(BF16), place_ref[h]).astype(BF16)

    lane = lax.broadcasted_iota(jnp.int32, (tq, LANES), 1)

    acc = jnp.zeros((tq, BRANCH_W), F32)
    for h in range(N_HEADS):
        q = fq_ref[0, :, h * LANES:(h + 1) * LANES]
        (pr,) = _softmax_parts([_dot_nt(q, ka_s[h])], None)
        acc = acc + _dot(pr, vam_s[h])
    o_ref[0, :, 0:256] = acc.astype(BF16)

    if latent:
        span = tq + 2 * WINDOW
        q0 = qi * tq
        start = pl.multiple_of(jnp.clip(q0 - WINDOW, 0, t - span), LANES)
        k_loc = fkv_ref[0, pl.ds(start, span), 256:384]
        qpos = q0 + lax.broadcasted_iota(jnp.int32, (tq, span), 0)
        kpos = start + lax.broadcasted_iota(jnp.int32, (tq, span), 1)
        valid = jnp.abs(qpos - kpos) <= WINDOW
    else:
        k_loc = fkv_ref[0, :, 256:384]
    acc = jnp.zeros((tq, BRANCH_W), F32)
    for g in range(2):
        blk = fq_ref[0, :, 512 + g * LANES:512 + (g + 1) * LANES]
        for j in range(2):
            h = 2 * j + g
            qm = jnp.where((lane // HEAD_DIM) == j, blk.astype(F32), 0.0).astype(BF16)
            s_loc = _dot_nt(qm, k_loc)
            sink = sink_ref[h]
            if latent:
                s_loc = jnp.where(valid, s_loc, NEG_INF)
                p_loc, p_ctx = _softmax_parts([s_loc, _dot_nt(qm, kcc_s[...])], sink)
                acc = acc + _dot(p_loc, vcm_s[h, pl.ds(start, span), :]) + _dot(p_ctx, vccm_s[h])
            else:
                (p_loc,) = _softmax_parts([s_loc], sink)
                acc = acc + _dot(p_loc, vcm_s[h])
    o_ref[0, :, 256:512] = acc.astype(BF16)

    acc = jnp.zeros((tq, BRANCH_W), F32)
    kd = kd_s[...]
    for g in range(2):
        blk = fq_ref[0, :, 768 + g * LANES:768 + (g + 1) * LANES]
        for j in range(2):
            h = 2 * j + g
            qm = jnp.where((lane // HEAD_DIM) == j, blk.astype(F32), 0.0).astype(BF16)
            (pr,) = _softmax_parts([_dot_nt(qm, kd)], None)
            acc = acc + _dot(pr, vdm_s[h])
    o_ref[0, :, 512:768] = acc.astype(BF16)


def _attn(fq, fkv, caches, lw, layer, tq):
    b, t, _ = fq.shape
    latent = caches is not None
    n_cache = caches[0].shape[2] if latent else 0
    s_len = t + n_cache
    const = lambda *s: pl.BlockSpec(s, lambda i, q: (0,) * len(s))
    in_specs = [pl.BlockSpec((1, tq, FQ_W), lambda i, q: (i, q, 0)),
                pl.BlockSpec((1, t, FKV_W), lambda i, q: (i, 0, 0))]
    args = [fq, fkv]
    if latent:
        in_specs += [pl.BlockSpec((1, 1, n_cache, LANES), lambda i, q: (i, layer, 0, 0))] * 6
        args += list(caches)
    in_specs += [const(N_HEADS, 256, 128), const(N_HEADS, 256, 256), const(N_HEADS, 128, 256),
                 pl.BlockSpec(memory_space=pltpu.SMEM)]
    args += [lw["wka"], lw["wuvm"], lw["place"], lw["sink"]]
    scratch = [pltpu.VMEM((s_len, 256), BF16), pltpu.VMEM((N_HEADS, s_len, 128), BF16),
               pltpu.VMEM((N_HEADS, s_len, 256), BF16), pltpu.VMEM((s_len, 128), BF16),
               pltpu.VMEM((s_len, 128), BF16), pltpu.VMEM((N_HEADS, s_len, 256), BF16),
               pltpu.VMEM((N_HEADS, t, 256), BF16)]
    if latent:
        scratch += [pltpu.VMEM((n_cache, 128), BF16), pltpu.VMEM((N_HEADS, n_cache, 256), BF16)]
    return pl.pallas_call(
        functools.partial(_attn_kernel, t=t, tq=tq, n_cache=n_cache),
        grid=(b, t // tq), in_specs=in_specs,
        out_specs=pl.BlockSpec((1, tq, 768), lambda i, q: (i, q, 0)),
        out_shape=jax.ShapeDtypeStruct((b, t, 768), BF16),
        scratch_shapes=scratch,
        compiler_params=pltpu.CompilerParams(dimension_semantics=("arbitrary", "arbitrary"),
                                             vmem_limit_bytes=VMEM_LIMIT),
        name="attn_latent" if latent else "attn_ctx",
    )(*args)


def _ret_kernel(fr_ref, s0f_ref, s0b_ref, decf_ref, decb_ref, gain_ref, bd_ref,
                o_ref, sf_ref, sb_ref, acc_s, *, t):
    nc = t // CHUNK
    w = BRANCH_W
    lane_w = lax.broadcasted_iota(jnp.int32, (CHUNK, w), 1) // HEAD_DIM
    row_w = lax.broadcasted_iota(jnp.int32, (CHUNK, w), 0).astype(F32)

    def lane_decay(dec_ref):
        v = jnp.zeros((CHUNK, w), F32)
        for h in range(N_HEADS):
            v = jnp.where(lane_w == h, dec_ref[h], v)
        return jax.nn.log_sigmoid(v)

    lgf = lane_decay(decf_ref)
    lgb = lane_decay(decb_ref)
    qdec_f = jnp.exp((row_w + 1.0) * lgf)
    kdec_f = jnp.exp((CHUNK - 1.0 - row_w) * lgf)
    qdec_b = jnp.exp((CHUNK - row_w) * lgb)
    kdec_b = jnp.exp(row_w * lgb)
    cdec_f = jnp.concatenate([jnp.exp(CHUNK * lgf)] * (w // CHUNK), axis=0)
    cdec_b = jnp.concatenate([jnp.exp(CHUNK * lgb)] * (w // CHUNK), axis=0)
    r2 = lax.broadcasted_iota(jnp.int32, (w, w), 0) // HEAD_DIM
    c2 = lax.broadcasted_iota(jnp.int32, (w, w), 1) // HEAD_DIM
    diag = r2 == c2

    ii = lax.broadcasted_iota(jnp.int32, (CHUNK, CHUNK), 0).astype(F32)
    jj = lax.broadcasted_iota(jnp.int32, (CHUNK, CHUNK), 1).astype(F32)
    diff = ii - jj
    dmats = []
    for h in range(N_HEADS):
        lf = jax.nn.log_sigmoid(jnp.full((CHUNK, CHUNK), decf_ref[h], F32))
        lb = jax.nn.log_sigmoid(jnp.full((CHUNK, CHUNK), decb_ref[h], F32))
        d_f = jnp.where(diff >= 0, jnp.exp(jnp.maximum(diff, 0.0) * lf), 0.0)
        d_b = jnp.where(diff < 0, jnp.exp(jnp.maximum(-diff, 0.0) * lb), 0.0)
        dmats.append(d_f + d_b)

    sf_ref[0] = s0f_ref[0]
    sb_ref[0] = s0b_ref[0]

    def fwd(n, carry):
        r0 = pl.multiple_of(n * CHUNK, CHUNK)
        q = fr_ref[0, pl.ds(r0, CHUNK), 0:256]
        k = fr_ref[0, pl.ds(r0, CHUNK), 256:512]
        v = fr_ref[0, pl.ds(r0, CHUNK), 512:768]
        kb, vb = k.astype(BF16), v.astype(BF16)
        out = _dot(q.astype(BF16), sf_ref[0].astype(BF16)) * qdec_f
        for h in range(N_HEADS):
            msk = lane_w == h
            qk = _dot_nt(jnp.where(msk, q, 0.0).astype(BF16), kb)
            att = (qk * dmats[h]).astype(BF16)
            out = out + _dot(att, jnp.where(msk, v, 0.0).astype(BF16))
        acc_s[pl.ds(r0, CHUNK), :] = out
        kv = _dot_tn((k * kdec_f).astype(BF16), vb)
        sf_ref[0] = sf_ref[0] * cdec_f + jnp.where(diag, kv, 0.0)
        return carry

    lax.fori_loop(0, nc, fwd, 0)

    gain = gain_ref[...]
    bd = bd_ref[...]

    def bwd(i, carry):
        n = nc - 1 - i
        r0 = pl.multiple_of(n * CHUNK, CHUNK)
        q = fr_ref[0, pl.ds(r0, CHUNK), 0:256]
        k = fr_ref[0, pl.ds(r0, CHUNK), 256:512]
        v = fr_ref[0, pl.ds(r0, CHUNK), 512:768]
        gate = fr_ref[0, pl.ds(r0, CHUNK), 768:1024]
        out = acc_s[pl.ds(r0, CHUNK), :] + _dot(q.astype(BF16), sb_ref[0].astype(BF16)) * qdec_b
        kv = _dot_tn((k * kdec_b).astype(BF16), v.astype(BF16))
        sb_ref[0] = sb_ref[0] * cdec_b + jnp.where(diag, kv, 0.0)
        mu = _seg_sum(out, bd) * (1.0 / HEAD_DIM)
        d = out - mu
        var = _seg_sum(d * d, bd) * (1.0 / HEAD_DIM)
        y = d * lax.rsqrt(var + NORM_EPS) * gain
        o_ref[0, pl.ds(r0, CHUNK), :] = (gate * jax.nn.sigmoid(gate) * y).astype(BF16)
        return carry

    lax.fori_loop(0, nc, bwd, 0)


def _ret(fr, s0f, s0b, lw, layer):
    b, t, _ = fr.shape
    stride = 1 if s0f.shape[0] == b else 0
    s_spec = pl.BlockSpec((1, BRANCH_W, BRANCH_W), lambda i: (i * stride, 0, 0))
    smem = pl.BlockSpec(memory_space=pltpu.SMEM)
    st_spec = pl.BlockSpec((1, BRANCH_W, BRANCH_W), lambda i: (i, 0, 0))
    return pl.pallas_call(
        functools.partial(_ret_kernel, t=t),
        grid=(b,),
        in_specs=[pl.BlockSpec((1, t, 1024), lambda i: (i, 0, 0)), s_spec, s_spec, smem, smem,
                  pl.BlockSpec((1, BRANCH_W), lambda i: (0, 0)),
                  pl.BlockSpec((BRANCH_W, BRANCH_W), lambda i: (0, 0))],
        out_specs=[pl.BlockSpec((1, t, BRANCH_W), lambda i: (i, 0, 0)), st_spec, st_spec],
        out_shape=[jax.ShapeDtypeStruct((b, t, BRANCH_W), BF16),
                   jax.ShapeDtypeStruct((b, BRANCH_W, BRANCH_W), F32),
                   jax.ShapeDtypeStruct((b, BRANCH_W, BRANCH_W), F32)],
        scratch_shapes=[pltpu.VMEM((t, BRANCH_W), F32)],
        compiler_params=pltpu.CompilerParams(dimension_semantics=("arbitrary",),
                                             vmem_limit_bytes=VMEM_LIMIT),
        name="retention",
    )(fr, s0f, s0b, lw["decf"], lw["decb"], lw["gn"], lw["bd"])


def _merge_kernel(x_ref, mod_ref, oatt_ref, oret_ref, wg_ref, wb_ref, wo_ref, g_ref, b_ref, o_ref):
    x = x_ref[0]
    sh1 = mod_ref[0, 0:1, :]
    sc1 = mod_ref[0, 1:2, :]
    g1 = mod_ref[0, 2:3, :]
    h = (x * (1.0 + sc1) + sh1).astype(BF16)
    branches = (oatt_ref[0, :, 0:256], oret_ref[0], oatt_ref[0, :, 256:512], oatt_ref[0, :, 512:768])
    tsum = None
    for i, o in enumerate(branches):
        gate = jax.nn.sigmoid(_dot(h, wg_ref[:, i * D_MODEL:(i + 1) * D_MODEL]))
        term = gate * _dot(o, wb_ref[i])
        tsum = term if tsum is None else tsum + term
    y = _dot(tsum.astype(BF16), wo_ref[...])
    o_ref[0] = _layernorm(ALPHA * x + g1 * y, g_ref[...], b_ref[...])


def _merge(x, mod, oatt, oret, lw, tm, mod_off, mod_stride):
    g, tg, _ = x.shape
    const = lambda *s: pl.BlockSpec(s, lambda i, t: (0,) * len(s))
    tok = lambda w: pl.BlockSpec((1, tm, w), lambda i, t: (i, t, 0))
    return pl.pallas_call(
        _merge_kernel,
        grid=(g, tg // tm),
        in_specs=[tok(D_MODEL), pl.BlockSpec((1, 6, D_MODEL), lambda i, t: (mod_off + i * mod_stride, 0, 0)),
                  tok(768), tok(BRANCH_W), const(D_MODEL, 4 * D_MODEL), const(4, BRANCH_W, D_MODEL),
                  const(D_MODEL, D_MODEL), const(1, D_MODEL), const(1, D_MODEL)],
        out_specs=tok(D_MODEL),
        out_shape=jax.ShapeDtypeStruct((g, tg, D_MODEL), F32),
        compiler_params=pltpu.CompilerParams(dimension_semantics=("arbitrary", "arbitrary"),
                                             vmem_limit_bytes=VMEM_LIMIT),
        name="merge",
    )(x, mod, oatt, oret, lw["wg"], lw["wb"], lw["wo"], lw["ln1g"], lw["ln1b"])


def _mlp_kernel(x_ref, mod_ref, wup_ref, wdn_ref, g_ref, b_ref, o_ref):
    x = x_ref[0]
    sh2 = mod_ref[0, 3:4, :]
    sc2 = mod_ref[0, 4:5, :]
    g2 = mod_ref[0, 5:6, :]
    h = (x * (1.0 + sc2) + sh2).astype(BF16)
    u = jnp.maximum(_dot(h, wup_ref[...]), 0.0)
    f = _dot((u * u).astype(BF16), wdn_ref[...])
    o_ref[0] = _layernorm(ALPHA * x + g2 * f, g_ref[...], b_ref[...])


def _mlp(x, mod, lw, tm, mod_off, mod_stride):
    g, tg, _ = x.shape
    const = lambda *s: pl.BlockSpec(s, lambda i, t: (0,) * len(s))
    tok = lambda w: pl.BlockSpec((1, tm, w), lambda i, t: (i, t, 0))
    return pl.pallas_call(
        _mlp_kernel,
        grid=(g, tg // tm),
        in_specs=[tok(D_MODEL), pl.BlockSpec((1, 6, D_MODEL), lambda i, t: (mod_off + i * mod_stride, 0, 0)),
                  const(D_MODEL, D_FF), const(D_FF, D_MODEL), const(1, D_MODEL), const(1, D_MODEL)],
        out_specs=tok(D_MODEL),
        out_shape=jax.ShapeDtypeStruct((g, tg, D_MODEL), F32),
        compiler_params=pltpu.CompilerParams(dimension_semantics=("arbitrary", "arbitrary"),
                                             vmem_limit_bytes=VMEM_LIMIT),
        name="mlp",
    )(x, mod, lw["wup"], lw["wdn"], lw["ln2g"], lw["ln2b"])


def _swap_heads(wcols):
    k = wcols.shape[0]
    return wcols.reshape(k, 2, 2, HEAD_DIM).transpose(0, 2, 1, 3).reshape(k, 4 * HEAD_DIM)


def _layer_weights(l, w_in, mla_q_norm, mla_w_uq, mla_kv_norm, mla_w_uk, mla_w_uv, ret_decay_fwd, ret_decay_bwd,
                   ret_gn_gain, win_sink, gqa_q_norm, gqa_k_norm, w_branch, w_o, ln1_g, ln1_b, w_up, w_down,
                   ln2_g, ln2_b):
    wi = w_in[l]
    o_kpe = 256 + 128
    o_ret = o_kpe + MLA_ROPE
    o_wq = o_ret + 1024
    o_wk = o_wq + 256
    o_gq = o_wk + 256
    o_gk = o_gq + 256
    o_gate = o_gk + 256
    wmix = jnp.concatenate([
        wi[:, :o_ret], jnp.zeros((D_MODEL, LANES - MLA_ROPE), F32), wi[:, o_ret:o_wq],
        _swap_heads(wi[:, o_wq:o_wk]), wi[:, o_wk:o_gq],
        _swap_heads(wi[:, o_gq:o_gk]), wi[:, o_gk:o_gate]], axis=1).astype(BF16)
    wuq = jnp.pad(mla_w_uq[l].reshape(256, N_HEADS, HEAD_DIM + MLA_ROPE),
                  ((0, 0), (0, 0), (0, LANES - HEAD_DIM - MLA_ROPE))).reshape(256, N_HEADS * LANES).astype(BF16)
    uk = mla_w_uk[l].reshape(MLA_KV_RANK, N_HEADS, HEAD_DIM).transpose(1, 0, 2)
    top = jnp.pad(uk, ((0, 0), (0, 0), (0, LANES - HEAD_DIM)))
    eye = jnp.zeros((LANES, LANES), F32).at[jnp.arange(MLA_ROPE), HEAD_DIM + jnp.arange(MLA_ROPE)].set(1.0)
    wka = jnp.concatenate([top, jnp.broadcast_to(eye, (N_HEADS, LANES, LANES))], axis=1).astype(BF16)
    head_of_col = jnp.arange(BRANCH_W) // HEAD_DIM
    uvm = jnp.where(head_of_col[None, None, :] == jnp.arange(N_HEADS)[:, None, None], mla_w_uv[l][None], 0.0)
    wuvm = jnp.pad(uvm, ((0, 0), (0, 256 - MLA_KV_RANK), (0, 0))).astype(BF16)
    r = jnp.arange(LANES)[:, None]
    c = jnp.arange(BRANCH_W)[None, :]
    place = jnp.stack([((c // HEAD_DIM == h) & (r == (h // 2) * HEAD_DIM + c - h * HEAD_DIM)) for h in range(N_HEADS)]
                      ).astype(BF16)
    bd = (jnp.arange(BRANCH_W)[:, None] // HEAD_DIM == jnp.arange(BRANCH_W)[None, :] // HEAD_DIM).astype(BF16)
    return dict(
        wmix=wmix, wg=wi[:, o_gate:].astype(BF16), wuq=wuq, wka=wka, wuvm=wuvm, place=place, bd=bd,
        qn=mla_q_norm[l][None], kvn=mla_kv_norm[l][None],
        gqn=jnp.tile(gqa_q_norm[l], N_HEADS)[None], gkn=jnp.tile(gqa_k_norm[l], 2)[None],
        decf=ret_decay_fwd[l], decb=ret_decay_bwd[l], gn=ret_gn_gain[l][None], sink=win_sink[l],
        wb=w_branch[l].astype(BF16), wo=w_o[l].astype(BF16), ln1g=ln1_g[l][None], ln1b=ln1_b[l][None],
        wup=w_up[l].astype(BF16), wdn=w_down[l].astype(BF16), ln2g=ln2_g[l][None], ln2b=ln2_b[l][None])


def _axial_tables(t, rot_dim):
    rows = t // GRID_W
    row = jnp.repeat(jnp.arange(rows, dtype=F32), GRID_W)
    col = (jnp.arange(t) % GRID_W).astype(F32)
    n_freq = rot_dim // 4
    inv = ROPE_BASE ** (-jnp.arange(n_freq, dtype=F32) / n_freq)
    ang = jnp.concatenate([row[:, None] * inv, col[:, None] * inv], axis=-1)
    return jnp.cos(ang), jnp.sin(ang)


def _rope_tables(t):
    ca, sa = _axial_tables(t, MLA_ROPE)
    ch, sh = _axial_tables(t, HEAD_DIM)
    one = lambda n: jnp.ones((t, n), F32)
    zero = lambda n: jnp.zeros((t, n), F32)
    cq = jnp.concatenate([one(HEAD_DIM), ca, ca, one(32)], axis=1)
    sq = jnp.concatenate([zero(HEAD_DIM), -sa, sa, zero(32)], axis=1)
    ck = jnp.concatenate([ca, ca, one(96)], axis=1)
    sk = jnp.concatenate([-sa, sa, zero(96)], axis=1)
    chh = jnp.concatenate([ch, ch, ch, ch], axis=1)
    shh = jnp.concatenate([-sh, sh, -sh, sh], axis=1)
    return cq, sq, ck, sk, chh, shh


def _block_diag(s):
    b = s.shape[0]
    eye = jnp.eye(N_HEADS, dtype=s.dtype)
    return jnp.einsum("bhde,hg->bhdge", s, eye).reshape(b, BRANCH_W, BRANCH_W)


def _diag_blocks(s):
    b = s.shape[0]
    s5 = s.reshape(b, N_HEADS, HEAD_DIM, N_HEADS, HEAD_DIM)
    return jnp.stack([s5[:, h, :, h, :] for h in range(N_HEADS)], axis=1)


def kernel(x_prompt, x_sample, cache_mla_ckv, cache_mla_kpe, cache_win_k, cache_win_v, cache_gqa_k, cache_gqa_v,
           state_ret_fwd, state_ret_bwd, c, c_ctx, w_ada, b_ada, w_in, mla_q_norm, mla_w_uq, mla_kv_norm, mla_w_uk,
           mla_w_uv, ret_decay_fwd, ret_decay_bwd, ret_gn_gain, win_sink, gqa_q_norm, gqa_k_norm, w_branch, w_o,
           ln1_g, ln1_b, w_up, w_down, ln2_g, ln2_b):
    batch, seq, _ = x_prompt.shape
    dec_b, dec_t, _ = x_sample.shape
    past = cache_mla_ckv.shape[2]

    cond8 = jnp.concatenate([c_ctx[None], c, jnp.zeros((8 - 1 - dec_b, D_MODEL), F32)], axis=0)
    mod = _ada(cond8, w_ada, b_ada).reshape(DEPTH, 8, 6, D_MODEL)
    rope = _rope_tables(dec_t)
    caches = (cache_mla_ckv,
              jnp.pad(cache_mla_kpe, ((0, 0), (0, 0), (0, 0), (0, LANES - MLA_ROPE))),
              cache_win_k.reshape(dec_b, DEPTH, past, LANES), cache_win_v.reshape(dec_b, DEPTH, past, LANES),
              cache_gqa_k.reshape(dec_b, DEPTH, past, LANES), cache_gqa_v.reshape(dec_b, DEPTH, past, LANES))

    zero_state = jnp.zeros((1, BRANCH_W, BRANCH_W), F32)
    xp = x_prompt.reshape(1, batch * seq, D_MODEL)
    xs = x_sample
    ctx_out = []
    for l in range(DEPTH):
        lw = _layer_weights(l, w_in, mla_q_norm, mla_w_uq, mla_kv_norm, mla_w_uk, mla_w_uv, ret_decay_fwd,
                            ret_decay_bwd, ret_gn_gain, win_sink, gqa_q_norm, gqa_k_norm, w_branch, w_o, ln1_g,
                            ln1_b, w_up, w_down, ln2_g, ln2_b)
        fq, fkv, fr, ockv, okpe, owk, owv, ogk, ogv = _proj(xp, mod[l], lw, None, False, 512, 0, 0)
        per_b = lambda a: a.reshape(batch, seq, a.shape[-1])
        oatt = _attn(per_b(fq), per_b(fkv), None, lw, l, seq)
        oret, s_f, s_b = _ret(per_b(fr), zero_state, zero_state, lw, l)
        flat = lambda a: a.reshape(1, batch * seq, a.shape[-1])
        x1 = _merge(xp, mod[l], flat(oatt), flat(oret), lw, 256, 0, 0)
        xp = _mlp(x1, mod[l], lw, 256, 0, 0)
        ctx_out.append((per_b(ockv), per_b(okpe), per_b(owk).reshape(batch, seq, 2, HEAD_DIM),
                        per_b(owv).reshape(batch, seq, 2, HEAD_DIM), per_b(ogk).reshape(batch, seq, 2, HEAD_DIM),
                        per_b(ogv).reshape(batch, seq, 2, HEAD_DIM), _diag_blocks(s_f), _diag_blocks(s_b)))
        fq, fkv, fr = _proj(xs, mod[l], lw, rope, True, 512, 1, 1)
        oatt = _attn(fq, fkv, caches, lw, l, 256)
        oret, _, _ = _ret(fr, _block_diag(state_ret_fwd[:, l]), _block_diag(state_ret_bwd[:, l]), lw, l)
        x1 = _merge(xs, mod[l], oatt, oret, lw, 256, 1, 1)
        xs = _mlp(x1, mod[l], lw, 256, 1, 1)

    y_prompt = xp.reshape(batch, seq, D_MODEL)
    stacked = [jnp.stack([ctx_out[l][i] for l in range(DEPTH)], axis=1) for i in range(8)]
    return (y_prompt, xs, *stacked)
```

```python
import functools

import jax
import jax.numpy as jnp
import numpy as np
from jax import lax
from jax.experimental import pallas as pl
from jax.experimental.pallas import tpu as pltpu

D_MODEL = 1024
DEPTH = 2
GRID_W = 64
CHUNK = 128
WINDOW = 128
ROPE_BASE = 10000.0
NORM_EPS = 1e-6
NEG_INF = -1e30
HEAD_DIM = 64
N_HEADS = 4
BRANCH_W = 256
MLA_ROPE = 32
MLA_KV_RANK = 128
MLA_SCALE = (HEAD_DIM + MLA_ROPE) ** -0.5
ATT_SCALE = HEAD_DIM ** -0.5
D_FF = 4 * D_MODEL
ALPHA = (2.0 * DEPTH) ** 0.25
LANES = 128

P_QLAT, P_KVLAT, P_KPE, P_RET, P_WQ, P_WK, P_WV, P_GQ, P_GK, P_GV, P_END = (
    0, 256, 384, 512, 1536, 1792, 1920, 2048, 2304, 2432, 2560)
FQ_W = 1024
FKV_W = 768
VMEM_LIMIT = 56 * 1024 * 1024
TM_PROJ = 512
TM_MERGE = 256
TM_MLP = 256
TQ_LATENT = 256

F32 = jnp.float32
BF16 = jnp.bfloat16


def _dot(a, b):
    return jnp.dot(a, b, preferred_element_type=F32)


def _dot_nt(a, b):
    return lax.dot_general(a, b, (((1,), (1,)), ((), ())), preferred_element_type=F32)


def _dot_tn(a, b):
    return lax.dot_general(a, b, (((0,), (0,)), ((), ())), preferred_element_type=F32)


def _layernorm(x, g, b):
    mu = jnp.mean(x, -1, keepdims=True)
    d = x - mu
    var = jnp.mean(d * d, -1, keepdims=True)
    return d * lax.rsqrt(var + NORM_EPS) * g + b


def _rmsnorm(x, g):
    return x * lax.rsqrt(jnp.mean(x * x, -1, keepdims=True) + NORM_EPS) * g


def _seg_sum(x, ones_bd):
    hi = x.astype(BF16)
    lo = (x - hi.astype(F32)).astype(BF16)
    return _dot(hi, ones_bd) + _dot(lo, ones_bd)


def _rope_block(x, cos, sin, half, first):
    rot = jnp.where(first, pltpu.roll(x, LANES - half, 1), pltpu.roll(x, half, 1))
    return x * cos + rot * sin


def _ada_kernel(cond_ref, w_ref, b_ref, o_ref):
    cnd = cond_ref[...]
    s = (cnd * jax.nn.sigmoid(cnd)).astype(BF16)
    o_ref[0] = _dot(s, w_ref[0].astype(BF16)) + b_ref[0]


def _ada(cond8, w_ada, b_ada):
    tn = 1024
    n = w_ada.shape[-1]
    return pl.pallas_call(
        _ada_kernel,
        grid=(DEPTH, n // tn),
        in_specs=[pl.BlockSpec((8, D_MODEL), lambda l, j: (0, 0)),
                  pl.BlockSpec((1, D_MODEL, tn), lambda l, j: (l, 0, j)),
                  pl.BlockSpec((1, 1, tn), lambda l, j: (l, 0, j))],
        out_specs=pl.BlockSpec((1, 8, tn), lambda l, j: (l, 0, j)),
        out_shape=jax.ShapeDtypeStruct((DEPTH, 8, n), F32),
        compiler_params=pltpu.CompilerParams(dimension_semantics=("arbitrary", "arbitrary")),
        name="ada_mod",
    )(cond8, w_ada, b_ada.reshape(DEPTH, 1, n))


def _proj_kernel(*refs, latent):
    if latent:
        (x_ref, mod_ref, wmix_ref, qn_ref, wuq_ref, kvn_ref, gqn_ref, gkn_ref, bd_ref,
         cq_ref, sq_ref, ck_ref, sk_ref, ch_ref, sh_ref,
         fq_ref, fkv_ref, fr_ref) = refs
    else:
        (x_ref, mod_ref, wmix_ref, qn_ref, wuq_ref, kvn_ref, gqn_ref, gkn_ref, bd_ref,
         fq_ref, fkv_ref, fr_ref, ockv_ref, okpe_ref, owk_ref, owv_ref, ogk_ref, ogv_ref) = refs

    x = x_ref[...]
    sh1 = mod_ref[0, 0:1, :]
    sc1 = mod_ref[0, 1:2, :]
    h = (x * (1.0 + sc1) + sh1).astype(BF16)
    p = _dot(h, wmix_ref[0])
    tm = p.shape[0]

    lane = lax.broadcasted_iota(jnp.int32, (tm, LANES), 1)
    first_head = (lane % HEAD_DIM) < (HEAD_DIM // 2)

    def rope_heads(v):
        if not latent:
            return v
        cos, sin = ch_ref[...], sh_ref[...]
        blocks = [_rope_block(v[:, j:j + LANES], cos, sin, HEAD_DIM // 2, first_head)
                  for j in range(0, v.shape[1], LANES)]
        return blocks[0] if len(blocks) == 1 else jnp.concatenate(blocks, axis=1)

    qn = _rmsnorm(p[:, P_QLAT:P_KVLAT], qn_ref[0]).astype(BF16)
    qa = _dot(qn, wuq_ref[0])
    if latent:
        cos, sin = cq_ref[...], sq_ref[...]
        first = (lane >= HEAD_DIM) & (lane < HEAD_DIM + MLA_ROPE // 2)
        qa = jnp.concatenate(
            [_rope_block(qa[:, j:j + LANES], cos, sin, MLA_ROPE // 2, first) for j in range(0, 512, LANES)],
            axis=1)
    fq_ref[:, 0:512] = (qa * MLA_SCALE).astype(BF16)

    ckv = _rmsnorm(p[:, P_KVLAT:P_KPE], kvn_ref[0])
    kpe = p[:, P_KPE:P_RET]
    if latent:
        first = lane < MLA_ROPE // 2
        kpe_r = _rope_block(kpe, ck_ref[...], sk_ref[...], MLA_ROPE // 2, first)
    else:
        kpe_r = kpe
        ockv_ref[...] = ckv
        okpe_ref[...] = kpe[:, 0:MLA_ROPE]
    fkv_ref[:, 0:128] = ckv.astype(BF16)
    fkv_ref[:, 128:256] = kpe_r.astype(BF16)

    fr_ref[:, 0:256] = p[:, P_RET:P_RET + 256]
    fr_ref[:, 256:512] = p[:, P_RET + 256:P_RET + 512] * (HEAD_DIM ** -0.5)
    fr_ref[:, 512:1024] = p[:, P_RET + 512:P_WQ]

    fq_ref[:, 512:768] = (rope_heads(p[:, P_WQ:P_WK]) * ATT_SCALE).astype(BF16)
    wk = p[:, P_WK:P_WV]
    wv = p[:, P_WV:P_GQ]
    fkv_ref[:, 256:384] = rope_heads(wk).astype(BF16)
    fkv_ref[:, 384:512] = wv.astype(BF16)

    bd = bd_ref[...]
    gq = p[:, P_GQ:P_GK]
    gqn = gq * lax.rsqrt(_seg_sum(gq * gq, bd) * (1.0 / HEAD_DIM) + NORM_EPS) * gqn_ref[0]
    fq_ref[:, 768:1024] = (rope_heads(gqn) * ATT_SCALE).astype(BF16)
    gk = p[:, P_GK:P_GV]
    gkn = gk * lax.rsqrt(_seg_sum(gk * gk, bd[0:128, 0:128]) * (1.0 / HEAD_DIM) + NORM_EPS) * gkn_ref[0]
    gv = p[:, P_GV:P_END]
    fkv_ref[:, 512:640] = rope_heads(gkn).astype(BF16)
    fkv_ref[:, 640:768] = gv.astype(BF16)
    if not latent:
        owk_ref[...] = wk
        owv_ref[...] = wv
        ogk_ref[...] = gkn
        ogv_ref[...] = gv


def _layer_spec(layer, *s):
    return pl.BlockSpec((1,) + s, lambda t: (layer,) + (0,) * len(s))


def _mod_spec(tm, group_len):
    if group_len is None:
        return pl.BlockSpec((1, 6, D_MODEL), lambda t: (0, 0, 0))
    per = group_len // tm
    return pl.BlockSpec((1, 6, D_MODEL), lambda t: (1 + t // per, 0, 0))


def _proj(x, mod, wts, layer, rope, tm, group_len):
    latent = group_len is not None
    nt = x.shape[0]
    tok = lambda w: pl.BlockSpec((tm, w), lambda t: (t, 0))
    in_specs = [tok(D_MODEL), _mod_spec(tm, group_len),
                _layer_spec(layer, D_MODEL, P_END), _layer_spec(layer, 1, 256), _layer_spec(layer, 256, 512),
                _layer_spec(layer, 1, 128), _layer_spec(layer, 1, 256), _layer_spec(layer, 1, 128),
                pl.BlockSpec((256, 256), lambda t: (0, 0))]
    args = [x, mod, wts["wmix"], wts["qn"], wts["wuq"], wts["kvn"], wts["gqn"], wts["gkn"], wts["bd"]]
    out_specs = [tok(FQ_W), tok(FKV_W), tok(1024)]
    out_shape = [jax.ShapeDtypeStruct((nt, FQ_W), BF16), jax.ShapeDtypeStruct((nt, FKV_W), BF16),
                 jax.ShapeDtypeStruct((nt, 1024), F32)]
    if latent:
        per = group_len // tm
        in_specs += [pl.BlockSpec((tm, LANES), lambda t: (t % per, 0))] * 6
        args += list(rope)
    else:
        out_specs += [tok(128), tok(MLA_ROPE), tok(128), tok(128), tok(128), tok(128)]
        out_shape += [jax.ShapeDtypeStruct((nt, w), F32) for w in (128, MLA_ROPE, 128, 128, 128, 128)]
    return pl.pallas_call(
        functools.partial(_proj_kernel, latent=latent),
        grid=(nt // tm,), in_specs=in_specs, out_specs=out_specs, out_shape=out_shape,
        compiler_params=pltpu.CompilerParams(dimension_semantics=("arbitrary",), vmem_limit_bytes=VMEM_LIMIT),
        name="proj_latent" if latent else "proj_ctx",
    )(*args)


def _softmax_parts(parts, sink):
    m = parts[0].max(-1, keepdims=True)
    for s in parts[1:]:
        m = jnp.maximum(m, s.max(-1, keepdims=True))
    if sink is not None:
        m = jnp.maximum(m, sink)
    es = [jnp.exp(s - m) for s in parts]
    den = es[0].sum(-1, keepdims=True)
    for e in es[1:]:
        den = den + e.sum(-1, keepdims=True)
    if sink is not None:
        den = den + jnp.exp(sink - m)
    inv = 1.0 / den
    return [(e * inv).astype(BF16) for e in es]


def _attn_kernel(*refs, t, tq, n_cache, layer):
    latent = n_cache > 0
    s_len = t + n_cache
    if latent:
        (fq_ref, fkv_ref, cckv_ref, ckpe_ref, cwk_ref, cwv_ref, cgk_ref, cgv_ref,
         wka_ref, wuvm_ref, place_ref, sink_ref, o_ref,
         ckpe_s, ka_s, vam_s, kd_s, vd_s, vdm_s, vcm_s, kcc_s, vccm_s) = refs
    else:
        (fq_ref, fkv_ref, wka_ref, wuvm_ref, place_ref, sink_ref, o_ref,
         ckpe_s, ka_s, vam_s, kd_s, vd_s, vdm_s, vcm_s) = refs

    qi = pl.program_id(1)

    @pl.when(qi == 0)
    def _():
        ckpe_s[0:t, :] = fkv_ref[0, :, 0:256]
        kd_s[0:t, :] = fkv_ref[0, :, 512:640]
        vd_s[0:t, :] = fkv_ref[0, :, 640:768]
        if latent:
            ckpe_s[t:s_len, 0:128] = cckv_ref[0, 0].astype(BF16)
            ckpe_s[t:s_len, 128:256] = ckpe_ref[0, 0].astype(BF16)
            kd_s[t:s_len, :] = cgk_ref[0, 0].astype(BF16)
            vd_s[t:s_len, :] = cgv_ref[0, 0].astype(BF16)
            kcc_s[...] = cwk_ref[0, 0].astype(BF16)
        ck = ckpe_s[...]
        vd = vd_s[...]
        vc = fkv_ref[0, :, 384:512]
        for h in range(N_HEADS):
            ka_s[h] = _dot(ck, wka_ref[0, h]).astype(BF16)
            vam_s[h] = _dot(ck, wuvm_ref[0, h]).astype(BF16)
            vdm_s[h] = _dot(vd, place_ref[h]).astype(BF16)
            vcm_s[h] = _dot(vc, place_ref[h]).astype(BF16)
            if latent:
                vccm_s[h] = _dot(cwv_ref[0, 0].astype(BF16), place_ref[h]).astype(BF16)

    lane = lax.broadcasted_iota(jnp.int32, (tq, LANES), 1)

    acc = jnp.zeros((tq, BRANCH_W), F32)
    for h in range(N_HEADS):
        q = fq_ref[0, :, h * LANES:(h + 1) * LANES]
        (pr,) = _softmax_parts([_dot_nt(q, ka_s[h])], None)
        acc = acc + _dot(pr, vam_s[h])
    o_ref[0, :, 0:256] = acc.astype(BF16)

    if latent:
        span = tq + 2 * WINDOW
        q0 = qi * tq
        start = pl.multiple_of(jnp.clip(q0 - WINDOW, 0, t - span), LANES)
        k_loc = fkv_ref[0, pl.ds(start, span), 256:384]
        qpos = q0 + lax.broadcasted_iota(jnp.int32, (tq, span), 0)
        kpos = start + lax.broadcasted_iota(jnp.int32, (tq, span), 1)
        valid = jnp.abs(qpos - kpos) <= WINDOW
    else:
        k_loc = fkv_ref[0, :, 256:384]
    acc = jnp.zeros((tq, BRANCH_W), F32)
    for g in range(2):
        blk = fq_ref[0, :, 512 + g * LANES:512 + (g + 1) * LANES]
        for j in range(2):
            h = 2 * j + g
            qm = jnp.where((lane // HEAD_DIM) == j, blk.astype(F32), 0.0).astype(BF16)
            s_loc = _dot_nt(qm, k_loc)
            sink = sink_ref[layer, h]
            if latent:
                s_loc = jnp.where(valid, s_loc, NEG_INF)
                p_loc, p_ctx = _softmax_parts([s_loc, _dot_nt(qm, kcc_s[...])], sink)
                acc = acc + _dot(p_loc, vcm_s[h, pl.ds(start, span), :]) + _dot(p_ctx, vccm_s[h])
            else:
                (p_loc,) = _softmax_parts([s_loc], sink)
                acc = acc + _dot(p_loc, vcm_s[h])
    o_ref[0, :, 256:512] = acc.astype(BF16)

    acc = jnp.zeros((tq, BRANCH_W), F32)
    kd = kd_s[...]
    for g in range(2):
        blk = fq_ref[0, :, 768 + g * LANES:768 + (g + 1) * LANES]
        for j in range(2):
            h = 2 * j + g
            qm = jnp.where((lane // HEAD_DIM) == j, blk.astype(F32), 0.0).astype(BF16)
            (pr,) = _softmax_parts([_dot_nt(qm, kd)], None)
            acc = acc + _dot(pr, vdm_s[h])
    o_ref[0, :, 512:768] = acc.astype(BF16)


def _attn(fq, fkv, caches, wts, layer, tq):
    b, t, _ = fq.shape
    latent = caches is not None
    n_cache = caches[0].shape[2] if latent else 0
    s_len = t + n_cache
    in_specs = [pl.BlockSpec((1, tq, FQ_W), lambda i, q: (i, q, 0)),
                pl.BlockSpec((1, t, FKV_W), lambda i, q: (i, 0, 0))]
    args = [fq, fkv]
    if latent:
        in_specs += [pl.BlockSpec((1, 1, n_cache, LANES), lambda i, q: (i, layer, 0, 0))] * 6
        args += list(caches)
    in_specs += [pl.BlockSpec((1, N_HEADS, 256, 128), lambda i, q: (layer, 0, 0, 0)),
                 pl.BlockSpec((1, N_HEADS, 256, 256), lambda i, q: (layer, 0, 0, 0)),
                 pl.BlockSpec((N_HEADS, 128, 256), lambda i, q: (0, 0, 0)),
                 pl.BlockSpec(memory_space=pltpu.SMEM)]
    args += [wts["wka"], wts["wuvm"], wts["place"], wts["sink"]]
    scratch = [pltpu.VMEM((s_len, 256), BF16), pltpu.VMEM((N_HEADS, s_len, 128), BF16),
               pltpu.VMEM((N_HEADS, s_len, 256), BF16), pltpu.VMEM((s_len, 128), BF16),
               pltpu.VMEM((s_len, 128), BF16), pltpu.VMEM((N_HEADS, s_len, 256), BF16),
               pltpu.VMEM((N_HEADS, t, 256), BF16)]
    if latent:
        scratch += [pltpu.VMEM((n_cache, 128), BF16), pltpu.VMEM((N_HEADS, n_cache, 256), BF16)]
    return pl.pallas_call(
        functools.partial(_attn_kernel, t=t, tq=tq, n_cache=n_cache, layer=layer),
        grid=(b, t // tq), in_specs=in_specs,
        out_specs=pl.BlockSpec((1, tq, 768), lambda i, q: (i, q, 0)),
        out_shape=jax.ShapeDtypeStruct((b, t, 768), BF16),
        scratch_shapes=scratch,
        compiler_params=pltpu.CompilerParams(dimension_semantics=("arbitrary", "arbitrary"),
                                             vmem_limit_bytes=VMEM_LIMIT),
        name="attn_latent" if latent else "attn_ctx",
    )(*args)


def _ret_kernel(*refs, t, layer, want_state):
    if want_state:
        (fr_ref, s0f_ref, s0b_ref, decf_ref, decb_ref, gain_ref, bd_ref, o_ref, sf_ref, sb_ref,
         dm_s, dec_s, cd_s, kvf_s, kvb_s, stf_s, stb_s, sf_s, sb_s) = refs
    else:
        (fr_ref, s0f_ref, s0b_ref, decf_ref, decb_ref, gain_ref, bd_ref, o_ref,
         dm_s, dec_s, cd_s, kvf_s, kvb_s, stf_s, stb_s, sf_s, sb_s) = refs
    nc = t // CHUNK
    w = BRANCH_W
    lane_w = lax.broadcasted_iota(jnp.int32, (CHUNK, w), 1) // HEAD_DIM

    @pl.when(pl.program_id(0) == 0)
    def _():
        row_w = lax.broadcasted_iota(jnp.int32, (CHUNK, w), 0).astype(F32)

        def lane_decay(dec_ref):
            v = jnp.zeros((CHUNK, w), F32)
            for h in range(N_HEADS):
                v = jnp.where(lane_w == h, dec_ref[layer, h], v)
            return jax.nn.log_sigmoid(v)

        lgf = lane_decay(decf_ref)
        lgb = lane_decay(decb_ref)
        dec_s[0] = jnp.exp((row_w + 1.0) * lgf)
        dec_s[1] = jnp.exp((CHUNK - 1.0 - row_w) * lgf)
        dec_s[2] = jnp.exp((CHUNK - row_w) * lgb)
        dec_s[3] = jnp.exp(row_w * lgb)
        cd_s[0] = jnp.concatenate([jnp.exp(CHUNK * lgf)] * (w // CHUNK), axis=0)
        cd_s[1] = jnp.concatenate([jnp.exp(CHUNK * lgb)] * (w // CHUNK), axis=0)
        ii = lax.broadcasted_iota(jnp.int32, (CHUNK, CHUNK), 0).astype(F32)
        jj = lax.broadcasted_iota(jnp.int32, (CHUNK, CHUNK), 1).astype(F32)
        diff = ii - jj
        for h in range(N_HEADS):
            lf = jax.nn.log_sigmoid(jnp.full((CHUNK, CHUNK), decf_ref[layer, h], F32))
            lb = jax.nn.log_sigmoid(jnp.full((CHUNK, CHUNK), decb_ref[layer, h], F32))
            d_f = jnp.where(diff >= 0, jnp.exp(jnp.maximum(diff, 0.0) * lf), 0.0)
            d_b = jnp.where(diff < 0, jnp.exp(jnp.maximum(-diff, 0.0) * lb), 0.0)
            dm_s[h] = d_f + d_b

    r2 = lax.broadcasted_iota(jnp.int32, (w, w), 0) // HEAD_DIM
    c2 = lax.broadcasted_iota(jnp.int32, (w, w), 1) // HEAD_DIM
    diag = r2 == c2

    def kv_body(n, carry):
        r0 = pl.multiple_of(n * CHUNK, CHUNK)
        k = fr_ref[0, pl.ds(r0, CHUNK), 256:512]
        vb = fr_ref[0, pl.ds(r0, CHUNK), 512:768].astype(BF16)
        kvf_s[n] = jnp.where(diag, _dot_tn((k * dec_s[1]).astype(BF16), vb), 0.0)
        kvb_s[n] = jnp.where(diag, _dot_tn((k * dec_s[3]).astype(BF16), vb), 0.0)
        return carry

    lax.fori_loop(0, nc, kv_body, 0, unroll=2)

    sf_s[...] = s0f_ref[0]
    sb_s[...] = s0b_ref[0]

    def scan_body(i, carry):
        m = nc - 1 - i
        sf = sf_s[...]
        stf_s[i] = sf.astype(BF16)
        sf_s[...] = sf * cd_s[0] + kvf_s[i]
        sb = sb_s[...]
        stb_s[m] = sb.astype(BF16)
        sb_s[...] = sb * cd_s[1] + kvb_s[m]
        return carry

    lax.fori_loop(0, nc, scan_body, 0)
    if want_state:
        for h in range(N_HEADS):
            sl = slice(h * HEAD_DIM, (h + 1) * HEAD_DIM)
            sf_ref[0, h] = sf_s[sl, sl]
            sb_ref[0, h] = sb_s[sl, sl]

    gain = gain_ref[0]
    bd = bd_ref[...]

    def out_body(n, carry):
        r0 = pl.multiple_of(n * CHUNK, CHUNK)
        q = fr_ref[0, pl.ds(r0, CHUNK), 0:256]
        kb = fr_ref[0, pl.ds(r0, CHUNK), 256:512].astype(BF16)
        v = fr_ref[0, pl.ds(r0, CHUNK), 512:768]
        gate = fr_ref[0, pl.ds(r0, CHUNK), 768:1024]
        qb = q.astype(BF16)
        out = _dot(qb, stf_s[n]) * dec_s[0] + _dot(qb, stb_s[n]) * dec_s[2]
        for h in range(N_HEADS):
            msk = lane_w == h
            qk = _dot_nt(jnp.where(msk, q, 0.0).astype(BF16), kb)
            att = (qk * dm_s[h]).astype(BF16)
            out = out + _dot(att, jnp.where(msk, v, 0.0).astype(BF16))
        mu = _seg_sum(out, bd) * (1.0 / HEAD_DIM)
        d = out - mu
        var = _seg_sum(d * d, bd) * (1.0 / HEAD_DIM)
        y = d * lax.rsqrt(var + NORM_EPS) * gain
        o_ref[0, pl.ds(r0, CHUNK), :] = (gate * jax.nn.sigmoid(gate) * y).astype(BF16)
        return carry

    lax.fori_loop(0, nc, out_body, 0, unroll=2)


def _ret(fr, s0f, s0b, wts, layer, want_state):
    b, t, _ = fr.shape
    nc = t // CHUNK
    stride = 1 if s0f.shape[0] == b else 0
    s_spec = pl.BlockSpec((1, BRANCH_W, BRANCH_W), lambda i: (i * stride, 0, 0))
    smem = pl.BlockSpec(memory_space=pltpu.SMEM)
    out_specs = [pl.BlockSpec((1, t, BRANCH_W), lambda i: (i, 0, 0))]
    out_shape = [jax.ShapeDtypeStruct((b, t, BRANCH_W), BF16)]
    if want_state:
        st_spec = pl.BlockSpec((1, N_HEADS, HEAD_DIM, HEAD_DIM), lambda i: (i, 0, 0, 0))
        out_specs += [st_spec, st_spec]
        out_shape += [jax.ShapeDtypeStruct((b, N_HEADS, HEAD_DIM, HEAD_DIM), F32)] * 2
    sq = (BRANCH_W, BRANCH_W)
    return pl.pallas_call(
        functools.partial(_ret_kernel, t=t, layer=layer, want_state=want_state),
        grid=(b,),
        in_specs=[pl.BlockSpec((1, t, 1024), lambda i: (i, 0, 0)), s_spec, s_spec, smem, smem,
                  pl.BlockSpec((1, 1, BRANCH_W), lambda i: (layer, 0, 0)),
                  pl.BlockSpec(sq, lambda i: (0, 0))],
        out_specs=out_specs, out_shape=out_shape,
        scratch_shapes=[pltpu.VMEM((N_HEADS, CHUNK, CHUNK), F32), pltpu.VMEM((4, CHUNK, BRANCH_W), F32),
                        pltpu.VMEM((2,) + sq, F32), pltpu.VMEM((nc,) + sq, F32), pltpu.VMEM((nc,) + sq, F32),
                        pltpu.VMEM((nc,) + sq, BF16), pltpu.VMEM((nc,) + sq, BF16),
                        pltpu.VMEM(sq, F32), pltpu.VMEM(sq, F32)],
        compiler_params=pltpu.CompilerParams(dimension_semantics=("arbitrary",),
                                             vmem_limit_bytes=VMEM_LIMIT),
        name="retention",
    )(fr, s0f, s0b, wts["decf"], wts["decb"], wts["gn"], wts["bd"])


def _merge_kernel(x_ref, mod_ref, oatt_ref, oret_ref, wg_ref, wb_ref, wo_ref, g_ref, b_ref, o_ref):
    x = x_ref[...]
    sh1 = mod_ref[0, 0:1, :]
    sc1 = mod_ref[0, 1:2, :]
    g1 = mod_ref[0, 2:3, :]
    h = (x * (1.0 + sc1) + sh1).astype(BF16)
    branches = (oatt_ref[:, 0:256], oret_ref[...], oatt_ref[:, 256:512], oatt_ref[:, 512:768])
    tsum = None
    for i, o in enumerate(branches):
        gate = jax.nn.sigmoid(_dot(h, wg_ref[0, :, i * D_MODEL:(i + 1) * D_MODEL]))
        term = gate * _dot(o, wb_ref[0, i])
        tsum = term if tsum is None else tsum + term
    y = _dot(tsum.astype(BF16), wo_ref[0])
    o_ref[...] = _layernorm(ALPHA * x + g1 * y, g_ref[0], b_ref[0])


def _merge(x, mod, oatt, oret, wts, layer, tm, group_len):
    nt = x.shape[0]
    tok = lambda w: pl.BlockSpec((tm, w), lambda t: (t, 0))
    return pl.pallas_call(
        _merge_kernel,
        grid=(nt // tm,),
        in_specs=[tok(D_MODEL), _mod_spec(tm, group_len), tok(768), tok(BRANCH_W),
                  _layer_spec(layer, D_MODEL, 4 * D_MODEL), _layer_spec(layer, 4, BRANCH_W, D_MODEL),
                  _layer_spec(layer, D_MODEL, D_MODEL), _layer_spec(layer, 1, D_MODEL),
                  _layer_spec(layer, 1, D_MODEL)],
        out_specs=tok(D_MODEL),
        out_shape=jax.ShapeDtypeStruct((nt, D_MODEL), F32),
        compiler_params=pltpu.CompilerParams(dimension_semantics=("arbitrary",), vmem_limit_bytes=VMEM_LIMIT),
        name="merge",
    )(x, mod, oatt, oret, wts["wg"], wts["wb"], wts["wo"], wts["ln1g"], wts["ln1b"])


def _mlp_kernel(x_ref, mod_ref, wup_ref, wdn_ref, g_ref, b_ref, o_ref):
    x = x_ref[...]
    sh2 = mod_ref[0, 3:4, :]
    sc2 = mod_ref[0, 4:5, :]
    g2 = mod_ref[0, 5:6, :]
    h = (x * (1.0 + sc2) + sh2).astype(BF16)
    u = jnp.maximum(_dot(h, wup_ref[0]), 0.0)
    f = _dot((u * u).astype(BF16), wdn_ref[0])
    o_ref[...] = _layernorm(ALPHA * x + g2 * f, g_ref[0], b_ref[0])


def _mlp(x, mod, wts, layer, tm, group_len):
    nt = x.shape[0]
    tok = lambda w: pl.BlockSpec((tm, w), lambda t: (t, 0))
    return pl.pallas_call(
        _mlp_kernel,
        grid=(nt // tm,),
        in_specs=[tok(D_MODEL), _mod_spec(tm, group_len), _layer_spec(layer, D_MODEL, D_FF),
                  _layer_spec(layer, D_FF, D_MODEL), _layer_spec(layer, 1, D_MODEL), _layer_spec(layer, 1, D_MODEL)],
        out_specs=tok(D_MODEL),
        out_shape=jax.ShapeDtypeStruct((nt, D_MODEL), F32),
        compiler_params=pltpu.CompilerParams(dimension_semantics=("arbitrary",), vmem_limit_bytes=VMEM_LIMIT),
        name="mlp",
    )(x, mod, wts["wup"], wts["wdn"], wts["ln2g"], wts["ln2b"])


def _swap_heads(wcols):
    lead = wcols.shape[:-1]
    return jnp.swapaxes(wcols.reshape(*lead, 2, 2, HEAD_DIM), -3, -2).reshape(*lead, 4 * HEAD_DIM)


def _prep_weights(w_in, mla_q_norm, mla_w_uq, mla_kv_norm, mla_w_uk, mla_w_uv, ret_decay_fwd, ret_decay_bwd,
                  ret_gn_gain, win_sink, gqa_q_norm, gqa_k_norm, w_branch, w_o, ln1_g, ln1_b, w_up, w_down,
                  ln2_g, ln2_b):
    o_ret = 256 + 128 + MLA_ROPE
    o_wq = o_ret + 1024
    o_wk = o_wq + 256
    o_gq = o_wk + 256
    o_gk = o_gq + 256
    o_gate = o_gk + 256
    wmix = jnp.concatenate([
        w_in[..., :o_ret], jnp.zeros((DEPTH, D_MODEL, LANES - MLA_ROPE), F32), w_in[..., o_ret:o_wq],
        _swap_heads(w_in[..., o_wq:o_wk]), w_in[..., o_wk:o_gq],
        _swap_heads(w_in[..., o_gq:o_gk]), w_in[..., o_gk:o_gate]], axis=-1).astype(BF16)
    wuq = jnp.pad(mla_w_uq.reshape(DEPTH, 256, N_HEADS, HEAD_DIM + MLA_ROPE),
                  ((0, 0), (0, 0), (0, 0), (0, LANES - HEAD_DIM - MLA_ROPE))
                  ).reshape(DEPTH, 256, N_HEADS * LANES).astype(BF16)
    uk = mla_w_uk.reshape(DEPTH, MLA_KV_RANK, N_HEADS, HEAD_DIM).transpose(0, 2, 1, 3)
    top = jnp.pad(uk, ((0, 0), (0, 0), (0, 0), (0, LANES - HEAD_DIM)))
    eye = np.zeros((LANES, LANES), np.float32)
    eye[np.arange(MLA_ROPE), HEAD_DIM + np.arange(MLA_ROPE)] = 1.0
    wka = jnp.concatenate([top, jnp.broadcast_to(eye, (DEPTH, N_HEADS, LANES, LANES))], axis=2).astype(BF16)
    head_of_col = np.arange(BRANCH_W) // HEAD_DIM
    sel = head_of_col[None, None, None, :] == np.arange(N_HEADS)[None, :, None, None]
    uvm = jnp.where(sel, mla_w_uv[:, None], 0.0)
    wuvm = jnp.pad(uvm, ((0, 0), (0, 0), (0, 256 - MLA_KV_RANK), (0, 0))).astype(BF16)
    r = np.arange(LANES)[:, None]
    c = np.arange(BRANCH_W)[None, :]
    place = np.stack([((c // HEAD_DIM == h) & (r == (h // 2) * HEAD_DIM + c - h * HEAD_DIM))
                      for h in range(N_HEADS)]).astype(np.float32)
    bd = (np.arange(BRANCH_W)[:, None] // HEAD_DIM == np.arange(BRANCH_W)[None, :] // HEAD_DIM).astype(np.float32)
    row = lambda a: a[:, None, :]
    return dict(
        wmix=wmix, wg=w_in[..., o_gate:].astype(BF16), wuq=wuq, wka=wka, wuvm=wuvm,
        place=jnp.asarray(place, BF16), bd=jnp.asarray(bd, BF16),
        qn=row(mla_q_norm), kvn=row(mla_kv_norm),
        gqn=row(jnp.tile(gqa_q_norm, (1, N_HEADS))), gkn=row(jnp.tile(gqa_k_norm, (1, 2))),
        decf=ret_decay_fwd, decb=ret_decay_bwd, gn=row(ret_gn_gain), sink=win_sink,
        wb=w_branch.astype(BF16), wo=w_o.astype(BF16), ln1g=row(ln1_g), ln1b=row(ln1_b),
        wup=w_up.astype(BF16), wdn=w_down.astype(BF16), ln2g=row(ln2_g), ln2b=row(ln2_b))


def _axial_tables(t, rot_dim):
    rows = t // GRID_W
    row = np.repeat(np.arange(rows, dtype=np.float32), GRID_W)
    col = (np.arange(t) % GRID_W).astype(np.float32)
    n_freq = rot_dim // 4
    inv = np.power(np.float32(ROPE_BASE), -np.arange(n_freq, dtype=np.float32) / np.float32(n_freq))
    ang = np.concatenate([row[:, None] * inv, col[:, None] * inv], axis=-1).astype(np.float32)
    return np.cos(ang), np.sin(ang)


def _rope_tables(t):
    ca, sa = _axial_tables(t, MLA_ROPE)
    ch, sh = _axial_tables(t, HEAD_DIM)
    one = lambda n: np.ones((t, n), np.float32)
    zero = lambda n: np.zeros((t, n), np.float32)
    cq = np.concatenate([one(HEAD_DIM), ca, ca, one(32)], axis=1)
    sq = np.concatenate([zero(HEAD_DIM), -sa, sa, zero(32)], axis=1)
    ck = np.concatenate([ca, ca, one(96)], axis=1)
    sk = np.concatenate([-sa, sa, zero(96)], axis=1)
    chh = np.concatenate([ch, ch, ch, ch], axis=1)
    shh = np.concatenate([-sh, sh, -sh, sh], axis=1)
    return tuple(jnp.asarray(a, F32) for a in (cq, sq, ck, sk, chh, shh))


def _block_diag(s):
    b = s.shape[0]
    same = np.eye(N_HEADS, dtype=bool)[None, :, None, :, None]
    return jnp.where(same, s[:, :, :, None, :], 0.0).reshape(b, BRANCH_W, BRANCH_W)


def kernel(x_prompt, x_sample, cache_mla_ckv, cache_mla_kpe, cache_win_k, cache_win_v, cache_gqa_k, cache_gqa_v,
           state_ret_fwd, state_ret_bwd, c, c_ctx, w_ada, b_ada, w_in, mla_q_norm, mla_w_uq, mla_kv_norm, mla_w_uk,
           mla_w_uv, ret_decay_fwd, ret_decay_bwd, ret_gn_gain, win_sink, gqa_q_norm, gqa_k_norm, w_branch, w_o,
           ln1_g, ln1_b, w_up, w_down, ln2_g, ln2_b):
    batch, seq, _ = x_prompt.shape
    dec_b, dec_t, _ = x_sample.shape
    past = cache_mla_ckv.shape[2]

    cond8 = jnp.concatenate([c_ctx[None], c, jnp.zeros((8 - 1 - dec_b, D_MODEL), F32)], axis=0)
    mod = _ada(cond8, w_ada, b_ada).reshape(DEPTH, 8, 6, D_MODEL)
    rope = _rope_tables(dec_t)
    caches = (cache_mla_ckv,
              jnp.pad(cache_mla_kpe, ((0, 0), (0, 0), (0, 0), (0, LANES - MLA_ROPE))),
              cache_win_k.reshape(dec_b, DEPTH, past, LANES), cache_win_v.reshape(dec_b, DEPTH, past, LANES),
              cache_gqa_k.reshape(dec_b, DEPTH, past, LANES), cache_gqa_v.reshape(dec_b, DEPTH, past, LANES))

    wts = _prep_weights(w_in, mla_q_norm, mla_w_uq, mla_kv_norm, mla_w_uk, mla_w_uv, ret_decay_fwd, ret_decay_bwd,
                        ret_gn_gain, win_sink, gqa_q_norm, gqa_k_norm, w_branch, w_o, ln1_g, ln1_b, w_up, w_down,
                        ln2_g, ln2_b)
    zero_state = jnp.zeros((1, BRANCH_W, BRANCH_W), F32)
    xp = x_prompt.reshape(batch * seq, D_MODEL)
    xs = x_sample.reshape(dec_b * dec_t, D_MODEL)
    per_b = lambda a: a.reshape(batch, seq, a.shape[-1])
    per_d = lambda a: a.reshape(dec_b, dec_t, a.shape[-1])
    flat = lambda a: a.reshape(-1, a.shape[-1])
    ctx_out = []
    for l in range(DEPTH):
        fq, fkv, fr, ockv, okpe, owk, owv, ogk, ogv = _proj(xp, mod[l], wts, l, None, TM_PROJ, None)
        oatt = _attn(per_b(fq), per_b(fkv), None, wts, l, seq)
        oret, s_f, s_b = _ret(per_b(fr), zero_state, zero_state, wts, l, True)
        x1 = _merge(xp, mod[l], flat(oatt), flat(oret), wts, l, TM_MERGE, None)
        xp = _mlp(x1, mod[l], wts, l, TM_MLP, None)
        heads = lambda a: a.reshape(batch, seq, 2, HEAD_DIM)
        ctx_out.append((per_b(ockv), per_b(okpe), heads(owk), heads(owv), heads(ogk), heads(ogv), s_f, s_b))
        fq, fkv, fr = _proj(xs, mod[l], wts, l, rope, TM_PROJ, dec_t)
        oatt = _attn(per_d(fq), per_d(fkv), caches, wts, l, TQ_LATENT)
        (oret,) = _ret(per_d(fr), _block_diag(state_ret_fwd[:, l]), _block_diag(state_ret_bwd[:, l]), wts, l, False)
        x1 = _merge(xs, mod[l], flat(oatt), flat(oret), wts, l, TM_MERGE, dec_t)
        xs = _mlp(x1, mod[l], wts, l, TM_MLP, dec_t)

    stacked = [jnp.stack([ctx_out[l][i] for l in range(DEPTH)], axis=1) for i in range(8)]
    return (per_b(xp), per_d(xs), *stacked)
```

```python
import functools

import jax
import jax.numpy as jnp
import numpy as np
from jax import lax
from jax.experimental import pallas as pl
from jax.experimental.pallas import tpu as pltpu

D_MODEL = 1024
DEPTH = 2
GRID_W = 64
CHUNK = 128
WINDOW = 128
ROPE_BASE = 10000.0
NORM_EPS = 1e-6
NEG_INF = -1e30
HEAD_DIM = 64
N_HEADS = 4
BRANCH_W = 256
MLA_ROPE = 32
MLA_KV_RANK = 128
MLA_SCALE = (HEAD_DIM + MLA_ROPE) ** -0.5
ATT_SCALE = HEAD_DIM ** -0.5
D_FF = 4 * D_MODEL
ALPHA = (2.0 * DEPTH) ** 0.25
LANES = 128

P_QLAT, P_KVLAT, P_KPE, P_RET, P_WQ, P_WK, P_WV, P_GQ, P_GK, P_GV, P_END = (
    0, 256, 384, 512, 1536, 1792, 1920, 2048, 2304, 2432, 2560)
FQ_W = 1024
FKV_W = 768
VMEM_LIMIT = 56 * 1024 * 1024
TM_PROJ = 512
TM_MERGE = 256
TM_MLP = 256
TQ_LATENT = 256
RET_GROUP = 4
RET_CHUNKS_PER_STEP = 8

F32 = jnp.float32
BF16 = jnp.bfloat16


def _dot(a, b):
    return jnp.dot(a, b, preferred_element_type=F32)


def _dot_nt(a, b):
    return lax.dot_general(a, b, (((1,), (1,)), ((), ())), preferred_element_type=F32)


def _dot_tn(a, b):
    return lax.dot_general(a, b, (((0,), (0,)), ((), ())), preferred_element_type=F32)


def _layernorm(x, g, b):
    mu = jnp.mean(x, -1, keepdims=True)
    d = x - mu
    var = jnp.mean(d * d, -1, keepdims=True)
    return d * lax.rsqrt(var + NORM_EPS) * g + b


def _rmsnorm(x, g):
    return x * lax.rsqrt(jnp.mean(x * x, -1, keepdims=True) + NORM_EPS) * g


def _seg_sum(x, ones_bd):
    hi = x.astype(BF16)
    lo = (x - hi.astype(F32)).astype(BF16)
    return _dot(hi, ones_bd) + _dot(lo, ones_bd)


def _rope_block(x, cos, sin, half, first):
    rot = jnp.where(first, pltpu.roll(x, LANES - half, 1), pltpu.roll(x, half, 1))
    return x * cos + rot * sin


def _ada_kernel(cond_ref, w_ref, b_ref, o_ref):
    cnd = cond_ref[...]
    s = (cnd * jax.nn.sigmoid(cnd)).astype(BF16)
    o_ref[0] = _dot(s, w_ref[0].astype(BF16)) + b_ref[0]


def _ada(cond8, w_ada, b_ada):
    tn = 1024
    n = w_ada.shape[-1]
    return pl.pallas_call(
        _ada_kernel,
        grid=(DEPTH, n // tn),
        in_specs=[pl.BlockSpec((8, D_MODEL), lambda l, j: (0, 0)),
                  pl.BlockSpec((1, D_MODEL, tn), lambda l, j: (l, 0, j)),
                  pl.BlockSpec((1, 1, tn), lambda l, j: (l, 0, j))],
        out_specs=pl.BlockSpec((1, 8, tn), lambda l, j: (l, 0, j)),
        out_shape=jax.ShapeDtypeStruct((DEPTH, 8, n), F32),
        compiler_params=pltpu.CompilerParams(dimension_semantics=("arbitrary", "arbitrary")),
        name="ada_mod",
    )(cond8, w_ada, b_ada.reshape(DEPTH, 1, n))


def _store_transposed(o_ref, val):
    n, r, seq = o_ref.shape
    for i in range(n):
        o_ref[i] = val[i * seq:(i + 1) * seq, :].T[0:r, :]


def _proj_kernel(*refs, latent):
    if latent:
        (x_ref, mod_ref, wmix_ref, qn_ref, wuq_ref, kvn_ref, gqn_ref, gkn_ref, bd_ref,
         cq_ref, sq_ref, ck_ref, sk_ref, ch_ref, sh_ref,
         fq_ref, fkv_ref, fr_ref) = refs
    else:
        (x_ref, mod_ref, wmix_ref, qn_ref, wuq_ref, kvn_ref, gqn_ref, gkn_ref, bd_ref,
         fq_ref, fkv_ref, fr_ref, ockv_ref, okpe_ref, owk_ref, owv_ref, ogk_ref, ogv_ref) = refs

    x = x_ref[...]
    sh1 = mod_ref[0, 0:1, :]
    sc1 = mod_ref[0, 1:2, :]
    h = (x * (1.0 + sc1) + sh1).astype(BF16)
    p = _dot(h, wmix_ref[0])
    tm = p.shape[0]

    lane = lax.broadcasted_iota(jnp.int32, (tm, LANES), 1)
    first_head = (lane % HEAD_DIM) < (HEAD_DIM // 2)

    def rope_heads(v):
        if not latent:
            return v
        cos, sin = ch_ref[...], sh_ref[...]
        blocks = [_rope_block(v[:, j:j + LANES], cos, sin, HEAD_DIM // 2, first_head)
                  for j in range(0, v.shape[1], LANES)]
        return blocks[0] if len(blocks) == 1 else jnp.concatenate(blocks, axis=1)

    qn = _rmsnorm(p[:, P_QLAT:P_KVLAT], qn_ref[0]).astype(BF16)
    qa = _dot(qn, wuq_ref[0])
    if latent:
        cos, sin = cq_ref[...], sq_ref[...]
        first = (lane >= HEAD_DIM) & (lane < HEAD_DIM + MLA_ROPE // 2)
        qa = jnp.concatenate(
            [_rope_block(qa[:, j:j + LANES], cos, sin, MLA_ROPE // 2, first) for j in range(0, 512, LANES)],
            axis=1)
    fq_ref[:, 0:512] = (qa * MLA_SCALE).astype(BF16)

    ckv = _rmsnorm(p[:, P_KVLAT:P_KPE], kvn_ref[0])
    kpe = p[:, P_KPE:P_RET]
    if latent:
        first = lane < MLA_ROPE // 2
        kpe_r = _rope_block(kpe, ck_ref[...], sk_ref[...], MLA_ROPE // 2, first)
    else:
        kpe_r = kpe
        ockv_ref[...] = ckv.reshape(ockv_ref.shape)
        _store_transposed(okpe_ref, kpe)
    fkv_ref[:, 0:128] = ckv.astype(BF16)
    fkv_ref[:, 128:256] = kpe_r.astype(BF16)

    fr_ref[:, 0:256] = p[:, P_RET:P_RET + 256]
    fr_ref[:, 256:512] = p[:, P_RET + 256:P_RET + 512] * (HEAD_DIM ** -0.5)
    fr_ref[:, 512:1024] = p[:, P_RET + 512:P_WQ]

    fq_ref[:, 512:768] = (rope_heads(p[:, P_WQ:P_WK]) * ATT_SCALE).astype(BF16)
    wk = p[:, P_WK:P_WV]
    wv = p[:, P_WV:P_GQ]
    fkv_ref[:, 256:384] = rope_heads(wk).astype(BF16)
    fkv_ref[:, 384:512] = wv.astype(BF16)

    bd = bd_ref[...]
    gq = p[:, P_GQ:P_GK]
    gqn = gq * lax.rsqrt(_seg_sum(gq * gq, bd) * (1.0 / HEAD_DIM) + NORM_EPS) * gqn_ref[0]
    fq_ref[:, 768:1024] = (rope_heads(gqn) * ATT_SCALE).astype(BF16)
    gk = p[:, P_GK:P_GV]
    gkn = gk * lax.rsqrt(_seg_sum(gk * gk, bd[0:128, 0:128]) * (1.0 / HEAD_DIM) + NORM_EPS) * gkn_ref[0]
    gv = p[:, P_GV:P_END]
    fkv_ref[:, 512:640] = rope_heads(gkn).astype(BF16)
    fkv_ref[:, 640:768] = gv.astype(BF16)
    if not latent:
        _store_transposed(owk_ref, wk)
        _store_transposed(owv_ref, wv)
        _store_transposed(ogk_ref, gkn)
        _store_transposed(ogv_ref, gv)


def _layer_spec(layer, *s):
    return pl.BlockSpec((1,) + s, lambda t: (layer,) + (0,) * len(s))


def _mod_spec(tm, group_len):
    if group_len is None:
        return pl.BlockSpec((1, 6, D_MODEL), lambda t: (0, 0, 0))
    per = group_len // tm
    return pl.BlockSpec((1, 6, D_MODEL), lambda t: (1 + t // per, 0, 0))


def _proj(x, mod, wts, layer, rope, tm, group_len, ctx_seq=None):
    latent = group_len is not None
    nt = x.shape[0]
    tok = lambda w: pl.BlockSpec((tm, w), lambda t: (t, 0))
    in_specs = [tok(D_MODEL), _mod_spec(tm, group_len),
                _layer_spec(layer, D_MODEL, P_END), _layer_spec(layer, 1, 256), _layer_spec(layer, 256, 512),
                _layer_spec(layer, 1, 128), _layer_spec(layer, 1, 256), _layer_spec(layer, 1, 128),
                pl.BlockSpec((256, 256), lambda t: (0, 0))]
    args = [x, mod, wts["wmix"], wts["qn"], wts["wuq"], wts["kvn"], wts["gqn"], wts["gkn"], wts["bd"]]
    out_specs = [tok(FQ_W), tok(FKV_W), tok(1024)]
    out_shape = [jax.ShapeDtypeStruct((nt, FQ_W), BF16), jax.ShapeDtypeStruct((nt, FKV_W), BF16),
                 jax.ShapeDtypeStruct((nt, 1024), F32)]
    if latent:
        per = group_len // tm
        in_specs += [pl.BlockSpec((tm, LANES), lambda t: (t % per, 0))] * 6
        args += list(rope)
    else:
        nb = tm // ctx_seq
        seqs = nt // ctx_seq
        out_specs += [pl.BlockSpec((nb, ctx_seq, 128), lambda t: (t, 0, 0))]
        out_shape += [jax.ShapeDtypeStruct((seqs, ctx_seq, 128), F32)]
        for r in (MLA_ROPE, 128, 128, 128, 128):
            out_specs.append(pl.BlockSpec((nb, r, ctx_seq), lambda t: (t, 0, 0)))
            out_shape.append(jax.ShapeDtypeStruct((seqs, r, ctx_seq), F32))
    return pl.pallas_call(
        functools.partial(_proj_kernel, latent=latent),
        grid=(nt // tm,), in_specs=in_specs, out_specs=out_specs, out_shape=out_shape,
        compiler_params=pltpu.CompilerParams(dimension_semantics=("arbitrary",), vmem_limit_bytes=VMEM_LIMIT),
        name="proj_latent" if latent else "proj_ctx",
    )(*args)


def _softmax_parts(parts, sink):
    m = parts[0].max(-1, keepdims=True)
    for s in parts[1:]:
        m = jnp.maximum(m, s.max(-1, keepdims=True))
    if sink is not None:
        m = jnp.maximum(m, sink)
    es = [jnp.exp(s - m) for s in parts]
    den = es[0].sum(-1, keepdims=True)
    for e in es[1:]:
        den = den + e.sum(-1, keepdims=True)
    if sink is not None:
        den = den + jnp.exp(sink - m)
    inv = 1.0 / den
    return [(e * inv).astype(BF16) for e in es]


def _attn_kernel(*refs, t, tq, n_cache, layer):
    latent = n_cache > 0
    s_len = t + n_cache
    if latent:
        (fq_ref, fkv_ref, cckv_ref, ckpe_ref, cwk_ref, cwv_ref, cgk_ref, cgv_ref,
         wka_ref, wuv_ref, sink_ref, o_ref, ckpe_s, ka_s, va_s, kd_s, vd_s, kcc_s, vcc_s) = refs
    else:
        (fq_ref, fkv_ref, wka_ref, wuv_ref, sink_ref, o_ref, ckpe_s, ka_s, va_s, kd_s, vd_s) = refs

    qi = pl.program_id(1)

    @pl.when(qi == 0)
    def _():
        ckpe_s[0:t, :] = fkv_ref[0, :, 0:256]
        kd_s[0:t, :] = fkv_ref[0, :, 512:640]
        vd_s[0:t, :] = fkv_ref[0, :, 640:768]
        if latent:
            ckpe_s[t:s_len, 0:128] = cckv_ref[0, 0].astype(BF16)
            ckpe_s[t:s_len, 128:256] = ckpe_ref[0, 0].astype(BF16)
            kd_s[t:s_len, :] = cgk_ref[0, 0].astype(BF16)
            vd_s[t:s_len, :] = cgv_ref[0, 0].astype(BF16)
            kcc_s[...] = cwk_ref[0, 0].astype(BF16)
            vcc_s[...] = cwv_ref[0, 0].astype(BF16)
        ck = ckpe_s[...]
        va_s[...] = _dot(ck, wuv_ref[0]).astype(BF16)
        for h in range(N_HEADS):
            ka_s[h] = _dot(ck, wka_ref[0, h]).astype(BF16)

    lane = lax.broadcasted_iota(jnp.int32, (tq, LANES), 1)
    half = [lane < HEAD_DIM, lane >= HEAD_DIM]

    def keep(x, j):
        return jnp.where(half[j], x, 0.0)

    def stacked_q(col, j):
        blks = [fq_ref[0, :, col + g * LANES:col + (g + 1) * LANES] for g in range(2)]
        return jnp.concatenate([keep(b.astype(F32), j).astype(BF16) for b in blks], axis=0)

    if latent:
        span = tq + 2 * WINDOW
        q0 = qi * tq
        start = pl.multiple_of(jnp.clip(q0 - WINDOW, 0, t - span), LANES)
        row2 = lax.broadcasted_iota(jnp.int32, (2 * tq, span), 0)
        qpos = q0 + jnp.where(row2 >= tq, row2 - tq, row2)
        kpos = start + lax.broadcasted_iota(jnp.int32, (2 * tq, span), 1)
        valid = jnp.abs(qpos - kpos) <= WINDOW
    row1 = lax.broadcasted_iota(jnp.int32, (2 * tq, 1), 0)

    acc_a = [jnp.zeros((tq, LANES), F32) for _ in range(2)]
    acc_c = [jnp.zeros((tq, LANES), F32) for _ in range(2)]
    acc_d = [jnp.zeros((tq, LANES), F32) for _ in range(2)]

    def a_scores(h):
        return [_dot_nt(fq_ref[0, :, h * LANES:(h + 1) * LANES], ka_s[h])]

    def a_finish(h, ps):
        blk = h // 2
        pv = _dot(ps[0], va_s[:, blk * LANES:(blk + 1) * LANES])
        acc_a[blk] = acc_a[blk] + keep(pv, h % 2)

    def c_scores(j):
        qs = stacked_q(512, j)
        if latent:
            s_loc = _dot_nt(qs, fkv_ref[0, pl.ds(start, span), 256:384])
            return [jnp.where(valid, s_loc, NEG_INF), _dot_nt(qs, kcc_s[...])]
        return [_dot_nt(qs, fkv_ref[0, :, 256:384])]

    def c_finish(j, ps):
        if latent:
            pv = _dot(ps[0], fkv_ref[0, pl.ds(start, span), 384:512]) + _dot(ps[1], vcc_s[...])
        else:
            pv = _dot(ps[0], fkv_ref[0, :, 384:512])
        for g in range(2):
            acc_c[g] = acc_c[g] + keep(pv[g * tq:(g + 1) * tq], j)

    def d_scores(j):
        return [_dot_nt(stacked_q(768, j), kd_s[...])]

    def d_finish(j, ps):
        pv = _dot(ps[0], vd_s[...])
        for g in range(2):
            acc_d[g] = acc_d[g] + keep(pv[g * tq:(g + 1) * tq], j)

    def c_sink(j):
        return jnp.where(row1 < tq, sink_ref[layer, 2 * j], sink_ref[layer, 2 * j + 1])

    jobs = [(functools.partial(a_scores, h), functools.partial(a_finish, h), None) for h in range(N_HEADS)]
    jobs += [(functools.partial(d_scores, j), functools.partial(d_finish, j), None) for j in range(2)]
    jobs += [(functools.partial(c_scores, j), functools.partial(c_finish, j), functools.partial(c_sink, j))
             for j in range(2)]
    scores = jobs[0][0]()
    for i, (_, finish, sink) in enumerate(jobs):
        nxt = jobs[i + 1][0]() if i + 1 < len(jobs) else None
        finish(_softmax_parts(scores, None if sink is None else sink()))
        scores = nxt

    for g in range(2):
        o_ref[0, :, g * LANES:(g + 1) * LANES] = acc_a[g].astype(BF16)
        o_ref[0, :, 256 + g * LANES:256 + (g + 1) * LANES] = acc_c[g].astype(BF16)
        o_ref[0, :, 512 + g * LANES:512 + (g + 1) * LANES] = acc_d[g].astype(BF16)


def _attn(fq, fkv, caches, wts, layer, tq):
    b, t, _ = fq.shape
    latent = caches is not None
    n_cache = caches[0].shape[2] if latent else 0
    s_len = t + n_cache
    in_specs = [pl.BlockSpec((1, tq, FQ_W), lambda i, q: (i, q, 0)),
                pl.BlockSpec((1, t, FKV_W), lambda i, q: (i, 0, 0))]
    args = [fq, fkv]
    if latent:
        in_specs += [pl.BlockSpec((1, 1, n_cache, LANES), lambda i, q: (i, layer, 0, 0))] * 6
        args += list(caches)
    in_specs += [pl.BlockSpec((1, N_HEADS, 256, 128), lambda i, q: (layer, 0, 0, 0)),
                 pl.BlockSpec((1, 256, 256), lambda i, q: (layer, 0, 0)),
                 pl.BlockSpec(memory_space=pltpu.SMEM)]
    args += [wts["wka"], wts["wuv"], wts["sink"]]
    scratch = [pltpu.VMEM((s_len, 256), BF16), pltpu.VMEM((N_HEADS, s_len, 128), BF16),
               pltpu.VMEM((s_len, 256), BF16), pltpu.VMEM((s_len, 128), BF16), pltpu.VMEM((s_len, 128), BF16)]
    if latent:
        scratch += [pltpu.VMEM((n_cache, 128), BF16), pltpu.VMEM((n_cache, 128), BF16)]
    return pl.pallas_call(
        functools.partial(_attn_kernel, t=t, tq=tq, n_cache=n_cache, layer=layer),
        grid=(b, t // tq), in_specs=in_specs,
        out_specs=pl.BlockSpec((1, tq, 768), lambda i, q: (i, q, 0)),
        out_shape=jax.ShapeDtypeStruct((b, t, 768), BF16),
        scratch_shapes=scratch,
        compiler_params=pltpu.CompilerParams(dimension_semantics=("arbitrary", "arbitrary"),
                                             vmem_limit_bytes=VMEM_LIMIT),
        name="attn_latent" if latent else "attn_ctx",
    )(*args)


def _ret_kernel(*refs, t, bb, s0_stride, layer, want_state):
    if want_state:
        (fr_ref, s0f_ref, s0b_ref, decf_ref, decb_ref, gain_ref, bd_ref, o_ref, sf_ref, sb_ref,
         dm_s, dec_s, cd_s, kvf_s, kvb_s, stf_s, stb_s, sf_s, sb_s) = refs
    else:
        (fr_ref, s0f_ref, s0b_ref, decf_ref, decb_ref, gain_ref, bd_ref, o_ref,
         dm_s, dec_s, cd_s, kvf_s, kvb_s, stf_s, stb_s, sf_s, sb_s) = refs
    nc = t // CHUNK
    w = BRANCH_W
    lane_w = lax.broadcasted_iota(jnp.int32, (CHUNK, w), 1) // HEAD_DIM

    @pl.when(pl.program_id(0) == 0)
    def _():
        row_w = lax.broadcasted_iota(jnp.int32, (CHUNK, w), 0).astype(F32)

        def lane_decay(dec_ref):
            v = jnp.zeros((CHUNK, w), F32)
            for h in range(N_HEADS):
                v = jnp.where(lane_w == h, dec_ref[layer, h], v)
            return jax.nn.log_sigmoid(v)

        lgf = lane_decay(decf_ref)
        lgb = lane_decay(decb_ref)
        dec_s[0] = jnp.exp((row_w + 1.0) * lgf)
        dec_s[1] = jnp.exp((CHUNK - 1.0 - row_w) * lgf)
        dec_s[2] = jnp.exp((CHUNK - row_w) * lgb)
        dec_s[3] = jnp.exp(row_w * lgb)
        cd_s[0] = jnp.concatenate([jnp.exp(CHUNK * lgf)] * (w // CHUNK), axis=0)
        cd_s[1] = jnp.concatenate([jnp.exp(CHUNK * lgb)] * (w // CHUNK), axis=0)
        ii = lax.broadcasted_iota(jnp.int32, (CHUNK, CHUNK), 0).astype(F32)
        jj = lax.broadcasted_iota(jnp.int32, (CHUNK, CHUNK), 1).astype(F32)
        diff = ii - jj
        for h in range(N_HEADS):
            lf = jax.nn.log_sigmoid(jnp.full((CHUNK, CHUNK), decf_ref[layer, h], F32))
            lb = jax.nn.log_sigmoid(jnp.full((CHUNK, CHUNK), decb_ref[layer, h], F32))
            d_f = jnp.where(diff >= 0, jnp.exp(jnp.maximum(diff, 0.0) * lf), 0.0)
            d_b = jnp.where(diff < 0, jnp.exp(jnp.maximum(-diff, 0.0) * lb), 0.0)
            dm_s[h] = d_f + d_b

    r2 = lax.broadcasted_iota(jnp.int32, (w, w), 0) // HEAD_DIM
    c2 = lax.broadcasted_iota(jnp.int32, (w, w), 1) // HEAD_DIM
    diag = r2 == c2

    jobs = [(bi, n) for bi in range(bb) for n in range(nc)]
    groups = [jobs[i:i + RET_GROUP] for i in range(0, len(jobs), RET_GROUP)]
    rows = lambda n: slice(n * CHUNK, (n + 1) * CHUNK)
    slot = lambda bi, n: bi * nc + n

    for grp in groups:
        kvs = []
        for bi, n in grp:
            k = fr_ref[bi, rows(n), 256:512]
            vb = fr_ref[bi, rows(n), 512:768].astype(BF16)
            kk = jnp.concatenate([k * dec_s[1], k * dec_s[3]], axis=1).astype(BF16)
            kvs.append(_dot_tn(kk, vb))
        for (bi, n), kv in zip(grp, kvs):
            kvf_s[slot(bi, n)] = jnp.where(diag, kv[0:w], 0.0)
            kvb_s[slot(bi, n)] = jnp.where(diag, kv[w:2 * w], 0.0)

    for bi in range(bb):
        sf_s[bi] = s0f_ref[bi * s0_stride]
        sb_s[bi] = s0b_ref[bi * s0_stride]
    for i in range(nc):
        m = nc - 1 - i
        for bi in range(bb):
            sf = sf_s[bi]
            stf_s[slot(bi, i)] = sf.astype(BF16)
            sf_s[bi] = sf * cd_s[0] + kvf_s[slot(bi, i)]
            sb = sb_s[bi]
            stb_s[slot(bi, m)] = sb.astype(BF16)
            sb_s[bi] = sb * cd_s[1] + kvb_s[slot(bi, m)]
    if want_state:
        for bi in range(bb):
            for h in range(N_HEADS):
                sl = slice(h * HEAD_DIM, (h + 1) * HEAD_DIM)
                sf_ref[bi, h] = sf_s[bi, sl, sl]
                sb_ref[bi, h] = sb_s[bi, sl, sl]

    gain = gain_ref[0]
    bd2 = jnp.concatenate([bd_ref[...], bd_ref[...]], axis=0)

    def seg_mean(xs):
        cat = []
        for x in xs:
            hi = x.astype(BF16)
            cat.append(jnp.concatenate([hi, (x - hi.astype(F32)).astype(BF16)], axis=1))
        return [_dot(c, bd2) * (1.0 / HEAD_DIM) for c in cat]

    for grp in groups:
        qs = [fr_ref[bi, rows(n), 0:256] for bi, n in grp]
        vs = [fr_ref[bi, rows(n), 512:768] for bi, n in grp]
        qks, inter_f, inter_b = [], [], []
        for (bi, n), q in zip(grp, qs):
            kb = fr_ref[bi, rows(n), 256:512].astype(BF16)
            qstack = jnp.concatenate([jnp.where(lane_w == h, q, 0.0) for h in range(N_HEADS)], axis=0)
            qks.append(_dot_nt(qstack.astype(BF16), kb))
            qb = q.astype(BF16)
            inter_f.append(_dot(qb, stf_s[slot(bi, n)]))
            inter_b.append(_dot(qb, stb_s[slot(bi, n)]))
        outs = []
        for qk, v, i_f, i_b in zip(qks, vs, inter_f, inter_b):
            att = jnp.concatenate([qk[rows(h)] * dm_s[h] for h in range(N_HEADS)], axis=1)
            vstack = jnp.concatenate([jnp.where(lane_w == h, v, 0.0) for h in range(N_HEADS)], axis=0)
            outs.append(_dot(att.astype(BF16), vstack.astype(BF16)) + i_f * dec_s[0] + i_b * dec_s[2])
        ds = [o - mu for o, mu in zip(outs, seg_mean(outs))]
        var = seg_mean([d * d for d in ds])
        for (bi, n), d, vr in zip(grp, ds, var):
            gate = fr_ref[bi, rows(n), 768:1024]
            y = d * lax.rsqrt(vr + NORM_EPS) * gain
            o_ref[bi, rows(n), :] = (gate * jax.nn.sigmoid(gate) * y).astype(BF16)


def _ret(fr, s0f, s0b, wts, layer, want_state):
    b, t, _ = fr.shape
    nc = t // CHUNK
    bb = max(1, RET_CHUNKS_PER_STEP // nc)
    shared = s0f.shape[0] != b
    s_spec = (pl.BlockSpec((1, BRANCH_W, BRANCH_W), lambda i: (0, 0, 0)) if shared
              else pl.BlockSpec((bb, BRANCH_W, BRANCH_W), lambda i: (i, 0, 0)))
    smem = pl.BlockSpec(memory_space=pltpu.SMEM)
    out_specs = [pl.BlockSpec((bb, t, BRANCH_W), lambda i: (i, 0, 0))]
    out_shape = [jax.ShapeDtypeStruct((b, t, BRANCH_W), BF16)]
    if want_state:
        st_spec = pl.BlockSpec((bb, N_HEADS, HEAD_DIM, HEAD_DIM), lambda i: (i, 0, 0, 0))
        out_specs += [st_spec, st_spec]
        out_shape += [jax.ShapeDtypeStruct((b, N_HEADS, HEAD_DIM, HEAD_DIM), F32)] * 2
    sq = (BRANCH_W, BRANCH_W)
    return pl.pallas_call(
        functools.partial(_ret_kernel, t=t, bb=bb, s0_stride=0 if shared else 1, layer=layer,
                          want_state=want_state),
        grid=(b // bb,),
        in_specs=[pl.BlockSpec((bb, t, 1024), lambda i: (i, 0, 0)), s_spec, s_spec, smem, smem,
                  pl.BlockSpec((1, 1, BRANCH_W), lambda i: (layer, 0, 0)),
                  pl.BlockSpec(sq, lambda i: (0, 0))],
        out_specs=out_specs, out_shape=out_shape,
        scratch_shapes=[pltpu.VMEM((N_HEADS, CHUNK, CHUNK), F32), pltpu.VMEM((4, CHUNK, BRANCH_W), F32),
                        pltpu.VMEM((2,) + sq, F32),
                        pltpu.VMEM((bb * nc,) + sq, F32), pltpu.VMEM((bb * nc,) + sq, F32),
                        pltpu.VMEM((bb * nc,) + sq, BF16), pltpu.VMEM((bb * nc,) + sq, BF16),
                        pltpu.VMEM((bb,) + sq, F32), pltpu.VMEM((bb,) + sq, F32)],
        compiler_params=pltpu.CompilerParams(dimension_semantics=("arbitrary",),
                                             vmem_limit_bytes=VMEM_LIMIT),
        name="retention",
    )(fr, s0f, s0b, wts["decf"], wts["decb"], wts["gn"], wts["bd"])


def _merge_kernel(x_ref, mod_ref, oatt_ref, oret_ref, wg_ref, wb_ref, wo_ref, g_ref, b_ref, o_ref):
    x = x_ref[...]
    sh1 = mod_ref[0, 0:1, :]
    sc1 = mod_ref[0, 1:2, :]
    g1 = mod_ref[0, 2:3, :]
    h = (x * (1.0 + sc1) + sh1).astype(BF16)
    branches = (oatt_ref[:, 0:256], oret_ref[...], oatt_ref[:, 256:512], oatt_ref[:, 512:768])
    tsum = None
    for i, o in enumerate(branches):
        gate = jax.nn.sigmoid(_dot(h, wg_ref[0, :, i * D_MODEL:(i + 1) * D_MODEL]))
        term = gate * _dot(o, wb_ref[0, i])
        tsum = term if tsum is None else tsum + term
    y = _dot(tsum.astype(BF16), wo_ref[0])
    o_ref[...] = _layernorm(ALPHA * x + g1 * y, g_ref[0], b_ref[0])


def _merge(x, mod, oatt, oret, wts, layer, tm, group_len):
    nt = x.shape[0]
    tok = lambda w: pl.BlockSpec((tm, w), lambda t: (t, 0))
    return pl.pallas_call(
        _merge_kernel,
        grid=(nt // tm,),
        in_specs=[tok(D_MODEL), _mod_spec(tm, group_len), tok(768), tok(BRANCH_W),
                  _layer_spec(layer, D_MODEL, 4 * D_MODEL), _layer_spec(layer, 4, BRANCH_W, D_MODEL),
                  _layer_spec(layer, D_MODEL, D_MODEL), _layer_spec(layer, 1, D_MODEL),
                  _layer_spec(layer, 1, D_MODEL)],
        out_specs=tok(D_MODEL),
        out_shape=jax.ShapeDtypeStruct((nt, D_MODEL), F32),
        compiler_params=pltpu.CompilerParams(dimension_semantics=("arbitrary",), vmem_limit_bytes=VMEM_LIMIT),
        name="merge",
    )(x, mod, oatt, oret, wts["wg"], wts["wb"], wts["wo"], wts["ln1g"], wts["ln1b"])


def _mlp_kernel(x_ref, mod_ref, wup_ref, wdn_ref, g_ref, b_ref, o_ref):
    x = x_ref[...]
    sh2 = mod_ref[0, 3:4, :]
    sc2 = mod_ref[0, 4:5, :]
    g2 = mod_ref[0, 5:6, :]
    h = (x * (1.0 + sc2) + sh2).astype(BF16)
    u = jnp.maximum(_dot(h, wup_ref[0]), 0.0)
    f = _dot((u * u).astype(BF16), wdn_ref[0])
    o_ref[...] = _layernorm(ALPHA * x + g2 * f, g_ref[0], b_ref[0])


def _mlp(x, mod, wts, layer, tm, group_len):
    nt = x.shape[0]
    tok = lambda w: pl.BlockSpec((tm, w), lambda t: (t, 0))
    return pl.pallas_call(
        _mlp_kernel,
        grid=(nt // tm,),
        in_specs=[tok(D_MODEL), _mod_spec(tm, group_len), _layer_spec(layer, D_MODEL, D_FF),
                  _layer_spec(layer, D_FF, D_MODEL), _layer_spec(layer, 1, D_MODEL), _layer_spec(layer, 1, D_MODEL)],
        out_specs=tok(D_MODEL),
        out_shape=jax.ShapeDtypeStruct((nt, D_MODEL), F32),
        compiler_params=pltpu.CompilerParams(dimension_semantics=("arbitrary",), vmem_limit_bytes=VMEM_LIMIT),
        name="mlp",
    )(x, mod, wts["wup"], wts["wdn"], wts["ln2g"], wts["ln2b"])


def _swap_heads(wcols):
    lead = wcols.shape[:-1]
    return jnp.swapaxes(wcols.reshape(*lead, 2, 2, HEAD_DIM), -3, -2).reshape(*lead, 4 * HEAD_DIM)


def _prep_weights(w_in, mla_q_norm, mla_w_uq, mla_kv_norm, mla_w_uk, mla_w_uv, ret_decay_fwd, ret_decay_bwd,
                  ret_gn_gain, win_sink, gqa_q_norm, gqa_k_norm, w_branch, w_o, ln1_g, ln1_b, w_up, w_down,
                  ln2_g, ln2_b):
    o_ret = 256 + 128 + MLA_ROPE
    o_wq = o_ret + 1024
    o_wk = o_wq + 256
    o_gq = o_wk + 256
    o_gk = o_gq + 256
    o_gate = o_gk + 256
    wmix = jnp.concatenate([
        w_in[..., :o_ret], jnp.zeros((DEPTH, D_MODEL, LANES - MLA_ROPE), F32), w_in[..., o_ret:o_wq],
        _swap_heads(w_in[..., o_wq:o_wk]), w_in[..., o_wk:o_gq],
        _swap_heads(w_in[..., o_gq:o_gk]), w_in[..., o_gk:o_gate]], axis=-1).astype(BF16)
    wuq = jnp.pad(mla_w_uq.reshape(DEPTH, 256, N_HEADS, HEAD_DIM + MLA_ROPE),
                  ((0, 0), (0, 0), (0, 0), (0, LANES - HEAD_DIM - MLA_ROPE))
                  ).reshape(DEPTH, 256, N_HEADS * LANES).astype(BF16)
    uk = mla_w_uk.reshape(DEPTH, MLA_KV_RANK, N_HEADS, HEAD_DIM).transpose(0, 2, 1, 3)
    top = jnp.pad(uk, ((0, 0), (0, 0), (0, 0), (0, LANES - HEAD_DIM)))
    eye = np.zeros((LANES, LANES), np.float32)
    eye[np.arange(MLA_ROPE), HEAD_DIM + np.arange(MLA_ROPE)] = 1.0
    wka = jnp.concatenate([top, jnp.broadcast_to(eye, (DEPTH, N_HEADS, LANES, LANES))], axis=2).astype(BF16)
    wuv = jnp.pad(mla_w_uv, ((0, 0), (0, 256 - MLA_KV_RANK), (0, 0))).astype(BF16)
    bd = (np.arange(BRANCH_W)[:, None] // HEAD_DIM == np.arange(BRANCH_W)[None, :] // HEAD_DIM).astype(np.float32)
    wb_cd = jnp.swapaxes(w_branch[:, 2:].reshape(DEPTH, 2, 2, 2, HEAD_DIM, D_MODEL), 2, 3
                         ).reshape(DEPTH, 2, BRANCH_W, D_MODEL)
    w_branch = jnp.concatenate([w_branch[:, :2], wb_cd], axis=1)
    row = lambda a: a[:, None, :]
    return dict(
        wmix=wmix, wg=w_in[..., o_gate:].astype(BF16), wuq=wuq, wka=wka, wuv=wuv, bd=jnp.asarray(bd, BF16),
        qn=row(mla_q_norm), kvn=row(mla_kv_norm),
        gqn=row(jnp.tile(gqa_q_norm, (1, N_HEADS))), gkn=row(jnp.tile(gqa_k_norm, (1, 2))),
        decf=ret_decay_fwd, decb=ret_decay_bwd, gn=row(ret_gn_gain), sink=win_sink,
        wb=w_branch.astype(BF16), wo=w_o.astype(BF16), ln1g=row(ln1_g), ln1b=row(ln1_b),
        wup=w_up.astype(BF16), wdn=w_down.astype(BF16), ln2g=row(ln2_g), ln2b=row(ln2_b))


def _axial_tables(t, rot_dim):
    rows = t // GRID_W
    row = np.repeat(np.arange(rows, dtype=np.float32), GRID_W)
    col = (np.arange(t) % GRID_W).astype(np.float32)
    n_freq = rot_dim // 4
    inv = np.power(np.float32(ROPE_BASE), -np.arange(n_freq, dtype=np.float32) / np.float32(n_freq))
    ang = np.concatenate([row[:, None] * inv, col[:, None] * inv], axis=-1).astype(np.float32)
    return np.cos(ang), np.sin(ang)


def _rope_tables(t):
    ca, sa = _axial_tables(t, MLA_ROPE)
    ch, sh = _axial_tables(t, HEAD_DIM)
    one = lambda n: np.ones((t, n), np.float32)
    zero = lambda n: np.zeros((t, n), np.float32)
    cq = np.concatenate([one(HEAD_DIM), ca, ca, one(32)], axis=1)
    sq = np.concatenate([zero(HEAD_DIM), -sa, sa, zero(32)], axis=1)
    ck = np.concatenate([ca, ca, one(96)], axis=1)
    sk = np.concatenate([-sa, sa, zero(96)], axis=1)
    chh = np.concatenate([ch, ch, ch, ch], axis=1)
    shh = np.concatenate([-sh, sh, -sh, sh], axis=1)
    return tuple(jnp.asarray(a, F32) for a in (cq, sq, ck, sk, chh, shh))


def _block_diag(s):
    b = s.shape[0]
    same = np.eye(N_HEADS, dtype=bool)[None, :, None, :, None]
    return jnp.where(same, s[:, :, :, None, :], 0.0).reshape(b, BRANCH_W, BRANCH_W)


def kernel(x_prompt, x_sample, cache_mla_ckv, cache_mla_kpe, cache_win_k, cache_win_v, cache_gqa_k, cache_gqa_v,
           state_ret_fwd, state_ret_bwd, c, c_ctx, w_ada, b_ada, w_in, mla_q_norm, mla_w_uq, mla_kv_norm, mla_w_uk,
           mla_w_uv, ret_decay_fwd, ret_decay_bwd, ret_gn_gain, win_sink, gqa_q_norm, gqa_k_norm, w_branch, w_o,
           ln1_g, ln1_b, w_up, w_down, ln2_g, ln2_b):
    batch, seq, _ = x_prompt.shape
    dec_b, dec_t, _ = x_sample.shape
    past = cache_mla_ckv.shape[2]

    cond8 = jnp.concatenate([c_ctx[None], c, jnp.zeros((8 - 1 - dec_b, D_MODEL), F32)], axis=0)
    mod = _ada(cond8, w_ada, b_ada).reshape(DEPTH, 8, 6, D_MODEL)
    rope = _rope_tables(dec_t)
    caches = (cache_mla_ckv,
              jnp.pad(cache_mla_kpe, ((0, 0), (0, 0), (0, 0), (0, LANES - MLA_ROPE))),
              cache_win_k.reshape(dec_b, DEPTH, past, LANES), cache_win_v.reshape(dec_b, DEPTH, past, LANES),
              cache_gqa_k.reshape(dec_b, DEPTH, past, LANES), cache_gqa_v.reshape(dec_b, DEPTH, past, LANES))

    wts = _prep_weights(w_in, mla_q_norm, mla_w_uq, mla_kv_norm, mla_w_uk, mla_w_uv, ret_decay_fwd, ret_decay_bwd,
                        ret_gn_gain, win_sink, gqa_q_norm, gqa_k_norm, w_branch, w_o, ln1_g, ln1_b, w_up, w_down,
                        ln2_g, ln2_b)
    zero_state = jnp.zeros((1, BRANCH_W, BRANCH_W), F32)
    xp = x_prompt.reshape(batch * seq, D_MODEL)
    xs = x_sample.reshape(dec_b * dec_t, D_MODEL)
    per_b = lambda a: a.reshape(batch, seq, a.shape[-1])
    per_d = lambda a: a.reshape(dec_b, dec_t, a.shape[-1])
    flat = lambda a: a.reshape(-1, a.shape[-1])
    ctx_out = []
    for l in range(DEPTH):
        fq, fkv, fr, ockv, okpe, owk, owv, ogk, ogv = _proj(xp, mod[l], wts, l, None, TM_PROJ, None, seq)
        oatt = _attn(per_b(fq), per_b(fkv), None, wts, l, seq)
        oret, s_f, s_b = _ret(per_b(fr), zero_state, zero_state, wts, l, True)
        x1 = _merge(xp, mod[l], flat(oatt), flat(oret), wts, l, TM_MERGE, None)
        xp = _mlp(x1, mod[l], wts, l, TM_MLP, None)
        heads = lambda a: a.reshape(batch, 2, HEAD_DIM, seq)
        ctx_out.append((ockv, okpe, heads(owk), heads(owv), heads(ogk), heads(ogv), s_f, s_b))
        fq, fkv, fr = _proj(xs, mod[l], wts, l, rope, TM_PROJ, dec_t)
        oatt = _attn(per_d(fq), per_d(fkv), caches, wts, l, TQ_LATENT)
        (oret,) = _ret(per_d(fr), _block_diag(state_ret_fwd[:, l]), _block_diag(state_ret_bwd[:, l]), wts, l, False)
        x1 = _merge(xs, mod[l], flat(oatt), flat(oret), wts, l, TM_MERGE, dec_t)
        xs = _mlp(x1, mod[l], wts, l, TM_MLP, dec_t)

    stacked = [jnp.stack([ctx_out[l][i] for l in range(DEPTH)], axis=1) for i in range(8)]
    stacked[1] = jnp.swapaxes(stacked[1], 2, 3)
    for i in range(2, 6):
        stacked[i] = jnp.transpose(stacked[i], (0, 1, 4, 2, 3))
    return (per_b(xp), per_d(xs), *stacked)
```

```python
import functools

import jax
import jax.numpy as jnp
import numpy as np
from jax import lax
from jax.experimental import pallas as pl
from jax.experimental.pallas import tpu as pltpu

D_MODEL = 1024
DEPTH = 2
GRID_W = 64
CHUNK = 128
WINDOW = 128
ROPE_BASE = 10000.0
NORM_EPS = 1e-6
NEG_INF = -1e30
HEAD_DIM = 64
N_HEADS = 4
BRANCH_W = 256
MLA_ROPE = 32
MLA_KV_RANK = 128
MLA_SCALE = (HEAD_DIM + MLA_ROPE) ** -0.5
ATT_SCALE = HEAD_DIM ** -0.5
D_FF = 4 * D_MODEL
ALPHA = (2.0 * DEPTH) ** 0.25
LANES = 128

P_QLAT, P_KVLAT, P_KPE, P_RET, P_WQ, P_WK, P_WV, P_GQ, P_GK, P_GV, P_END = (
    0, 256, 384, 512, 1536, 1792, 1920, 2048, 2304, 2432, 2560)
FQ_W = 1024
FKV_W = 768
VMEM_LIMIT = 56 * 1024 * 1024
TM_PROJ = 512
TM_MERGE = 512
TM_MLP = 512
SUB_ROWS = 256
SUB_ROWS_LATENT_PROJ = 256
TQ_LATENT = 256
RET_GROUP = 4
RET_CHUNKS_PER_STEP = 8

F32 = jnp.float32
BF16 = jnp.bfloat16


def _dot(a, b):
    return jnp.dot(a, b, preferred_element_type=F32)


def _dot_nt(a, b):
    return lax.dot_general(a, b, (((1,), (1,)), ((), ())), preferred_element_type=F32)


def _dot_tn(a, b):
    return lax.dot_general(a, b, (((0,), (0,)), ((), ())), preferred_element_type=F32)


def _sub_tiles(tm, sub_rows=SUB_ROWS):
    n = max(1, tm // sub_rows)
    step = tm // n
    return [slice(i * step, (i + 1) * step) for i in range(n)]


def _layernorm(x, g, b):
    mu = jnp.mean(x, -1, keepdims=True)
    d = x - mu
    var = jnp.mean(d * d, -1, keepdims=True)
    return d * lax.rsqrt(var + NORM_EPS) * g + b


def _rmsnorm(x, g):
    return x * lax.rsqrt(jnp.mean(x * x, -1, keepdims=True) + NORM_EPS) * g


def _seg_sum(x, ones_bd):
    hi = x.astype(BF16)
    lo = (x - hi.astype(F32)).astype(BF16)
    return _dot(hi, ones_bd) + _dot(lo, ones_bd)


def _rope_block(x, cos, sin, half, first):
    rot = jnp.where(first, pltpu.roll(x, LANES - half, 1), pltpu.roll(x, half, 1))
    return x * cos + rot * sin


def _ada_kernel(cond_ref, w_ref, b_ref, o_ref):
    cnd = cond_ref[...]
    s = (cnd * jax.nn.sigmoid(cnd)).astype(BF16)
    o_ref[0] = _dot(s, w_ref[0].astype(BF16)) + b_ref[0]


def _ada(cond8, w_ada, b_ada):
    tn = 1024
    n = w_ada.shape[-1]
    return pl.pallas_call(
        _ada_kernel,
        grid=(DEPTH, n // tn),
        in_specs=[pl.BlockSpec((8, D_MODEL), lambda l, j: (0, 0)),
                  pl.BlockSpec((1, D_MODEL, tn), lambda l, j: (l, 0, j)),
                  pl.BlockSpec((1, 1, tn), lambda l, j: (l, 0, j))],
        out_specs=pl.BlockSpec((1, 8, tn), lambda l, j: (l, 0, j)),
        out_shape=jax.ShapeDtypeStruct((DEPTH, 8, n), F32),
        compiler_params=pltpu.CompilerParams(dimension_semantics=("arbitrary", "arbitrary")),
        name="ada_mod",
    )(cond8, w_ada, b_ada.reshape(DEPTH, 1, n))


def _store_per_seq(o_ref, val, row0, transposed):
    seq = o_ref.shape[2] if transposed else o_ref.shape[1]
    for i in range(val.shape[0] // seq):
        blk = val[i * seq:(i + 1) * seq, :]
        o_ref[row0 // seq + i] = blk.T[0:o_ref.shape[1], :] if transposed else blk


def _proj_kernel(*refs, latent):
    if latent:
        (x_ref, mod_ref, wmix_ref, qn_ref, wuq_ref, kvn_ref, gqn_ref, gkn_ref, bd_ref,
         cq_ref, sq_ref, ck_ref, sk_ref, ch_ref, sh_ref,
         fq_ref, fkv_ref, fr_ref) = refs
    else:
        (x_ref, mod_ref, wmix_ref, qn_ref, wuq_ref, kvn_ref, gqn_ref, gkn_ref, bd_ref,
         fq_ref, fkv_ref, fr_ref, ockv_ref, okpe_ref, owk_ref, owv_ref, ogk_ref, ogv_ref) = refs

    sh1 = mod_ref[0, 0:1, :]
    sc1 = mod_ref[0, 1:2, :]
    subs = _sub_tiles(x_ref.shape[0], SUB_ROWS_LATENT_PROJ if latent else SUB_ROWS)
    ps = [_dot((x_ref[r, :] * (1.0 + sc1) + sh1).astype(BF16), wmix_ref[0]) for r in subs]
    for r, p in zip(subs, ps):
        _proj_post(r, p, refs, latent)


def _proj_post(r, p, refs, latent):
    if latent:
        (_, _, _, qn_ref, wuq_ref, kvn_ref, gqn_ref, gkn_ref, bd_ref,
         cq_ref, sq_ref, ck_ref, sk_ref, ch_ref, sh_ref, fq_ref, fkv_ref, fr_ref) = refs
    else:
        (_, _, _, qn_ref, wuq_ref, kvn_ref, gqn_ref, gkn_ref, bd_ref,
         fq_ref, fkv_ref, fr_ref, ockv_ref, okpe_ref, owk_ref, owv_ref, ogk_ref, ogv_ref) = refs
    rows = p.shape[0]
    lane = lax.broadcasted_iota(jnp.int32, (rows, LANES), 1)
    first_head = (lane % HEAD_DIM) < (HEAD_DIM // 2)

    def rope_heads(v):
        if not latent:
            return v
        cos, sin = ch_ref[r, :], sh_ref[r, :]
        blocks = [_rope_block(v[:, j:j + LANES], cos, sin, HEAD_DIM // 2, first_head)
                  for j in range(0, v.shape[1], LANES)]
        return blocks[0] if len(blocks) == 1 else jnp.concatenate(blocks, axis=1)

    qn = _rmsnorm(p[:, P_QLAT:P_KVLAT], qn_ref[0]).astype(BF16)
    qa = _dot(qn, wuq_ref[0])
    if latent:
        cos, sin = cq_ref[r, :], sq_ref[r, :]
        first = (lane >= HEAD_DIM) & (lane < HEAD_DIM + MLA_ROPE // 2)
        qa = jnp.concatenate(
            [_rope_block(qa[:, j:j + LANES], cos, sin, MLA_ROPE // 2, first) for j in range(0, 512, LANES)],
            axis=1)
    fq_ref[r, 0:512] = (qa * MLA_SCALE).astype(BF16)

    ckv = _rmsnorm(p[:, P_KVLAT:P_KPE], kvn_ref[0])
    kpe = p[:, P_KPE:P_RET]
    if latent:
        first = lane < MLA_ROPE // 2
        kpe_r = _rope_block(kpe, ck_ref[r, :], sk_ref[r, :], MLA_ROPE // 2, first)
    else:
        kpe_r = kpe
        _store_per_seq(ockv_ref, ckv, r.start, False)
        _store_per_seq(okpe_ref, kpe, r.start, True)
    fkv_ref[r, 0:128] = ckv.astype(BF16)
    fkv_ref[r, 128:256] = kpe_r.astype(BF16)

    fr_ref[r, 0:256] = p[:, P_RET:P_RET + 256]
    fr_ref[r, 256:512] = p[:, P_RET + 256:P_RET + 512] * (HEAD_DIM ** -0.5)
    fr_ref[r, 512:1024] = p[:, P_RET + 512:P_WQ]

    fq_ref[r, 512:768] = (rope_heads(p[:, P_WQ:P_WK]) * ATT_SCALE).astype(BF16)
    wk = p[:, P_WK:P_WV]
    wv = p[:, P_WV:P_GQ]
    fkv_ref[r, 256:384] = rope_heads(wk).astype(BF16)
    fkv_ref[r, 384:512] = wv.astype(BF16)

    bd = bd_ref[...]
    gq = p[:, P_GQ:P_GK]
    gqn = gq * lax.rsqrt(_seg_sum(gq * gq, bd) * (1.0 / HEAD_DIM) + NORM_EPS) * gqn_ref[0]
    fq_ref[r, 768:1024] = (rope_heads(gqn) * ATT_SCALE).astype(BF16)
    gk = p[:, P_GK:P_GV]
    gkn = gk * lax.rsqrt(_seg_sum(gk * gk, bd[0:128, 0:128]) * (1.0 / HEAD_DIM) + NORM_EPS) * gkn_ref[0]
    gv = p[:, P_GV:P_END]
    fkv_ref[r, 512:640] = rope_heads(gkn).astype(BF16)
    fkv_ref[r, 640:768] = gv.astype(BF16)
    if not latent:
        _store_per_seq(owk_ref, wk, r.start, True)
        _store_per_seq(owv_ref, wv, r.start, True)
        _store_per_seq(ogk_ref, gkn, r.start, True)
        _store_per_seq(ogv_ref, gv, r.start, True)


def _layer_spec(layer, *s):
    return pl.BlockSpec((1,) + s, lambda t: (layer,) + (0,) * len(s), pipeline_mode=pl.Buffered(1))


def _mod_spec(tm, group_len):
    if group_len is None:
        return pl.BlockSpec((1, 6, D_MODEL), lambda t: (0, 0, 0))
    per = group_len // tm
    return pl.BlockSpec((1, 6, D_MODEL), lambda t: (1 + t // per, 0, 0))


def _proj(x, mod, wts, layer, rope, tm, group_len, ctx_seq=None):
    latent = group_len is not None
    nt = x.shape[0]
    tok = lambda w: pl.BlockSpec((tm, w), lambda t: (t, 0))
    in_specs = [tok(D_MODEL), _mod_spec(tm, group_len),
                _layer_spec(layer, D_MODEL, P_END), _layer_spec(layer, 1, 256), _layer_spec(layer, 256, 512),
                _layer_spec(layer, 1, 128), _layer_spec(layer, 1, 256), _layer_spec(layer, 1, 128),
                pl.BlockSpec((256, 256), lambda t: (0, 0))]
    args = [x, mod, wts["wmix"], wts["qn"], wts["wuq"], wts["kvn"], wts["gqn"], wts["gkn"], wts["bd"]]
    out_specs = [tok(FQ_W), tok(FKV_W), tok(1024)]
    out_shape = [jax.ShapeDtypeStruct((nt, FQ_W), BF16), jax.ShapeDtypeStruct((nt, FKV_W), BF16),
                 jax.ShapeDtypeStruct((nt, 1024), F32)]
    if latent:
        per = group_len // tm
        in_specs += [pl.BlockSpec((tm, LANES), lambda t: (t % per, 0))] * 6
        args += list(rope)
    else:
        nb = tm // ctx_seq
        seqs = nt // ctx_seq
        sub = _sub_tiles(tm)[0]
        assert (sub.stop - sub.start) % ctx_seq == 0, "a sub-tile must hold whole context sequences"
        out_specs += [pl.BlockSpec((nb, ctx_seq, 128), lambda t: (t, 0, 0))]
        out_shape += [jax.ShapeDtypeStruct((seqs, ctx_seq, 128), F32)]
        for r in (MLA_ROPE, 128, 128, 128, 128):
            out_specs.append(pl.BlockSpec((nb, r, ctx_seq), lambda t: (t, 0, 0)))
            out_shape.append(jax.ShapeDtypeStruct((seqs, r, ctx_seq), F32))
    return pl.pallas_call(
        functools.partial(_proj_kernel, latent=latent),
        grid=(nt // tm,), in_specs=in_specs, out_specs=out_specs, out_shape=out_shape,
        compiler_params=pltpu.CompilerParams(dimension_semantics=("arbitrary",), vmem_limit_bytes=VMEM_LIMIT),
        name="proj_latent" if latent else "proj_ctx",
    )(*args)


def _softmax_parts(parts, sink):
    m = parts[0].max(-1, keepdims=True)
    for s in parts[1:]:
        m = jnp.maximum(m, s.max(-1, keepdims=True))
    if sink is not None:
        m = jnp.maximum(m, sink)
    es = [jnp.exp(s - m) for s in parts]
    den = es[0].sum(-1, keepdims=True)
    for e in es[1:]:
        den = den + e.sum(-1, keepdims=True)
    if sink is not None:
        den = den + jnp.exp(sink - m)
    inv = 1.0 / den
    return [(e * inv).astype(BF16) for e in es]


def _attn_kernel(*refs, t, tq, n_cache, layer):
    latent = n_cache > 0
    s_len = t + n_cache
    if latent:
        (fq_ref, fkv_ref, cckv_ref, ckpe_ref, cwk_ref, cwv_ref, cgk_ref, cgv_ref,
         wka_ref, wuv_ref, sink_ref, o_ref, ckpe_s, ka_s, va_s, kd_s, vd_s, kcc_s, vcc_s) = refs
    else:
        (fq_ref, fkv_ref, wka_ref, wuv_ref, sink_ref, o_ref, ckpe_s, ka_s, va_s, kd_s, vd_s) = refs

    qi = pl.program_id(1)

    @pl.when(qi == 0)
    def _():
        ckpe_s[0:t, :] = fkv_ref[0, :, 0:256]
        kd_s[0:t, :] = fkv_ref[0, :, 512:640]
        vd_s[0:t, :] = fkv_ref[0, :, 640:768]
        if latent:
            ckpe_s[t:s_len, 0:128] = cckv_ref[0, 0].astype(BF16)
            ckpe_s[t:s_len, 128:256] = ckpe_ref[0, 0].astype(BF16)
            kd_s[t:s_len, :] = cgk_ref[0, 0].astype(BF16)
            vd_s[t:s_len, :] = cgv_ref[0, 0].astype(BF16)
            kcc_s[...] = cwk_ref[0, 0].astype(BF16)
            vcc_s[...] = cwv_ref[0, 0].astype(BF16)
        ck = ckpe_s[...]
        va_s[...] = _dot(ck, wuv_ref[0]).astype(BF16)
        for h in range(N_HEADS):
            ka_s[h] = _dot(ck, wka_ref[0, h]).astype(BF16)

    lane = lax.broadcasted_iota(jnp.int32, (tq, LANES), 1)
    half = [lane < HEAD_DIM, lane >= HEAD_DIM]

    def keep(x, j):
        return jnp.where(half[j], x, 0.0)

    def stacked_q(col, j):
        blks = [fq_ref[0, :, col + g * LANES:col + (g + 1) * LANES] for g in range(2)]
        return jnp.concatenate([keep(b.astype(F32), j).astype(BF16) for b in blks], axis=0)

    if latent:
        span = tq + 2 * WINDOW
        q0 = qi * tq
        start = pl.multiple_of(jnp.clip(q0 - WINDOW, 0, t - span), LANES)
        row2 = lax.broadcasted_iota(jnp.int32, (2 * tq, span), 0)
        qpos = q0 + jnp.where(row2 >= tq, row2 - tq, row2)
        kpos = start + lax.broadcasted_iota(jnp.int32, (2 * tq, span), 1)
        valid = jnp.abs(qpos - kpos) <= WINDOW
    row1 = lax.broadcasted_iota(jnp.int32, (2 * tq, 1), 0)

    acc_a = [jnp.zeros((tq, LANES), F32) for _ in range(2)]
    acc_c = [jnp.zeros((tq, LANES), F32) for _ in range(2)]
    acc_d = [jnp.zeros((tq, LANES), F32) for _ in range(2)]

    def a_scores(h):
        return [_dot_nt(fq_ref[0, :, h * LANES:(h + 1) * LANES], ka_s[h])]

    def a_finish(h, ps):
        blk = h // 2
        pv = _dot(ps[0], va_s[:, blk * LANES:(blk + 1) * LANES])
        acc_a[blk] = acc_a[blk] + keep(pv, h % 2)

    def c_scores(j):
        qs = stacked_q(512, j)
        if latent:
            s_loc = _dot_nt(qs, fkv_ref[0, pl.ds(start, span), 256:384])
            return [jnp.where(valid, s_loc, NEG_INF), _dot_nt(qs, kcc_s[...])]
        return [_dot_nt(qs, fkv_ref[0, :, 256:384])]

    def c_finish(j, ps):
        if latent:
            pv = _dot(ps[0], fkv_ref[0, pl.ds(start, span), 384:512]) + _dot(ps[1], vcc_s[...])
        else:
            pv = _dot(ps[0], fkv_ref[0, :, 384:512])
        for g in range(2):
            acc_c[g] = acc_c[g] + keep(pv[g * tq:(g + 1) * tq], j)

    def d_scores(j):
        return [_dot_nt(stacked_q(768, j), kd_s[...])]

    def d_finish(j, ps):
        pv = _dot(ps[0], vd_s[...])
        for g in range(2):
            acc_d[g] = acc_d[g] + keep(pv[g * tq:(g + 1) * tq], j)

    def c_sink(j):
        return jnp.where(row1 < tq, sink_ref[layer, 2 * j], sink_ref[layer, 2 * j + 1])

    jobs = [(functools.partial(a_scores, h), functools.partial(a_finish, h), None) for h in range(N_HEADS)]
    jobs += [(functools.partial(d_scores, j), functools.partial(d_finish, j), None) for j in range(2)]
    jobs += [(functools.partial(c_scores, j), functools.partial(c_finish, j), functools.partial(c_sink, j))
             for j in range(2)]
    scores = jobs[0][0]()
    for i, (_, finish, sink) in enumerate(jobs):
        nxt = jobs[i + 1][0]() if i + 1 < len(jobs) else None
        finish(_softmax_parts(scores, None if sink is None else sink()))
        scores = nxt

    for g in range(2):
        o_ref[0, :, g * LANES:(g + 1) * LANES] = acc_a[g].astype(BF16)
        o_ref[0, :, 256 + g * LANES:256 + (g + 1) * LANES] = acc_c[g].astype(BF16)
        o_ref[0, :, 512 + g * LANES:512 + (g + 1) * LANES] = acc_d[g].astype(BF16)


def _attn(fq, fkv, caches, wts, layer, tq):
    b, t, _ = fq.shape
    latent = caches is not None
    n_cache = caches[0].shape[2] if latent else 0
    s_len = t + n_cache
    in_specs = [pl.BlockSpec((1, tq, FQ_W), lambda i, q: (i, q, 0)),
                pl.BlockSpec((1, t, FKV_W), lambda i, q: (i, 0, 0))]
    args = [fq, fkv]
    if latent:
        in_specs += [pl.BlockSpec((1, 1, n_cache, LANES), lambda i, q: (i, layer, 0, 0))] * 6
        args += list(caches)
    in_specs += [pl.BlockSpec((1, N_HEADS, 256, 128), lambda i, q: (layer, 0, 0, 0)),
                 pl.BlockSpec((1, 256, 256), lambda i, q: (layer, 0, 0)),
                 pl.BlockSpec(memory_space=pltpu.SMEM)]
    args += [wts["wka"], wts["wuv"], wts["sink"]]
    scratch = [pltpu.VMEM((s_len, 256), BF16), pltpu.VMEM((N_HEADS, s_len, 128), BF16),
               pltpu.VMEM((s_len, 256), BF16), pltpu.VMEM((s_len, 128), BF16), pltpu.VMEM((s_len, 128), BF16)]
    if latent:
        scratch += [pltpu.VMEM((n_cache, 128), BF16), pltpu.VMEM((n_cache, 128), BF16)]
    return pl.pallas_call(
        functools.partial(_attn_kernel, t=t, tq=tq, n_cache=n_cache, layer=layer),
        grid=(b, t // tq), in_specs=in_specs,
        out_specs=pl.BlockSpec((1, tq, 768), lambda i, q: (i, q, 0)),
        out_shape=jax.ShapeDtypeStruct((b, t, 768), BF16),
        scratch_shapes=scratch,
        compiler_params=pltpu.CompilerParams(dimension_semantics=("arbitrary", "arbitrary"),
                                             vmem_limit_bytes=VMEM_LIMIT),
        name="attn_latent" if latent else "attn_ctx",
    )(*args)


def _ret_kernel(*refs, t, bb, s0_stride, layer, want_state):
    if want_state:
        (fr_ref, s0f_ref, s0b_ref, decf_ref, decb_ref, gain_ref, bd_ref, o_ref, sf_ref, sb_ref,
         dm_s, dec_s, cd_s, kvf_s, kvb_s, stf_s, stb_s, sf_s, sb_s) = refs
    else:
        (fr_ref, s0f_ref, s0b_ref, decf_ref, decb_ref, gain_ref, bd_ref, o_ref,
         dm_s, dec_s, cd_s, kvf_s, kvb_s, stf_s, stb_s, sf_s, sb_s) = refs
    nc = t // CHUNK
    w = BRANCH_W
    lane_w = lax.broadcasted_iota(jnp.int32, (CHUNK, w), 1) // HEAD_DIM

    @pl.when(pl.program_id(0) == 0)
    def _():
        row_w = lax.broadcasted_iota(jnp.int32, (CHUNK, w), 0).astype(F32)

        def lane_decay(dec_ref):
            v = jnp.zeros((CHUNK, w), F32)
            for h in range(N_HEADS):
                v = jnp.where(lane_w == h, dec_ref[layer, h], v)
            return jax.nn.log_sigmoid(v)

        lgf = lane_decay(decf_ref)
        lgb = lane_decay(decb_ref)
        dec_s[0] = jnp.exp((row_w + 1.0) * lgf)
        dec_s[1] = jnp.exp((CHUNK - 1.0 - row_w) * lgf)
        dec_s[2] = jnp.exp((CHUNK - row_w) * lgb)
        dec_s[3] = jnp.exp(row_w * lgb)
        cd_s[0] = jnp.concatenate([jnp.exp(CHUNK * lgf)] * (w // CHUNK), axis=0)
        cd_s[1] = jnp.concatenate([jnp.exp(CHUNK * lgb)] * (w // CHUNK), axis=0)
        ii = lax.broadcasted_iota(jnp.int32, (CHUNK, CHUNK), 0).astype(F32)
        jj = lax.broadcasted_iota(jnp.int32, (CHUNK, CHUNK), 1).astype(F32)
        diff = ii - jj
        for h in range(N_HEADS):
            lf = jax.nn.log_sigmoid(jnp.full((CHUNK, CHUNK), decf_ref[layer, h], F32))
            lb = jax.nn.log_sigmoid(jnp.full((CHUNK, CHUNK), decb_ref[layer, h], F32))
            d_f = jnp.where(diff >= 0, jnp.exp(jnp.maximum(diff, 0.0) * lf), 0.0)
            d_b = jnp.where(diff < 0, jnp.exp(jnp.maximum(-diff, 0.0) * lb), 0.0)
            dm_s[h] = d_f + d_b

    r2 = lax.broadcasted_iota(jnp.int32, (w, w), 0) // HEAD_DIM
    c2 = lax.broadcasted_iota(jnp.int32, (w, w), 1) // HEAD_DIM
    diag = r2 == c2

    jobs = [(bi, n) for bi in range(bb) for n in range(nc)]
    groups = [jobs[i:i + RET_GROUP] for i in range(0, len(jobs), RET_GROUP)]
    rows = lambda n: slice(n * CHUNK, (n + 1) * CHUNK)
    slot = lambda bi, n: bi * nc + n

    for grp in groups:
        kvs = []
        for bi, n in grp:
            k = fr_ref[bi, rows(n), 256:512]
            vb = fr_ref[bi, rows(n), 512:768].astype(BF16)
            kk = jnp.concatenate([k * dec_s[1], k * dec_s[3]], axis=1).astype(BF16)
            kvs.append(_dot_tn(kk, vb))
        for (bi, n), kv in zip(grp, kvs):
            kvf_s[slot(bi, n)] = jnp.where(diag, kv[0:w], 0.0)
            kvb_s[slot(bi, n)] = jnp.where(diag, kv[w:2 * w], 0.0)

    for bi in range(bb):
        sf_s[bi] = s0f_ref[bi * s0_stride]
        sb_s[bi] = s0b_ref[bi * s0_stride]
    for i in range(nc):
        m = nc - 1 - i
        for bi in range(bb):
            sf = sf_s[bi]
            stf_s[slot(bi, i)] = sf.astype(BF16)
            sf_s[bi] = sf * cd_s[0] + kvf_s[slot(bi, i)]
            sb = sb_s[bi]
            stb_s[slot(bi, m)] = sb.astype(BF16)
            sb_s[bi] = sb * cd_s[1] + kvb_s[slot(bi, m)]
    if want_state:
        for bi in range(bb):
            for h in range(N_HEADS):
                sl = slice(h * HEAD_DIM, (h + 1) * HEAD_DIM)
                sf_ref[bi, h] = sf_s[bi, sl, sl]
                sb_ref[bi, h] = sb_s[bi, sl, sl]

    gain = gain_ref[0]
    bd2 = jnp.concatenate([bd_ref[...], bd_ref[...]], axis=0)

    def seg_mean(xs):
        cat = []
        for x in xs:
            hi = x.astype(BF16)
            cat.append(jnp.concatenate([hi, (x - hi.astype(F32)).astype(BF16)], axis=1))
        return [_dot(c, bd2) * (1.0 / HEAD_DIM) for c in cat]

    for grp in groups:
        qs = [fr_ref[bi, rows(n), 0:256] for bi, n in grp]
        vs = [fr_ref[bi, rows(n), 512:768] for bi, n in grp]
        qks, inter_f, inter_b = [], [], []
        for (bi, n), q in zip(grp, qs):
            kb = fr_ref[bi, rows(n), 256:512].astype(BF16)
            qstack = jnp.concatenate([jnp.where(lane_w == h, q, 0.0) for h in range(N_HEADS)], axis=0)
            qks.append(_dot_nt(qstack.astype(BF16), kb))
            qb = q.astype(BF16)
            inter_f.append(_dot(qb, stf_s[slot(bi, n)]))
            inter_b.append(_dot(qb, stb_s[slot(bi, n)]))
        outs = []
        for qk, v, i_f, i_b in zip(qks, vs, inter_f, inter_b):
            att = jnp.concatenate([qk[rows(h)] * dm_s[h] for h in range(N_HEADS)], axis=1)
            vstack = jnp.concatenate([jnp.where(lane_w == h, v, 0.0) for h in range(N_HEADS)], axis=0)
            outs.append(_dot(att.astype(BF16), vstack.astype(BF16)) + i_f * dec_s[0] + i_b * dec_s[2])
        ds = [o - mu for o, mu in zip(outs, seg_mean(outs))]
        var = seg_mean([d * d for d in ds])
        for (bi, n), d, vr in zip(grp, ds, var):
            gate = fr_ref[bi, rows(n), 768:1024]
            y = d * lax.rsqrt(vr + NORM_EPS) * gain
            o_ref[bi, rows(n), :] = (gate * jax.nn.sigmoid(gate) * y).astype(BF16)


def _ret(fr, s0f, s0b, wts, layer, want_state):
    b, t, _ = fr.shape
    nc = t // CHUNK
    bb = max(1, RET_CHUNKS_PER_STEP // nc)
    shared = s0f.shape[0] != b
    s_spec = (pl.BlockSpec((1, BRANCH_W, BRANCH_W), lambda i: (0, 0, 0)) if shared
              else pl.BlockSpec((bb, BRANCH_W, BRANCH_W), lambda i: (i, 0, 0)))
    smem = pl.BlockSpec(memory_space=pltpu.SMEM)
    out_specs = [pl.BlockSpec((bb, t, BRANCH_W), lambda i: (i, 0, 0))]
    out_shape = [jax.ShapeDtypeStruct((b, t, BRANCH_W), BF16)]
    if want_state:
        st_spec = pl.BlockSpec((bb, N_HEADS, HEAD_DIM, HEAD_DIM), lambda i: (i, 0, 0, 0))
        out_specs += [st_spec, st_spec]
        out_shape += [jax.ShapeDtypeStruct((b, N_HEADS, HEAD_DIM, HEAD_DIM), F32)] * 2
    sq = (BRANCH_W, BRANCH_W)
    return pl.pallas_call(
        functools.partial(_ret_kernel, t=t, bb=bb, s0_stride=0 if shared else 1, layer=layer,
                          want_state=want_state),
        grid=(b // bb,),
        in_specs=[pl.BlockSpec((bb, t, 1024), lambda i: (i, 0, 0)), s_spec, s_spec, smem, smem,
                  pl.BlockSpec((1, 1, BRANCH_W), lambda i: (layer, 0, 0)),
                  pl.BlockSpec(sq, lambda i: (0, 0))],
        out_specs=out_specs, out_shape=out_shape,
        scratch_shapes=[pltpu.VMEM((N_HEADS, CHUNK, CHUNK), F32), pltpu.VMEM((4, CHUNK, BRANCH_W), F32),
                        pltpu.VMEM((2,) + sq, F32),
                        pltpu.VMEM((bb * nc,) + sq, F32), pltpu.VMEM((bb * nc,) + sq, F32),
                        pltpu.VMEM((bb * nc,) + sq, BF16), pltpu.VMEM((bb * nc,) + sq, BF16),
                        pltpu.VMEM((bb,) + sq, F32), pltpu.VMEM((bb,) + sq, F32)],
        compiler_params=pltpu.CompilerParams(dimension_semantics=("arbitrary",),
                                             vmem_limit_bytes=VMEM_LIMIT),
        name="retention",
    )(fr, s0f, s0b, wts["decf"], wts["decb"], wts["gn"], wts["bd"])


def _merge_kernel(x_ref, mod_ref, oatt_ref, oret_ref, wg_ref, wb_ref, wo_ref, g_ref, b_ref, o_ref):
    sh1 = mod_ref[0, 0:1, :]
    sc1 = mod_ref[0, 1:2, :]
    g1 = mod_ref[0, 2:3, :]
    subs = _sub_tiles(x_ref.shape[0])
    xs = [x_ref[r, :] for r in subs]
    tsums = []
    for r, x in zip(subs, xs):
        h = (x * (1.0 + sc1) + sh1).astype(BF16)
        branches = (oatt_ref[r, 0:256], oret_ref[r, :], oatt_ref[r, 256:512], oatt_ref[r, 512:768])
        tsum = None
        for i, o in enumerate(branches):
            gate = jax.nn.sigmoid(_dot(h, wg_ref[0, :, i * D_MODEL:(i + 1) * D_MODEL]))
            term = gate * _dot(o, wb_ref[0, i])
            tsum = term if tsum is None else tsum + term
        tsums.append(tsum)
    ys = [_dot(t.astype(BF16), wo_ref[0]) for t in tsums]
    for r, x, y in zip(subs, xs, ys):
        o_ref[r, :] = _layernorm(ALPHA * x + g1 * y, g_ref[0], b_ref[0])


def _merge(x, mod, oatt, oret, wts, layer, tm, group_len):
    nt = x.shape[0]
    tok = lambda w: pl.BlockSpec((tm, w), lambda t: (t, 0))
    return pl.pallas_call(
        _merge_kernel,
        grid=(nt // tm,),
        in_specs=[tok(D_MODEL), _mod_spec(tm, group_len), tok(768), tok(BRANCH_W),
                  _layer_spec(layer, D_MODEL, 4 * D_MODEL), _layer_spec(layer, 4, BRANCH_W, D_MODEL),
                  _layer_spec(layer, D_MODEL, D_MODEL), _layer_spec(layer, 1, D_MODEL),
                  _layer_spec(layer, 1, D_MODEL)],
        out_specs=tok(D_MODEL),
        out_shape=jax.ShapeDtypeStruct((nt, D_MODEL), F32),
        compiler_params=pltpu.CompilerParams(dimension_semantics=("arbitrary",), vmem_limit_bytes=VMEM_LIMIT),
        name="merge",
    )(x, mod, oatt, oret, wts["wg"], wts["wb"], wts["wo"], wts["ln1g"], wts["ln1b"])


def _mlp_kernel(x_ref, mod_ref, wup_ref, wdn_ref, g_ref, b_ref, o_ref):
    sh2 = mod_ref[0, 3:4, :]
    sc2 = mod_ref[0, 4:5, :]
    g2 = mod_ref[0, 5:6, :]
    subs = _sub_tiles(x_ref.shape[0])
    xs = [x_ref[r, :] for r in subs]
    us = [_dot((x * (1.0 + sc2) + sh2).astype(BF16), wup_ref[0]) for x in xs]
    fs = []
    for u in us:
        u = jnp.maximum(u, 0.0)
        fs.append(_dot((u * u).astype(BF16), wdn_ref[0]))
    for r, x, f in zip(subs, xs, fs):
        o_ref[r, :] = _layernorm(ALPHA * x + g2 * f, g_ref[0], b_ref[0])


def _mlp(x, mod, wts, layer, tm, group_len):
    nt = x.shape[0]
    tok = lambda w: pl.BlockSpec((tm, w), lambda t: (t, 0))
    return pl.pallas_call(
        _mlp_kernel,
        grid=(nt // tm,),
        in_specs=[tok(D_MODEL), _mod_spec(tm, group_len), _layer_spec(layer, D_MODEL, D_FF),
                  _layer_spec(layer, D_FF, D_MODEL), _layer_spec(layer, 1, D_MODEL), _layer_spec(layer, 1, D_MODEL)],
        out_specs=tok(D_MODEL),
        out_shape=jax.ShapeDtypeStruct((nt, D_MODEL), F32),
        compiler_params=pltpu.CompilerParams(dimension_semantics=("arbitrary",), vmem_limit_bytes=VMEM_LIMIT),
        name="mlp",
    )(x, mod, wts["wup"], wts["wdn"], wts["ln2g"], wts["ln2b"])


def _win_kernel(w_ref, wmix_ref, wg_ref):
    w = w_ref[0]
    o_ret = 256 + 128 + MLA_ROPE
    o_wq, o_wk, o_gq, o_gk, o_gate = o_ret + 1024, o_ret + 1280, o_ret + 1536, o_ret + 1792, o_ret + 2048
    lane = lax.broadcasted_iota(jnp.int32, (w.shape[0], LANES), 1)

    def put(dst, val):
        wmix_ref[0, :, dst:dst + val.shape[1]] = val.astype(BF16)

    def swapped(c0):
        hd = lambda h: w[:, c0 + h * HEAD_DIM:c0 + (h + 1) * HEAD_DIM]
        return jnp.concatenate([hd(0), hd(2), hd(1), hd(3)], axis=1)

    put(P_QLAT, w[:, 0:P_KPE])
    put(P_KPE, jnp.where(lane < MLA_ROPE, w[:, P_KPE:P_RET], 0.0))
    put(P_RET, w[:, o_ret:o_wq])
    put(P_WQ, swapped(o_wq))
    put(P_WK, w[:, o_wk:o_gq])
    put(P_GQ, swapped(o_gq))
    put(P_GK, w[:, o_gk:o_gate])
    wg_ref[0] = w[:, o_gate:].astype(BF16)


def _prep_win(w_in):
    rows = 256
    n_in = w_in.shape[-1]
    return pl.pallas_call(
        _win_kernel,
        grid=(DEPTH, D_MODEL // rows),
        in_specs=[pl.BlockSpec((1, rows, n_in), lambda l, r: (l, r, 0))],
        out_specs=[pl.BlockSpec((1, rows, P_END), lambda l, r: (l, r, 0)),
                   pl.BlockSpec((1, rows, 4 * D_MODEL), lambda l, r: (l, r, 0))],
        out_shape=[jax.ShapeDtypeStruct((DEPTH, D_MODEL, P_END), BF16),
                   jax.ShapeDtypeStruct((DEPTH, D_MODEL, 4 * D_MODEL), BF16)],
        compiler_params=pltpu.CompilerParams(dimension_semantics=("arbitrary", "arbitrary"),
                                             vmem_limit_bytes=VMEM_LIMIT),
        name="prep_w_in",
    )(w_in)


def _prep_weights(w_in, mla_q_norm, mla_w_uq, mla_kv_norm, mla_w_uk, mla_w_uv, ret_decay_fwd, ret_decay_bwd,
                  ret_gn_gain, win_sink, gqa_q_norm, gqa_k_norm, w_branch, w_o, ln1_g, ln1_b, w_up, w_down,
                  ln2_g, ln2_b):
    wmix, wg = _prep_win(w_in)
    wuq = jnp.pad(mla_w_uq.reshape(DEPTH, 256, N_HEADS, HEAD_DIM + MLA_ROPE),
                  ((0, 0), (0, 0), (0, 0), (0, LANES - HEAD_DIM - MLA_ROPE))
                  ).reshape(DEPTH, 256, N_HEADS * LANES).astype(BF16)
    uk = mla_w_uk.reshape(DEPTH, MLA_KV_RANK, N_HEADS, HEAD_DIM).transpose(0, 2, 1, 3)
    top = jnp.pad(uk, ((0, 0), (0, 0), (0, 0), (0, LANES - HEAD_DIM)))
    eye = np.zeros((LANES, LANES), np.float32)
    eye[np.arange(MLA_ROPE), HEAD_DIM + np.arange(MLA_ROPE)] = 1.0
    wka = jnp.concatenate([top, jnp.broadcast_to(eye, (DEPTH, N_HEADS, LANES, LANES))], axis=2).astype(BF16)
    wuv = jnp.pad(mla_w_uv, ((0, 0), (0, 256 - MLA_KV_RANK), (0, 0))).astype(BF16)
    bd = (np.arange(BRANCH_W)[:, None] // HEAD_DIM == np.arange(BRANCH_W)[None, :] // HEAD_DIM).astype(np.float32)
    wb_cd = jnp.swapaxes(w_branch[:, 2:].reshape(DEPTH, 2, 2, 2, HEAD_DIM, D_MODEL), 2, 3
                         ).reshape(DEPTH, 2, BRANCH_W, D_MODEL)
    w_branch = jnp.concatenate([w_branch[:, :2], wb_cd], axis=1)
    row = lambda a: a[:, None, :]
    return dict(
        wmix=wmix, wg=wg, wuq=wuq, wka=wka, wuv=wuv, bd=jnp.asarray(bd, BF16),
        qn=row(mla_q_norm), kvn=row(mla_kv_norm),
        gqn=row(jnp.tile(gqa_q_norm, (1, N_HEADS))), gkn=row(jnp.tile(gqa_k_norm, (1, 2))),
        decf=ret_decay_fwd, decb=ret_decay_bwd, gn=row(ret_gn_gain), sink=win_sink,
        wb=w_branch.astype(BF16), wo=w_o.astype(BF16), ln1g=row(ln1_g), ln1b=row(ln1_b),
        wup=w_up.astype(BF16), wdn=w_down.astype(BF16), ln2g=row(ln2_g), ln2b=row(ln2_b))


def _axial_tables(t, rot_dim):
    rows = t // GRID_W
    row = np.repeat(np.arange(rows, dtype=np.float32), GRID_W)
    col = (np.arange(t) % GRID_W).astype(np.float32)
    n_freq = rot_dim // 4
    inv = np.power(np.float32(ROPE_BASE), -np.arange(n_freq, dtype=np.float32) / np.float32(n_freq))
    ang = np.concatenate([row[:, None] * inv, col[:, None] * inv], axis=-1).astype(np.float32)
    return np.cos(ang), np.sin(ang)


def _rope_tables(t):
    ca, sa = _axial_tables(t, MLA_ROPE)
    ch, sh = _axial_tables(t, HEAD_DIM)
    one = lambda n: np.ones((t, n), np.float32)
    zero = lambda n: np.zeros((t, n), np.float32)
    cq = np.concatenate([one(HEAD_DIM), ca, ca, one(32)], axis=1)
    sq = np.concatenate([zero(HEAD_DIM), -sa, sa, zero(32)], axis=1)
    ck = np.concatenate([ca, ca, one(96)], axis=1)
    sk = np.concatenate([-sa, sa, zero(96)], axis=1)
    chh = np.concatenate([ch, ch, ch, ch], axis=1)
    shh = np.concatenate([-sh, sh, -sh, sh], axis=1)
    return tuple(jnp.asarray(a, F32) for a in (cq, sq, ck, sk, chh, shh))


def _block_diag(s):
    b = s.shape[0]
    same = np.eye(N_HEADS, dtype=bool)[None, :, None, :, None]
    return jnp.where(same, s[:, :, :, None, :], 0.0).reshape(b, BRANCH_W, BRANCH_W)


def kernel(x_prompt, x_sample, cache_mla_ckv, cache_mla_kpe, cache_win_k, cache_win_v, cache_gqa_k, cache_gqa_v,
           state_ret_fwd, state_ret_bwd, c, c_ctx, w_ada, b_ada, w_in, mla_q_norm, mla_w_uq, mla_kv_norm, mla_w_uk,
           mla_w_uv, ret_decay_fwd, ret_decay_bwd, ret_gn_gain, win_sink, gqa_q_norm, gqa_k_norm, w_branch, w_o,
           ln1_g, ln1_b, w_up, w_down, ln2_g, ln2_b):
    batch, seq, _ = x_prompt.shape
    dec_b, dec_t, _ = x_sample.shape
    past = cache_mla_ckv.shape[2]

    cond8 = jnp.concatenate([c_ctx[None], c, jnp.zeros((8 - 1 - dec_b, D_MODEL), F32)], axis=0)
    mod = _ada(cond8, w_ada, b_ada).reshape(DEPTH, 8, 6, D_MODEL)
    rope = _rope_tables(dec_t)
    caches = (cache_mla_ckv,
              jnp.pad(cache_mla_kpe, ((0, 0), (0, 0), (0, 0), (0, LANES - MLA_ROPE))),
              cache_win_k.reshape(dec_b, DEPTH, past, LANES), cache_win_v.reshape(dec_b, DEPTH, past, LANES),
              cache_gqa_k.reshape(dec_b, DEPTH, past, LANES), cache_gqa_v.reshape(dec_b, DEPTH, past, LANES))

    wts = _prep_weights(w_in, mla_q_norm, mla_w_uq, mla_kv_norm, mla_w_uk, mla_w_uv, ret_decay_fwd, ret_decay_bwd,
                        ret_gn_gain, win_sink, gqa_q_norm, gqa_k_norm, w_branch, w_o, ln1_g, ln1_b, w_up, w_down,
                        ln2_g, ln2_b)
    zero_state = jnp.zeros((1, BRANCH_W, BRANCH_W), F32)
    xp = x_prompt.reshape(batch * seq, D_MODEL)
    xs = x_sample.reshape(dec_b * dec_t, D_MODEL)
    per_b = lambda a: a.reshape(batch, seq, a.shape[-1])
    per_d = lambda a: a.reshape(dec_b, dec_t, a.shape[-1])
    flat = lambda a: a.reshape(-1, a.shape[-1])
    ctx_out = []
    for l in range(DEPTH):
        fq, fkv, fr, ockv, okpe, owk, owv, ogk, ogv = _proj(xp, mod[l], wts, l, None, TM_PROJ, None, seq)
        oatt = _attn(per_b(fq), per_b(fkv), None, wts, l, seq)
        oret, s_f, s_b = _ret(per_b(fr), zero_state, zero_state, wts, l, True)
        x1 = _merge(xp, mod[l], flat(oatt), flat(oret), wts, l, TM_MERGE, None)
        xp = _mlp(x1, mod[l], wts, l, TM_MLP, None)
        heads = lambda a: a.reshape(batch, 2, HEAD_DIM, seq)
        ctx_out.append((ockv, okpe, heads(owk), heads(owv), heads(ogk), heads(ogv), s_f, s_b))
        fq, fkv, fr = _proj(xs, mod[l], wts, l, rope, TM_PROJ, dec_t)
        oatt = _attn(per_d(fq), per_d(fkv), caches, wts, l, TQ_LATENT)
        (oret,) = _ret(per_d(fr), _block_diag(state_ret_fwd[:, l]), _block_diag(state_ret_bwd[:, l]), wts, l, False)
        x1 = _merge(xs, mod[l], flat(oatt), flat(oret), wts, l, TM_MERGE, dec_t)
        xs = _mlp(x1, mod[l], wts, l, TM_MLP, dec_t)

    stacked = [jnp.stack([ctx_out[l][i] for l in range(DEPTH)], axis=1) for i in range(8)]
    stacked[1] = jnp.swapaxes(stacked[1], 2, 3)
    for i in range(2, 6):
        stacked[i] = jnp.transpose(stacked[i], (0, 1, 4, 2, 3))
    return (per_b(xp), per_d(xs), *stacked)
```

```python
import functools

import jax
import jax.numpy as jnp
import numpy as np
from jax import lax
from jax.experimental import pallas as pl
from jax.experimental.pallas import tpu as pltpu

D_MODEL = 1024
DEPTH = 2
GRID_W = 64
CHUNK = 128
WINDOW = 128
ROPE_BASE = 10000.0
NORM_EPS = 1e-6
NEG_INF = -1e30
HEAD_DIM = 64
N_HEADS = 4
BRANCH_W = 256
MLA_ROPE = 32
MLA_KV_RANK = 128
LOG2E = 1.4426950408889634
MLA_SCALE = (HEAD_DIM + MLA_ROPE) ** -0.5 * LOG2E
ATT_SCALE = HEAD_DIM ** -0.5 * LOG2E
D_FF = 4 * D_MODEL
ALPHA = (2.0 * DEPTH) ** 0.25
LANES = 128

P_QLAT, P_KVLAT, P_KPE, P_RET, P_WQ, P_WK, P_WV, P_GQ, P_GK, P_GV, P_END = (
    0, 256, 384, 512, 1536, 1792, 1920, 2048, 2304, 2432, 2560)
FQ_W = 1024
FKV_W = 768
VMEM_LIMIT = 56 * 1024 * 1024
TM_PROJ = 512
TM_MERGE = 512
TM_MLP = 512
SUB_ROWS = 256
SUB_ROWS_LATENT_PROJ = 256
TQ_LATENT = 256
ATTN_LOOKAHEAD = 2
MXU_SUM_MIN_KEYS = 1024
RET_GROUP = 4
RET_CHUNKS_PER_STEP = 8

F32 = jnp.float32
BF16 = jnp.bfloat16


def _dot(a, b):
    return jnp.dot(a, b, preferred_element_type=F32)


def _dot_nt(a, b):
    return lax.dot_general(a, b, (((1,), (1,)), ((), ())), preferred_element_type=F32)


def _dot_tn(a, b):
    return lax.dot_general(a, b, (((0,), (0,)), ((), ())), preferred_element_type=F32)


def _sub_tiles(tm, sub_rows=SUB_ROWS):
    n = max(1, tm // sub_rows)
    step = tm // n
    return [slice(i * step, (i + 1) * step) for i in range(n)]


def _layernorm(x, g, b):
    mu = jnp.mean(x, -1, keepdims=True)
    d = x - mu
    var = jnp.mean(d * d, -1, keepdims=True)
    return d * lax.rsqrt(var + NORM_EPS) * g + b


def _rmsnorm(x, g):
    return x * lax.rsqrt(jnp.mean(x * x, -1, keepdims=True) + NORM_EPS) * g


def _seg_sum(x, ones_bd):
    hi = x.astype(BF16)
    lo = (x - hi.astype(F32)).astype(BF16)
    return _dot(hi, ones_bd) + _dot(lo, ones_bd)


def _rope_block(x, cos, sin, half, first):
    rot = jnp.where(first, pltpu.roll(x, LANES - half, 1), pltpu.roll(x, half, 1))
    return x * cos + rot * sin


def _ada_kernel(cond_ref, w_ref, b_ref, o_ref):
    cnd = cond_ref[...]
    s = (cnd * jax.nn.sigmoid(cnd)).astype(BF16)
    o_ref[0] = _dot(s, w_ref[0].astype(BF16)) + b_ref[0]


def _ada(cond8, w_ada, b_ada):
    tn = 1024
    n = w_ada.shape[-1]
    return pl.pallas_call(
        _ada_kernel,
        grid=(DEPTH, n // tn),
        in_specs=[pl.BlockSpec((8, D_MODEL), lambda l, j: (0, 0)),
                  pl.BlockSpec((1, D_MODEL, tn), lambda l, j: (l, 0, j)),
                  pl.BlockSpec((1, 1, tn), lambda l, j: (l, 0, j))],
        out_specs=pl.BlockSpec((1, 8, tn), lambda l, j: (l, 0, j)),
        out_shape=jax.ShapeDtypeStruct((DEPTH, 8, n), F32),
        compiler_params=pltpu.CompilerParams(dimension_semantics=("arbitrary", "arbitrary")),
        name="ada_mod",
    )(cond8, w_ada, b_ada.reshape(DEPTH, 1, n))


def _store_per_seq(o_ref, val, row0, transposed):
    seq = o_ref.shape[2] if transposed else o_ref.shape[1]
    for i in range(val.shape[0] // seq):
        blk = val[i * seq:(i + 1) * seq, :]
        o_ref[row0 // seq + i] = blk.T[0:o_ref.shape[1], :] if transposed else blk


def _proj_kernel(*refs, latent):
    if latent:
        (x_ref, mod_ref, wmix_ref, qn_ref, wuq_ref, kvn_ref, gqn_ref, gkn_ref, bd_ref,
         cq_ref, sq_ref, ck_ref, sk_ref, ch_ref, sh_ref,
         fq_ref, fkv_ref, fr_ref) = refs
    else:
        (x_ref, mod_ref, wmix_ref, qn_ref, wuq_ref, kvn_ref, gqn_ref, gkn_ref, bd_ref,
         fq_ref, fkv_ref, fr_ref, ockv_ref, okpe_ref, owk_ref, owv_ref, ogk_ref, ogv_ref) = refs

    sh1 = mod_ref[0, 0:1, :]
    sc1 = mod_ref[0, 1:2, :]
    subs = _sub_tiles(x_ref.shape[0], SUB_ROWS_LATENT_PROJ if latent else SUB_ROWS)
    ps = [_dot((x_ref[r, :] * (1.0 + sc1) + sh1).astype(BF16), wmix_ref[0]) for r in subs]
    for r, p in zip(subs, ps):
        _proj_post(r, p, refs, latent)


def _proj_post(r, p, refs, latent):
    if latent:
        (_, _, _, qn_ref, wuq_ref, kvn_ref, gqn_ref, gkn_ref, bd_ref,
         cq_ref, sq_ref, ck_ref, sk_ref, ch_ref, sh_ref, fq_ref, fkv_ref, fr_ref) = refs
    else:
        (_, _, _, qn_ref, wuq_ref, kvn_ref, gqn_ref, gkn_ref, bd_ref,
         fq_ref, fkv_ref, fr_ref, ockv_ref, okpe_ref, owk_ref, owv_ref, ogk_ref, ogv_ref) = refs
    rows = p.shape[0]
    lane = lax.broadcasted_iota(jnp.int32, (rows, LANES), 1)
    first_head = (lane % HEAD_DIM) < (HEAD_DIM // 2)

    def rope_heads(v):
        if not latent:
            return v
        cos, sin = ch_ref[r, :], sh_ref[r, :]
        blocks = [_rope_block(v[:, j:j + LANES], cos, sin, HEAD_DIM // 2, first_head)
                  for j in range(0, v.shape[1], LANES)]
        return blocks[0] if len(blocks) == 1 else jnp.concatenate(blocks, axis=1)

    qn = _rmsnorm(p[:, P_QLAT:P_KVLAT], qn_ref[0]).astype(BF16)
    qa = _dot(qn, wuq_ref[0])
    if latent:
        cos, sin = cq_ref[r, :], sq_ref[r, :]
        first = (lane >= HEAD_DIM) & (lane < HEAD_DIM + MLA_ROPE // 2)
        qa = jnp.concatenate(
            [_rope_block(qa[:, j:j + LANES], cos, sin, MLA_ROPE // 2, first) for j in range(0, 512, LANES)],
            axis=1)
    fq_ref[r, 0:512] = (qa * MLA_SCALE).astype(BF16)

    ckv = _rmsnorm(p[:, P_KVLAT:P_KPE], kvn_ref[0])
    kpe = p[:, P_KPE:P_RET]
    if latent:
        first = lane < MLA_ROPE // 2
        kpe_r = _rope_block(kpe, ck_ref[r, :], sk_ref[r, :], MLA_ROPE // 2, first)
    else:
        kpe_r = kpe
        _store_per_seq(ockv_ref, ckv, r.start, False)
        _store_per_seq(okpe_ref, kpe, r.start, True)
    fkv_ref[r, 0:128] = ckv.astype(BF16)
    fkv_ref[r, 128:256] = kpe_r.astype(BF16)

    fr_ref[r, 0:256] = p[:, P_RET:P_RET + 256]
    fr_ref[r, 256:512] = p[:, P_RET + 256:P_RET + 512] * (HEAD_DIM ** -0.5)
    fr_ref[r, 512:1024] = p[:, P_RET + 512:P_WQ]

    fq_ref[r, 512:768] = (rope_heads(p[:, P_WQ:P_WK]) * ATT_SCALE).astype(BF16)
    wk = p[:, P_WK:P_WV]
    wv = p[:, P_WV:P_GQ]
    fkv_ref[r, 256:384] = rope_heads(wk).astype(BF16)
    fkv_ref[r, 384:512] = wv.astype(BF16)

    bd = bd_ref[...]
    gq = p[:, P_GQ:P_GK]
    gqn = gq * lax.rsqrt(_seg_sum(gq * gq, bd) * (1.0 / HEAD_DIM) + NORM_EPS) * gqn_ref[0]
    fq_ref[r, 768:1024] = (rope_heads(gqn) * ATT_SCALE).astype(BF16)
    gk = p[:, P_GK:P_GV]
    gkn = gk * lax.rsqrt(_seg_sum(gk * gk, bd[0:128, 0:128]) * (1.0 / HEAD_DIM) + NORM_EPS) * gkn_ref[0]
    gv = p[:, P_GV:P_END]
    fkv_ref[r, 512:640] = rope_heads(gkn).astype(BF16)
    fkv_ref[r, 640:768] = gv.astype(BF16)
    if not latent:
        _store_per_seq(owk_ref, wk, r.start, True)
        _store_per_seq(owv_ref, wv, r.start, True)
        _store_per_seq(ogk_ref, gkn, r.start, True)
        _store_per_seq(ogv_ref, gv, r.start, True)


def _layer_spec(layer, *s):
    return pl.BlockSpec((1,) + s, lambda t: (layer,) + (0,) * len(s), pipeline_mode=pl.Buffered(1))


def _mod_spec(tm, group_len):
    if group_len is None:
        return pl.BlockSpec((1, 6, D_MODEL), lambda t: (0, 0, 0))
    per = group_len // tm
    return pl.BlockSpec((1, 6, D_MODEL), lambda t: (1 + t // per, 0, 0))


def _proj(x, mod, wts, layer, rope, tm, group_len, ctx_seq=None):
    latent = group_len is not None
    nt = x.shape[0]
    tok = lambda w: pl.BlockSpec((tm, w), lambda t: (t, 0))
    in_specs = [tok(D_MODEL), _mod_spec(tm, group_len),
                _layer_spec(layer, D_MODEL, P_END), _layer_spec(layer, 1, 256), _layer_spec(layer, 256, 512),
                _layer_spec(layer, 1, 128), _layer_spec(layer, 1, 256), _layer_spec(layer, 1, 128),
                pl.BlockSpec((256, 256), lambda t: (0, 0))]
    args = [x, mod, wts["wmix"], wts["qn"], wts["wuq"], wts["kvn"], wts["gqn"], wts["gkn"], wts["bd"]]
    out_specs = [tok(FQ_W), tok(FKV_W), tok(1024)]
    out_shape = [jax.ShapeDtypeStruct((nt, FQ_W), BF16), jax.ShapeDtypeStruct((nt, FKV_W), BF16),
                 jax.ShapeDtypeStruct((nt, 1024), F32)]
    if latent:
        per = group_len // tm
        in_specs += [pl.BlockSpec((tm, LANES), lambda t: (t % per, 0))] * 6
        args += list(rope)
    else:
        nb = tm // ctx_seq
        seqs = nt // ctx_seq
        sub = _sub_tiles(tm)[0]
        assert (sub.stop - sub.start) % ctx_seq == 0, "a sub-tile must hold whole context sequences"
        out_specs += [pl.BlockSpec((nb, ctx_seq, 128), lambda t: (t, 0, 0))]
        out_shape += [jax.ShapeDtypeStruct((seqs, ctx_seq, 128), F32)]
        for r in (MLA_ROPE, 128, 128, 128, 128):
            out_specs.append(pl.BlockSpec((nb, r, ctx_seq), lambda t: (t, 0, 0)))
            out_shape.append(jax.ShapeDtypeStruct((seqs, r, ctx_seq), F32))
    return pl.pallas_call(
        functools.partial(_proj_kernel, latent=latent),
        grid=(nt // tm,), in_specs=in_specs, out_specs=out_specs, out_shape=out_shape,
        compiler_params=pltpu.CompilerParams(dimension_semantics=("arbitrary",), vmem_limit_bytes=VMEM_LIMIT),
        name="proj_latent" if latent else "proj_ctx",
    )(*args)


def _softmax_weights(parts, sink, mxu_sum):
    m = parts[0].max(-1, keepdims=True)
    for s in parts[1:]:
        m = jnp.maximum(m, s.max(-1, keepdims=True))
    if sink is not None:
        m = jnp.maximum(m, sink)
    es = [jnp.exp2(s - m) for s in parts]
    extra = None if sink is None else jnp.exp2(sink - m)
    if not mxu_sum:
        for e in es:
            extra = e.sum(-1, keepdims=True) if extra is None else extra + e.sum(-1, keepdims=True)
    return [e.astype(BF16) for e in es], extra


def _normalised(pv, extra):
    if pv.shape[1] == LANES:
        return pv * (1.0 / extra)
    den = pv[:, LANES:LANES + 1]
    if extra is not None:
        den = den + extra
    return pv[:, 0:LANES] * (1.0 / den)


def _window_start(qi, tq, t):
    return jnp.clip(qi * tq - WINDOW, 0, t - (tq + 2 * WINDOW))


def _band_bias(tq):
    span = tq + 2 * WINDOW
    r = (np.arange(2 * tq) % tq)[None, :, None]
    c = np.arange(span)[None, None, :]
    off = (np.arange(3) * WINDOW)[:, None, None]
    return jnp.asarray(np.where(np.abs(r + off - c) <= WINDOW, 0.0, NEG_INF), F32)


def _attn_kernel(*refs, t, tq, n_cache, layer):
    latent = n_cache > 0
    s_len = t + n_cache
    if latent:
        (fq_ref, fkv_ref, cckv_ref, ckpe_ref, cwk_ref, cwv_ref, cgk_ref, cgv_ref, bias_ref,
         wka_ref, wuv_ref, sink_ref, o_ref, ckpe_s, ka_s, va_s, kd_s, vd_s, vc_s, kcc_s, vcc_s) = refs
    else:
        (fq_ref, fkv_ref, wka_ref, wuv_ref, sink_ref, o_ref, ckpe_s, ka_s, va_s, kd_s, vd_s, vc_s) = refs

    qi = pl.program_id(1)

    @pl.when(qi == 0)
    def _():
        ckpe_s[0:t, :] = fkv_ref[0, :, 0:256]
        kd_s[0:t, :] = fkv_ref[0, :, 512:640]
        vd_s[0:t, 0:LANES] = fkv_ref[0, :, 640:768]
        vc_s[:, 0:LANES] = fkv_ref[0, :, 384:512]
        if latent:
            ckpe_s[t:s_len, 0:128] = cckv_ref[0, 0].astype(BF16)
            ckpe_s[t:s_len, 128:256] = ckpe_ref[0, 0].astype(BF16)
            kd_s[t:s_len, :] = cgk_ref[0, 0].astype(BF16)
            vd_s[t:s_len, 0:LANES] = cgv_ref[0, 0].astype(BF16)
            kcc_s[...] = cwk_ref[0, 0].astype(BF16)
            vcc_s[:, 0:LANES] = cwv_ref[0, 0].astype(BF16)
            vcc_s[:, LANES:] = jnp.ones((n_cache, LANES), BF16)
        vd_s[:, LANES:] = jnp.ones((s_len, LANES), BF16)
        vc_s[:, LANES:] = jnp.ones((t, LANES), BF16)
        ck = ckpe_s[...]
        va = _dot(ck, wuv_ref[0]).astype(BF16)
        for blk in range(2):
            va_s[blk, :, 0:LANES] = va[:, blk * LANES:(blk + 1) * LANES]
            va_s[blk, :, LANES:] = jnp.ones((s_len, LANES), BF16)
        for h in range(N_HEADS):
            ka_s[h] = _dot(ck, wka_ref[0, h]).astype(BF16)

    lane = lax.broadcasted_iota(jnp.int32, (tq, LANES), 1)
    half = [lane < HEAD_DIM, lane >= HEAD_DIM]

    def keep(x, j):
        return jnp.where(half[j], x, 0.0)

    def stacked_q(col, j):
        blks = [fq_ref[0, :, col + g * LANES:col + (g + 1) * LANES] for g in range(2)]
        return jnp.concatenate([keep(b.astype(F32), j).astype(BF16) for b in blks], axis=0)

    if latent:
        span = tq + 2 * WINDOW
        start = pl.multiple_of(_window_start(qi, tq, t), LANES)
    row1 = lax.broadcasted_iota(jnp.int32, (2 * tq, 1), 0)
    mxu_sum = s_len >= MXU_SUM_MIN_KEYS
    vw = 2 * LANES if mxu_sum else LANES

    acc_a = [jnp.zeros((tq, LANES), F32) for _ in range(2)]
    acc_c = [jnp.zeros((tq, LANES), F32) for _ in range(2)]
    acc_d = [jnp.zeros((tq, LANES), F32) for _ in range(2)]

    def a_scores(h):
        return [_dot_nt(fq_ref[0, :, h * LANES:(h + 1) * LANES], ka_s[h])]

    def a_finish(h, ps, sink_term):
        blk = h // 2
        acc_a[blk] = acc_a[blk] + keep(_normalised(_dot(ps[0], va_s[blk, :, 0:vw]), sink_term), h % 2)

    def c_scores(j):
        qs = stacked_q(512, j)
        if latent:
            return [_dot_nt(qs, fkv_ref[0, pl.ds(start, span), 256:384]) + bias_ref[0], _dot_nt(qs, kcc_s[...])]
        return [_dot_nt(qs, fkv_ref[0, :, 256:384])]

    def c_finish(j, ps, sink_term):
        if latent:
            pv = _dot(ps[0], vc_s[pl.ds(start, span), 0:vw]) + _dot(ps[1], vcc_s[:, 0:vw])
        else:
            pv = _dot(ps[0], vc_s[:, 0:vw])
        pv = _normalised(pv, sink_term)
        for g in range(2):
            acc_c[g] = acc_c[g] + keep(pv[g * tq:(g + 1) * tq], j)

    def d_scores(j):
        return [_dot_nt(stacked_q(768, j), kd_s[...])]

    def d_finish(j, ps, sink_term):
        pv = _normalised(_dot(ps[0], vd_s[:, 0:vw]), sink_term)
        for g in range(2):
            acc_d[g] = acc_d[g] + keep(pv[g * tq:(g + 1) * tq], j)

    def c_sink(j):
        return jnp.where(row1 < tq, sink_ref[layer, 2 * j], sink_ref[layer, 2 * j + 1]) * LOG2E

    jobs = [(functools.partial(a_scores, h), functools.partial(a_finish, h), None) for h in range(N_HEADS)]
    jobs += [(functools.partial(d_scores, j), functools.partial(d_finish, j), None) for j in range(2)]
    jobs += [(functools.partial(c_scores, j), functools.partial(c_finish, j), functools.partial(c_sink, j))
             for j in range(2)]
    pending = [job[0]() for job in jobs[:ATTN_LOOKAHEAD]]
    for i, (_, finish, sink) in enumerate(jobs):
        if i + ATTN_LOOKAHEAD < len(jobs):
            pending.append(jobs[i + ATTN_LOOKAHEAD][0]())
        finish(*_softmax_weights(pending.pop(0), None if sink is None else sink(), mxu_sum))

    for g in range(2):
        o_ref[0, :, g * LANES:(g + 1) * LANES] = acc_a[g].astype(BF16)
        o_ref[0, :, 256 + g * LANES:256 + (g + 1) * LANES] = acc_c[g].astype(BF16)
        o_ref[0, :, 512 + g * LANES:512 + (g + 1) * LANES] = acc_d[g].astype(BF16)


def _attn(fq, fkv, caches, wts, layer, tq):
    b, t, _ = fq.shape
    latent = caches is not None
    n_cache = caches[0].shape[2] if latent else 0
    s_len = t + n_cache
    in_specs = [pl.BlockSpec((1, tq, FQ_W), lambda i, q: (i, q, 0)),
                pl.BlockSpec((1, t, FKV_W), lambda i, q: (i, 0, 0))]
    args = [fq, fkv]
    if latent:
        in_specs += [pl.BlockSpec((1, 1, n_cache, LANES), lambda i, q: (i, layer, 0, 0))] * 6
        args += list(caches)
        span = tq + 2 * WINDOW
        assert t >= span and WINDOW <= tq
        in_specs.append(pl.BlockSpec((1, 2 * tq, span),
                                     lambda i, q: ((q * tq - _window_start(q, tq, t)) // WINDOW, 0, 0)))
        args.append(_band_bias(tq))
    in_specs += [pl.BlockSpec((1, N_HEADS, 256, 128), lambda i, q: (layer, 0, 0, 0)),
                 pl.BlockSpec((1, 256, 256), lambda i, q: (layer, 0, 0)),
                 pl.BlockSpec(memory_space=pltpu.SMEM)]
    args += [wts["wka"], wts["wuv"], wts["sink"]]
    scratch = [pltpu.VMEM((s_len, 256), BF16), pltpu.VMEM((N_HEADS, s_len, 128), BF16),
               pltpu.VMEM((2, s_len, 256), BF16), pltpu.VMEM((s_len, 128), BF16), pltpu.VMEM((s_len, 256), BF16),
               pltpu.VMEM((t, 256), BF16)]
    if latent:
        scratch += [pltpu.VMEM((n_cache, 128), BF16), pltpu.VMEM((n_cache, 256), BF16)]
    return pl.pallas_call(
        functools.partial(_attn_kernel, t=t, tq=tq, n_cache=n_cache, layer=layer),
        grid=(b, t // tq), in_specs=in_specs,
        out_specs=pl.BlockSpec((1, tq, 768), lambda i, q: (i, q, 0)),
        out_shape=jax.ShapeDtypeStruct((b, t, 768), BF16),
        scratch_shapes=scratch,
        compiler_params=pltpu.CompilerParams(dimension_semantics=("arbitrary", "arbitrary"),
                                             vmem_limit_bytes=VMEM_LIMIT),
        name="attn_latent" if latent else "attn_ctx",
    )(*args)


def _ret_kernel(*refs, t, bb, s0_stride, layer, want_state):
    if want_state:
        (fr_ref, s0f_ref, s0b_ref, decf_ref, decb_ref, gain_ref, bd_ref, o_ref, sf_ref, sb_ref,
         dm_s, dec_s, cd_s, kvf_s, kvb_s, stf_s, stb_s, sf_s, sb_s) = refs
    else:
        (fr_ref, s0f_ref, s0b_ref, decf_ref, decb_ref, gain_ref, bd_ref, o_ref,
         dm_s, dec_s, cd_s, kvf_s, kvb_s, stf_s, stb_s, sf_s, sb_s) = refs
    nc = t // CHUNK
    w = BRANCH_W
    lane_w = lax.broadcasted_iota(jnp.int32, (CHUNK, w), 1) // HEAD_DIM

    @pl.when(pl.program_id(0) == 0)
    def _():
        row_w = lax.broadcasted_iota(jnp.int32, (CHUNK, w), 0).astype(F32)

        def lane_decay(dec_ref):
            v = jnp.zeros((CHUNK, w), F32)
            for h in range(N_HEADS):
                v = jnp.where(lane_w == h, dec_ref[layer, h], v)
            return jax.nn.log_sigmoid(v)

        lgf = lane_decay(decf_ref)
        lgb = lane_decay(decb_ref)
        dec_s[0] = jnp.exp((row_w + 1.0) * lgf)
        dec_s[1] = jnp.exp((CHUNK - 1.0 - row_w) * lgf)
        dec_s[2] = jnp.exp((CHUNK - row_w) * lgb)
        dec_s[3] = jnp.exp(row_w * lgb)
        cd_s[0] = jnp.concatenate([jnp.exp(CHUNK * lgf)] * (w // CHUNK), axis=0)
        cd_s[1] = jnp.concatenate([jnp.exp(CHUNK * lgb)] * (w // CHUNK), axis=0)
        ii = lax.broadcasted_iota(jnp.int32, (CHUNK, CHUNK), 0).astype(F32)
        jj = lax.broadcasted_iota(jnp.int32, (CHUNK, CHUNK), 1).astype(F32)
        diff = ii - jj
        for h in range(N_HEADS):
            lf = jax.nn.log_sigmoid(jnp.full((CHUNK, CHUNK), decf_ref[layer, h], F32))
            lb = jax.nn.log_sigmoid(jnp.full((CHUNK, CHUNK), decb_ref[layer, h], F32))
            d_f = jnp.where(diff >= 0, jnp.exp(jnp.maximum(diff, 0.0) * lf), 0.0)
            d_b = jnp.where(diff < 0, jnp.exp(jnp.maximum(-diff, 0.0) * lb), 0.0)
            dm_s[h] = d_f + d_b

    r2 = lax.broadcasted_iota(jnp.int32, (w, w), 0) // HEAD_DIM
    c2 = lax.broadcasted_iota(jnp.int32, (w, w), 1) // HEAD_DIM
    diag = r2 == c2

    jobs = [(bi, n) for bi in range(bb) for n in range(nc)]
    groups = [jobs[i:i + RET_GROUP] for i in range(0, len(jobs), RET_GROUP)]
    rows = lambda n: slice(n * CHUNK, (n + 1) * CHUNK)
    slot = lambda bi, n: bi * nc + n

    for grp in groups:
        kvs = []
        for bi, n in grp:
            k = fr_ref[bi, rows(n), 256:512]
            vb = fr_ref[bi, rows(n), 512:768].astype(BF16)
            kk = jnp.concatenate([k * dec_s[1], k * dec_s[3]], axis=1).astype(BF16)
            kvs.append(_dot_tn(kk, vb))
        for (bi, n), kv in zip(grp, kvs):
            kvf_s[slot(bi, n)] = jnp.where(diag, kv[0:w], 0.0)
            kvb_s[slot(bi, n)] = jnp.where(diag, kv[w:2 * w], 0.0)

    for bi in range(bb):
        sf_s[bi] = s0f_ref[bi * s0_stride]
        sb_s[bi] = s0b_ref[bi * s0_stride]
    for i in range(nc):
        m = nc - 1 - i
        for bi in range(bb):
            sf = sf_s[bi]
            stf_s[slot(bi, i)] = sf.astype(BF16)
            sf_s[bi] = sf * cd_s[0] + kvf_s[slot(bi, i)]
            sb = sb_s[bi]
            stb_s[slot(bi, m)] = sb.astype(BF16)
            sb_s[bi] = sb * cd_s[1] + kvb_s[slot(bi, m)]
    if want_state:
        for bi in range(bb):
            for h in range(N_HEADS):
                sl = slice(h * HEAD_DIM, (h + 1) * HEAD_DIM)
                sf_ref[bi, h] = sf_s[bi, sl, sl]
                sb_ref[bi, h] = sb_s[bi, sl, sl]

    gain = gain_ref[0]
    bd2 = jnp.concatenate([bd_ref[...], bd_ref[...]], axis=0)

    def seg_mean(xs):
        cat = []
        for x in xs:
            hi = x.astype(BF16)
            cat.append(jnp.concatenate([hi, (x - hi.astype(F32)).astype(BF16)], axis=1))
        return [_dot(c, bd2) * (1.0 / HEAD_DIM) for c in cat]

    for grp in groups:
        qs = [fr_ref[bi, rows(n), 0:256] for bi, n in grp]
        vs = [fr_ref[bi, rows(n), 512:768] for bi, n in grp]
        qks, inter_f, inter_b = [], [], []
        for (bi, n), q in zip(grp, qs):
            kb = fr_ref[bi, rows(n), 256:512].astype(BF16)
            qstack = jnp.concatenate([jnp.where(lane_w == h, q, 0.0) for h in range(N_HEADS)], axis=0)
            qks.append(_dot_nt(qstack.astype(BF16), kb))
            qb = q.astype(BF16)
            inter_f.append(_dot(qb, stf_s[slot(bi, n)]))
            inter_b.append(_dot(qb, stb_s[slot(bi, n)]))
        outs = []
        for qk, v, i_f, i_b in zip(qks, vs, inter_f, inter_b):
            att = jnp.concatenate([qk[rows(h)] * dm_s[h] for h in range(N_HEADS)], axis=1)
            vstack = jnp.concatenate([jnp.where(lane_w == h, v, 0.0) for h in range(N_HEADS)], axis=0)
            outs.append(_dot(att.astype(BF16), vstack.astype(BF16)) + i_f * dec_s[0] + i_b * dec_s[2])
        ds = [o - mu for o, mu in zip(outs, seg_mean(outs))]
        var = seg_mean([d * d for d in ds])
        for (bi, n), d, vr in zip(grp, ds, var):
            gate = fr_ref[bi, rows(n), 768:1024]
            y = d * lax.rsqrt(vr + NORM_EPS) * gain
            o_ref[bi, rows(n), :] = (gate * jax.nn.sigmoid(gate) * y).astype(BF16)


def _ret(fr, s0f, s0b, wts, layer, want_state):
    b, t, _ = fr.shape
    nc = t // CHUNK
    bb = max(1, RET_CHUNKS_PER_STEP // nc)
    shared = s0f.shape[0] != b
    s_spec = (pl.BlockSpec((1, BRANCH_W, BRANCH_W), lambda i: (0, 0, 0)) if shared
              else pl.BlockSpec((bb, BRANCH_W, BRANCH_W), lambda i: (i, 0, 0)))
    smem = pl.BlockSpec(memory_space=pltpu.SMEM)
    out_specs = [pl.BlockSpec((bb, t, BRANCH_W), lambda i: (i, 0, 0))]
    out_shape = [jax.ShapeDtypeStruct((b, t, BRANCH_W), BF16)]
    if want_state:
        st_spec = pl.BlockSpec((bb, N_HEADS, HEAD_DIM, HEAD_DIM), lambda i: (i, 0, 0, 0))
        out_specs += [st_spec, st_spec]
        out_shape += [jax.ShapeDtypeStruct((b, N_HEADS, HEAD_DIM, HEAD_DIM), F32)] * 2
    sq = (BRANCH_W, BRANCH_W)
    return pl.pallas_call(
        functools.partial(_ret_kernel, t=t, bb=bb, s0_stride=0 if shared else 1, layer=layer,
                          want_state=want_state),
        grid=(b // bb,),
        in_specs=[pl.BlockSpec((bb, t, 1024), lambda i: (i, 0, 0)), s_spec, s_spec, smem, smem,
                  pl.BlockSpec((1, 1, BRANCH_W), lambda i: (layer, 0, 0)),
                  pl.BlockSpec(sq, lambda i: (0, 0))],
        out_specs=out_specs, out_shape=out_shape,
        scratch_shapes=[pltpu.VMEM((N_HEADS, CHUNK, CHUNK), F32), pltpu.VMEM((4, CHUNK, BRANCH_W), F32),
                        pltpu.VMEM((2,) + sq, F32),
                        pltpu.VMEM((bb * nc,) + sq, F32), pltpu.VMEM((bb * nc,) + sq, F32),
                        pltpu.VMEM((bb * nc,) + sq, BF16), pltpu.VMEM((bb * nc,) + sq, BF16),
                        pltpu.VMEM((bb,) + sq, F32), pltpu.VMEM((bb,) + sq, F32)],
        compiler_params=pltpu.CompilerParams(dimension_semantics=("arbitrary",),
                                             vmem_limit_bytes=VMEM_LIMIT),
        name="retention",
    )(fr, s0f, s0b, wts["decf"], wts["decb"], wts["gn"], wts["bd"])


def _merge_kernel(x_ref, mod_ref, oatt_ref, oret_ref, wg_ref, wb_ref, wo_ref, g_ref, b_ref, o_ref):
    sh1 = mod_ref[0, 0:1, :]
    sc1 = mod_ref[0, 1:2, :]
    g1 = mod_ref[0, 2:3, :]
    subs = _sub_tiles(x_ref.shape[0])
    xs = [x_ref[r, :] for r in subs]
    tsums = []
    for r, x in zip(subs, xs):
        h = (x * (1.0 + sc1) + sh1).astype(BF16)
        branches = (oatt_ref[r, 0:256], oret_ref[r, :], oatt_ref[r, 256:512], oatt_ref[r, 512:768])
        tsum = None
        for i, o in enumerate(branches):
            gate = jax.nn.sigmoid(_dot(h, wg_ref[0, :, i * D_MODEL:(i + 1) * D_MODEL]))
            term = gate * _dot(o, wb_ref[0, i])
            tsum = term if tsum is None else tsum + term
        tsums.append(tsum)
    ys = [_dot(t.astype(BF16), wo_ref[0]) for t in tsums]
    for r, x, y in zip(subs, xs, ys):
        o_ref[r, :] = _layernorm(ALPHA * x + g1 * y, g_ref[0], b_ref[0])


def _merge(x, mod, oatt, oret, wts, layer, tm, group_len):
    nt = x.shape[0]
    tok = lambda w: pl.BlockSpec((tm, w), lambda t: (t, 0))
    return pl.pallas_call(
        _merge_kernel,
        grid=(nt // tm,),
        in_specs=[tok(D_MODEL), _mod_spec(tm, group_len), tok(768), tok(BRANCH_W),
                  _layer_spec(layer, D_MODEL, 4 * D_MODEL), _layer_spec(layer, 4, BRANCH_W, D_MODEL),
                  _layer_spec(layer, D_MODEL, D_MODEL), _layer_spec(layer, 1, D_MODEL),
                  _layer_spec(layer, 1, D_MODEL)],
        out_specs=tok(D_MODEL),
        out_shape=jax.ShapeDtypeStruct((nt, D_MODEL), F32),
        compiler_params=pltpu.CompilerParams(dimension_semantics=("arbitrary",), vmem_limit_bytes=VMEM_LIMIT),
        name="merge",
    )(x, mod, oatt, oret, wts["wg"], wts["wb"], wts["wo"], wts["ln1g"], wts["ln1b"])


def _mlp_kernel(x_ref, mod_ref, wup_ref, wdn_ref, g_ref, b_ref, o_ref):
    sh2 = mod_ref[0, 3:4, :]
    sc2 = mod_ref[0, 4:5, :]
    g2 = mod_ref[0, 5:6, :]
    subs = _sub_tiles(x_ref.shape[0])
    xs = [x_ref[r, :] for r in subs]
    us = [_dot((x * (1.0 + sc2) + sh2).astype(BF16), wup_ref[0]) for x in xs]
    fs = []
    for u in us:
        u = jnp.maximum(u, 0.0)
        fs.append(_dot((u * u).astype(BF16), wdn_ref[0]))
    for r, x, f in zip(subs, xs, fs):
        o_ref[r, :] = _layernorm(ALPHA * x + g2 * f, g_ref[0], b_ref[0])


def _mlp(x, mod, wts, layer, tm, group_len):
    nt = x.shape[0]
    tok = lambda w: pl.BlockSpec((tm, w), lambda t: (t, 0))
    return pl.pallas_call(
        _mlp_kernel,
        grid=(nt // tm,),
        in_specs=[tok(D_MODEL), _mod_spec(tm, group_len), _layer_spec(layer, D_MODEL, D_FF),
                  _layer_spec(layer, D_FF, D_MODEL), _layer_spec(layer, 1, D_MODEL), _layer_spec(layer, 1, D_MODEL)],
        out_specs=tok(D_MODEL),
        out_shape=jax.ShapeDtypeStruct((nt, D_MODEL), F32),
        compiler_params=pltpu.CompilerParams(dimension_semantics=("arbitrary",), vmem_limit_bytes=VMEM_LIMIT),
        name="mlp",
    )(x, mod, wts["wup"], wts["wdn"], wts["ln2g"], wts["ln2b"])


def _win_kernel(w_ref, wmix_ref, wg_ref):
    w = w_ref[0]
    o_ret = 256 + 128 + MLA_ROPE
    o_wq, o_wk, o_gq, o_gk, o_gate = o_ret + 1024, o_ret + 1280, o_ret + 1536, o_ret + 1792, o_ret + 2048
    lane = lax.broadcasted_iota(jnp.int32, (w.shape[0], LANES), 1)

    def put(dst, val):
        wmix_ref[0, :, dst:dst + val.shape[1]] = val.astype(BF16)

    def swapped(c0):
        hd = lambda h: w[:, c0 + h * HEAD_DIM:c0 + (h + 1) * HEAD_DIM]
        return jnp.concatenate([hd(0), hd(2), hd(1), hd(3)], axis=1)

    put(P_QLAT, w[:, 0:P_KPE])
    put(P_KPE, jnp.where(lane < MLA_ROPE, w[:, P_KPE:P_RET], 0.0))
    put(P_RET, w[:, o_ret:o_wq])
    put(P_WQ, swapped(o_wq))
    put(P_WK, w[:, o_wk:o_gq])
    put(P_GQ, swapped(o_gq))
    put(P_GK, w[:, o_gk:o_gate])
    wg_ref[0] = w[:, o_gate:].astype(BF16)


def _prep_win(w_in):
    rows = 256
    n_in = w_in.shape[-1]
    return pl.pallas_call(
        _win_kernel,
        grid=(DEPTH, D_MODEL // rows),
        in_specs=[pl.BlockSpec((1, rows, n_in), lambda l, r: (l, r, 0))],
        out_specs=[pl.BlockSpec((1, rows, P_END), lambda l, r: (l, r, 0)),
                   pl.BlockSpec((1, rows, 4 * D_MODEL), lambda l, r: (l, r, 0))],
        out_shape=[jax.ShapeDtypeStruct((DEPTH, D_MODEL, P_END), BF16),
                   jax.ShapeDtypeStruct((DEPTH, D_MODEL, 4 * D_MODEL), BF16)],
        compiler_params=pltpu.CompilerParams(dimension_semantics=("arbitrary", "arbitrary"),
                                             vmem_limit_bytes=VMEM_LIMIT),
        name="prep_w_in",
    )(w_in)


def _prep_weights(w_in, mla_q_norm, mla_w_uq, mla_kv_norm, mla_w_uk, mla_w_uv, ret_decay_fwd, ret_decay_bwd,
                  ret_gn_gain, win_sink, gqa_q_norm, gqa_k_norm, w_branch, w_o, ln1_g, ln1_b, w_up, w_down,
                  ln2_g, ln2_b):
    wmix, wg = _prep_win(w_in)
    wuq = jnp.pad(mla_w_uq.reshape(DEPTH, 256, N_HEADS, HEAD_DIM + MLA_ROPE),
                  ((0, 0), (0, 0), (0, 0), (0, LANES - HEAD_DIM - MLA_ROPE))
                  ).reshape(DEPTH, 256, N_HEADS * LANES).astype(BF16)
    uk = mla_w_uk.reshape(DEPTH, MLA_KV_RANK, N_HEADS, HEAD_DIM).transpose(0, 2, 1, 3)
    top = jnp.pad(uk, ((0, 0), (0, 0), (0, 0), (0, LANES - HEAD_DIM)))
    eye = np.zeros((LANES, LANES), np.float32)
    eye[np.arange(MLA_ROPE), HEAD_DIM + np.arange(MLA_ROPE)] = 1.0
    wka = jnp.concatenate([top, jnp.broadcast_to(eye, (DEPTH, N_HEADS, LANES, LANES))], axis=2).astype(BF16)
    wuv = jnp.pad(mla_w_uv, ((0, 0), (0, 256 - MLA_KV_RANK), (0, 0))).astype(BF16)
    bd = (np.arange(BRANCH_W)[:, None] // HEAD_DIM == np.arange(BRANCH_W)[None, :] // HEAD_DIM).astype(np.float32)
    wb_cd = jnp.swapaxes(w_branch[:, 2:].reshape(DEPTH, 2, 2, 2, HEAD_DIM, D_MODEL), 2, 3
                         ).reshape(DEPTH, 2, BRANCH_W, D_MODEL)
    w_branch = jnp.concatenate([w_branch[:, :2], wb_cd], axis=1)
    row = lambda a: a[:, None, :]
    return dict(
        wmix=wmix, wg=wg, wuq=wuq, wka=wka, wuv=wuv, bd=jnp.asarray(bd, BF16),
        qn=row(mla_q_norm), kvn=row(mla_kv_norm),
        gqn=row(jnp.tile(gqa_q_norm, (1, N_HEADS))), gkn=row(jnp.tile(gqa_k_norm, (1, 2))),
        decf=ret_decay_fwd, decb=ret_decay_bwd, gn=row(ret_gn_gain), sink=win_sink,
        wb=w_branch.astype(BF16), wo=w_o.astype(BF16), ln1g=row(ln1_g), ln1b=row(ln1_b),
        wup=w_up.astype(BF16), wdn=w_down.astype(BF16), ln2g=row(ln2_g), ln2b=row(ln2_b))


def _axial_tables(t, rot_dim):
    rows = t // GRID_W
    row = np.repeat(np.arange(rows, dtype=np.float32), GRID_W)
    col = (np.arange(t) % GRID_W).astype(np.float32)
    n_freq = rot_dim // 4
    inv = np.power(np.float32(ROPE_BASE), -np.arange(n_freq, dtype=np.float32) / np.float32(n_freq))
    ang = np.concatenate([row[:, None] * inv, col[:, None] * inv], axis=-1).astype(np.float32)
    return np.cos(ang), np.sin(ang)


def _rope_tables(t):
    ca, sa = _axial_tables(t, MLA_ROPE)
    ch, sh = _axial_tables(t, HEAD_DIM)
    one = lambda n: np.ones((t, n), np.float32)
    zero = lambda n: np.zeros((t, n), np.float32)
    cq = np.concatenate([one(HEAD_DIM), ca, ca, one(32)], axis=1)
    sq = np.concatenate([zero(HEAD_DIM), -sa, sa, zero(32)], axis=1)
    ck = np.concatenate([ca, ca, one(96)], axis=1)
    sk = np.concatenate([-sa, sa, zero(96)], axis=1)
    chh = np.concatenate([ch, ch, ch, ch], axis=1)
    shh = np.concatenate([-sh, sh, -sh, sh], axis=1)
    return tuple(jnp.asarray(a, F32) for a in (cq, sq, ck, sk, chh, shh))


def _block_diag(s):
    b = s.shape[0]
    same = np.eye(N_HEADS, dtype=bool)[None, :, None, :, None]
    return jnp.where(same, s[:, :, :, None, :], 0.0).reshape(b, BRANCH_W, BRANCH_W)


def kernel(x_prompt, x_sample, cache_mla_ckv, cache_mla_kpe, cache_win_k, cache_win_v, cache_gqa_k, cache_gqa_v,
           state_ret_fwd, state_ret_bwd, c, c_ctx, w_ada, b_ada, w_in, mla_q_norm, mla_w_uq, mla_kv_norm, mla_w_uk,
           mla_w_uv, ret_decay_fwd, ret_decay_bwd, ret_gn_gain, win_sink, gqa_q_norm, gqa_k_norm, w_branch, w_o,
           ln1_g, ln1_b, w_up, w_down, ln2_g, ln2_b):
    batch, seq, _ = x_prompt.shape
    dec_b, dec_t, _ = x_sample.shape
    past = cache_mla_ckv.shape[2]

    cond8 = jnp.concatenate([c_ctx[None], c, jnp.zeros((8 - 1 - dec_b, D_MODEL), F32)], axis=0)
    mod = _ada(cond8, w_ada, b_ada).reshape(DEPTH, 8, 6, D_MODEL)
    rope = _rope_tables(dec_t)
    caches = (cache_mla_ckv,
              jnp.pad(cache_mla_kpe, ((0, 0), (0, 0), (0, 0), (0, LANES - MLA_ROPE))),
              cache_win_k.reshape(dec_b, DEPTH, past, LANES), cache_win_v.reshape(dec_b, DEPTH, past, LANES),
              cache_gqa_k.reshape(dec_b, DEPTH, past, LANES), cache_gqa_v.reshape(dec_b, DEPTH, past, LANES))

    wts = _prep_weights(w_in, mla_q_norm, mla_w_uq, mla_kv_norm, mla_w_uk, mla_w_uv, ret_decay_fwd, ret_decay_bwd,
                        ret_gn_gain, win_sink, gqa_q_norm, gqa_k_norm, w_branch, w_o, ln1_g, ln1_b, w_up, w_down,
                        ln2_g, ln2_b)
    zero_state = jnp.zeros((1, BRANCH_W, BRANCH_W), F32)
    xp = x_prompt.reshape(batch * seq, D_MODEL)
    xs = x_sample.reshape(dec_b * dec_t, D_MODEL)
    per_b = lambda a: a.reshape(batch, seq, a.shape[-1])
    per_d = lambda a: a.reshape(dec_b, dec_t, a.shape[-1])
    flat = lambda a: a.reshape(-1, a.shape[-1])
    ctx_out = []
    for l in range(DEPTH):
        fq, fkv, fr, ockv, okpe, owk, owv, ogk, ogv = _proj(xp, mod[l], wts, l, None, TM_PROJ, None, seq)
        oatt = _attn(per_b(fq), per_b(fkv), None, wts, l, seq)
        oret, s_f, s_b = _ret(per_b(fr), zero_state, zero_state, wts, l, True)
        x1 = _merge(xp, mod[l], flat(oatt), flat(oret), wts, l, TM_MERGE, None)
        xp = _mlp(x1, mod[l], wts, l, TM_MLP, None)
        heads = lambda a: a.reshape(batch, 2, HEAD_DIM, seq)
        ctx_out.append((ockv, okpe, heads(owk), heads(owv), heads(ogk), heads(ogv), s_f, s_b))
        fq, fkv, fr = _proj(xs, mod[l], wts, l, rope, TM_PROJ, dec_t)
        oatt = _attn(per_d(fq), per_d(fkv), caches, wts, l, TQ_LATENT)
        (oret,) = _ret(per_d(fr), _block_diag(state_ret_fwd[:, l]), _block_diag(state_ret_bwd[:, l]), wts, l, False)
        x1 = _merge(xs, mod[l], flat(oatt), flat(oret), wts, l, TM_MERGE, dec_t)
        xs = _mlp(x1, mod[l], wts, l, TM_MLP, dec_t)

    stacked = [jnp.stack([ctx_out[l][i] for l in range(DEPTH)], axis=1) for i in range(8)]
    stacked[1] = jnp.swapaxes(stacked[1], 2, 3)
    for i in range(2, 6):
        stacked[i] = jnp.transpose(stacked[i], (0, 1, 4, 2, 3))
    return (per_b(xp), per_d(xs), *stacked)
```

```python
import functools

import jax
import jax.numpy as jnp
import numpy as np
from jax import lax
from jax.experimental import pallas as pl
from jax.experimental.pallas import tpu as pltpu

D_MODEL = 1024
DEPTH = 2
GRID_W = 64
CHUNK = 128
WINDOW = 128
ROPE_BASE = 10000.0
NORM_EPS = 1e-6
NEG_INF = -1e30
HEAD_DIM = 64
N_HEADS = 4
BRANCH_W = 256
MLA_ROPE = 32
MLA_KV_RANK = 128
LOG2E = 1.4426950408889634
MLA_SCALE = (HEAD_DIM + MLA_ROPE) ** -0.5 * LOG2E
ATT_SCALE = HEAD_DIM ** -0.5 * LOG2E
D_FF = 4 * D_MODEL
ALPHA = (2.0 * DEPTH) ** 0.25
LANES = 128

P_QLAT, P_KVLAT, P_KPE, P_RET, P_WQ, P_WK, P_WV, P_GQ, P_GK, P_GV, P_END = (
    0, 256, 384, 512, 1536, 1792, 1920, 2048, 2304, 2432, 2560)
FQ_W = 1024
FKV_W = 768
VMEM_LIMIT = 56 * 1024 * 1024
TM_PROJ = 512
TM_MERGE = 1024
TM_MLP = 1024
SUB_ROWS = 256
SUB_ROWS_LATENT_PROJ = 256
TQ_LATENT = 256
ATTN_LOOKAHEAD = 2
MXU_SUM_MIN_KEYS = 1024
RET_GROUP = 4
RET_CHUNKS_PER_STEP = 8

F32 = jnp.float32
BF16 = jnp.bfloat16


def _dot(a, b):
    return jnp.dot(a, b, preferred_element_type=F32)


def _dot_nt(a, b):
    return lax.dot_general(a, b, (((1,), (1,)), ((), ())), preferred_element_type=F32)


def _dot_tn(a, b):
    return lax.dot_general(a, b, (((0,), (0,)), ((), ())), preferred_element_type=F32)


def _sub_tiles(tm, sub_rows=SUB_ROWS):
    n = max(1, tm // sub_rows)
    step = tm // n
    return [slice(i * step, (i + 1) * step) for i in range(n)]


def _layernorm(x, g, b):
    mu = jnp.mean(x, -1, keepdims=True)
    d = x - mu
    var = jnp.mean(d * d, -1, keepdims=True)
    return d * lax.rsqrt(var + NORM_EPS) * g + b


def _rmsnorm(x, g):
    return x * lax.rsqrt(jnp.mean(x * x, -1, keepdims=True) + NORM_EPS) * g


def _seg_sum(x, ones_bd):
    hi = x.astype(BF16)
    lo = (x - hi.astype(F32)).astype(BF16)
    return _dot(hi, ones_bd) + _dot(lo, ones_bd)


def _rope_block(x, cos, sin, half, first):
    rot = jnp.where(first, pltpu.roll(x, LANES - half, 1), pltpu.roll(x, half, 1))
    return x * cos + rot * sin


def _ada_kernel(cond_ref, w_ref, b_ref, o_ref):
    cnd = cond_ref[...]
    s = (cnd * jax.nn.sigmoid(cnd)).astype(BF16)
    o_ref[0] = _dot(s, w_ref[0].astype(BF16)) + b_ref[0]


def _ada(cond8, w_ada, b_ada):
    tn = 1024
    n = w_ada.shape[-1]
    return pl.pallas_call(
        _ada_kernel,
        grid=(DEPTH, n // tn),
        in_specs=[pl.BlockSpec((8, D_MODEL), lambda l, j: (0, 0)),
                  pl.BlockSpec((1, D_MODEL, tn), lambda l, j: (l, 0, j)),
                  pl.BlockSpec((1, 1, tn), lambda l, j: (l, 0, j))],
        out_specs=pl.BlockSpec((1, 8, tn), lambda l, j: (l, 0, j)),
        out_shape=jax.ShapeDtypeStruct((DEPTH, 8, n), F32),
        compiler_params=pltpu.CompilerParams(dimension_semantics=("arbitrary", "arbitrary")),
        name="ada_mod",
    )(cond8, w_ada, b_ada.reshape(DEPTH, 1, n))


def _store_per_seq(o_ref, val, row0, transposed):
    seq = o_ref.shape[2] if transposed else o_ref.shape[1]
    for i in range(val.shape[0] // seq):
        blk = val[i * seq:(i + 1) * seq, :]
        o_ref[row0 // seq + i] = blk.T[0:o_ref.shape[1], :] if transposed else blk


def _proj_kernel(*refs, latent, n_alias):
    refs = refs[:9] + refs[9 + n_alias:]
    x_ref, mod_ref, wmix_ref = refs[:3]
    sh1 = mod_ref[0, 0:1, :]
    sc1 = mod_ref[0, 1:2, :]
    subs = _sub_tiles(x_ref.shape[0], SUB_ROWS_LATENT_PROJ if latent else SUB_ROWS)
    ps = [_dot_nt((x_ref[r, :] * (1.0 + sc1) + sh1).astype(BF16), wmix_ref[0]) for r in subs]
    for r, p in zip(subs, ps):
        _proj_post(r, p, refs, latent)


def _proj_post(r, p, refs, latent):
    if latent:
        (_, _, _, qn_ref, wuq_ref, kvn_ref, gqn_ref, gkn_ref, bd_ref,
         cq_ref, sq_ref, ck_ref, sk_ref, ch_ref, sh_ref, fq_ref, fkv_ref, fr_ref) = refs
    else:
        (_, _, _, qn_ref, wuq_ref, kvn_ref, gqn_ref, gkn_ref, bd_ref,
         fq_ref, fkv_ref, fr_ref, ockv_ref, okpe_ref, owk_ref, owv_ref, ogk_ref, ogv_ref) = refs
    rows = p.shape[0]
    lane = lax.broadcasted_iota(jnp.int32, (rows, LANES), 1)
    first_head = (lane % HEAD_DIM) < (HEAD_DIM // 2)

    def rope_heads(v):
        if not latent:
            return v
        cos, sin = ch_ref[r, :], sh_ref[r, :]
        blocks = [_rope_block(v[:, j:j + LANES], cos, sin, HEAD_DIM // 2, first_head)
                  for j in range(0, v.shape[1], LANES)]
        return blocks[0] if len(blocks) == 1 else jnp.concatenate(blocks, axis=1)

    qn = _rmsnorm(p[:, P_QLAT:P_KVLAT], qn_ref[0]).astype(BF16)
    qa = _dot(qn, wuq_ref[0])
    if latent:
        cos, sin = cq_ref[r, :], sq_ref[r, :]
        first = (lane >= HEAD_DIM) & (lane < HEAD_DIM + MLA_ROPE // 2)
        qa = jnp.concatenate(
            [_rope_block(qa[:, j:j + LANES], cos, sin, MLA_ROPE // 2, first) for j in range(0, 512, LANES)],
            axis=1)
    fq_ref[r, 0:512] = (qa * MLA_SCALE).astype(BF16)

    ckv = _rmsnorm(p[:, P_KVLAT:P_KPE], kvn_ref[0])
    kpe = p[:, P_KPE:P_RET]
    if latent:
        first = lane < MLA_ROPE // 2
        kpe_r = _rope_block(kpe, ck_ref[r, :], sk_ref[r, :], MLA_ROPE // 2, first)
    else:
        kpe_r = kpe
        _store_per_seq(ockv_ref, ckv, r.start, False)
        _store_per_seq(okpe_ref, kpe, r.start, True)
    fkv_ref[r, 0:128] = ckv.astype(BF16)
    fkv_ref[r, 128:256] = kpe_r.astype(BF16)

    fr_ref[r, 0:256] = p[:, P_RET:P_RET + 256]
    fr_ref[r, 256:512] = p[:, P_RET + 256:P_RET + 512] * (HEAD_DIM ** -0.5)
    fr_ref[r, 512:1024] = p[:, P_RET + 512:P_WQ]

    fq_ref[r, 512:768] = (rope_heads(p[:, P_WQ:P_WK]) * ATT_SCALE).astype(BF16)
    wk = p[:, P_WK:P_WV]
    wv = p[:, P_WV:P_GQ]
    fkv_ref[r, 256:384] = rope_heads(wk).astype(BF16)
    fkv_ref[r, 384:512] = wv.astype(BF16)

    bd = bd_ref[...]
    gq = p[:, P_GQ:P_GK]
    gqn = gq * lax.rsqrt(_seg_sum(gq * gq, bd) * (1.0 / HEAD_DIM) + NORM_EPS) * gqn_ref[0]
    fq_ref[r, 768:1024] = (rope_heads(gqn) * ATT_SCALE).astype(BF16)
    gk = p[:, P_GK:P_GV]
    gkn = gk * lax.rsqrt(_seg_sum(gk * gk, bd[0:128, 0:128]) * (1.0 / HEAD_DIM) + NORM_EPS) * gkn_ref[0]
    gv = p[:, P_GV:P_END]
    fkv_ref[r, 512:640] = rope_heads(gkn).astype(BF16)
    fkv_ref[r, 640:768] = gv.astype(BF16)
    if not latent:
        _store_per_seq(owk_ref, wk, r.start, True)
        _store_per_seq(owv_ref, wv, r.start, True)
        _store_per_seq(ogk_ref, gkn, r.start, True)
        _store_per_seq(ogv_ref, gv, r.start, True)


def _layer_spec(layer, *s):
    return pl.BlockSpec((1,) + s, lambda t: (layer,) + (0,) * len(s), pipeline_mode=pl.Buffered(1))


def _mod_spec(tm, group_len):
    if group_len is None:
        return pl.BlockSpec((1, 6, D_MODEL), lambda t: (0, 0, 0))
    per = group_len // tm
    return pl.BlockSpec((1, 6, D_MODEL), lambda t: (1 + t // per, 0, 0))


def _proj(x, mod, wts, layer, rope, tm, group_len, ctx_seq=None, caches=None):
    latent = group_len is not None
    aliases = {}
    nt = x.shape[0]
    tok = lambda w: pl.BlockSpec((tm, w), lambda t: (t, 0))
    in_specs = [tok(D_MODEL), _mod_spec(tm, group_len),
                _layer_spec(layer, P_END, D_MODEL), _layer_spec(layer, 1, 256), _layer_spec(layer, 256, 512),
                _layer_spec(layer, 1, 128), _layer_spec(layer, 1, 256), _layer_spec(layer, 1, 128),
                pl.BlockSpec((256, 256), lambda t: (0, 0))]
    args = [x, mod, wts["wmix"], wts["qn"], wts["wuq"], wts["kvn"], wts["gqn"], wts["gkn"], wts["bd"]]
    out_specs = [tok(FQ_W), tok(FKV_W), tok(1024)]
    out_shape = [jax.ShapeDtypeStruct((nt, FQ_W), BF16), jax.ShapeDtypeStruct((nt, FKV_W), BF16),
                 jax.ShapeDtypeStruct((nt, 1024), F32)]
    if latent:
        per = group_len // tm
        in_specs += [pl.BlockSpec((tm, LANES), lambda t: (t % per, 0))] * 6
        args += list(rope)
    else:
        nb = tm // ctx_seq
        seqs = nt // ctx_seq
        sub = _sub_tiles(tm)[0]
        assert (sub.stop - sub.start) % ctx_seq == 0, "a sub-tile must hold whole context sequences"
        out_specs += [pl.BlockSpec((nb, None, ctx_seq, 128), lambda t: (t, layer, 0, 0))]
        out_shape += [jax.ShapeDtypeStruct((seqs, DEPTH, ctx_seq, 128), F32)]
        for r in (MLA_ROPE, 128, 128, 128, 128):
            out_specs.append(pl.BlockSpec((nb, None, r, ctx_seq), lambda t: (t, layer, 0, 0)))
            out_shape.append(jax.ShapeDtypeStruct((seqs, DEPTH, r, ctx_seq), F32))
        if caches is not None:
            aliases = {len(args) + i: 3 + i for i in range(len(caches))}
            in_specs += [pl.BlockSpec(memory_space=pl.ANY)] * len(caches)
            args += list(caches)
    return pl.pallas_call(
        functools.partial(_proj_kernel, latent=latent, n_alias=len(aliases)),
        grid=(nt // tm,), in_specs=in_specs, out_specs=out_specs, out_shape=out_shape,
        input_output_aliases=aliases,
        compiler_params=pltpu.CompilerParams(dimension_semantics=("arbitrary",), vmem_limit_bytes=VMEM_LIMIT),
        name="proj_latent" if latent else "proj_ctx",
    )(*args)


def _softmax_weights(parts, sink, mxu_sum):
    m = parts[0].max(-1, keepdims=True)
    for s in parts[1:]:
        m = jnp.maximum(m, s.max(-1, keepdims=True))
    if sink is not None:
        m = jnp.maximum(m, sink)
    es = [jnp.exp2(s - m) for s in parts]
    extra = None if sink is None else jnp.exp2(sink - m)
    if not mxu_sum:
        for e in es:
            extra = e.sum(-1, keepdims=True) if extra is None else extra + e.sum(-1, keepdims=True)
    return [e.astype(BF16) for e in es], extra


def _normalised(pv, extra):
    if pv.shape[1] == LANES:
        return pv * (1.0 / extra)
    den = pv[:, LANES:LANES + 1]
    if extra is not None:
        den = den + extra
    return pv[:, 0:LANES] * (1.0 / den)


def _window_start(qi, tq, t):
    return jnp.clip(qi * tq - WINDOW, 0, t - (tq + 2 * WINDOW))


def _band_bias(tq):
    span = tq + 2 * WINDOW
    r = (np.arange(2 * tq) % tq)[None, :, None]
    c = np.arange(span)[None, None, :]
    off = (np.arange(3) * WINDOW)[:, None, None]
    return jnp.asarray(np.where(np.abs(r + off - c) <= WINDOW, 0.0, NEG_INF), F32)


def _attn_kernel(*refs, t, tq, n_cache, layer):
    latent = n_cache > 0
    s_len = t + n_cache
    if latent:
        (fq_ref, fkv_ref, cckv_ref, ckpe_ref, cwk_ref, cwv_ref, cgk_ref, cgv_ref, bias_ref,
         wka_ref, wuv_ref, sink_ref, o_ref, ckpe_s, ka_s, va_s, kd_s, vd_s, vc_s, kcc_s, vcc_s) = refs
    else:
        (fq_ref, fkv_ref, wka_ref, wuv_ref, sink_ref, o_ref, ckpe_s, ka_s, va_s, kd_s, vd_s, vc_s) = refs

    qi = pl.program_id(1)

    @pl.when(qi == 0)
    def _():
        ckpe_s[0:t, :] = fkv_ref[0, :, 0:256]
        kd_s[0:t, :] = fkv_ref[0, :, 512:640]
        vd_s[0:t, 0:LANES] = fkv_ref[0, :, 640:768]
        vc_s[:, 0:LANES] = fkv_ref[0, :, 384:512]
        if latent:
            ckpe_s[t:s_len, 0:128] = cckv_ref[0, 0].astype(BF16)
            ckpe_s[t:s_len, 128:256] = ckpe_ref[0, 0].astype(BF16)
            kd_s[t:s_len, :] = cgk_ref[0, 0].astype(BF16)
            vd_s[t:s_len, 0:LANES] = cgv_ref[0, 0].astype(BF16)
            kcc_s[...] = cwk_ref[0, 0].astype(BF16)
            vcc_s[:, 0:LANES] = cwv_ref[0, 0].astype(BF16)
            vcc_s[:, LANES:] = jnp.ones((n_cache, LANES), BF16)
        vd_s[:, LANES:] = jnp.ones((s_len, LANES), BF16)
        vc_s[:, LANES:] = jnp.ones((t, LANES), BF16)
        ck = ckpe_s[...]
        va = _dot(ck, wuv_ref[0]).astype(BF16)
        for blk in range(2):
            va_s[blk, :, 0:LANES] = va[:, blk * LANES:(blk + 1) * LANES]
            va_s[blk, :, LANES:] = jnp.ones((s_len, LANES), BF16)
        for h in range(N_HEADS):
            ka_s[h] = _dot(ck, wka_ref[0, h]).astype(BF16)

    lane = lax.broadcasted_iota(jnp.int32, (tq, LANES), 1)
    half = [lane < HEAD_DIM, lane >= HEAD_DIM]

    def keep(x, j):
        return jnp.where(half[j], x, 0.0)

    def stacked_q(col, j):
        blks = [fq_ref[0, :, col + g * LANES:col + (g + 1) * LANES] for g in range(2)]
        return jnp.concatenate([keep(b.astype(F32), j).astype(BF16) for b in blks], axis=0)

    if latent:
        span = tq + 2 * WINDOW
        start = pl.multiple_of(_window_start(qi, tq, t), LANES)
    row1 = lax.broadcasted_iota(jnp.int32, (2 * tq, 1), 0)
    mxu_sum = s_len >= MXU_SUM_MIN_KEYS
    vw = 2 * LANES if mxu_sum else LANES

    acc_a = [jnp.zeros((tq, LANES), F32) for _ in range(2)]
    acc_c = [jnp.zeros((tq, LANES), F32) for _ in range(2)]
    acc_d = [jnp.zeros((tq, LANES), F32) for _ in range(2)]

    def a_scores(h):
        return [_dot_nt(fq_ref[0, :, h * LANES:(h + 1) * LANES], ka_s[h])]

    def a_finish(h, ps, sink_term):
        blk = h // 2
        acc_a[blk] = acc_a[blk] + keep(_normalised(_dot(ps[0], va_s[blk, :, 0:vw]), sink_term), h % 2)

    def c_scores(j):
        qs = stacked_q(512, j)
        if latent:
            return [_dot_nt(qs, fkv_ref[0, pl.ds(start, span), 256:384]) + bias_ref[0], _dot_nt(qs, kcc_s[...])]
        return [_dot_nt(qs, fkv_ref[0, :, 256:384])]

    def c_finish(j, ps, sink_term):
        if latent:
            pv = _dot(ps[0], vc_s[pl.ds(start, span), 0:vw]) + _dot(ps[1], vcc_s[:, 0:vw])
        else:
            pv = _dot(ps[0], vc_s[:, 0:vw])
        pv = _normalised(pv, sink_term)
        for g in range(2):
            acc_c[g] = acc_c[g] + keep(pv[g * tq:(g + 1) * tq], j)

    def d_scores(j):
        return [_dot_nt(stacked_q(768, j), kd_s[...])]

    def d_finish(j, ps, sink_term):
        pv = _normalised(_dot(ps[0], vd_s[:, 0:vw]), sink_term)
        for g in range(2):
            acc_d[g] = acc_d[g] + keep(pv[g * tq:(g + 1) * tq], j)

    def c_sink(j):
        return jnp.where(row1 < tq, sink_ref[layer, 2 * j], sink_ref[layer, 2 * j + 1]) * LOG2E

    jobs = [(functools.partial(a_scores, h), functools.partial(a_finish, h), None) for h in range(N_HEADS)]
    jobs += [(functools.partial(d_scores, j), functools.partial(d_finish, j), None) for j in range(2)]
    jobs += [(functools.partial(c_scores, j), functools.partial(c_finish, j), functools.partial(c_sink, j))
             for j in range(2)]
    pending = [job[0]() for job in jobs[:ATTN_LOOKAHEAD]]
    for i, (_, finish, sink) in enumerate(jobs):
        if i + ATTN_LOOKAHEAD < len(jobs):
            pending.append(jobs[i + ATTN_LOOKAHEAD][0]())
        finish(*_softmax_weights(pending.pop(0), None if sink is None else sink(), mxu_sum))

    for g in range(2):
        o_ref[0, :, g * LANES:(g + 1) * LANES] = acc_a[g].astype(BF16)
        o_ref[0, :, 256 + g * LANES:256 + (g + 1) * LANES] = acc_c[g].astype(BF16)
        o_ref[0, :, 512 + g * LANES:512 + (g + 1) * LANES] = acc_d[g].astype(BF16)


def _attn(fq, fkv, caches, wts, layer, tq):
    b, t, _ = fq.shape
    latent = caches is not None
    n_cache = caches[0].shape[2] if latent else 0
    s_len = t + n_cache
    in_specs = [pl.BlockSpec((1, tq, FQ_W), lambda i, q: (i, q, 0)),
                pl.BlockSpec((1, t, FKV_W), lambda i, q: (i, 0, 0))]
    args = [fq, fkv]
    if latent:
        in_specs += [pl.BlockSpec((1, 1, n_cache, LANES), lambda i, q: (i, layer, 0, 0))] * 6
        args += list(caches)
        span = tq + 2 * WINDOW
        assert t >= span and WINDOW <= tq
        in_specs.append(pl.BlockSpec((1, 2 * tq, span),
                                     lambda i, q: ((q * tq - _window_start(q, tq, t)) // WINDOW, 0, 0)))
        args.append(_band_bias(tq))
    in_specs += [pl.BlockSpec((1, N_HEADS, 256, 128), lambda i, q: (layer, 0, 0, 0)),
                 pl.BlockSpec((1, 256, 256), lambda i, q: (layer, 0, 0)),
                 pl.BlockSpec(memory_space=pltpu.SMEM)]
    args += [wts["wka"], wts["wuv"], wts["sink"]]
    scratch = [pltpu.VMEM((s_len, 256), BF16), pltpu.VMEM((N_HEADS, s_len, 128), BF16),
               pltpu.VMEM((2, s_len, 256), BF16), pltpu.VMEM((s_len, 128), BF16), pltpu.VMEM((s_len, 256), BF16),
               pltpu.VMEM((t, 256), BF16)]
    if latent:
        scratch += [pltpu.VMEM((n_cache, 128), BF16), pltpu.VMEM((n_cache, 256), BF16)]
    return pl.pallas_call(
        functools.partial(_attn_kernel, t=t, tq=tq, n_cache=n_cache, layer=layer),
        grid=(b, t // tq), in_specs=in_specs,
        out_specs=pl.BlockSpec((1, tq, 768), lambda i, q: (i, q, 0)),
        out_shape=jax.ShapeDtypeStruct((b, t, 768), BF16),
        scratch_shapes=scratch,
        compiler_params=pltpu.CompilerParams(dimension_semantics=("arbitrary", "arbitrary"),
                                             vmem_limit_bytes=VMEM_LIMIT),
        name="attn_latent" if latent else "attn_ctx",
    )(*args)


def _ret_kernel(*refs, t, bb, s0_stride, layer, want_state, n_alias):
    refs = refs[:7] + refs[7 + n_alias:]
    if want_state:
        (fr_ref, s0f_ref, s0b_ref, decf_ref, decb_ref, gain_ref, bd_ref, o_ref, sf_ref, sb_ref,
         dm_s, dec_s, cd_s, kvf_s, kvb_s, stf_s, stb_s, sf_s, sb_s) = refs
    else:
        (fr_ref, s0f_ref, s0b_ref, decf_ref, decb_ref, gain_ref, bd_ref, o_ref,
         dm_s, dec_s, cd_s, kvf_s, kvb_s, stf_s, stb_s, sf_s, sb_s) = refs
    nc = t // CHUNK
    w = BRANCH_W
    lane_w = lax.broadcasted_iota(jnp.int32, (CHUNK, w), 1) // HEAD_DIM

    @pl.when(pl.program_id(0) == 0)
    def _():
        row_w = lax.broadcasted_iota(jnp.int32, (CHUNK, w), 0).astype(F32)

        def lane_decay(dec_ref):
            v = jnp.zeros((CHUNK, w), F32)
            for h in range(N_HEADS):
                v = jnp.where(lane_w == h, dec_ref[layer, h], v)
            return jax.nn.log_sigmoid(v)

        lgf = lane_decay(decf_ref)
        lgb = lane_decay(decb_ref)
        dec_s[0] = jnp.exp((row_w + 1.0) * lgf)
        dec_s[1] = jnp.exp((CHUNK - 1.0 - row_w) * lgf)
        dec_s[2] = jnp.exp((CHUNK - row_w) * lgb)
        dec_s[3] = jnp.exp(row_w * lgb)
        cd_s[0] = jnp.concatenate([jnp.exp(CHUNK * lgf)] * (w // CHUNK), axis=0)
        cd_s[1] = jnp.concatenate([jnp.exp(CHUNK * lgb)] * (w // CHUNK), axis=0)
        ii = lax.broadcasted_iota(jnp.int32, (CHUNK, CHUNK), 0).astype(F32)
        jj = lax.broadcasted_iota(jnp.int32, (CHUNK, CHUNK), 1).astype(F32)
        diff = ii - jj
        for h in range(N_HEADS):
            lf = jax.nn.log_sigmoid(jnp.full((CHUNK, CHUNK), decf_ref[layer, h], F32))
            lb = jax.nn.log_sigmoid(jnp.full((CHUNK, CHUNK), decb_ref[layer, h], F32))
            d_f = jnp.where(diff >= 0, jnp.exp(jnp.maximum(diff, 0.0) * lf), 0.0)
            d_b = jnp.where(diff < 0, jnp.exp(jnp.maximum(-diff, 0.0) * lb), 0.0)
            dm_s[h] = d_f + d_b

    r2 = lax.broadcasted_iota(jnp.int32, (w, w), 0) // HEAD_DIM
    c2 = lax.broadcasted_iota(jnp.int32, (w, w), 1) // HEAD_DIM
    diag = r2 == c2

    jobs = [(bi, n) for bi in range(bb) for n in range(nc)]
    groups = [jobs[i:i + RET_GROUP] for i in range(0, len(jobs), RET_GROUP)]
    rows = lambda n: slice(n * CHUNK, (n + 1) * CHUNK)
    slot = lambda bi, n: bi * nc + n

    for grp in groups:
        kvs = []
        for bi, n in grp:
            k = fr_ref[bi, rows(n), 256:512]
            vb = fr_ref[bi, rows(n), 512:768].astype(BF16)
            kk = jnp.concatenate([k * dec_s[1], k * dec_s[3]], axis=1).astype(BF16)
            kvs.append(_dot_tn(kk, vb))
        for (bi, n), kv in zip(grp, kvs):
            kvf_s[slot(bi, n)] = jnp.where(diag, kv[0:w], 0.0)
            kvb_s[slot(bi, n)] = jnp.where(diag, kv[w:2 * w], 0.0)

    for bi in range(bb):
        sf_s[bi] = s0f_ref[bi * s0_stride]
        sb_s[bi] = s0b_ref[bi * s0_stride]
    for i in range(nc):
        m = nc - 1 - i
        for bi in range(bb):
            sf = sf_s[bi]
            stf_s[slot(bi, i)] = sf.astype(BF16)
            sf_s[bi] = sf * cd_s[0] + kvf_s[slot(bi, i)]
            sb = sb_s[bi]
            stb_s[slot(bi, m)] = sb.astype(BF16)
            sb_s[bi] = sb * cd_s[1] + kvb_s[slot(bi, m)]
    if want_state:
        for bi in range(bb):
            for h in range(N_HEADS):
                sl = slice(h * HEAD_DIM, (h + 1) * HEAD_DIM)
                sf_ref[bi, h] = sf_s[bi, sl, sl]
                sb_ref[bi, h] = sb_s[bi, sl, sl]

    gain = gain_ref[0]
    bd2 = jnp.concatenate([bd_ref[...], bd_ref[...]], axis=0)

    def seg_mean(xs):
        cat = []
        for x in xs:
            hi = x.astype(BF16)
            cat.append(jnp.concatenate([hi, (x - hi.astype(F32)).astype(BF16)], axis=1))
        return [_dot(c, bd2) * (1.0 / HEAD_DIM) for c in cat]

    for grp in groups:
        qs = [fr_ref[bi, rows(n), 0:256] for bi, n in grp]
        vs = [fr_ref[bi, rows(n), 512:768] for bi, n in grp]
        qks, inter_f, inter_b = [], [], []
        for (bi, n), q in zip(grp, qs):
            kb = fr_ref[bi, rows(n), 256:512].astype(BF16)
            qstack = jnp.concatenate([jnp.where(lane_w == h, q, 0.0) for h in range(N_HEADS)], axis=0)
            qks.append(_dot_nt(qstack.astype(BF16), kb))
            qb = q.astype(BF16)
            inter_f.append(_dot(qb, stf_s[slot(bi, n)]))
            inter_b.append(_dot(qb, stb_s[slot(bi, n)]))
        outs = []
        for qk, v, i_f, i_b in zip(qks, vs, inter_f, inter_b):
            att = jnp.concatenate([qk[rows(h)] * dm_s[h] for h in range(N_HEADS)], axis=1)
            vstack = jnp.concatenate([jnp.where(lane_w == h, v, 0.0) for h in range(N_HEADS)], axis=0)
            outs.append(_dot(att.astype(BF16), vstack.astype(BF16)) + i_f * dec_s[0] + i_b * dec_s[2])
        ds = [o - mu for o, mu in zip(outs, seg_mean(outs))]
        var = seg_mean([d * d for d in ds])
        for (bi, n), d, vr in zip(grp, ds, var):
            gate = fr_ref[bi, rows(n), 768:1024]
            y = d * lax.rsqrt(vr + NORM_EPS) * gain
            o_ref[bi, rows(n), :] = (gate * jax.nn.sigmoid(gate) * y).astype(BF16)


def _ret(fr, s0f, s0b, wts, layer, want_state, states=None):
    b, t, _ = fr.shape
    nc = t // CHUNK
    bb = max(1, RET_CHUNKS_PER_STEP // nc)
    shared = s0f.shape[0] != b
    s_spec = (pl.BlockSpec((1, BRANCH_W, BRANCH_W), lambda i: (0, 0, 0)) if shared
              else pl.BlockSpec((bb, BRANCH_W, BRANCH_W), lambda i: (i, 0, 0)))
    smem = pl.BlockSpec(memory_space=pltpu.SMEM)
    out_specs = [pl.BlockSpec((bb, t, BRANCH_W), lambda i: (i, 0, 0))]
    out_shape = [jax.ShapeDtypeStruct((b, t, BRANCH_W), BF16)]
    in_specs = [pl.BlockSpec((bb, t, 1024), lambda i: (i, 0, 0)), s_spec, s_spec, smem, smem,
                pl.BlockSpec((1, 1, BRANCH_W), lambda i: (layer, 0, 0)),
                pl.BlockSpec((BRANCH_W, BRANCH_W), lambda i: (0, 0))]
    args = [fr, s0f, s0b, wts["decf"], wts["decb"], wts["gn"], wts["bd"]]
    aliases = {}
    if want_state:
        st_spec = pl.BlockSpec((bb, None, N_HEADS, HEAD_DIM, HEAD_DIM), lambda i: (i, layer, 0, 0, 0))
        out_specs += [st_spec, st_spec]
        out_shape += [jax.ShapeDtypeStruct((b, DEPTH, N_HEADS, HEAD_DIM, HEAD_DIM), F32)] * 2
        if states is not None:
            aliases = {len(args) + i: 1 + i for i in range(2)}
            in_specs += [pl.BlockSpec(memory_space=pl.ANY)] * 2
            args += list(states)
    sq = (BRANCH_W, BRANCH_W)
    return pl.pallas_call(
        functools.partial(_ret_kernel, t=t, bb=bb, s0_stride=0 if shared else 1, layer=layer,
                          want_state=want_state, n_alias=len(aliases)),
        grid=(b // bb,),
        in_specs=in_specs, out_specs=out_specs, out_shape=out_shape, input_output_aliases=aliases,
        scratch_shapes=[pltpu.VMEM((N_HEADS, CHUNK, CHUNK), F32), pltpu.VMEM((4, CHUNK, BRANCH_W), F32),
                        pltpu.VMEM((2,) + sq, F32),
                        pltpu.VMEM((bb * nc,) + sq, F32), pltpu.VMEM((bb * nc,) + sq, F32),
                        pltpu.VMEM((bb * nc,) + sq, BF16), pltpu.VMEM((bb * nc,) + sq, BF16),
                        pltpu.VMEM((bb,) + sq, F32), pltpu.VMEM((bb,) + sq, F32)],
        compiler_params=pltpu.CompilerParams(dimension_semantics=("arbitrary",),
                                             vmem_limit_bytes=VMEM_LIMIT),
        name="retention",
    )(*args)


def _merge_kernel(x_ref, mod_ref, oatt_ref, oret_ref, wg_ref, wb_ref, wo_ref, g_ref, b_ref, o_ref):
    sh1 = mod_ref[0, 0:1, :]
    sc1 = mod_ref[0, 1:2, :]
    g1 = mod_ref[0, 2:3, :]
    subs = _sub_tiles(x_ref.shape[0])

    def gated(r):
        h = (x_ref[r, :] * (1.0 + sc1) + sh1).astype(BF16)
        branches = (oatt_ref[r, 0:256], oret_ref[r, :], oatt_ref[r, 256:512], oatt_ref[r, 512:768])
        tsum = None
        for i, o in enumerate(branches):
            gate = jax.nn.sigmoid(_dot_nt(h, wg_ref[0, i * D_MODEL:(i + 1) * D_MODEL, :]))
            term = gate * _dot(o, wb_ref[0, i])
            tsum = term if tsum is None else tsum + term
        return tsum

    def finish(r, y):
        o_ref[r, :] = _layernorm(ALPHA * x_ref[r, :] + g1 * y, g_ref[0], b_ref[0])

    sums = [gated(subs[0])]
    done = None
    for i, r in enumerate(subs):
        if i + 1 < len(subs):
            sums.append(gated(subs[i + 1]))
        y = _dot(sums.pop(0).astype(BF16), wo_ref[0])
        if done is not None:
            finish(*done)
        done = (r, y)
    finish(*done)


def _merge(x, mod, oatt, oret, wts, layer, tm, group_len):
    nt = x.shape[0]
    tok = lambda w: pl.BlockSpec((tm, w), lambda t: (t, 0))
    return pl.pallas_call(
        _merge_kernel,
        grid=(nt // tm,),
        in_specs=[tok(D_MODEL), _mod_spec(tm, group_len), tok(768), tok(BRANCH_W),
                  _layer_spec(layer, 4 * D_MODEL, D_MODEL), _layer_spec(layer, 4, BRANCH_W, D_MODEL),
                  _layer_spec(layer, D_MODEL, D_MODEL), _layer_spec(layer, 1, D_MODEL),
                  _layer_spec(layer, 1, D_MODEL)],
        out_specs=tok(D_MODEL),
        out_shape=jax.ShapeDtypeStruct((nt, D_MODEL), F32),
        compiler_params=pltpu.CompilerParams(dimension_semantics=("arbitrary",), vmem_limit_bytes=VMEM_LIMIT),
        name="merge",
    )(x, mod, oatt, oret, wts["wg"], wts["wb"], wts["wo"], wts["ln1g"], wts["ln1b"])


def _mlp_kernel(x_ref, mod_ref, wup_ref, wdn_ref, g_ref, b_ref, o_ref):
    sh2 = mod_ref[0, 3:4, :]
    sc2 = mod_ref[0, 4:5, :]
    g2 = mod_ref[0, 5:6, :]
    subs = _sub_tiles(x_ref.shape[0])

    def up(r):
        return _dot((x_ref[r, :] * (1.0 + sc2) + sh2).astype(BF16), wup_ref[0])

    def down(u):
        u = jnp.maximum(u, 0.0)
        return _dot((u * u).astype(BF16), wdn_ref[0])

    def finish(r, f):
        o_ref[r, :] = _layernorm(ALPHA * x_ref[r, :] + g2 * f, g_ref[0], b_ref[0])

    ups = [up(subs[0])]
    done = None
    for i, r in enumerate(subs):
        if i + 1 < len(subs):
            ups.append(up(subs[i + 1]))
        f = down(ups.pop(0))
        if done is not None:
            finish(*done)
        done = (r, f)
    finish(*done)


def _mlp(x, mod, wts, layer, tm, group_len):
    nt = x.shape[0]
    tok = lambda w: pl.BlockSpec((tm, w), lambda t: (t, 0))
    return pl.pallas_call(
        _mlp_kernel,
        grid=(nt // tm,),
        in_specs=[tok(D_MODEL), _mod_spec(tm, group_len), _layer_spec(layer, D_MODEL, D_FF),
                  _layer_spec(layer, D_FF, D_MODEL), _layer_spec(layer, 1, D_MODEL), _layer_spec(layer, 1, D_MODEL)],
        out_specs=tok(D_MODEL),
        out_shape=jax.ShapeDtypeStruct((nt, D_MODEL), F32),
        compiler_params=pltpu.CompilerParams(dimension_semantics=("arbitrary",), vmem_limit_bytes=VMEM_LIMIT),
        name="mlp",
    )(x, mod, wts["wup"], wts["wdn"], wts["ln2g"], wts["ln2b"])


def _win_kernel(w_ref, wmix_ref, wg_ref):
    o_ret = 256 + 128 + MLA_ROPE
    o_wq, o_wk, o_gq, o_gk, o_gate = o_ret + 1024, o_ret + 1280, o_ret + 1536, o_ret + 1792, o_ret + 2048

    def put(dst, src, n):
        wmix_ref[0, dst:dst + n, :] = w_ref[0, src:src + n, :].astype(BF16)

    def put_swapped(dst, src):
        for i, h in enumerate((0, 2, 1, 3)):
            put(dst + i * HEAD_DIM, src + h * HEAD_DIM, HEAD_DIM)

    put(P_QLAT, 0, o_ret)
    pad = P_RET - o_ret
    wmix_ref[0, o_ret:P_RET, :] = jnp.zeros((pad, w_ref.shape[2]), BF16)
    put(P_RET, o_ret, o_wq - o_ret)
    put_swapped(P_WQ, o_wq)
    put(P_WK, o_wk, o_gq - o_wk)
    put_swapped(P_GQ, o_gq)
    put(P_GK, o_gk, o_gate - o_gk)
    wg_ref[0] = w_ref[0, o_gate:, :].astype(BF16)


def _prep_win(w_in):
    w_t = jnp.swapaxes(w_in, 1, 2)
    n_in = w_t.shape[1]
    kb = D_MODEL // 2
    return pl.pallas_call(
        _win_kernel,
        grid=(DEPTH, D_MODEL // kb),
        in_specs=[pl.BlockSpec((1, n_in, kb), lambda l, k: (l, 0, k))],
        out_specs=[pl.BlockSpec((1, P_END, kb), lambda l, k: (l, 0, k)),
                   pl.BlockSpec((1, 4 * D_MODEL, kb), lambda l, k: (l, 0, k))],
        out_shape=[jax.ShapeDtypeStruct((DEPTH, P_END, D_MODEL), BF16),
                   jax.ShapeDtypeStruct((DEPTH, 4 * D_MODEL, D_MODEL), BF16)],
        compiler_params=pltpu.CompilerParams(dimension_semantics=("arbitrary", "arbitrary"),
                                             vmem_limit_bytes=VMEM_LIMIT),
        name="prep_w_in",
    )(w_t)


def _prep_weights(w_in, mla_q_norm, mla_w_uq, mla_kv_norm, mla_w_uk, mla_w_uv, ret_decay_fwd, ret_decay_bwd,
                  ret_gn_gain, win_sink, gqa_q_norm, gqa_k_norm, w_branch, w_o, ln1_g, ln1_b, w_up, w_down,
                  ln2_g, ln2_b):
    wmix, wg = _prep_win(w_in)
    wuq = jnp.pad(mla_w_uq.reshape(DEPTH, 256, N_HEADS, HEAD_DIM + MLA_ROPE),
                  ((0, 0), (0, 0), (0, 0), (0, LANES - HEAD_DIM - MLA_ROPE))
                  ).reshape(DEPTH, 256, N_HEADS * LANES).astype(BF16)
    uk = mla_w_uk.reshape(DEPTH, MLA_KV_RANK, N_HEADS, HEAD_DIM).transpose(0, 2, 1, 3)
    top = jnp.pad(uk, ((0, 0), (0, 0), (0, 0), (0, LANES - HEAD_DIM)))
    eye = np.zeros((LANES, LANES), np.float32)
    eye[np.arange(MLA_ROPE), HEAD_DIM + np.arange(MLA_ROPE)] = 1.0
    wka = jnp.concatenate([top, jnp.broadcast_to(eye, (DEPTH, N_HEADS, LANES, LANES))], axis=2).astype(BF16)
    wuv = jnp.pad(mla_w_uv, ((0, 0), (0, 256 - MLA_KV_RANK), (0, 0))).astype(BF16)
    bd = (np.arange(BRANCH_W)[:, None] // HEAD_DIM == np.arange(BRANCH_W)[None, :] // HEAD_DIM).astype(np.float32)
    wb_cd = jnp.swapaxes(w_branch[:, 2:].reshape(DEPTH, 2, 2, 2, HEAD_DIM, D_MODEL), 2, 3
                         ).reshape(DEPTH, 2, BRANCH_W, D_MODEL)
    w_branch = jnp.concatenate([w_branch[:, :2], wb_cd], axis=1)
    row = lambda a: a[:, None, :]
    return dict(
        wmix=wmix, wg=wg, wuq=wuq, wka=wka, wuv=wuv, bd=jnp.asarray(bd, BF16),
        qn=row(mla_q_norm), kvn=row(mla_kv_norm),
        gqn=row(jnp.tile(gqa_q_norm, (1, N_HEADS))), gkn=row(jnp.tile(gqa_k_norm, (1, 2))),
        decf=ret_decay_fwd, decb=ret_decay_bwd, gn=row(ret_gn_gain), sink=win_sink,
        wb=w_branch.astype(BF16), wo=w_o.astype(BF16), ln1g=row(ln1_g), ln1b=row(ln1_b),
        wup=w_up.astype(BF16), wdn=w_down.astype(BF16), ln2g=row(ln2_g), ln2b=row(ln2_b))


def _axial_tables(t, rot_dim):
    rows = t // GRID_W
    row = np.repeat(np.arange(rows, dtype=np.float32), GRID_W)
    col = (np.arange(t) % GRID_W).astype(np.float32)
    n_freq = rot_dim // 4
    inv = np.power(np.float32(ROPE_BASE), -np.arange(n_freq, dtype=np.float32) / np.float32(n_freq))
    ang = np.concatenate([row[:, None] * inv, col[:, None] * inv], axis=-1).astype(np.float32)
    return np.cos(ang), np.sin(ang)


def _rope_tables(t):
    ca, sa = _axial_tables(t, MLA_ROPE)
    ch, sh = _axial_tables(t, HEAD_DIM)
    one = lambda n: np.ones((t, n), np.float32)
    zero = lambda n: np.zeros((t, n), np.float32)
    cq = np.concatenate([one(HEAD_DIM), ca, ca, one(32)], axis=1)
    sq = np.concatenate([zero(HEAD_DIM), -sa, sa, zero(32)], axis=1)
    ck = np.concatenate([ca, ca, one(96)], axis=1)
    sk = np.concatenate([-sa, sa, zero(96)], axis=1)
    chh = np.concatenate([ch, ch, ch, ch], axis=1)
    shh = np.concatenate([-sh, sh, -sh, sh], axis=1)
    return tuple(jnp.asarray(a, F32) for a in (cq, sq, ck, sk, chh, shh))


def _block_diag(s):
    b = s.shape[0]
    same = np.eye(N_HEADS, dtype=bool)[None, :, None, :, None]
    return jnp.where(same, s[:, :, :, None, :], 0.0).reshape(b, BRANCH_W, BRANCH_W)


def kernel(x_prompt, x_sample, cache_mla_ckv, cache_mla_kpe, cache_win_k, cache_win_v, cache_gqa_k, cache_gqa_v,
           state_ret_fwd, state_ret_bwd, c, c_ctx, w_ada, b_ada, w_in, mla_q_norm, mla_w_uq, mla_kv_norm, mla_w_uk,
           mla_w_uv, ret_decay_fwd, ret_decay_bwd, ret_gn_gain, win_sink, gqa_q_norm, gqa_k_norm, w_branch, w_o,
           ln1_g, ln1_b, w_up, w_down, ln2_g, ln2_b):
    batch, seq, _ = x_prompt.shape
    dec_b, dec_t, _ = x_sample.shape
    past = cache_mla_ckv.shape[2]

    cond8 = jnp.concatenate([c_ctx[None], c, jnp.zeros((8 - 1 - dec_b, D_MODEL), F32)], axis=0)
    mod = _ada(cond8, w_ada, b_ada).reshape(DEPTH, 8, 6, D_MODEL)
    rope = _rope_tables(dec_t)
    caches = (cache_mla_ckv,
              jnp.pad(cache_mla_kpe, ((0, 0), (0, 0), (0, 0), (0, LANES - MLA_ROPE))),
              cache_win_k.reshape(dec_b, DEPTH, past, LANES), cache_win_v.reshape(dec_b, DEPTH, past, LANES),
              cache_gqa_k.reshape(dec_b, DEPTH, past, LANES), cache_gqa_v.reshape(dec_b, DEPTH, past, LANES))

    wts = _prep_weights(w_in, mla_q_norm, mla_w_uq, mla_kv_norm, mla_w_uk, mla_w_uv, ret_decay_fwd, ret_decay_bwd,
                        ret_gn_gain, win_sink, gqa_q_norm, gqa_k_norm, w_branch, w_o, ln1_g, ln1_b, w_up, w_down,
                        ln2_g, ln2_b)
    zero_state = jnp.zeros((1, BRANCH_W, BRANCH_W), F32)
    xp = x_prompt.reshape(batch * seq, D_MODEL)
    xs = x_sample.reshape(dec_b * dec_t, D_MODEL)
    per_b = lambda a: a.reshape(batch, seq, a.shape[-1])
    per_d = lambda a: a.reshape(dec_b, dec_t, a.shape[-1])
    flat = lambda a: a.reshape(-1, a.shape[-1])
    new_caches = None
    new_states = None
    for l in range(DEPTH):
        fq, fkv, fr, *new_caches = _proj(xp, mod[l], wts, l, None, TM_PROJ, None, seq, new_caches)
        oatt = _attn(per_b(fq), per_b(fkv), None, wts, l, seq)
        oret, *new_states = _ret(per_b(fr), zero_state, zero_state, wts, l, True, new_states)
        x1 = _merge(xp, mod[l], flat(oatt), flat(oret), wts, l, TM_MERGE, None)
        xp = _mlp(x1, mod[l], wts, l, TM_MLP, None)
        fq, fkv, fr = _proj(xs, mod[l], wts, l, rope, TM_PROJ, dec_t)
        oatt = _attn(per_d(fq), per_d(fkv), caches, wts, l, TQ_LATENT)
        (oret,) = _ret(per_d(fr), _block_diag(state_ret_fwd[:, l]), _block_diag(state_ret_bwd[:, l]), wts, l, False)
        x1 = _merge(xs, mod[l], flat(oatt), flat(oret), wts, l, TM_MERGE, dec_t)
        xs = _mlp(x1, mod[l], wts, l, TM_MLP, dec_t)

    ckv, kpe, *kv = new_caches
    kv = [jnp.transpose(a.reshape(batch, DEPTH, 2, HEAD_DIM, seq), (0, 1, 4, 2, 3)) for a in kv]
    return (per_b(xp), per_d(xs), ckv, jnp.swapaxes(kpe, 2, 3), *kv, *new_states)
```

```python
import functools

import jax
import jax.numpy as jnp
import numpy as np
from jax import lax
from jax.experimental import pallas as pl
from jax.experimental.pallas import tpu as pltpu

D_MODEL = 1024
DEPTH = 2
GRID_W = 64
CHUNK = 128
WINDOW = 128
ROPE_BASE = 10000.0
NORM_EPS = 1e-6
NEG_INF = -1e30
HEAD_DIM = 64
N_HEADS = 4
BRANCH_W = 256
MLA_ROPE = 32
MLA_KV_RANK = 128
LOG2E = 1.4426950408889634
MLA_SCALE = (HEAD_DIM + MLA_ROPE) ** -0.5 * LOG2E
ATT_SCALE = HEAD_DIM ** -0.5 * LOG2E
D_FF = 4 * D_MODEL
ALPHA = (2.0 * DEPTH) ** 0.25
LANES = 128

P_QLAT, P_KVLAT, P_KPE, P_RET, P_WQ, P_WK, P_WV, P_GQ, P_GK, P_GV, P_END = (
    0, 256, 384, 512, 1536, 1792, 1920, 2048, 2304, 2432, 2560)
FQ_W = 1024
FKV_W = 768
VMEM_LIMIT = 56 * 1024 * 1024
TM_PROJ = 512
TM_MERGE = 1024
TM_MLP = 1024
TM_POST = 512
SUB_ROWS = 256
SUB_ROWS_LATENT_PROJ = 256
TQ_LATENT = 256
ATTN_LOOKAHEAD = 2
MXU_SUM_MIN_KEYS = 1024
RET_GROUP = 4
RET_CHUNKS_PER_STEP = 8

F32 = jnp.float32
BF16 = jnp.bfloat16


def _dot(a, b):
    return jnp.dot(a, b, preferred_element_type=F32)


def _dot_nt(a, b):
    return lax.dot_general(a, b, (((1,), (1,)), ((), ())), preferred_element_type=F32)


def _dot_tn(a, b):
    return lax.dot_general(a, b, (((0,), (0,)), ((), ())), preferred_element_type=F32)


def _sub_tiles(tm, sub_rows=SUB_ROWS):
    n = max(1, tm // sub_rows)
    step = tm // n
    return [slice(i * step, (i + 1) * step) for i in range(n)]


def _layernorm(x, g, b):
    mu = jnp.mean(x, -1, keepdims=True)
    d = x - mu
    var = jnp.mean(d * d, -1, keepdims=True)
    return d * lax.rsqrt(var + NORM_EPS) * g + b


def _rmsnorm(x, g):
    return x * lax.rsqrt(jnp.mean(x * x, -1, keepdims=True) + NORM_EPS) * g


def _seg_sum(x, ones_bd):
    hi = x.astype(BF16)
    lo = (x - hi.astype(F32)).astype(BF16)
    return _dot(hi, ones_bd) + _dot(lo, ones_bd)


def _rope_block(x, cos, sin, half, first):
    rot = jnp.where(first, pltpu.roll(x, LANES - half, 1), pltpu.roll(x, half, 1))
    return x * cos + rot * sin


def _ada_kernel(cond_ref, w_ref, b_ref, o_ref):
    cnd = cond_ref[...]
    s = (cnd * jax.nn.sigmoid(cnd)).astype(BF16)
    o_ref[0] = _dot(s, w_ref[0].astype(BF16)) + b_ref[0]


def _ada(cond8, w_ada, b_ada):
    tn = 1024
    n = w_ada.shape[-1]
    return pl.pallas_call(
        _ada_kernel,
        grid=(DEPTH, n // tn),
        in_specs=[pl.BlockSpec((8, D_MODEL), lambda l, j: (0, 0)),
                  pl.BlockSpec((1, D_MODEL, tn), lambda l, j: (l, 0, j)),
                  pl.BlockSpec((1, 1, tn), lambda l, j: (l, 0, j))],
        out_specs=pl.BlockSpec((1, 8, tn), lambda l, j: (l, 0, j)),
        out_shape=jax.ShapeDtypeStruct((DEPTH, 8, n), F32),
        compiler_params=pltpu.CompilerParams(dimension_semantics=("arbitrary", "arbitrary")),
        name="ada_mod",
    )(cond8, w_ada, b_ada.reshape(DEPTH, 1, n))


def _store_per_seq(o_ref, val, row0, transposed):
    seq = o_ref.shape[2] if transposed else o_ref.shape[1]
    for i in range(val.shape[0] // seq):
        blk = val[i * seq:(i + 1) * seq, :]
        o_ref[row0 // seq + i] = blk.T[0:o_ref.shape[1], :] if transposed else blk


def _proj_kernel(*refs, latent, n_alias):
    refs = refs[:9] + refs[9 + n_alias:]
    x_ref, mod_ref, wmix_ref = refs[:3]
    sh1 = mod_ref[0, 0:1, :]
    sc1 = mod_ref[0, 1:2, :]
    subs = _sub_tiles(x_ref.shape[0], SUB_ROWS_LATENT_PROJ if latent else SUB_ROWS)
    ps = [_dot_nt((x_ref[r, :] * (1.0 + sc1) + sh1).astype(BF16), wmix_ref[0]) for r in subs]
    for r, p in zip(subs, ps):
        _proj_post(r, p, refs, latent)


def _proj_post(r, p, refs, latent):
    if latent:
        (_, _, _, qn_ref, wuq_ref, kvn_ref, gqn_ref, gkn_ref, bd_ref,
         cq_ref, sq_ref, ck_ref, sk_ref, ch_ref, sh_ref, fq_ref, fkv_ref, fr_ref) = refs
    else:
        (_, _, _, qn_ref, wuq_ref, kvn_ref, gqn_ref, gkn_ref, bd_ref,
         fq_ref, fkv_ref, fr_ref, ockv_ref, okpe_ref, owk_ref, owv_ref, ogk_ref, ogv_ref) = refs
    rows = p.shape[0]
    lane = lax.broadcasted_iota(jnp.int32, (rows, LANES), 1)
    first_head = (lane % HEAD_DIM) < (HEAD_DIM // 2)

    def rope_heads(v):
        if not latent:
            return v
        cos, sin = ch_ref[r, :], sh_ref[r, :]
        blocks = [_rope_block(v[:, j:j + LANES], cos, sin, HEAD_DIM // 2, first_head)
                  for j in range(0, v.shape[1], LANES)]
        return blocks[0] if len(blocks) == 1 else jnp.concatenate(blocks, axis=1)

    qn = _rmsnorm(p[:, P_QLAT:P_KVLAT], qn_ref[0]).astype(BF16)
    qa = _dot(qn, wuq_ref[0])
    if latent:
        cos, sin = cq_ref[r, :], sq_ref[r, :]
        first = (lane >= HEAD_DIM) & (lane < HEAD_DIM + MLA_ROPE // 2)
        qa = jnp.concatenate(
            [_rope_block(qa[:, j:j + LANES], cos, sin, MLA_ROPE // 2, first) for j in range(0, 512, LANES)],
            axis=1)
    fq_ref[r, 0:512] = (qa * MLA_SCALE).astype(BF16)

    ckv = _rmsnorm(p[:, P_KVLAT:P_KPE], kvn_ref[0])
    kpe = p[:, P_KPE:P_RET]
    if latent:
        first = lane < MLA_ROPE // 2
        kpe_r = _rope_block(kpe, ck_ref[r, :], sk_ref[r, :], MLA_ROPE // 2, first)
    else:
        kpe_r = kpe
        _store_per_seq(ockv_ref, ckv, r.start, False)
        _store_per_seq(okpe_ref, kpe, r.start, True)
    fkv_ref[r, 0:128] = ckv.astype(BF16)
    fkv_ref[r, 128:256] = kpe_r.astype(BF16)

    fr_ref[r, 0:256] = p[:, P_RET:P_RET + 256]
    fr_ref[r, 256:512] = p[:, P_RET + 256:P_RET + 512] * (HEAD_DIM ** -0.5)
    fr_ref[r, 512:1024] = p[:, P_RET + 512:P_WQ]

    fq_ref[r, 512:768] = (rope_heads(p[:, P_WQ:P_WK]) * ATT_SCALE).astype(BF16)
    wk = p[:, P_WK:P_WV]
    wv = p[:, P_WV:P_GQ]
    fkv_ref[r, 256:384] = rope_heads(wk).astype(BF16)
    fkv_ref[r, 384:512] = wv.astype(BF16)

    bd = bd_ref[...]
    gq = p[:, P_GQ:P_GK]
    gqn = gq * lax.rsqrt(_seg_sum(gq * gq, bd) * (1.0 / HEAD_DIM) + NORM_EPS) * gqn_ref[0]
    fq_ref[r, 768:1024] = (rope_heads(gqn) * ATT_SCALE).astype(BF16)
    gk = p[:, P_GK:P_GV]
    gkn = gk * lax.rsqrt(_seg_sum(gk * gk, bd[0:128, 0:128]) * (1.0 / HEAD_DIM) + NORM_EPS) * gkn_ref[0]
    gv = p[:, P_GV:P_END]
    fkv_ref[r, 512:640] = rope_heads(gkn).astype(BF16)
    fkv_ref[r, 640:768] = gv.astype(BF16)
    if not latent:
        _store_per_seq(owk_ref, wk, r.start, True)
        _store_per_seq(owv_ref, wv, r.start, True)
        _store_per_seq(ogk_ref, gkn, r.start, True)
        _store_per_seq(ogv_ref, gv, r.start, True)


def _layer_spec(layer, *s):
    return pl.BlockSpec((1,) + s, lambda t: (layer,) + (0,) * len(s), pipeline_mode=pl.Buffered(1))


def _mod_spec(tm, group_len):
    if group_len is None:
        return pl.BlockSpec((1, 6, D_MODEL), lambda t: (0, 0, 0))
    per = group_len // tm
    return pl.BlockSpec((1, 6, D_MODEL), lambda t: (1 + t // per, 0, 0))


def _proj(x, mod, wts, layer, rope, tm, group_len, ctx_seq=None, caches=None):
    latent = group_len is not None
    aliases = {}
    nt = x.shape[0]
    tok = lambda w: pl.BlockSpec((tm, w), lambda t: (t, 0))
    in_specs = [tok(D_MODEL), _mod_spec(tm, group_len),
                _layer_spec(layer, P_END, D_MODEL), _layer_spec(layer, 1, 256), _layer_spec(layer, 256, 512),
                _layer_spec(layer, 1, 128), _layer_spec(layer, 1, 256), _layer_spec(layer, 1, 128),
                pl.BlockSpec((256, 256), lambda t: (0, 0))]
    args = [x, mod, wts["wmix"], wts["qn"], wts["wuq"], wts["kvn"], wts["gqn"], wts["gkn"], wts["bd"]]
    out_specs = [tok(FQ_W), tok(FKV_W), tok(1024)]
    out_shape = [jax.ShapeDtypeStruct((nt, FQ_W), BF16), jax.ShapeDtypeStruct((nt, FKV_W), BF16),
                 jax.ShapeDtypeStruct((nt, 1024), F32)]
    if latent:
        per = group_len // tm
        in_specs += [pl.BlockSpec((tm, LANES), lambda t: (t % per, 0))] * 6
        args += list(rope)
    else:
        nb = tm // ctx_seq
        seqs = nt // ctx_seq
        sub = _sub_tiles(tm)[0]
        assert (sub.stop - sub.start) % ctx_seq == 0, "a sub-tile must hold whole context sequences"
        out_specs += [pl.BlockSpec((nb, None, ctx_seq, 128), lambda t: (t, layer, 0, 0))]
        out_shape += [jax.ShapeDtypeStruct((seqs, DEPTH, ctx_seq, 128), F32)]
        for r in (MLA_ROPE, 128, 128, 128, 128):
            out_specs.append(pl.BlockSpec((nb, None, r, ctx_seq), lambda t: (t, layer, 0, 0)))
            out_shape.append(jax.ShapeDtypeStruct((seqs, DEPTH, r, ctx_seq), F32))
        if caches is not None:
            aliases = {len(args) + i: 3 + i for i in range(len(caches))}
            in_specs += [pl.BlockSpec(memory_space=pl.ANY)] * len(caches)
            args += list(caches)
    return pl.pallas_call(
        functools.partial(_proj_kernel, latent=latent, n_alias=len(aliases)),
        grid=(nt // tm,), in_specs=in_specs, out_specs=out_specs, out_shape=out_shape,
        input_output_aliases=aliases,
        compiler_params=pltpu.CompilerParams(dimension_semantics=("arbitrary",), vmem_limit_bytes=VMEM_LIMIT),
        name="proj_latent" if latent else "proj_ctx",
    )(*args)


def _softmax_weights(parts, sink, mxu_sum):
    m = parts[0].max(-1, keepdims=True)
    for s in parts[1:]:
        m = jnp.maximum(m, s.max(-1, keepdims=True))
    if sink is not None:
        m = jnp.maximum(m, sink)
    es = [jnp.exp2(s - m) for s in parts]
    extra = None if sink is None else jnp.exp2(sink - m)
    if not mxu_sum:
        for e in es:
            extra = e.sum(-1, keepdims=True) if extra is None else extra + e.sum(-1, keepdims=True)
    return [e.astype(BF16) for e in es], extra


def _normalised(pv, extra):
    if pv.shape[1] == LANES:
        return pv * (1.0 / extra)
    den = pv[:, LANES:LANES + 1]
    if extra is not None:
        den = den + extra
    return pv[:, 0:LANES] * (1.0 / den)


def _window_start(qi, tq, t):
    return jnp.clip(qi * tq - WINDOW, 0, t - (tq + 2 * WINDOW))


def _band_bias(tq):
    span = tq + 2 * WINDOW
    r = (np.arange(2 * tq) % tq)[None, :, None]
    c = np.arange(span)[None, None, :]
    off = (np.arange(3) * WINDOW)[:, None, None]
    return jnp.asarray(np.where(np.abs(r + off - c) <= WINDOW, 0.0, NEG_INF), F32)


def _attn_kernel(*refs, t, tq, n_cache, layer):
    latent = n_cache > 0
    s_len = t + n_cache
    if latent:
        (fq_ref, fkv_ref, cckv_ref, ckpe_ref, cwk_ref, cwv_ref, cgk_ref, cgv_ref, bias_ref,
         wka_ref, wuv_ref, sink_ref, o_ref, ckpe_s, ka_s, va_s, kd_s, vd_s, vc_s, kcc_s, vcc_s) = refs
    else:
        (fq_ref, fkv_ref, wka_ref, wuv_ref, sink_ref, o_ref, ckpe_s, ka_s, va_s, kd_s, vd_s, vc_s) = refs

    qi = pl.program_id(1)

    @pl.when(qi == 0)
    def _():
        ckpe_s[0:t, :] = fkv_ref[0, :, 0:256]
        kd_s[0:t, :] = fkv_ref[0, :, 512:640]
        vd_s[0:t, 0:LANES] = fkv_ref[0, :, 640:768]
        vc_s[:, 0:LANES] = fkv_ref[0, :, 384:512]
        if latent:
            ckpe_s[t:s_len, 0:128] = cckv_ref[0, 0].astype(BF16)
            ckpe_s[t:s_len, 128:256] = ckpe_ref[0, 0].astype(BF16)
            kd_s[t:s_len, :] = cgk_ref[0, 0].astype(BF16)
            vd_s[t:s_len, 0:LANES] = cgv_ref[0, 0].astype(BF16)
            kcc_s[...] = cwk_ref[0, 0].astype(BF16)
            vcc_s[:, 0:LANES] = cwv_ref[0, 0].astype(BF16)
            vcc_s[:, LANES:] = jnp.ones((n_cache, LANES), BF16)
        vd_s[:, LANES:] = jnp.ones((s_len, LANES), BF16)
        vc_s[:, LANES:] = jnp.ones((t, LANES), BF16)
        ck = ckpe_s[...]
        va = _dot(ck, wuv_ref[0]).astype(BF16)
        for blk in range(2):
            va_s[blk, :, 0:LANES] = va[:, blk * LANES:(blk + 1) * LANES]
            va_s[blk, :, LANES:] = jnp.ones((s_len, LANES), BF16)
        for h in range(N_HEADS):
            ka_s[h] = _dot(ck, wka_ref[0, h]).astype(BF16)

    lane = lax.broadcasted_iota(jnp.int32, (tq, LANES), 1)
    half = [lane < HEAD_DIM, lane >= HEAD_DIM]

    def keep(x, j):
        return jnp.where(half[j], x, 0.0)

    def stacked_q(col, j):
        blks = [fq_ref[0, :, col + g * LANES:col + (g + 1) * LANES] for g in range(2)]
        return jnp.concatenate([keep(b.astype(F32), j).astype(BF16) for b in blks], axis=0)

    if latent:
        span = tq + 2 * WINDOW
        start = pl.multiple_of(_window_start(qi, tq, t), LANES)
    row1 = lax.broadcasted_iota(jnp.int32, (2 * tq, 1), 0)
    mxu_sum = s_len >= MXU_SUM_MIN_KEYS
    vw = 2 * LANES if mxu_sum else LANES

    acc_a = [jnp.zeros((tq, LANES), F32) for _ in range(2)]
    acc_c = [jnp.zeros((tq, LANES), F32) for _ in range(2)]
    acc_d = [jnp.zeros((tq, LANES), F32) for _ in range(2)]

    def a_scores(h):
        return [_dot_nt(fq_ref[0, :, h * LANES:(h + 1) * LANES], ka_s[h])]

    def a_finish(h, ps, sink_term):
        blk = h // 2
        acc_a[blk] = acc_a[blk] + keep(_normalised(_dot(ps[0], va_s[blk, :, 0:vw]), sink_term), h % 2)

    def c_scores(j):
        qs = stacked_q(512, j)
        if latent:
            return [_dot_nt(qs, fkv_ref[0, pl.ds(start, span), 256:384]) + bias_ref[0], _dot_nt(qs, kcc_s[...])]
        return [_dot_nt(qs, fkv_ref[0, :, 256:384])]

    def c_finish(j, ps, sink_term):
        if latent:
            pv = _dot(ps[0], vc_s[pl.ds(start, span), 0:vw]) + _dot(ps[1], vcc_s[:, 0:vw])
        else:
            pv = _dot(ps[0], vc_s[:, 0:vw])
        pv = _normalised(pv, sink_term)
        for g in range(2):
            acc_c[g] = acc_c[g] + keep(pv[g * tq:(g + 1) * tq], j)

    def d_scores(j):
        return [_dot_nt(stacked_q(768, j), kd_s[...])]

    def d_finish(j, ps, sink_term):
        pv = _normalised(_dot(ps[0], vd_s[:, 0:vw]), sink_term)
        for g in range(2):
            acc_d[g] = acc_d[g] + keep(pv[g * tq:(g + 1) * tq], j)

    def c_sink(j):
        return jnp.where(row1 < tq, sink_ref[layer, 2 * j], sink_ref[layer, 2 * j + 1]) * LOG2E

    jobs = [(functools.partial(a_scores, h), functools.partial(a_finish, h), None) for h in range(N_HEADS)]
    jobs += [(functools.partial(d_scores, j), functools.partial(d_finish, j), None) for j in range(2)]
    jobs += [(functools.partial(c_scores, j), functools.partial(c_finish, j), functools.partial(c_sink, j))
             for j in range(2)]
    pending = [job[0]() for job in jobs[:ATTN_LOOKAHEAD]]
    for i, (_, finish, sink) in enumerate(jobs):
        if i + ATTN_LOOKAHEAD < len(jobs):
            pending.append(jobs[i + ATTN_LOOKAHEAD][0]())
        finish(*_softmax_weights(pending.pop(0), None if sink is None else sink(), mxu_sum))

    for g in range(2):
        o_ref[0, :, g * LANES:(g + 1) * LANES] = acc_a[g].astype(BF16)
        o_ref[0, :, 256 + g * LANES:256 + (g + 1) * LANES] = acc_c[g].astype(BF16)
        o_ref[0, :, 512 + g * LANES:512 + (g + 1) * LANES] = acc_d[g].astype(BF16)


def _attn(fq, fkv, caches, wts, layer, tq):
    b, t, _ = fq.shape
    latent = caches is not None
    n_cache = caches[0].shape[2] if latent else 0
    s_len = t + n_cache
    in_specs = [pl.BlockSpec((1, tq, FQ_W), lambda i, q: (i, q, 0)),
                pl.BlockSpec((1, t, FKV_W), lambda i, q: (i, 0, 0))]
    args = [fq, fkv]
    if latent:
        in_specs += [pl.BlockSpec((1, 1, n_cache, LANES), lambda i, q: (i, layer, 0, 0))] * 6
        args += list(caches)
        span = tq + 2 * WINDOW
        assert t >= span and WINDOW <= tq
        in_specs.append(pl.BlockSpec((1, 2 * tq, span),
                                     lambda i, q: ((q * tq - _window_start(q, tq, t)) // WINDOW, 0, 0)))
        args.append(_band_bias(tq))
    in_specs += [pl.BlockSpec((1, N_HEADS, 256, 128), lambda i, q: (layer, 0, 0, 0)),
                 pl.BlockSpec((1, 256, 256), lambda i, q: (layer, 0, 0)),
                 pl.BlockSpec(memory_space=pltpu.SMEM)]
    args += [wts["wka"], wts["wuv"], wts["sink"]]
    scratch = [pltpu.VMEM((s_len, 256), BF16), pltpu.VMEM((N_HEADS, s_len, 128), BF16),
               pltpu.VMEM((2, s_len, 256), BF16), pltpu.VMEM((s_len, 128), BF16), pltpu.VMEM((s_len, 256), BF16),
               pltpu.VMEM((t, 256), BF16)]
    if latent:
        scratch += [pltpu.VMEM((n_cache, 128), BF16), pltpu.VMEM((n_cache, 256), BF16)]
    return pl.pallas_call(
        functools.partial(_attn_kernel, t=t, tq=tq, n_cache=n_cache, layer=layer),
        grid=(b, t // tq), in_specs=in_specs,
        out_specs=pl.BlockSpec((1, tq, 768), lambda i, q: (i, q, 0)),
        out_shape=jax.ShapeDtypeStruct((b, t, 768), BF16),
        scratch_shapes=scratch,
        compiler_params=pltpu.CompilerParams(dimension_semantics=("arbitrary", "arbitrary"),
                                             vmem_limit_bytes=VMEM_LIMIT),
        name="attn_latent" if latent else "attn_ctx",
    )(*args)


def _ret_kernel(*refs, t, bb, s0_stride, layer, want_state, n_alias):
    refs = refs[:7] + refs[7 + n_alias:]
    if want_state:
        (fr_ref, s0f_ref, s0b_ref, decf_ref, decb_ref, gain_ref, bd_ref, o_ref, sf_ref, sb_ref,
         dm_s, dec_s, cd_s, kvf_s, kvb_s, stf_s, stb_s, sf_s, sb_s) = refs
    else:
        (fr_ref, s0f_ref, s0b_ref, decf_ref, decb_ref, gain_ref, bd_ref, o_ref,
         dm_s, dec_s, cd_s, kvf_s, kvb_s, stf_s, stb_s, sf_s, sb_s) = refs
    nc = t // CHUNK
    w = BRANCH_W
    lane_w = lax.broadcasted_iota(jnp.int32, (CHUNK, w), 1) // HEAD_DIM

    @pl.when(pl.program_id(0) == 0)
    def _():
        row_w = lax.broadcasted_iota(jnp.int32, (CHUNK, w), 0).astype(F32)

        def lane_decay(dec_ref):
            v = jnp.zeros((CHUNK, w), F32)
            for h in range(N_HEADS):
                v = jnp.where(lane_w == h, dec_ref[layer, h], v)
            return jax.nn.log_sigmoid(v)

        lgf = lane_decay(decf_ref)
        lgb = lane_decay(decb_ref)
        dec_s[0] = jnp.exp((row_w + 1.0) * lgf)
        dec_s[1] = jnp.exp((CHUNK - 1.0 - row_w) * lgf)
        dec_s[2] = jnp.exp((CHUNK - row_w) * lgb)
        dec_s[3] = jnp.exp(row_w * lgb)
        cd_s[0] = jnp.concatenate([jnp.exp(CHUNK * lgf)] * (w // CHUNK), axis=0)
        cd_s[1] = jnp.concatenate([jnp.exp(CHUNK * lgb)] * (w // CHUNK), axis=0)
        ii = lax.broadcasted_iota(jnp.int32, (CHUNK, CHUNK), 0).astype(F32)
        jj = lax.broadcasted_iota(jnp.int32, (CHUNK, CHUNK), 1).astype(F32)
        diff = ii - jj
        for h in range(N_HEADS):
            lf = jax.nn.log_sigmoid(jnp.full((CHUNK, CHUNK), decf_ref[layer, h], F32))
            lb = jax.nn.log_sigmoid(jnp.full((CHUNK, CHUNK), decb_ref[layer, h], F32))
            d_f = jnp.where(diff >= 0, jnp.exp(jnp.maximum(diff, 0.0) * lf), 0.0)
            d_b = jnp.where(diff < 0, jnp.exp(jnp.maximum(-diff, 0.0) * lb), 0.0)
            dm_s[h] = d_f + d_b

    r2 = lax.broadcasted_iota(jnp.int32, (w, w), 0) // HEAD_DIM
    c2 = lax.broadcasted_iota(jnp.int32, (w, w), 1) // HEAD_DIM
    diag = r2 == c2

    jobs = [(bi, n) for bi in range(bb) for n in range(nc)]
    groups = [jobs[i:i + RET_GROUP] for i in range(0, len(jobs), RET_GROUP)]
    rows = lambda n: slice(n * CHUNK, (n + 1) * CHUNK)
    slot = lambda bi, n: bi * nc + n

    for grp in groups:
        kvs = []
        for bi, n in grp:
            k = fr_ref[bi, rows(n), 256:512]
            vb = fr_ref[bi, rows(n), 512:768].astype(BF16)
            kk = jnp.concatenate([k * dec_s[1], k * dec_s[3]], axis=1).astype(BF16)
            kvs.append(_dot_tn(kk, vb))
        for (bi, n), kv in zip(grp, kvs):
            kvf_s[slot(bi, n)] = jnp.where(diag, kv[0:w], 0.0)
            kvb_s[slot(bi, n)] = jnp.where(diag, kv[w:2 * w], 0.0)

    for bi in range(bb):
        sf_s[bi] = s0f_ref[bi * s0_stride]
        sb_s[bi] = s0b_ref[bi * s0_stride]
    for i in range(nc):
        m = nc - 1 - i
        for bi in range(bb):
            sf = sf_s[bi]
            stf_s[slot(bi, i)] = sf.astype(BF16)
            sf_s[bi] = sf * cd_s[0] + kvf_s[slot(bi, i)]
            sb = sb_s[bi]
            stb_s[slot(bi, m)] = sb.astype(BF16)
            sb_s[bi] = sb * cd_s[1] + kvb_s[slot(bi, m)]
    if want_state:
        for bi in range(bb):
            for h in range(N_HEADS):
                sl = slice(h * HEAD_DIM, (h + 1) * HEAD_DIM)
                sf_ref[bi, h] = sf_s[bi, sl, sl]
                sb_ref[bi, h] = sb_s[bi, sl, sl]

    gain = gain_ref[0]
    bd2 = jnp.concatenate([bd_ref[...], bd_ref[...]], axis=0)

    def seg_mean(xs):
        cat = []
        for x in xs:
            hi = x.astype(BF16)
            cat.append(jnp.concatenate([hi, (x - hi.astype(F32)).astype(BF16)], axis=1))
        return [_dot(c, bd2) * (1.0 / HEAD_DIM) for c in cat]

    for grp in groups:
        qs = [fr_ref[bi, rows(n), 0:256] for bi, n in grp]
        vs = [fr_ref[bi, rows(n), 512:768] for bi, n in grp]
        qks, inter_f, inter_b = [], [], []
        for (bi, n), q in zip(grp, qs):
            kb = fr_ref[bi, rows(n), 256:512].astype(BF16)
            qstack = jnp.concatenate([jnp.where(lane_w == h, q, 0.0) for h in range(N_HEADS)], axis=0)
            qks.append(_dot_nt(qstack.astype(BF16), kb))
            qb = q.astype(BF16)
            inter_f.append(_dot(qb, stf_s[slot(bi, n)]))
            inter_b.append(_dot(qb, stb_s[slot(bi, n)]))
        outs = []
        for qk, v, i_f, i_b in zip(qks, vs, inter_f, inter_b):
            att = jnp.concatenate([qk[rows(h)] * dm_s[h] for h in range(N_HEADS)], axis=1)
            vstack = jnp.concatenate([jnp.where(lane_w == h, v, 0.0) for h in range(N_HEADS)], axis=0)
            outs.append(_dot(att.astype(BF16), vstack.astype(BF16)) + i_f * dec_s[0] + i_b * dec_s[2])
        ds = [o - mu for o, mu in zip(outs, seg_mean(outs))]
        var = seg_mean([d * d for d in ds])
        for (bi, n), d, vr in zip(grp, ds, var):
            gate = fr_ref[bi, rows(n), 768:1024]
            y = d * lax.rsqrt(vr + NORM_EPS) * gain
            o_ref[bi, rows(n), :] = (gate * jax.nn.sigmoid(gate) * y).astype(BF16)


def _ret(fr, s0f, s0b, wts, layer, want_state, states=None):
    b, t, _ = fr.shape
    nc = t // CHUNK
    bb = max(1, RET_CHUNKS_PER_STEP // nc)
    shared = s0f.shape[0] != b
    s_spec = (pl.BlockSpec((1, BRANCH_W, BRANCH_W), lambda i: (0, 0, 0)) if shared
              else pl.BlockSpec((bb, BRANCH_W, BRANCH_W), lambda i: (i, 0, 0)))
    smem = pl.BlockSpec(memory_space=pltpu.SMEM)
    out_specs = [pl.BlockSpec((bb, t, BRANCH_W), lambda i: (i, 0, 0))]
    out_shape = [jax.ShapeDtypeStruct((b, t, BRANCH_W), BF16)]
    in_specs = [pl.BlockSpec((bb, t, 1024), lambda i: (i, 0, 0)), s_spec, s_spec, smem, smem,
                pl.BlockSpec((1, 1, BRANCH_W), lambda i: (layer, 0, 0)),
                pl.BlockSpec((BRANCH_W, BRANCH_W), lambda i: (0, 0))]
    args = [fr, s0f, s0b, wts["decf"], wts["decb"], wts["gn"], wts["bd"]]
    aliases = {}
    if want_state:
        st_spec = pl.BlockSpec((bb, None, N_HEADS, HEAD_DIM, HEAD_DIM), lambda i: (i, layer, 0, 0, 0))
        out_specs += [st_spec, st_spec]
        out_shape += [jax.ShapeDtypeStruct((b, DEPTH, N_HEADS, HEAD_DIM, HEAD_DIM), F32)] * 2
        if states is not None:
            aliases = {len(args) + i: 1 + i for i in range(2)}
            in_specs += [pl.BlockSpec(memory_space=pl.ANY)] * 2
            args += list(states)
    sq = (BRANCH_W, BRANCH_W)
    return pl.pallas_call(
        functools.partial(_ret_kernel, t=t, bb=bb, s0_stride=0 if shared else 1, layer=layer,
                          want_state=want_state, n_alias=len(aliases)),
        grid=(b // bb,),
        in_specs=in_specs, out_specs=out_specs, out_shape=out_shape, input_output_aliases=aliases,
        scratch_shapes=[pltpu.VMEM((N_HEADS, CHUNK, CHUNK), F32), pltpu.VMEM((4, CHUNK, BRANCH_W), F32),
                        pltpu.VMEM((2,) + sq, F32),
                        pltpu.VMEM((bb * nc,) + sq, F32), pltpu.VMEM((bb * nc,) + sq, F32),
                        pltpu.VMEM((bb * nc,) + sq, BF16), pltpu.VMEM((bb * nc,) + sq, BF16),
                        pltpu.VMEM((bb,) + sq, F32), pltpu.VMEM((bb,) + sq, F32)],
        compiler_params=pltpu.CompilerParams(dimension_semantics=("arbitrary",),
                                             vmem_limit_bytes=VMEM_LIMIT),
        name="retention",
    )(*args)


def _merge_kernel(x_ref, mod_ref, oatt_ref, oret_ref, wg_ref, wb_ref, wo_ref, g_ref, b_ref, o_ref):
    sh1 = mod_ref[0, 0:1, :]
    sc1 = mod_ref[0, 1:2, :]
    g1 = mod_ref[0, 2:3, :]
    subs = _sub_tiles(x_ref.shape[0])

    def gated(r):
        h = (x_ref[r, :] * (1.0 + sc1) + sh1).astype(BF16)
        branches = (oatt_ref[r, 0:256], oret_ref[r, :], oatt_ref[r, 256:512], oatt_ref[r, 512:768])
        tsum = None
        for i, o in enumerate(branches):
            gate = jax.nn.sigmoid(_dot_nt(h, wg_ref[0, i * D_MODEL:(i + 1) * D_MODEL, :]))
            term = gate * _dot(o, wb_ref[0, i])
            tsum = term if tsum is None else tsum + term
        return tsum

    def finish(r, y):
        o_ref[r, :] = _layernorm(ALPHA * x_ref[r, :] + g1 * y, g_ref[0], b_ref[0])

    sums = [gated(subs[0])]
    done = None
    for i, r in enumerate(subs):
        if i + 1 < len(subs):
            sums.append(gated(subs[i + 1]))
        y = _dot(sums.pop(0).astype(BF16), wo_ref[0])
        if done is not None:
            finish(*done)
        done = (r, y)
    finish(*done)


def _merge(x, mod, oatt, oret, wts, layer, tm, group_len):
    nt = x.shape[0]
    tok = lambda w: pl.BlockSpec((tm, w), lambda t: (t, 0))
    return pl.pallas_call(
        _merge_kernel,
        grid=(nt // tm,),
        in_specs=[tok(D_MODEL), _mod_spec(tm, group_len), tok(768), tok(BRANCH_W),
                  _layer_spec(layer, 4 * D_MODEL, D_MODEL), _layer_spec(layer, 4, BRANCH_W, D_MODEL),
                  _layer_spec(layer, D_MODEL, D_MODEL), _layer_spec(layer, 1, D_MODEL),
                  _layer_spec(layer, 1, D_MODEL)],
        out_specs=tok(D_MODEL),
        out_shape=jax.ShapeDtypeStruct((nt, D_MODEL), F32),
        compiler_params=pltpu.CompilerParams(dimension_semantics=("arbitrary",), vmem_limit_bytes=VMEM_LIMIT),
        name="merge",
    )(x, mod, oatt, oret, wts["wg"], wts["wb"], wts["wo"], wts["ln1g"], wts["ln1b"])


def _mlp_kernel(x_ref, mod_ref, wup_ref, wdn_ref, g_ref, b_ref, o_ref):
    sh2 = mod_ref[0, 3:4, :]
    sc2 = mod_ref[0, 4:5, :]
    g2 = mod_ref[0, 5:6, :]
    subs = _sub_tiles(x_ref.shape[0])

    def up(r):
        return _dot((x_ref[r, :] * (1.0 + sc2) + sh2).astype(BF16), wup_ref[0])

    def down(u):
        u = jnp.maximum(u, 0.0)
        return _dot((u * u).astype(BF16), wdn_ref[0])

    def finish(r, f):
        o_ref[r, :] = _layernorm(ALPHA * x_ref[r, :] + g2 * f, g_ref[0], b_ref[0])

    ups = [up(subs[0])]
    done = None
    for i, r in enumerate(subs):
        if i + 1 < len(subs):
            ups.append(up(subs[i + 1]))
        f = down(ups.pop(0))
        if done is not None:
            finish(*done)
        done = (r, f)
    finish(*done)


def _mlp(x, mod, wts, layer, tm, group_len):
    nt = x.shape[0]
    tok = lambda w: pl.BlockSpec((tm, w), lambda t: (t, 0))
    return pl.pallas_call(
        _mlp_kernel,
        grid=(nt // tm,),
        in_specs=[tok(D_MODEL), _mod_spec(tm, group_len), _layer_spec(layer, D_MODEL, D_FF),
                  _layer_spec(layer, D_FF, D_MODEL), _layer_spec(layer, 1, D_MODEL), _layer_spec(layer, 1, D_MODEL)],
        out_specs=tok(D_MODEL),
        out_shape=jax.ShapeDtypeStruct((nt, D_MODEL), F32),
        compiler_params=pltpu.CompilerParams(dimension_semantics=("arbitrary",), vmem_limit_bytes=VMEM_LIMIT),
        name="mlp",
    )(x, mod, wts["wup"], wts["wdn"], wts["ln2g"], wts["ln2b"])


def _post_kernel(x_ref, mod_ref, oatt_ref, oret_ref, wg_ref, wb_ref, wo_ref, g1_ref, b1_ref,
                 wup_ref, wdn_ref, g2_ref, b2_ref, o_ref):
    sh1, sc1, g1, sh2, sc2, g2 = [mod_ref[0, i:i + 1, :] for i in range(6)]
    subs = _sub_tiles(x_ref.shape[0])
    ff_half = D_FF // 2

    def gated(r):
        h = (x_ref[r, :] * (1.0 + sc1) + sh1).astype(BF16)
        branches = (oatt_ref[r, 0:256], oret_ref[r, :], oatt_ref[r, 256:512], oatt_ref[r, 512:768])
        tsum = None
        for i, o in enumerate(branches):
            gate = jax.nn.sigmoid(_dot_nt(h, wg_ref[0, i * D_MODEL:(i + 1) * D_MODEL, :]))
            term = gate * _dot(o, wb_ref[0, i])
            tsum = term if tsum is None else tsum + term
        return tsum

    sums = [gated(r) for r in subs]
    ys = [_dot(s.astype(BF16), wo_ref[0]) for s in sums]
    x1s = [_layernorm(ALPHA * x_ref[r, :] + g1 * y, g1_ref[0], b1_ref[0]) for r, y in zip(subs, ys)]
    hs = [(x1 * (1.0 + sc2) + sh2).astype(BF16) for x1 in x1s]
    us = [[_dot(h, wup_ref[0, :, c * ff_half:(c + 1) * ff_half]) for c in range(2)] for h in hs]
    fs = []
    for u2 in us:
        f = None
        for c, u in enumerate(u2):
            u = jnp.maximum(u, 0.0)
            part = _dot((u * u).astype(BF16), wdn_ref[0, c * ff_half:(c + 1) * ff_half, :])
            f = part if f is None else f + part
        fs.append(f)
    for r, x1, f in zip(subs, x1s, fs):
        o_ref[r, :] = _layernorm(ALPHA * x1 + g2 * f, g2_ref[0], b2_ref[0])


def _post(x, mod, oatt, oret, wts, layer, tm, group_len):
    nt = x.shape[0]
    tok = lambda w: pl.BlockSpec((tm, w), lambda t: (t, 0))
    vec = _layer_spec(layer, 1, D_MODEL)
    return pl.pallas_call(
        _post_kernel,
        grid=(nt // tm,),
        in_specs=[tok(D_MODEL), _mod_spec(tm, group_len), tok(768), tok(BRANCH_W),
                  _layer_spec(layer, 4 * D_MODEL, D_MODEL), _layer_spec(layer, 4, BRANCH_W, D_MODEL),
                  _layer_spec(layer, D_MODEL, D_MODEL), vec, vec,
                  _layer_spec(layer, D_MODEL, D_FF), _layer_spec(layer, D_FF, D_MODEL), vec, vec],
        out_specs=tok(D_MODEL),
        out_shape=jax.ShapeDtypeStruct((nt, D_MODEL), F32),
        compiler_params=pltpu.CompilerParams(dimension_semantics=("arbitrary",), vmem_limit_bytes=VMEM_LIMIT),
        name="merge_mlp",
    )(x, mod, oatt, oret, wts["wg"], wts["wb"], wts["wo"], wts["ln1g"], wts["ln1b"],
      wts["wup"], wts["wdn"], wts["ln2g"], wts["ln2b"])


def _win_kernel(w_ref, wmix_ref, wg_ref):
    o_ret = 256 + 128 + MLA_ROPE
    o_wq, o_wk, o_gq, o_gk, o_gate = o_ret + 1024, o_ret + 1280, o_ret + 1536, o_ret + 1792, o_ret + 2048

    def put(dst, src, n):
        wmix_ref[0, dst:dst + n, :] = w_ref[0, src:src + n, :].astype(BF16)

    def put_swapped(dst, src):
        for i, h in enumerate((0, 2, 1, 3)):
            put(dst + i * HEAD_DIM, src + h * HEAD_DIM, HEAD_DIM)

    put(P_QLAT, 0, o_ret)
    pad = P_RET - o_ret
    wmix_ref[0, o_ret:P_RET, :] = jnp.zeros((pad, w_ref.shape[2]), BF16)
    put(P_RET, o_ret, o_wq - o_ret)
    put_swapped(P_WQ, o_wq)
    put(P_WK, o_wk, o_gq - o_wk)
    put_swapped(P_GQ, o_gq)
    put(P_GK, o_gk, o_gate - o_gk)
    wg_ref[0] = w_ref[0, o_gate:, :].astype(BF16)


def _prep_win(w_in):
    w_t = jnp.swapaxes(w_in, 1, 2)
    n_in = w_t.shape[1]
    kb = D_MODEL // 2
    return pl.pallas_call(
        _win_kernel,
        grid=(DEPTH, D_MODEL // kb),
        in_specs=[pl.BlockSpec((1, n_in, kb), lambda l, k: (l, 0, k))],
        out_specs=[pl.BlockSpec((1, P_END, kb), lambda l, k: (l, 0, k)),
                   pl.BlockSpec((1, 4 * D_MODEL, kb), lambda l, k: (l, 0, k))],
        out_shape=[jax.ShapeDtypeStruct((DEPTH, P_END, D_MODEL), BF16),
                   jax.ShapeDtypeStruct((DEPTH, 4 * D_MODEL, D_MODEL), BF16)],
        compiler_params=pltpu.CompilerParams(dimension_semantics=("arbitrary", "arbitrary"),
                                             vmem_limit_bytes=VMEM_LIMIT),
        name="prep_w_in",
    )(w_t)


def _prep_weights(w_in, mla_q_norm, mla_w_uq, mla_kv_norm, mla_w_uk, mla_w_uv, ret_decay_fwd, ret_decay_bwd,
                  ret_gn_gain, win_sink, gqa_q_norm, gqa_k_norm, w_branch, w_o, ln1_g, ln1_b, w_up, w_down,
                  ln2_g, ln2_b):
    wmix, wg = _prep_win(w_in)
    wuq = jnp.pad(mla_w_uq.reshape(DEPTH, 256, N_HEADS, HEAD_DIM + MLA_ROPE),
                  ((0, 0), (0, 0), (0, 0), (0, LANES - HEAD_DIM - MLA_ROPE))
                  ).reshape(DEPTH, 256, N_HEADS * LANES).astype(BF16)
    uk = mla_w_uk.reshape(DEPTH, MLA_KV_RANK, N_HEADS, HEAD_DIM).transpose(0, 2, 1, 3)
    top = jnp.pad(uk, ((0, 0), (0, 0), (0, 0), (0, LANES - HEAD_DIM)))
    eye = np.zeros((LANES, LANES), np.float32)
    eye[np.arange(MLA_ROPE), HEAD_DIM + np.arange(MLA_ROPE)] = 1.0
    wka = jnp.concatenate([top, jnp.broadcast_to(eye, (DEPTH, N_HEADS, LANES, LANES))], axis=2).astype(BF16)
    wuv = jnp.pad(mla_w_uv, ((0, 0), (0, 256 - MLA_KV_RANK), (0, 0))).astype(BF16)
    bd = (np.arange(BRANCH_W)[:, None] // HEAD_DIM == np.arange(BRANCH_W)[None, :] // HEAD_DIM).astype(np.float32)
    wb_cd = jnp.swapaxes(w_branch[:, 2:].reshape(DEPTH, 2, 2, 2, HEAD_DIM, D_MODEL), 2, 3
                         ).reshape(DEPTH, 2, BRANCH_W, D_MODEL)
    w_branch = jnp.concatenate([w_branch[:, :2], wb_cd], axis=1)
    row = lambda a: a[:, None, :]
    return dict(
        wmix=wmix, wg=wg, wuq=wuq, wka=wka, wuv=wuv, bd=jnp.asarray(bd, BF16),
        qn=row(mla_q_norm), kvn=row(mla_kv_norm),
        gqn=row(jnp.tile(gqa_q_norm, (1, N_HEADS))), gkn=row(jnp.tile(gqa_k_norm, (1, 2))),
        decf=ret_decay_fwd, decb=ret_decay_bwd, gn=row(ret_gn_gain), sink=win_sink,
        wb=w_branch.astype(BF16), wo=w_o.astype(BF16), ln1g=row(ln1_g), ln1b=row(ln1_b),
        wup=w_up.astype(BF16), wdn=w_down.astype(BF16), ln2g=row(ln2_g), ln2b=row(ln2_b))


def _axial_tables(t, rot_dim):
    rows = t // GRID_W
    row = np.repeat(np.arange(rows, dtype=np.float32), GRID_W)
    col = (np.arange(t) % GRID_W).astype(np.float32)
    n_freq = rot_dim // 4
    inv = np.power(np.float32(ROPE_BASE), -np.arange(n_freq, dtype=np.float32) / np.float32(n_freq))
    ang = np.concatenate([row[:, None] * inv, col[:, None] * inv], axis=-1).astype(np.float32)
    return np.cos(ang), np.sin(ang)


def _rope_tables(t):
    ca, sa = _axial_tables(t, MLA_ROPE)
    ch, sh = _axial_tables(t, HEAD_DIM)
    one = lambda n: np.ones((t, n), np.float32)
    zero = lambda n: np.zeros((t, n), np.float32)
    cq = np.concatenate([one(HEAD_DIM), ca, ca, one(32)], axis=1)
    sq = np.concatenate([zero(HEAD_DIM), -sa, sa, zero(32)], axis=1)
    ck = np.concatenate([ca, ca, one(96)], axis=1)
    sk = np.concatenate([-sa, sa, zero(96)], axis=1)
    chh = np.concatenate([ch, ch, ch, ch], axis=1)
    shh = np.concatenate([-sh, sh, -sh, sh], axis=1)
    return tuple(jnp.asarray(a, F32) for a in (cq, sq, ck, sk, chh, shh))


def _block_diag(s):
    b = s.shape[0]
    same = np.eye(N_HEADS, dtype=bool)[None, :, None, :, None]
    return jnp.where(same, s[:, :, :, None, :], 0.0).reshape(b, BRANCH_W, BRANCH_W)


def kernel(x_prompt, x_sample, cache_mla_ckv, cache_mla_kpe, cache_win_k, cache_win_v, cache_gqa_k, cache_gqa_v,
           state_ret_fwd, state_ret_bwd, c, c_ctx, w_ada, b_ada, w_in, mla_q_norm, mla_w_uq, mla_kv_norm, mla_w_uk,
           mla_w_uv, ret_decay_fwd, ret_decay_bwd, ret_gn_gain, win_sink, gqa_q_norm, gqa_k_norm, w_branch, w_o,
           ln1_g, ln1_b, w_up, w_down, ln2_g, ln2_b):
    batch, seq, _ = x_prompt.shape
    dec_b, dec_t, _ = x_sample.shape
    past = cache_mla_ckv.shape[2]

    cond8 = jnp.concatenate([c_ctx[None], c, jnp.zeros((8 - 1 - dec_b, D_MODEL), F32)], axis=0)
    mod = _ada(cond8, w_ada, b_ada).reshape(DEPTH, 8, 6, D_MODEL)
    rope = _rope_tables(dec_t)
    caches = (cache_mla_ckv,
              jnp.pad(cache_mla_kpe, ((0, 0), (0, 0), (0, 0), (0, LANES - MLA_ROPE))),
              cache_win_k.reshape(dec_b, DEPTH, past, LANES), cache_win_v.reshape(dec_b, DEPTH, past, LANES),
              cache_gqa_k.reshape(dec_b, DEPTH, past, LANES), cache_gqa_v.reshape(dec_b, DEPTH, past, LANES))

    wts = _prep_weights(w_in, mla_q_norm, mla_w_uq, mla_kv_norm, mla_w_uk, mla_w_uv, ret_decay_fwd, ret_decay_bwd,
                        ret_gn_gain, win_sink, gqa_q_norm, gqa_k_norm, w_branch, w_o, ln1_g, ln1_b, w_up, w_down,
                        ln2_g, ln2_b)
    zero_state = jnp.zeros((1, BRANCH_W, BRANCH_W), F32)
    xp = x_prompt.reshape(batch * seq, D_MODEL)
    xs = x_sample.reshape(dec_b * dec_t, D_MODEL)
    per_b = lambda a: a.reshape(batch, seq, a.shape[-1])
    per_d = lambda a: a.reshape(dec_b, dec_t, a.shape[-1])
    flat = lambda a: a.reshape(-1, a.shape[-1])
    new_caches = None
    new_states = None
    for l in range(DEPTH):
        fq, fkv, fr, *new_caches = _proj(xp, mod[l], wts, l, None, TM_PROJ, None, seq, new_caches)
        oatt = _attn(per_b(fq), per_b(fkv), None, wts, l, seq)
        oret, *new_states = _ret(per_b(fr), zero_state, zero_state, wts, l, True, new_states)
        xp = _post(xp, mod[l], flat(oatt), flat(oret), wts, l, TM_POST, None)
        fq, fkv, fr = _proj(xs, mod[l], wts, l, rope, TM_PROJ, dec_t)
        oatt = _attn(per_d(fq), per_d(fkv), caches, wts, l, TQ_LATENT)
        (oret,) = _ret(per_d(fr), _block_diag(state_ret_fwd[:, l]), _block_diag(state_ret_bwd[:, l]), wts, l, False)
        xs = _post(xs, mod[l], flat(oatt), flat(oret), wts, l, TM_POST, dec_t)

    ckv, kpe, *kv = new_caches
    kv = [jnp.transpose(a.reshape(batch, DEPTH, 2, HEAD_DIM, seq), (0, 1, 4, 2, 3)) for a in kv]
    return (per_b(xp), per_d(xs), ckv, jnp.swapaxes(kpe, 2, 3), *kv, *new_states)
```

```python
import functools

import jax
import jax.numpy as jnp
import numpy as np
from jax import lax
from jax.experimental import pallas as pl
from jax.experimental.pallas import tpu as pltpu

D_MODEL = 1024
DEPTH = 2
GRID_W = 64
CHUNK = 128
WINDOW = 128
ROPE_BASE = 10000.0
NORM_EPS = 1e-6
NEG_INF = -1e30
HEAD_DIM = 64
N_HEADS = 4
BRANCH_W = 256
MLA_ROPE = 32
MLA_KV_RANK = 128
LOG2E = 1.4426950408889634
MLA_SCALE = (HEAD_DIM + MLA_ROPE) ** -0.5 * LOG2E
ATT_SCALE = HEAD_DIM ** -0.5 * LOG2E
D_FF = 4 * D_MODEL
ALPHA = (2.0 * DEPTH) ** 0.25
LANES = 128

P_QLAT, P_KVLAT, P_KPE, P_RET, P_WQ, P_WK, P_WV, P_GQ, P_GK, P_GV, P_END = (
    0, 256, 384, 512, 1536, 1792, 1920, 2048, 2304, 2432, 2560)
FQ_W = 1024
FKV_W = 768
VMEM_LIMIT = 60 * 1024 * 1024
TM_PROJ = 512
TM_POST = 512
SUB_ROWS = 256
SUB_ROWS_LATENT_PROJ = 256
TQ_LATENT = 256
ATTN_LOOKAHEAD = 2
MXU_SUM_MIN_KEYS = 1024
RET_GROUP = 4
RET_CHUNKS_PER_STEP = 8

F32 = jnp.float32
BF16 = jnp.bfloat16


def _dot(a, b):
    return jnp.dot(a, b, preferred_element_type=F32)


def _dot_nt(a, b):
    return lax.dot_general(a, b, (((1,), (1,)), ((), ())), preferred_element_type=F32)


def _dot_tn(a, b):
    return lax.dot_general(a, b, (((0,), (0,)), ((), ())), preferred_element_type=F32)


def _sub_tiles(tm, sub_rows=SUB_ROWS):
    n = max(1, tm // sub_rows)
    step = tm // n
    return [slice(i * step, (i + 1) * step) for i in range(n)]


def _layernorm(x, g, b):
    mu = jnp.mean(x, -1, keepdims=True)
    d = x - mu
    var = jnp.mean(d * d, -1, keepdims=True)
    return d * lax.rsqrt(var + NORM_EPS) * g + b


def _rmsnorm(x, g):
    return x * lax.rsqrt(jnp.mean(x * x, -1, keepdims=True) + NORM_EPS) * g


def _seg_sum(x, ones_bd):
    hi = x.astype(BF16)
    lo = (x - hi.astype(F32)).astype(BF16)
    return _dot(hi, ones_bd) + _dot(lo, ones_bd)


def _rope_block(x, cos, sin, half, first):
    rot = jnp.where(first, pltpu.roll(x, LANES - half, 1), pltpu.roll(x, half, 1))
    return x * cos + rot * sin


def _ada_kernel(cond_ref, w_ref, b_ref, o_ref):
    cnd = cond_ref[...]
    s = (cnd * jax.nn.sigmoid(cnd)).astype(BF16)
    o_ref[0] = _dot(s, w_ref[0].astype(BF16)) + b_ref[0]


def _ada(cond8, w_ada, b_ada):
    tn = 1024
    n = w_ada.shape[-1]
    return pl.pallas_call(
        _ada_kernel,
        grid=(DEPTH, n // tn),
        in_specs=[pl.BlockSpec((8, D_MODEL), lambda l, j: (0, 0)),
                  pl.BlockSpec((1, D_MODEL, tn), lambda l, j: (l, 0, j)),
                  pl.BlockSpec((1, 1, tn), lambda l, j: (l, 0, j))],
        out_specs=pl.BlockSpec((1, 8, tn), lambda l, j: (l, 0, j)),
        out_shape=jax.ShapeDtypeStruct((DEPTH, 8, n), F32),
        compiler_params=pltpu.CompilerParams(dimension_semantics=("arbitrary", "arbitrary")),
        name="ada_mod",
    )(cond8, w_ada, b_ada.reshape(DEPTH, 1, n))


def _store_per_seq(o_ref, val, row0, transposed):
    seq = o_ref.shape[2] if transposed else o_ref.shape[1]
    for i in range(val.shape[0] // seq):
        blk = val[i * seq:(i + 1) * seq, :]
        o_ref[row0 // seq + i] = blk.T[0:o_ref.shape[1], :] if transposed else blk


def _proj_kernel(*refs, latent, n_alias):
    refs = refs[:9] + refs[9 + n_alias:]
    x_ref, mod_ref, wmix_ref = refs[:3]
    sh1 = mod_ref[0, 0:1, :]
    sc1 = mod_ref[0, 1:2, :]
    subs = _sub_tiles(x_ref.shape[0], SUB_ROWS_LATENT_PROJ if latent else SUB_ROWS)
    ps = [_dot_nt((x_ref[r, :] * (1.0 + sc1) + sh1).astype(BF16), wmix_ref[0]) for r in subs]
    for r, p in zip(subs, ps):
        _proj_post(r, p, refs, latent)


def _proj_post(r, p, refs, latent):
    if latent:
        (_, _, _, qn_ref, wuq_ref, kvn_ref, gqn_ref, gkn_ref, bd_ref,
         cq_ref, sq_ref, ck_ref, sk_ref, ch_ref, sh_ref, fq_ref, fkv_ref, fr_ref) = refs
    else:
        (_, _, _, qn_ref, wuq_ref, kvn_ref, gqn_ref, gkn_ref, bd_ref,
         fq_ref, fkv_ref, fr_ref, ockv_ref, okpe_ref, owk_ref, owv_ref, ogk_ref, ogv_ref) = refs
    rows = p.shape[0]
    lane = lax.broadcasted_iota(jnp.int32, (rows, LANES), 1)
    first_head = (lane % HEAD_DIM) < (HEAD_DIM // 2)

    def rope_heads(v):
        if not latent:
            return v
        cos, sin = ch_ref[r, :], sh_ref[r, :]
        blocks = [_rope_block(v[:, j:j + LANES], cos, sin, HEAD_DIM // 2, first_head)
                  for j in range(0, v.shape[1], LANES)]
        return blocks[0] if len(blocks) == 1 else jnp.concatenate(blocks, axis=1)

    qn = _rmsnorm(p[:, P_QLAT:P_KVLAT], qn_ref[0]).astype(BF16)
    qa = _dot(qn, wuq_ref[0])
    if latent:
        cos, sin = cq_ref[r, :], sq_ref[r, :]
        first = (lane >= HEAD_DIM) & (lane < HEAD_DIM + MLA_ROPE // 2)
        qa = jnp.concatenate(
            [_rope_block(qa[:, j:j + LANES], cos, sin, MLA_ROPE // 2, first) for j in range(0, 512, LANES)],
            axis=1)
    fq_ref[r, 0:512] = (qa * MLA_SCALE).astype(BF16)

    ckv = _rmsnorm(p[:, P_KVLAT:P_KPE], kvn_ref[0])
    kpe = p[:, P_KPE:P_RET]
    if latent:
        first = lane < MLA_ROPE // 2
        kpe_r = _rope_block(kpe, ck_ref[r, :], sk_ref[r, :], MLA_ROPE // 2, first)
    else:
        kpe_r = kpe
        _store_per_seq(ockv_ref, ckv, r.start, False)
        _store_per_seq(okpe_ref, kpe, r.start, True)
    fkv_ref[r, 0:128] = ckv.astype(BF16)
    fkv_ref[r, 128:256] = kpe_r.astype(BF16)

    fr_ref[r, 0:256] = p[:, P_RET:P_RET + 256]
    fr_ref[r, 256:512] = p[:, P_RET + 256:P_RET + 512] * (HEAD_DIM ** -0.5)
    fr_ref[r, 512:1024] = p[:, P_RET + 512:P_WQ]

    fq_ref[r, 512:768] = (rope_heads(p[:, P_WQ:P_WK]) * ATT_SCALE).astype(BF16)
    wk = p[:, P_WK:P_WV]
    wv = p[:, P_WV:P_GQ]
    fkv_ref[r, 256:384] = rope_heads(wk).astype(BF16)
    fkv_ref[r, 384:512] = wv.astype(BF16)

    bd = bd_ref[...]
    gq = p[:, P_GQ:P_GK]
    gqn = gq * lax.rsqrt(_seg_sum(gq * gq, bd) * (1.0 / HEAD_DIM) + NORM_EPS) * gqn_ref[0]
    fq_ref[r, 768:1024] = (rope_heads(gqn) * ATT_SCALE).astype(BF16)
    gk = p[:, P_GK:P_GV]
    gkn = gk * lax.rsqrt(_seg_sum(gk * gk, bd[0:128, 0:128]) * (1.0 / HEAD_DIM) + NORM_EPS) * gkn_ref[0]
    gv = p[:, P_GV:P_END]
    fkv_ref[r, 512:640] = rope_heads(gkn).astype(BF16)
    fkv_ref[r, 640:768] = gv.astype(BF16)
    if not latent:
        _store_per_seq(owk_ref, wk, r.start, True)
        _store_per_seq(owv_ref, wv, r.start, True)
        _store_per_seq(ogk_ref, gkn, r.start, True)
        _store_per_seq(ogv_ref, gv, r.start, True)


def _layer_spec(layer, *s):
    return pl.BlockSpec((1,) + s, lambda t: (layer,) + (0,) * len(s), pipeline_mode=pl.Buffered(1))


def _mod_spec(tm, group_len):
    if group_len is None:
        return pl.BlockSpec((1, 6, D_MODEL), lambda t: (0, 0, 0))
    per = group_len // tm
    return pl.BlockSpec((1, 6, D_MODEL), lambda t: (1 + t // per, 0, 0))


def _proj(x, row0, nt, mod, wts, layer, rope, tm, group_len, ctx_seq=None, caches=None):
    latent = group_len is not None
    aliases = {}
    tile0 = row0 // tm
    tok = lambda w: pl.BlockSpec((tm, w), lambda t: (t, 0))
    in_specs = [pl.BlockSpec((tm, D_MODEL), lambda t: (t + tile0, 0)), _mod_spec(tm, group_len),
                _layer_spec(layer, P_END, D_MODEL), _layer_spec(layer, 1, 256), _layer_spec(layer, 256, 512),
                _layer_spec(layer, 1, 128), _layer_spec(layer, 1, 256), _layer_spec(layer, 1, 128),
                pl.BlockSpec((256, 256), lambda t: (0, 0))]
    args = [x, mod, wts["wmix"], wts["qn"], wts["wuq"], wts["kvn"], wts["gqn"], wts["gkn"], wts["bd"]]
    out_specs = [tok(FQ_W), tok(FKV_W), tok(1024)]
    out_shape = [jax.ShapeDtypeStruct((nt, FQ_W), BF16), jax.ShapeDtypeStruct((nt, FKV_W), BF16),
                 jax.ShapeDtypeStruct((nt, 1024), F32)]
    if latent:
        per = group_len // tm
        in_specs += [pl.BlockSpec((tm, LANES), lambda t: (t % per, 0))] * 6
        args += list(rope)
    else:
        nb = tm // ctx_seq
        seqs = nt // ctx_seq
        sub = _sub_tiles(tm)[0]
        assert (sub.stop - sub.start) % ctx_seq == 0, "a sub-tile must hold whole context sequences"
        out_specs += [pl.BlockSpec((nb, None, ctx_seq, 128), lambda t: (t, layer, 0, 0))]
        out_shape += [jax.ShapeDtypeStruct((seqs, DEPTH, ctx_seq, 128), F32)]
        for r in (MLA_ROPE, 128, 128, 128, 128):
            out_specs.append(pl.BlockSpec((nb, None, r, ctx_seq), lambda t: (t, layer, 0, 0)))
            out_shape.append(jax.ShapeDtypeStruct((seqs, DEPTH, r, ctx_seq), F32))
        if caches is not None:
            aliases = {len(args) + i: 3 + i for i in range(len(caches))}
            in_specs += [pl.BlockSpec(memory_space=pl.ANY)] * len(caches)
            args += list(caches)
    return pl.pallas_call(
        functools.partial(_proj_kernel, latent=latent, n_alias=len(aliases)),
        grid=(nt // tm,), in_specs=in_specs, out_specs=out_specs, out_shape=out_shape,
        input_output_aliases=aliases,
        compiler_params=pltpu.CompilerParams(dimension_semantics=("arbitrary",), vmem_limit_bytes=VMEM_LIMIT),
        name="proj_latent" if latent else "proj_ctx",
    )(*args)


def _softmax_weights(parts, sink, mxu_sum):
    m = parts[0].max(-1, keepdims=True)
    for s in parts[1:]:
        m = jnp.maximum(m, s.max(-1, keepdims=True))
    if sink is not None:
        m = jnp.maximum(m, sink)
    es = [jnp.exp2(s - m) for s in parts]
    extra = None if sink is None else jnp.exp2(sink - m)
    if not mxu_sum:
        for e in es:
            extra = e.sum(-1, keepdims=True) if extra is None else extra + e.sum(-1, keepdims=True)
    return [e.astype(BF16) for e in es], extra


def _normalised(pv, extra):
    if pv.shape[1] == LANES:
        return pv * (1.0 / extra)
    den = pv[:, LANES:LANES + 1]
    if extra is not None:
        den = den + extra
    return pv[:, 0:LANES] * (1.0 / den)


def _window_start(qi, tq, t):
    return jnp.clip(qi * tq - WINDOW, 0, t - (tq + 2 * WINDOW))


def _band_bias(tq):
    span = tq + 2 * WINDOW
    r = (np.arange(2 * tq) % tq)[None, :, None]
    c = np.arange(span)[None, None, :]
    off = (np.arange(3) * WINDOW)[:, None, None]
    return jnp.asarray(np.where(np.abs(r + off - c) <= WINDOW, 0.0, NEG_INF), F32)


def _attn_kernel(*refs, t, tq, n_cache, layer):
    latent = n_cache > 0
    s_len = t + n_cache
    if latent:
        (fq_ref, fkv_ref, cckv_ref, ckpe_ref, cwk_ref, cwv_ref, cgk_ref, cgv_ref, bias_ref,
         wka_ref, wuv_ref, sink_ref, o_ref, ckpe_s, ka_s, va_s, kd_s, vd_s, vc_s, kcc_s, vcc_s) = refs
    else:
        (fq_ref, fkv_ref, wka_ref, wuv_ref, sink_ref, o_ref, ckpe_s, ka_s, va_s, kd_s, vd_s, vc_s) = refs

    qi = pl.program_id(1)

    @pl.when(qi == 0)
    def _():
        ckpe_s[0:t, :] = fkv_ref[0, :, 0:256]
        kd_s[0:t, :] = fkv_ref[0, :, 512:640]
        vd_s[0:t, 0:LANES] = fkv_ref[0, :, 640:768]
        vc_s[:, 0:LANES] = fkv_ref[0, :, 384:512]
        if latent:
            ckpe_s[t:s_len, 0:128] = cckv_ref[0, 0].astype(BF16)
            ckpe_s[t:s_len, 128:256] = ckpe_ref[0, 0].astype(BF16)
            kd_s[t:s_len, :] = cgk_ref[0, 0].astype(BF16)
            vd_s[t:s_len, 0:LANES] = cgv_ref[0, 0].astype(BF16)
            kcc_s[...] = cwk_ref[0, 0].astype(BF16)
            vcc_s[:, 0:LANES] = cwv_ref[0, 0].astype(BF16)
            vcc_s[:, LANES:] = jnp.ones((n_cache, LANES), BF16)
        vd_s[:, LANES:] = jnp.ones((s_len, LANES), BF16)
        vc_s[:, LANES:] = jnp.ones((t, LANES), BF16)
        ck = ckpe_s[...]
        va = _dot(ck, wuv_ref[0]).astype(BF16)
        for blk in range(2):
            va_s[blk, :, 0:LANES] = va[:, blk * LANES:(blk + 1) * LANES]
            va_s[blk, :, LANES:] = jnp.ones((s_len, LANES), BF16)
        for h in range(N_HEADS):
            ka_s[h] = _dot(ck, wka_ref[0, h]).astype(BF16)

    lane = lax.broadcasted_iota(jnp.int32, (tq, LANES), 1)
    half = [lane < HEAD_DIM, lane >= HEAD_DIM]

    def keep(x, j):
        return jnp.where(half[j], x, 0.0)

    def stacked_q(col, j):
        blks = [fq_ref[0, :, col + g * LANES:col + (g + 1) * LANES] for g in range(2)]
        return jnp.concatenate([keep(b.astype(F32), j).astype(BF16) for b in blks], axis=0)

    if latent:
        span = tq + 2 * WINDOW
        start = pl.multiple_of(_window_start(qi, tq, t), LANES)
    row1 = lax.broadcasted_iota(jnp.int32, (2 * tq, 1), 0)
    mxu_sum = s_len >= MXU_SUM_MIN_KEYS
    vw = 2 * LANES if mxu_sum else LANES

    acc_a = [jnp.zeros((tq, LANES), F32) for _ in range(2)]
    acc_c = [jnp.zeros((tq, LANES), F32) for _ in range(2)]
    acc_d = [jnp.zeros((tq, LANES), F32) for _ in range(2)]

    def a_scores(h):
        return [_dot_nt(fq_ref[0, :, h * LANES:(h + 1) * LANES], ka_s[h])]

    def a_finish(h, ps, sink_term):
        blk = h // 2
        acc_a[blk] = acc_a[blk] + keep(_normalised(_dot(ps[0], va_s[blk, :, 0:vw]), sink_term), h % 2)

    def c_scores(j):
        qs = stacked_q(512, j)
        if latent:
            return [_dot_nt(qs, fkv_ref[0, pl.ds(start, span), 256:384]) + bias_ref[0], _dot_nt(qs, kcc_s[...])]
        return [_dot_nt(qs, fkv_ref[0, :, 256:384])]

    def c_finish(j, ps, sink_term):
        if latent:
            pv = _dot(ps[0], vc_s[pl.ds(start, span), 0:vw]) + _dot(ps[1], vcc_s[:, 0:vw])
        else:
            pv = _dot(ps[0], vc_s[:, 0:vw])
        pv = _normalised(pv, sink_term)
        for g in range(2):
            acc_c[g] = acc_c[g] + keep(pv[g * tq:(g + 1) * tq], j)

    def d_scores(j):
        return [_dot_nt(stacked_q(768, j), kd_s[...])]

    def d_finish(j, ps, sink_term):
        pv = _normalised(_dot(ps[0], vd_s[:, 0:vw]), sink_term)
        for g in range(2):
            acc_d[g] = acc_d[g] + keep(pv[g * tq:(g + 1) * tq], j)

    def c_sink(j):
        return jnp.where(row1 < tq, sink_ref[layer, 2 * j], sink_ref[layer, 2 * j + 1]) * LOG2E

    jobs = [(functools.partial(a_scores, h), functools.partial(a_finish, h), None) for h in range(N_HEADS)]
    jobs += [(functools.partial(d_scores, j), functools.partial(d_finish, j), None) for j in range(2)]
    jobs += [(functools.partial(c_scores, j), functools.partial(c_finish, j), functools.partial(c_sink, j))
             for j in range(2)]
    pending = [job[0]() for job in jobs[:ATTN_LOOKAHEAD]]
    for i, (_, finish, sink) in enumerate(jobs):
        if i + ATTN_LOOKAHEAD < len(jobs):
            pending.append(jobs[i + ATTN_LOOKAHEAD][0]())
        finish(*_softmax_weights(pending.pop(0), None if sink is None else sink(), mxu_sum))

    for g in range(2):
        o_ref[0, :, g * LANES:(g + 1) * LANES] = acc_a[g].astype(BF16)
        o_ref[0, :, 256 + g * LANES:256 + (g + 1) * LANES] = acc_c[g].astype(BF16)
        o_ref[0, :, 512 + g * LANES:512 + (g + 1) * LANES] = acc_d[g].astype(BF16)


def _attn(fq, fkv, caches, wts, layer, tq):
    b, t, _ = fq.shape
    latent = caches is not None
    n_cache = caches[0].shape[2] if latent else 0
    s_len = t + n_cache
    in_specs = [pl.BlockSpec((1, tq, FQ_W), lambda i, q: (i, q, 0)),
                pl.BlockSpec((1, t, FKV_W), lambda i, q: (i, 0, 0))]
    args = [fq, fkv]
    if latent:
        in_specs += [pl.BlockSpec((1, 1, n_cache, LANES), lambda i, q: (i, layer, 0, 0))] * 6
        args += list(caches)
        span = tq + 2 * WINDOW
        assert t >= span and WINDOW <= tq
        in_specs.append(pl.BlockSpec((1, 2 * tq, span),
                                     lambda i, q: ((q * tq - _window_start(q, tq, t)) // WINDOW, 0, 0)))
        args.append(_band_bias(tq))
    in_specs += [pl.BlockSpec((1, N_HEADS, 256, 128), lambda i, q: (layer, 0, 0, 0)),
                 pl.BlockSpec((1, 256, 256), lambda i, q: (layer, 0, 0)),
                 pl.BlockSpec(memory_space=pltpu.SMEM)]
    args += [wts["wka"], wts["wuv"], wts["sink"]]
    scratch = [pltpu.VMEM((s_len, 256), BF16), pltpu.VMEM((N_HEADS, s_len, 128), BF16),
               pltpu.VMEM((2, s_len, 256), BF16), pltpu.VMEM((s_len, 128), BF16), pltpu.VMEM((s_len, 256), BF16),
               pltpu.VMEM((t, 256), BF16)]
    if latent:
        scratch += [pltpu.VMEM((n_cache, 128), BF16), pltpu.VMEM((n_cache, 256), BF16)]
    return pl.pallas_call(
        functools.partial(_attn_kernel, t=t, tq=tq, n_cache=n_cache, layer=layer),
        grid=(b, t // tq), in_specs=in_specs,
        out_specs=pl.BlockSpec((1, tq, 768), lambda i, q: (i, q, 0)),
        out_shape=jax.ShapeDtypeStruct((b, t, 768), BF16),
        scratch_shapes=scratch,
        compiler_params=pltpu.CompilerParams(dimension_semantics=("arbitrary", "arbitrary"),
                                             vmem_limit_bytes=VMEM_LIMIT),
        name="attn_latent" if latent else "attn_ctx",
    )(*args)


def _ret_kernel(*refs, t, bb, s0_stride, layer, want_state, n_alias):
    refs = refs[:7] + refs[7 + n_alias:]
    if want_state:
        (fr_ref, s0f_ref, s0b_ref, decf_ref, decb_ref, gain_ref, bd_ref, o_ref, sf_ref, sb_ref,
         dm_s, dec_s, cd_s, kvf_s, kvb_s, stf_s, stb_s, sf_s, sb_s) = refs
    else:
        (fr_ref, s0f_ref, s0b_ref, decf_ref, decb_ref, gain_ref, bd_ref, o_ref,
         dm_s, dec_s, cd_s, kvf_s, kvb_s, stf_s, stb_s, sf_s, sb_s) = refs
    nc = t // CHUNK
    w = BRANCH_W
    lane_w = lax.broadcasted_iota(jnp.int32, (CHUNK, w), 1) // HEAD_DIM

    @pl.when(pl.program_id(0) == 0)
    def _():
        row_w = lax.broadcasted_iota(jnp.int32, (CHUNK, w), 0).astype(F32)

        def lane_decay(dec_ref):
            v = jnp.zeros((CHUNK, w), F32)
            for h in range(N_HEADS):
                v = jnp.where(lane_w == h, dec_ref[layer, h], v)
            return jax.nn.log_sigmoid(v)

        lgf = lane_decay(decf_ref)
        lgb = lane_decay(decb_ref)
        dec_s[0] = jnp.exp((row_w + 1.0) * lgf)
        dec_s[1] = jnp.exp((CHUNK - 1.0 - row_w) * lgf)
        dec_s[2] = jnp.exp((CHUNK - row_w) * lgb)
        dec_s[3] = jnp.exp(row_w * lgb)
        cd_s[0] = jnp.concatenate([jnp.exp(CHUNK * lgf)] * (w // CHUNK), axis=0)
        cd_s[1] = jnp.concatenate([jnp.exp(CHUNK * lgb)] * (w // CHUNK), axis=0)
        ii = lax.broadcasted_iota(jnp.int32, (CHUNK, CHUNK), 0).astype(F32)
        jj = lax.broadcasted_iota(jnp.int32, (CHUNK, CHUNK), 1).astype(F32)
        diff = ii - jj
        for h in range(N_HEADS):
            lf = jax.nn.log_sigmoid(jnp.full((CHUNK, CHUNK), decf_ref[layer, h], F32))
            lb = jax.nn.log_sigmoid(jnp.full((CHUNK, CHUNK), decb_ref[layer, h], F32))
            d_f = jnp.where(diff >= 0, jnp.exp(jnp.maximum(diff, 0.0) * lf), 0.0)
            d_b = jnp.where(diff < 0, jnp.exp(jnp.maximum(-diff, 0.0) * lb), 0.0)
            dm_s[h] = d_f + d_b

    r2 = lax.broadcasted_iota(jnp.int32, (w, w), 0) // HEAD_DIM
    c2 = lax.broadcasted_iota(jnp.int32, (w, w), 1) // HEAD_DIM
    diag = r2 == c2

    jobs = [(bi, n) for bi in range(bb) for n in range(nc)]
    groups = [jobs[i:i + RET_GROUP] for i in range(0, len(jobs), RET_GROUP)]
    rows = lambda n: slice(n * CHUNK, (n + 1) * CHUNK)
    slot = lambda bi, n: bi * nc + n

    for grp in groups:
        kvs = []
        for bi, n in grp:
            k = fr_ref[bi, rows(n), 256:512]
            vb = fr_ref[bi, rows(n), 512:768].astype(BF16)
            kk = jnp.concatenate([k * dec_s[1], k * dec_s[3]], axis=1).astype(BF16)
            kvs.append(_dot_tn(kk, vb))
        for (bi, n), kv in zip(grp, kvs):
            kvf_s[slot(bi, n)] = jnp.where(diag, kv[0:w], 0.0)
            kvb_s[slot(bi, n)] = jnp.where(diag, kv[w:2 * w], 0.0)

    for bi in range(bb):
        sf_s[bi] = s0f_ref[bi * s0_stride]
        sb_s[bi] = s0b_ref[bi * s0_stride]
    for i in range(nc):
        m = nc - 1 - i
        for bi in range(bb):
            sf = sf_s[bi]
            stf_s[slot(bi, i)] = sf.astype(BF16)
            sf_s[bi] = sf * cd_s[0] + kvf_s[slot(bi, i)]
            sb = sb_s[bi]
            stb_s[slot(bi, m)] = sb.astype(BF16)
            sb_s[bi] = sb * cd_s[1] + kvb_s[slot(bi, m)]
    if want_state:
        for bi in range(bb):
            for h in range(N_HEADS):
                sl = slice(h * HEAD_DIM, (h + 1) * HEAD_DIM)
                sf_ref[bi, h] = sf_s[bi, sl, sl]
                sb_ref[bi, h] = sb_s[bi, sl, sl]

    gain = gain_ref[0]
    bd2 = jnp.concatenate([bd_ref[...], bd_ref[...]], axis=0)

    def seg_mean(xs):
        cat = []
        for x in xs:
            hi = x.astype(BF16)
            cat.append(jnp.concatenate([hi, (x - hi.astype(F32)).astype(BF16)], axis=1))
        return [_dot(c, bd2) * (1.0 / HEAD_DIM) for c in cat]

    for grp in groups:
        qs = [fr_ref[bi, rows(n), 0:256] for bi, n in grp]
        vs = [fr_ref[bi, rows(n), 512:768] for bi, n in grp]
        qks, inter_f, inter_b = [], [], []
        for (bi, n), q in zip(grp, qs):
            kb = fr_ref[bi, rows(n), 256:512].astype(BF16)
            qstack = jnp.concatenate([jnp.where(lane_w == h, q, 0.0) for h in range(N_HEADS)], axis=0)
            qks.append(_dot_nt(qstack.astype(BF16), kb))
            qb = q.astype(BF16)
            inter_f.append(_dot(qb, stf_s[slot(bi, n)]))
            inter_b.append(_dot(qb, stb_s[slot(bi, n)]))
        outs = []
        for qk, v, i_f, i_b in zip(qks, vs, inter_f, inter_b):
            att = jnp.concatenate([qk[rows(h)] * dm_s[h] for h in range(N_HEADS)], axis=1)
            vstack = jnp.concatenate([jnp.where(lane_w == h, v, 0.0) for h in range(N_HEADS)], axis=0)
            outs.append(_dot(att.astype(BF16), vstack.astype(BF16)) + i_f * dec_s[0] + i_b * dec_s[2])
        ds = [o - mu for o, mu in zip(outs, seg_mean(outs))]
        var = seg_mean([d * d for d in ds])
        for (bi, n), d, vr in zip(grp, ds, var):
            gate = fr_ref[bi, rows(n), 768:1024]
            y = d * lax.rsqrt(vr + NORM_EPS) * gain
            o_ref[bi, rows(n), :] = (gate * jax.nn.sigmoid(gate) * y).astype(BF16)


def _ret(fr, s0f, s0b, wts, layer, want_state, states=None):
    b, t, _ = fr.shape
    nc = t // CHUNK
    bb = max(1, RET_CHUNKS_PER_STEP // nc)
    shared = s0f.shape[0] != b
    s_spec = (pl.BlockSpec((1, BRANCH_W, BRANCH_W), lambda i: (0, 0, 0)) if shared
              else pl.BlockSpec((bb, BRANCH_W, BRANCH_W), lambda i: (i, 0, 0)))
    smem = pl.BlockSpec(memory_space=pltpu.SMEM)
    out_specs = [pl.BlockSpec((bb, t, BRANCH_W), lambda i: (i, 0, 0))]
    out_shape = [jax.ShapeDtypeStruct((b, t, BRANCH_W), BF16)]
    in_specs = [pl.BlockSpec((bb, t, 1024), lambda i: (i, 0, 0)), s_spec, s_spec, smem, smem,
                pl.BlockSpec((1, 1, BRANCH_W), lambda i: (layer, 0, 0)),
                pl.BlockSpec((BRANCH_W, BRANCH_W), lambda i: (0, 0))]
    args = [fr, s0f, s0b, wts["decf"], wts["decb"], wts["gn"], wts["bd"]]
    aliases = {}
    if want_state:
        st_spec = pl.BlockSpec((bb, None, N_HEADS, HEAD_DIM, HEAD_DIM), lambda i: (i, layer, 0, 0, 0))
        out_specs += [st_spec, st_spec]
        out_shape += [jax.ShapeDtypeStruct((b, DEPTH, N_HEADS, HEAD_DIM, HEAD_DIM), F32)] * 2
        if states is not None:
            aliases = {len(args) + i: 1 + i for i in range(2)}
            in_specs += [pl.BlockSpec(memory_space=pl.ANY)] * 2
            args += list(states)
    sq = (BRANCH_W, BRANCH_W)
    return pl.pallas_call(
        functools.partial(_ret_kernel, t=t, bb=bb, s0_stride=0 if shared else 1, layer=layer,
                          want_state=want_state, n_alias=len(aliases)),
        grid=(b // bb,),
        in_specs=in_specs, out_specs=out_specs, out_shape=out_shape, input_output_aliases=aliases,
        scratch_shapes=[pltpu.VMEM((N_HEADS, CHUNK, CHUNK), F32), pltpu.VMEM((4, CHUNK, BRANCH_W), F32),
                        pltpu.VMEM((2,) + sq, F32),
                        pltpu.VMEM((bb * nc,) + sq, F32), pltpu.VMEM((bb * nc,) + sq, F32),
                        pltpu.VMEM((bb * nc,) + sq, BF16), pltpu.VMEM((bb * nc,) + sq, BF16),
                        pltpu.VMEM((bb,) + sq, F32), pltpu.VMEM((bb,) + sq, F32)],
        compiler_params=pltpu.CompilerParams(dimension_semantics=("arbitrary",),
                                             vmem_limit_bytes=VMEM_LIMIT),
        name="retention",
    )(*args)


def _post_kernel(*refs, n_x, n_out, ctx_tiles):
    x_refs, refs = refs[:n_x], refs[n_x:]
    (mod_ref, oatt_c, oatt_l, oret_c, oret_l, wg_ref, wb_ref, wo_ref, g1_ref, b1_ref,
     wup_ref, wdn_ref, g2_ref, b2_ref) = refs[:14]
    o_refs = refs[14:]
    is_ctx = pl.program_id(0) < ctx_tiles
    pick = lambda c_ref, l_ref, r, cols: jnp.where(is_ctx, c_ref[r, cols], l_ref[r, cols])
    every = slice(None)
    if n_x == 1:
        x_rows = lambda r: x_refs[0][r, :]
    else:
        x_rows = lambda r: pick(x_refs[0], x_refs[1], r, every)
    sh1, sc1, g1, sh2, sc2, g2 = [mod_ref[0, i:i + 1, :] for i in range(6)]
    subs = _sub_tiles(x_refs[0].shape[0])
    ff_half = D_FF // 2
    xs = [x_rows(r) for r in subs]

    def gated(r, x):
        h = (x * (1.0 + sc1) + sh1).astype(BF16)
        branches = (pick(oatt_c, oatt_l, r, slice(0, 256)), pick(oret_c, oret_l, r, every),
                    pick(oatt_c, oatt_l, r, slice(256, 512)), pick(oatt_c, oatt_l, r, slice(512, 768)))
        tsum = None
        for i, o in enumerate(branches):
            gate = jax.nn.sigmoid(_dot_nt(h, wg_ref[0, i * D_MODEL:(i + 1) * D_MODEL, :]))
            term = gate * _dot(o, wb_ref[0, i])
            tsum = term if tsum is None else tsum + term
        return tsum

    sums = [gated(r, x) for r, x in zip(subs, xs)]
    ys = [_dot(s.astype(BF16), wo_ref[0]) for s in sums]
    x1s = [_layernorm(ALPHA * x + g1 * y, g1_ref[0], b1_ref[0]) for x, y in zip(xs, ys)]
    hs = [(x1 * (1.0 + sc2) + sh2).astype(BF16) for x1 in x1s]
    us = [[_dot(h, wup_ref[0, :, c * ff_half:(c + 1) * ff_half]) for c in range(2)] for h in hs]
    fs = []
    for u2 in us:
        f = None
        for c, u in enumerate(u2):
            u = jnp.maximum(u, 0.0)
            part = _dot((u * u).astype(BF16), wdn_ref[0, c * ff_half:(c + 1) * ff_half, :])
            f = part if f is None else f + part
        fs.append(f)
    outs = [_layernorm(ALPHA * x1 + g2 * f, g2_ref[0], b2_ref[0]) for x1, f in zip(x1s, fs)]
    if n_out == 1:
        for r, o in zip(subs, outs):
            o_refs[0][r, :] = o
    else:
        @pl.when(is_ctx)
        def _():
            for r, o in zip(subs, outs):
                o_refs[0][r, :] = o

        @pl.when(jnp.logical_not(is_ctx))
        def _():
            for r, o in zip(subs, outs):
                o_refs[1][r, :] = o


def _post(xs, mod, oatts, orets, wts, layer, tm, latent_len, split_out):
    n_ctx, n_lat = oatts[0].shape[0], oatts[1].shape[0]
    ctx_tiles, lat_tiles = n_ctx // tm, n_lat // tm
    per = latent_len // tm
    all_tok = lambda w: pl.BlockSpec((tm, w), lambda t: (t, 0))
    ctx_tok = lambda w: pl.BlockSpec((tm, w), lambda t: (jnp.minimum(t, ctx_tiles - 1), 0))
    lat_tok = lambda w: pl.BlockSpec((tm, w), lambda t: (jnp.maximum(t - ctx_tiles, 0), 0))
    pair = lambda w: [ctx_tok(w), lat_tok(w)]
    vec = _layer_spec(layer, 1, D_MODEL)
    mod_spec = pl.BlockSpec((1, 6, D_MODEL),
                            lambda t: (jnp.where(t < ctx_tiles, 0, 1 + (t - ctx_tiles) // per), 0, 0))
    x_specs = [all_tok(D_MODEL)] if len(xs) == 1 else pair(D_MODEL)
    if split_out:
        out_specs = pair(D_MODEL)
        out_shape = [jax.ShapeDtypeStruct((n_ctx, D_MODEL), F32), jax.ShapeDtypeStruct((n_lat, D_MODEL), F32)]
    else:
        out_specs = [all_tok(D_MODEL)]
        out_shape = [jax.ShapeDtypeStruct((n_ctx + n_lat, D_MODEL), F32)]
    return pl.pallas_call(
        functools.partial(_post_kernel, n_x=len(xs), n_out=len(out_shape), ctx_tiles=ctx_tiles),
        grid=(ctx_tiles + lat_tiles,),
        in_specs=x_specs + [mod_spec] + pair(768) + pair(BRANCH_W) + [
            _layer_spec(layer, 4 * D_MODEL, D_MODEL), _layer_spec(layer, 4, BRANCH_W, D_MODEL),
            _layer_spec(layer, D_MODEL, D_MODEL), vec, vec,
            _layer_spec(layer, D_MODEL, D_FF), _layer_spec(layer, D_FF, D_MODEL), vec, vec],
        out_specs=out_specs, out_shape=out_shape,
        compiler_params=pltpu.CompilerParams(dimension_semantics=("arbitrary",), vmem_limit_bytes=VMEM_LIMIT),
        name="merge_mlp",
    )(*xs, mod, *oatts, *orets, wts["wg"], wts["wb"], wts["wo"], wts["ln1g"], wts["ln1b"],
      wts["wup"], wts["wdn"], wts["ln2g"], wts["ln2b"])


def _win_kernel(w_ref, wmix_ref, wg_ref):
    o_ret = 256 + 128 + MLA_ROPE
    o_wq, o_wk, o_gq, o_gk, o_gate = o_ret + 1024, o_ret + 1280, o_ret + 1536, o_ret + 1792, o_ret + 2048

    def put(dst, src, n):
        wmix_ref[0, dst:dst + n, :] = w_ref[0, src:src + n, :].astype(BF16)

    def put_swapped(dst, src):
        for i, h in enumerate((0, 2, 1, 3)):
            put(dst + i * HEAD_DIM, src + h * HEAD_DIM, HEAD_DIM)

    put(P_QLAT, 0, o_ret)
    pad = P_RET - o_ret
    wmix_ref[0, o_ret:P_RET, :] = jnp.zeros((pad, w_ref.shape[2]), BF16)
    put(P_RET, o_ret, o_wq - o_ret)
    put_swapped(P_WQ, o_wq)
    put(P_WK, o_wk, o_gq - o_wk)
    put_swapped(P_GQ, o_gq)
    put(P_GK, o_gk, o_gate - o_gk)
    wg_ref[0] = w_ref[0, o_gate:, :].astype(BF16)


def _prep_win(w_in):
    w_t = jnp.swapaxes(w_in, 1, 2)
    n_in = w_t.shape[1]
    kb = D_MODEL // 2
    return pl.pallas_call(
        _win_kernel,
        grid=(DEPTH, D_MODEL // kb),
        in_specs=[pl.BlockSpec((1, n_in, kb), lambda l, k: (l, 0, k))],
        out_specs=[pl.BlockSpec((1, P_END, kb), lambda l, k: (l, 0, k)),
                   pl.BlockSpec((1, 4 * D_MODEL, kb), lambda l, k: (l, 0, k))],
        out_shape=[jax.ShapeDtypeStruct((DEPTH, P_END, D_MODEL), BF16),
                   jax.ShapeDtypeStruct((DEPTH, 4 * D_MODEL, D_MODEL), BF16)],
        compiler_params=pltpu.CompilerParams(dimension_semantics=("arbitrary", "arbitrary"),
                                             vmem_limit_bytes=VMEM_LIMIT),
        name="prep_w_in",
    )(w_t)


def _prep_weights(w_in, mla_q_norm, mla_w_uq, mla_kv_norm, mla_w_uk, mla_w_uv, ret_decay_fwd, ret_decay_bwd,
                  ret_gn_gain, win_sink, gqa_q_norm, gqa_k_norm, w_branch, w_o, ln1_g, ln1_b, w_up, w_down,
                  ln2_g, ln2_b):
    wmix, wg = _prep_win(w_in)
    wuq = jnp.pad(mla_w_uq.reshape(DEPTH, 256, N_HEADS, HEAD_DIM + MLA_ROPE),
                  ((0, 0), (0, 0), (0, 0), (0, LANES - HEAD_DIM - MLA_ROPE))
                  ).reshape(DEPTH, 256, N_HEADS * LANES).astype(BF16)
    uk = mla_w_uk.reshape(DEPTH, MLA_KV_RANK, N_HEADS, HEAD_DIM).transpose(0, 2, 1, 3)
    top = jnp.pad(uk, ((0, 0), (0, 0), (0, 0), (0, LANES - HEAD_DIM)))
    eye = np.zeros((LANES, LANES), np.float32)
    eye[np.arange(MLA_ROPE), HEAD_DIM + np.arange(MLA_ROPE)] = 1.0
    wka = jnp.concatenate([top, jnp.broadcast_to(eye, (DEPTH, N_HEADS, LANES, LANES))], axis=2).astype(BF16)
    wuv = jnp.pad(mla_w_uv, ((0, 0), (0, 256 - MLA_KV_RANK), (0, 0))).astype(BF16)
    bd = (np.arange(BRANCH_W)[:, None] // HEAD_DIM == np.arange(BRANCH_W)[None, :] // HEAD_DIM).astype(np.float32)
    wb_cd = jnp.swapaxes(w_branch[:, 2:].reshape(DEPTH, 2, 2, 2, HEAD_DIM, D_MODEL), 2, 3
                         ).reshape(DEPTH, 2, BRANCH_W, D_MODEL)
    w_branch = jnp.concatenate([w_branch[:, :2], wb_cd], axis=1)
    row = lambda a: a[:, None, :]
    return dict(
        wmix=wmix, wg=wg, wuq=wuq, wka=wka, wuv=wuv, bd=jnp.asarray(bd, BF16),
        qn=row(mla_q_norm), kvn=row(mla_kv_norm),
        gqn=row(jnp.tile(gqa_q_norm, (1, N_HEADS))), gkn=row(jnp.tile(gqa_k_norm, (1, 2))),
        decf=ret_decay_fwd, decb=ret_decay_bwd, gn=row(ret_gn_gain), sink=win_sink,
        wb=w_branch.astype(BF16), wo=w_o.astype(BF16), ln1g=row(ln1_g), ln1b=row(ln1_b),
        wup=w_up.astype(BF16), wdn=w_down.astype(BF16), ln2g=row(ln2_g), ln2b=row(ln2_b))


def _axial_tables(t, rot_dim):
    rows = t // GRID_W
    row = np.repeat(np.arange(rows, dtype=np.float32), GRID_W)
    col = (np.arange(t) % GRID_W).astype(np.float32)
    n_freq = rot_dim // 4
    inv = np.power(np.float32(ROPE_BASE), -np.arange(n_freq, dtype=np.float32) / np.float32(n_freq))
    ang = np.concatenate([row[:, None] * inv, col[:, None] * inv], axis=-1).astype(np.float32)
    return np.cos(ang), np.sin(ang)


def _rope_tables(t):
    ca, sa = _axial_tables(t, MLA_ROPE)
    ch, sh = _axial_tables(t, HEAD_DIM)
    one = lambda n: np.ones((t, n), np.float32)
    zero = lambda n: np.zeros((t, n), np.float32)
    cq = np.concatenate([one(HEAD_DIM), ca, ca, one(32)], axis=1)
    sq = np.concatenate([zero(HEAD_DIM), -sa, sa, zero(32)], axis=1)
    ck = np.concatenate([ca, ca, one(96)], axis=1)
    sk = np.concatenate([-sa, sa, zero(96)], axis=1)
    chh = np.concatenate([ch, ch, ch, ch], axis=1)
    shh = np.concatenate([-sh, sh, -sh, sh], axis=1)
    return tuple(jnp.asarray(a, F32) for a in (cq, sq, ck, sk, chh, shh))


def _block_diag(s):
    b = s.shape[0]
    same = np.eye(N_HEADS, dtype=bool)[None, :, None, :, None]
    return jnp.where(same, s[:, :, :, None, :], 0.0).reshape(b, BRANCH_W, BRANCH_W)


def kernel(x_prompt, x_sample, cache_mla_ckv, cache_mla_kpe, cache_win_k, cache_win_v, cache_gqa_k, cache_gqa_v,
           state_ret_fwd, state_ret_bwd, c, c_ctx, w_ada, b_ada, w_in, mla_q_norm, mla_w_uq, mla_kv_norm, mla_w_uk,
           mla_w_uv, ret_decay_fwd, ret_decay_bwd, ret_gn_gain, win_sink, gqa_q_norm, gqa_k_norm, w_branch, w_o,
           ln1_g, ln1_b, w_up, w_down, ln2_g, ln2_b):
    batch, seq, _ = x_prompt.shape
    dec_b, dec_t, _ = x_sample.shape
    past = cache_mla_ckv.shape[2]

    cond8 = jnp.concatenate([c_ctx[None], c, jnp.zeros((8 - 1 - dec_b, D_MODEL), F32)], axis=0)
    mod = _ada(cond8, w_ada, b_ada).reshape(DEPTH, 8, 6, D_MODEL)
    rope = _rope_tables(dec_t)
    caches = (cache_mla_ckv,
              jnp.pad(cache_mla_kpe, ((0, 0), (0, 0), (0, 0), (0, LANES - MLA_ROPE))),
              cache_win_k.reshape(dec_b, DEPTH, past, LANES), cache_win_v.reshape(dec_b, DEPTH, past, LANES),
              cache_gqa_k.reshape(dec_b, DEPTH, past, LANES), cache_gqa_v.reshape(dec_b, DEPTH, past, LANES))

    wts = _prep_weights(w_in, mla_q_norm, mla_w_uq, mla_kv_norm, mla_w_uk, mla_w_uv, ret_decay_fwd, ret_decay_bwd,
                        ret_gn_gain, win_sink, gqa_q_norm, gqa_k_norm, w_branch, w_o, ln1_g, ln1_b, w_up, w_down,
                        ln2_g, ln2_b)
    zero_state = jnp.zeros((1, BRANCH_W, BRANCH_W), F32)
    n_ctx, n_lat = batch * seq, dec_b * dec_t
    per_b = lambda a: a.reshape(batch, seq, a.shape[-1])
    per_d = lambda a: a.reshape(dec_b, dec_t, a.shape[-1])
    flat = lambda a: a.reshape(-1, a.shape[-1])
    x = [x_prompt.reshape(n_ctx, D_MODEL), x_sample.reshape(n_lat, D_MODEL)]
    new_caches = None
    new_states = None
    for l in range(DEPTH):
        x_ctx, x_lat = (x[0], 0), ((x[1], 0) if len(x) == 2 else (x[0], n_ctx))
        fq, fkv, fr, *new_caches = _proj(*x_ctx, n_ctx, mod[l], wts, l, None, TM_PROJ, None, seq, new_caches)
        oatt_c = _attn(per_b(fq), per_b(fkv), None, wts, l, seq)
        oret_c, *new_states = _ret(per_b(fr), zero_state, zero_state, wts, l, True, new_states)
        fq, fkv, fr = _proj(*x_lat, n_lat, mod[l], wts, l, rope, TM_PROJ, dec_t)
        oatt_l = _attn(per_d(fq), per_d(fkv), caches, wts, l, TQ_LATENT)
        (oret_l,) = _ret(per_d(fr), _block_diag(state_ret_fwd[:, l]), _block_diag(state_ret_bwd[:, l]), wts, l, False)
        x = _post(x, mod[l], (flat(oatt_c), flat(oatt_l)), (flat(oret_c), flat(oret_l)), wts, l, TM_POST, dec_t,
                  split_out=l == DEPTH - 1)

    ckv, kpe, *kv = new_caches
    kv = [jnp.transpose(a.reshape(batch, DEPTH, 2, HEAD_DIM, seq), (0, 1, 4, 2, 3)) for a in kv]
    return (per_b(x[0]), per_d(x[1]), ckv, jnp.swapaxes(kpe, 2, 3), *kv, *new_states)
```

```python
import functools

import jax
import jax.numpy as jnp
import numpy as np
from jax import lax
from jax.experimental import pallas as pl
from jax.experimental.pallas import tpu as pltpu

D_MODEL = 1024
DEPTH = 2
GRID_W = 64
CHUNK = 128
WINDOW = 128
ROPE_BASE = 10000.0
NORM_EPS = 1e-6
NEG_INF = -1e30
HEAD_DIM = 64
N_HEADS = 4
BRANCH_W = 256
MLA_ROPE = 32
MLA_KV_RANK = 128
LOG2E = 1.4426950408889634
MLA_SCALE = (HEAD_DIM + MLA_ROPE) ** -0.5 * LOG2E
ATT_SCALE = HEAD_DIM ** -0.5 * LOG2E
D_FF = 4 * D_MODEL
ALPHA = (2.0 * DEPTH) ** 0.25
LANES = 128

P_QLAT, P_KVLAT, P_KPE, P_RET, P_WQ, P_WK, P_WV, P_GQ, P_GK, P_GV, P_END = (
    0, 256, 384, 512, 1536, 1792, 1920, 2048, 2304, 2432, 2560)
FQ_W = 1024
FKV_W = 768
VMEM_LIMIT = 60 * 1024 * 1024
TM_PROJ = 512
TM_POST = 512
SUB_ROWS = 256
SUB_ROWS_LATENT_PROJ = 256
TQ_LATENT = 256
ATTN_LOOKAHEAD = 2
MXU_SUM_MIN_KEYS = 1024
RET_GROUP = 4
RET_CHUNKS_PER_STEP = 8

F32 = jnp.float32
BF16 = jnp.bfloat16


def _dot(a, b):
    return jnp.dot(a, b, preferred_element_type=F32)


def _dot_nt(a, b):
    return lax.dot_general(a, b, (((1,), (1,)), ((), ())), preferred_element_type=F32)


def _dot_tn(a, b):
    return lax.dot_general(a, b, (((0,), (0,)), ((), ())), preferred_element_type=F32)


def _sub_tiles(tm, sub_rows=SUB_ROWS):
    n = max(1, tm // sub_rows)
    step = tm // n
    return [slice(i * step, (i + 1) * step) for i in range(n)]


def _layernorm(x, g, b):
    mu = jnp.mean(x, -1, keepdims=True)
    d = x - mu
    var = jnp.mean(d * d, -1, keepdims=True)
    return d * lax.rsqrt(var + NORM_EPS) * g + b


def _rmsnorm(x, g):
    return x * lax.rsqrt(jnp.mean(x * x, -1, keepdims=True) + NORM_EPS) * g


def _seg_sum(x, ones_bd):
    hi = x.astype(BF16)
    lo = (x - hi.astype(F32)).astype(BF16)
    return _dot(hi, ones_bd) + _dot(lo, ones_bd)


def _rope_block(x, cos, sin, half, first):
    rot = jnp.where(first, pltpu.roll(x, LANES - half, 1), pltpu.roll(x, half, 1))
    return x * cos + rot * sin


def _ada_kernel(cond_ref, w_ref, b_ref, o_ref):
    cnd = cond_ref[...]
    s = (cnd * jax.nn.sigmoid(cnd)).astype(BF16)
    o_ref[0] = _dot(s, w_ref[0].astype(BF16)) + b_ref[0]


def _ada(cond8, w_ada, b_ada):
    tn = 1024
    n = w_ada.shape[-1]
    return pl.pallas_call(
        _ada_kernel,
        grid=(DEPTH, n // tn),
        in_specs=[pl.BlockSpec((8, D_MODEL), lambda l, j: (0, 0)),
                  pl.BlockSpec((1, D_MODEL, tn), lambda l, j: (l, 0, j)),
                  pl.BlockSpec((1, 1, tn), lambda l, j: (l, 0, j))],
        out_specs=pl.BlockSpec((1, 8, tn), lambda l, j: (l, 0, j)),
        out_shape=jax.ShapeDtypeStruct((DEPTH, 8, n), F32),
        compiler_params=pltpu.CompilerParams(dimension_semantics=("arbitrary", "arbitrary")),
        name="ada_mod",
    )(cond8, w_ada, b_ada.reshape(DEPTH, 1, n))


def _store_per_seq(o_ref, val, row0, transposed):
    seq = o_ref.shape[2] if transposed else o_ref.shape[1]
    for i in range(val.shape[0] // seq):
        blk = val[i * seq:(i + 1) * seq, :]
        o_ref[row0 // seq + i] = blk.T[0:o_ref.shape[1], :] if transposed else blk


def _proj_kernel(*refs, latent, n_alias):
    refs = refs[:9] + refs[9 + n_alias:]
    x_ref, mod_ref, wmix_ref = refs[:3]
    sh1 = mod_ref[0, 0:1, :]
    sc1 = mod_ref[0, 1:2, :]
    subs = _sub_tiles(x_ref.shape[0], SUB_ROWS_LATENT_PROJ if latent else SUB_ROWS)
    ps = [_dot_nt((x_ref[r, :] * (1.0 + sc1) + sh1).astype(BF16), wmix_ref[0]) for r in subs]
    for r, p in zip(subs, ps):
        _proj_post(r, p, refs, latent)


def _proj_post(r, p, refs, latent):
    if latent:
        (_, _, _, qn_ref, wuq_ref, kvn_ref, gqn_ref, gkn_ref, bd_ref,
         cq_ref, sq_ref, ck_ref, sk_ref, ch_ref, sh_ref, fq_ref, fkv_ref, fr_ref) = refs
    else:
        (_, _, _, qn_ref, wuq_ref, kvn_ref, gqn_ref, gkn_ref, bd_ref,
         fq_ref, fkv_ref, fr_ref, ockv_ref, okpe_ref, owk_ref, owv_ref, ogk_ref, ogv_ref) = refs
    rows = p.shape[0]
    lane = lax.broadcasted_iota(jnp.int32, (rows, LANES), 1)
    first_head = (lane % HEAD_DIM) < (HEAD_DIM // 2)

    def rope_heads(v):
        if not latent:
            return v
        cos, sin = ch_ref[r, :], sh_ref[r, :]
        blocks = [_rope_block(v[:, j:j + LANES], cos, sin, HEAD_DIM // 2, first_head)
                  for j in range(0, v.shape[1], LANES)]
        return blocks[0] if len(blocks) == 1 else jnp.concatenate(blocks, axis=1)

    qn = _rmsnorm(p[:, P_QLAT:P_KVLAT], qn_ref[0]).astype(BF16)
    if latent:
        qa2 = _dot(qn, wuq_ref[0])
        cos, sin = cq_ref[r, :], sq_ref[r, :]
        qa = jnp.concatenate([qa2[:, j:j + LANES] * cos + qa2[:, 512 + j:512 + j + LANES] * sin
                              for j in range(0, 512, LANES)], axis=1)
    else:
        qa = _dot(qn, wuq_ref[0, :, 0:512])
    fq_ref[r, 0:512] = (qa * MLA_SCALE).astype(BF16)

    ckv = _rmsnorm(p[:, P_KVLAT:P_KPE], kvn_ref[0])
    kpe = p[:, P_KPE:P_RET]
    if latent:
        first = lane < MLA_ROPE // 2
        kpe_r = _rope_block(kpe, ck_ref[r, :], sk_ref[r, :], MLA_ROPE // 2, first)
    else:
        kpe_r = kpe
        _store_per_seq(ockv_ref, ckv, r.start, False)
        _store_per_seq(okpe_ref, kpe, r.start, True)
    fkv_ref[r, 0:128] = ckv.astype(BF16)
    fkv_ref[r, 128:256] = kpe_r.astype(BF16)

    fr_ref[r, 0:256] = p[:, P_RET:P_RET + 256]
    fr_ref[r, 256:512] = p[:, P_RET + 256:P_RET + 512] * (HEAD_DIM ** -0.5)
    fr_ref[r, 512:1024] = p[:, P_RET + 512:P_WQ]

    fq_ref[r, 512:768] = (rope_heads(p[:, P_WQ:P_WK]) * ATT_SCALE).astype(BF16)
    wk = p[:, P_WK:P_WV]
    wv = p[:, P_WV:P_GQ]
    fkv_ref[r, 256:384] = rope_heads(wk).astype(BF16)
    fkv_ref[r, 384:512] = wv.astype(BF16)

    bd = bd_ref[...]
    gq = p[:, P_GQ:P_GK]
    gqn = gq * lax.rsqrt(_seg_sum(gq * gq, bd) * (1.0 / HEAD_DIM) + NORM_EPS) * gqn_ref[0]
    fq_ref[r, 768:1024] = (rope_heads(gqn) * ATT_SCALE).astype(BF16)
    gk = p[:, P_GK:P_GV]
    gkn = gk * lax.rsqrt(_seg_sum(gk * gk, bd[0:128, 0:128]) * (1.0 / HEAD_DIM) + NORM_EPS) * gkn_ref[0]
    gv = p[:, P_GV:P_END]
    fkv_ref[r, 512:640] = rope_heads(gkn).astype(BF16)
    fkv_ref[r, 640:768] = gv.astype(BF16)
    if not latent:
        _store_per_seq(owk_ref, wk, r.start, True)
        _store_per_seq(owv_ref, wv, r.start, True)
        _store_per_seq(ogk_ref, gkn, r.start, True)
        _store_per_seq(ogv_ref, gv, r.start, True)


def _layer_spec(layer, *s):
    return pl.BlockSpec((1,) + s, lambda t: (layer,) + (0,) * len(s), pipeline_mode=pl.Buffered(1))


def _mod_spec(tm, group_len):
    if group_len is None:
        return pl.BlockSpec((1, 6, D_MODEL), lambda t: (0, 0, 0))
    per = group_len // tm
    return pl.BlockSpec((1, 6, D_MODEL), lambda t: (1 + t // per, 0, 0))


def _proj(x, row0, nt, mod, wts, layer, rope, tm, group_len, ctx_seq=None, caches=None):
    latent = group_len is not None
    aliases = {}
    tile0 = row0 // tm
    tok = lambda w: pl.BlockSpec((tm, w), lambda t: (t, 0))
    in_specs = [pl.BlockSpec((tm, D_MODEL), lambda t: (t + tile0, 0)), _mod_spec(tm, group_len),
                _layer_spec(layer, P_END, D_MODEL), _layer_spec(layer, 1, 256), _layer_spec(layer, 256, 1024),
                _layer_spec(layer, 1, 128), _layer_spec(layer, 1, 256), _layer_spec(layer, 1, 128),
                pl.BlockSpec((256, 256), lambda t: (0, 0))]
    args = [x, mod, wts["wmix"], wts["qn"], wts["wuq"], wts["kvn"], wts["gqn"], wts["gkn"], wts["bd"]]
    out_specs = [tok(FQ_W), tok(FKV_W), tok(1024)]
    out_shape = [jax.ShapeDtypeStruct((nt, FQ_W), BF16), jax.ShapeDtypeStruct((nt, FKV_W), BF16),
                 jax.ShapeDtypeStruct((nt, 1024), F32)]
    if latent:
        per = group_len // tm
        in_specs += [pl.BlockSpec((tm, LANES), lambda t: (t % per, 0))] * 6
        args += list(rope)
    else:
        nb = tm // ctx_seq
        seqs = nt // ctx_seq
        sub = _sub_tiles(tm)[0]
        assert (sub.stop - sub.start) % ctx_seq == 0, "a sub-tile must hold whole context sequences"
        out_specs += [pl.BlockSpec((nb, None, ctx_seq, 128), lambda t: (t, layer, 0, 0))]
        out_shape += [jax.ShapeDtypeStruct((seqs, DEPTH, ctx_seq, 128), F32)]
        for r in (MLA_ROPE, 128, 128, 128, 128):
            out_specs.append(pl.BlockSpec((nb, None, r, ctx_seq), lambda t: (t, layer, 0, 0)))
            out_shape.append(jax.ShapeDtypeStruct((seqs, DEPTH, r, ctx_seq), F32))
        if caches is not None:
            aliases = {len(args) + i: 3 + i for i in range(len(caches))}
            in_specs += [pl.BlockSpec(memory_space=pl.ANY)] * len(caches)
            args += list(caches)
    return pl.pallas_call(
        functools.partial(_proj_kernel, latent=latent, n_alias=len(aliases)),
        grid=(nt // tm,), in_specs=in_specs, out_specs=out_specs, out_shape=out_shape,
        input_output_aliases=aliases,
        compiler_params=pltpu.CompilerParams(dimension_semantics=("arbitrary",), vmem_limit_bytes=VMEM_LIMIT),
        name="proj_latent" if latent else "proj_ctx",
    )(*args)


def _softmax_weights(parts, sink, mxu_sum):
    m = parts[0].max(-1, keepdims=True)
    for s in parts[1:]:
        m = jnp.maximum(m, s.max(-1, keepdims=True))
    if sink is not None:
        m = jnp.maximum(m, sink)
    es = [jnp.exp2(s - m) for s in parts]
    extra = None if sink is None else jnp.exp2(sink - m)
    if not mxu_sum:
        for e in es:
            extra = e.sum(-1, keepdims=True) if extra is None else extra + e.sum(-1, keepdims=True)
    return [e.astype(BF16) for e in es], extra


def _normalised(pv, extra):
    if pv.shape[1] == LANES:
        return pv * (1.0 / extra)
    den = pv[:, LANES:LANES + 1]
    if extra is not None:
        den = den + extra
    return pv[:, 0:LANES] * (1.0 / den)


def _window_start(qi, tq, t):
    return jnp.clip(qi * tq - WINDOW, 0, t - (tq + 2 * WINDOW))


def _band_bias(tq):
    span = tq + 2 * WINDOW
    r = (np.arange(2 * tq) % tq)[None, :, None]
    c = np.arange(span)[None, None, :]
    off = (np.arange(3) * WINDOW)[:, None, None]
    return jnp.asarray(np.where(np.abs(r + off - c) <= WINDOW, 0.0, NEG_INF), F32)


def _attn_kernel(*refs, t, tq, n_cache, layer):
    latent = n_cache > 0
    s_len = t + n_cache
    if latent:
        (fq_ref, fkv_ref, cckv_ref, ckpe_ref, cwk_ref, cwv_ref, cgk_ref, cgv_ref, bias_ref,
         wka_ref, wuv_ref, sink_ref, o_ref, ckpe_s, ka_s, va_s, kd_s, vd_s, vc_s, kcc_s, vcc_s) = refs
    else:
        (fq_ref, fkv_ref, wka_ref, wuv_ref, sink_ref, o_ref, ckpe_s, ka_s, va_s, kd_s, vd_s, vc_s) = refs

    qi = pl.program_id(1)

    @pl.when(qi == 0)
    def _():
        ckpe_s[0:t, :] = fkv_ref[0, :, 0:256]
        kd_s[0:t, :] = fkv_ref[0, :, 512:640]
        vd_s[0:t, 0:LANES] = fkv_ref[0, :, 640:768]
        vc_s[:, 0:LANES] = fkv_ref[0, :, 384:512]
        if latent:
            ckpe_s[t:s_len, 0:128] = cckv_ref[0, 0].astype(BF16)
            ckpe_s[t:s_len, 128:256] = ckpe_ref[0, 0].astype(BF16)
            kd_s[t:s_len, :] = cgk_ref[0, 0].astype(BF16)
            vd_s[t:s_len, 0:LANES] = cgv_ref[0, 0].astype(BF16)
            kcc_s[...] = cwk_ref[0, 0].astype(BF16)
            vcc_s[:, 0:LANES] = cwv_ref[0, 0].astype(BF16)
            vcc_s[:, LANES:] = jnp.ones((n_cache, LANES), BF16)
        vd_s[:, LANES:] = jnp.ones((s_len, LANES), BF16)
        vc_s[:, LANES:] = jnp.ones((t, LANES), BF16)
        ck = ckpe_s[...]
        va = _dot(ck, wuv_ref[0]).astype(BF16)
        for blk in range(2):
            va_s[blk, :, 0:LANES] = va[:, blk * LANES:(blk + 1) * LANES]
            va_s[blk, :, LANES:] = jnp.ones((s_len, LANES), BF16)
        for h in range(N_HEADS):
            ka_s[h] = _dot(ck, wka_ref[0, h]).astype(BF16)

    lane = lax.broadcasted_iota(jnp.int32, (tq, LANES), 1)
    half = [lane < HEAD_DIM, lane >= HEAD_DIM]

    def keep(x, j):
        return jnp.where(half[j], x, 0.0)

    def stacked_q(col, j):
        blks = [fq_ref[0, :, col + g * LANES:col + (g + 1) * LANES] for g in range(2)]
        return jnp.concatenate([keep(b.astype(F32), j).astype(BF16) for b in blks], axis=0)

    if latent:
        span = tq + 2 * WINDOW
        start = pl.multiple_of(_window_start(qi, tq, t), LANES)
    row1 = lax.broadcasted_iota(jnp.int32, (2 * tq, 1), 0)
    mxu_sum = s_len >= MXU_SUM_MIN_KEYS
    vw = 2 * LANES if mxu_sum else LANES

    acc_a = [jnp.zeros((tq, LANES), F32) for _ in range(2)]
    acc_c = [jnp.zeros((tq, LANES), F32) for _ in range(2)]
    acc_d = [jnp.zeros((tq, LANES), F32) for _ in range(2)]

    def a_scores(h):
        return [_dot_nt(fq_ref[0, :, h * LANES:(h + 1) * LANES], ka_s[h])]

    def a_finish(h, ps, sink_term):
        blk = h // 2
        acc_a[blk] = acc_a[blk] + keep(_normalised(_dot(ps[0], va_s[blk, :, 0:vw]), sink_term), h % 2)

    def c_scores(j):
        qs = stacked_q(512, j)
        if latent:
            return [_dot_nt(qs, fkv_ref[0, pl.ds(start, span), 256:384]) + bias_ref[0], _dot_nt(qs, kcc_s[...])]
        return [_dot_nt(qs, fkv_ref[0, :, 256:384])]

    def c_finish(j, ps, sink_term):
        if latent:
            pv = _dot(ps[0], vc_s[pl.ds(start, span), 0:vw]) + _dot(ps[1], vcc_s[:, 0:vw])
        else:
            pv = _dot(ps[0], vc_s[:, 0:vw])
        pv = _normalised(pv, sink_term)
        for g in range(2):
            acc_c[g] = acc_c[g] + keep(pv[g * tq:(g + 1) * tq], j)

    def d_scores(j):
        return [_dot_nt(stacked_q(768, j), kd_s[...])]

    def d_finish(j, ps, sink_term):
        pv = _normalised(_dot(ps[0], vd_s[:, 0:vw]), sink_term)
        for g in range(2):
            acc_d[g] = acc_d[g] + keep(pv[g * tq:(g + 1) * tq], j)

    def c_sink(j):
        return jnp.where(row1 < tq, sink_ref[layer, 2 * j], sink_ref[layer, 2 * j + 1]) * LOG2E

    jobs = [(functools.partial(a_scores, h), functools.partial(a_finish, h), None) for h in range(N_HEADS)]
    jobs += [(functools.partial(d_scores, j), functools.partial(d_finish, j), None) for j in range(2)]
    jobs += [(functools.partial(c_scores, j), functools.partial(c_finish, j), functools.partial(c_sink, j))
             for j in range(2)]
    ahead = ATTN_LOOKAHEAD if mxu_sum else len(jobs)
    pending = [job[0]() for job in jobs[:ahead]]
    for i, (_, finish, sink) in enumerate(jobs):
        if i + ahead < len(jobs):
            pending.append(jobs[i + ahead][0]())
        finish(*_softmax_weights(pending.pop(0), None if sink is None else sink(), mxu_sum))

    for g in range(2):
        o_ref[0, :, g * LANES:(g + 1) * LANES] = acc_a[g].astype(BF16)
        o_ref[0, :, 256 + g * LANES:256 + (g + 1) * LANES] = acc_c[g].astype(BF16)
        o_ref[0, :, 512 + g * LANES:512 + (g + 1) * LANES] = acc_d[g].astype(BF16)


def _attn(fq, fkv, caches, wts, layer, tq):
    b, t, _ = fq.shape
    latent = caches is not None
    n_cache = caches[0].shape[2] if latent else 0
    s_len = t + n_cache
    in_specs = [pl.BlockSpec((1, tq, FQ_W), lambda i, q: (i, q, 0)),
                pl.BlockSpec((1, t, FKV_W), lambda i, q: (i, 0, 0))]
    args = [fq, fkv]
    if latent:
        in_specs += [pl.BlockSpec((1, 1, n_cache, LANES), lambda i, q: (i, layer, 0, 0))] * 6
        args += list(caches)
        span = tq + 2 * WINDOW
        assert t >= span and WINDOW <= tq
        in_specs.append(pl.BlockSpec((1, 2 * tq, span),
                                     lambda i, q: ((q * tq - _window_start(q, tq, t)) // WINDOW, 0, 0)))
        args.append(_band_bias(tq))
    in_specs += [pl.BlockSpec((1, N_HEADS, 256, 128), lambda i, q: (layer, 0, 0, 0)),
                 pl.BlockSpec((1, 256, 256), lambda i, q: (layer, 0, 0)),
                 pl.BlockSpec(memory_space=pltpu.SMEM)]
    args += [wts["wka"], wts["wuv"], wts["sink"]]
    scratch = [pltpu.VMEM((s_len, 256), BF16), pltpu.VMEM((N_HEADS, s_len, 128), BF16),
               pltpu.VMEM((2, s_len, 256), BF16), pltpu.VMEM((s_len, 128), BF16), pltpu.VMEM((s_len, 256), BF16),
               pltpu.VMEM((t, 256), BF16)]
    if latent:
        scratch += [pltpu.VMEM((n_cache, 128), BF16), pltpu.VMEM((n_cache, 256), BF16)]
    return pl.pallas_call(
        functools.partial(_attn_kernel, t=t, tq=tq, n_cache=n_cache, layer=layer),
        grid=(b, t // tq), in_specs=in_specs,
        out_specs=pl.BlockSpec((1, tq, 768), lambda i, q: (i, q, 0)),
        out_shape=jax.ShapeDtypeStruct((b, t, 768), BF16),
        scratch_shapes=scratch,
        compiler_params=pltpu.CompilerParams(dimension_semantics=("arbitrary", "arbitrary"),
                                             vmem_limit_bytes=VMEM_LIMIT),
        name="attn_latent" if latent else "attn_ctx",
    )(*args)


def _ret_kernel(*refs, t, bb, s0_stride, layer, want_state, n_alias):
    refs = refs[:7] + refs[7 + n_alias:]
    if want_state:
        (fr_ref, s0f_ref, s0b_ref, decf_ref, decb_ref, gain_ref, bd_ref, o_ref, sf_ref, sb_ref,
         dm_s, dec_s, cd_s, kvf_s, kvb_s, stf_s, stb_s, sf_s, sb_s) = refs
    else:
        (fr_ref, s0f_ref, s0b_ref, decf_ref, decb_ref, gain_ref, bd_ref, o_ref,
         dm_s, dec_s, cd_s, kvf_s, kvb_s, stf_s, stb_s, sf_s, sb_s) = refs
    nc = t // CHUNK
    w = BRANCH_W
    lane_w = lax.broadcasted_iota(jnp.int32, (CHUNK, w), 1) // HEAD_DIM

    @pl.when(pl.program_id(0) == 0)
    def _():
        row_w = lax.broadcasted_iota(jnp.int32, (CHUNK, w), 0).astype(F32)

        def lane_decay(dec_ref):
            v = jnp.zeros((CHUNK, w), F32)
            for h in range(N_HEADS):
                v = jnp.where(lane_w == h, dec_ref[layer, h], v)
            return jax.nn.log_sigmoid(v)

        lgf = lane_decay(decf_ref)
        lgb = lane_decay(decb_ref)
        dec_s[0] = jnp.exp((row_w + 1.0) * lgf)
        dec_s[1] = jnp.exp((CHUNK - 1.0 - row_w) * lgf)
        dec_s[2] = jnp.exp((CHUNK - row_w) * lgb)
        dec_s[3] = jnp.exp(row_w * lgb)
        cd_s[0] = jnp.concatenate([jnp.exp(CHUNK * lgf)] * (w // CHUNK), axis=0)
        cd_s[1] = jnp.concatenate([jnp.exp(CHUNK * lgb)] * (w // CHUNK), axis=0)
        ii = lax.broadcasted_iota(jnp.int32, (CHUNK, CHUNK), 0).astype(F32)
        jj = lax.broadcasted_iota(jnp.int32, (CHUNK, CHUNK), 1).astype(F32)
        diff = ii - jj
        for h in range(N_HEADS):
            lf = jax.nn.log_sigmoid(jnp.full((CHUNK, CHUNK), decf_ref[layer, h], F32))
            lb = jax.nn.log_sigmoid(jnp.full((CHUNK, CHUNK), decb_ref[layer, h], F32))
            d_f = jnp.where(diff >= 0, jnp.exp(jnp.maximum(diff, 0.0) * lf), 0.0)
            d_b = jnp.where(diff < 0, jnp.exp(jnp.maximum(-diff, 0.0) * lb), 0.0)
            dm_s[h] = d_f + d_b

    r2 = lax.broadcasted_iota(jnp.int32, (w, w), 0) // HEAD_DIM
    c2 = lax.broadcasted_iota(jnp.int32, (w, w), 1) // HEAD_DIM
    diag = r2 == c2

    jobs = [(bi, n) for bi in range(bb) for n in range(nc)]
    groups = [jobs[i:i + RET_GROUP] for i in range(0, len(jobs), RET_GROUP)]
    rows = lambda n: slice(n * CHUNK, (n + 1) * CHUNK)
    slot = lambda bi, n: bi * nc + n

    for grp in groups:
        kvs = []
        for bi, n in grp:
            k = fr_ref[bi, rows(n), 256:512]
            vb = fr_ref[bi, rows(n), 512:768].astype(BF16)
            kk = jnp.concatenate([k * dec_s[1], k * dec_s[3]], axis=1).astype(BF16)
            kvs.append(_dot_tn(kk, vb))
        for (bi, n), kv in zip(grp, kvs):
            kvf_s[slot(bi, n)] = jnp.where(diag, kv[0:w], 0.0)
            kvb_s[slot(bi, n)] = jnp.where(diag, kv[w:2 * w], 0.0)

    for bi in range(bb):
        sf_s[bi] = s0f_ref[bi * s0_stride]
        sb_s[bi] = s0b_ref[bi * s0_stride]
    for i in range(nc):
        m = nc - 1 - i
        for bi in range(bb):
            sf = sf_s[bi]
            stf_s[slot(bi, i)] = sf.astype(BF16)
            sf_s[bi] = sf * cd_s[0] + kvf_s[slot(bi, i)]
            sb = sb_s[bi]
            stb_s[slot(bi, m)] = sb.astype(BF16)
            sb_s[bi] = sb * cd_s[1] + kvb_s[slot(bi, m)]
    if want_state:
        for bi in range(bb):
            for h in range(N_HEADS):
                sl = slice(h * HEAD_DIM, (h + 1) * HEAD_DIM)
                sf_ref[bi, h] = sf_s[bi, sl, sl]
                sb_ref[bi, h] = sb_s[bi, sl, sl]

    gain = gain_ref[0]
    bd2 = jnp.concatenate([bd_ref[...], bd_ref[...]], axis=0)

    def seg_mean(xs):
        cat = []
        for x in xs:
            hi = x.astype(BF16)
            cat.append(jnp.concatenate([hi, (x - hi.astype(F32)).astype(BF16)], axis=1))
        return [_dot(c, bd2) * (1.0 / HEAD_DIM) for c in cat]

    for grp in groups:
        qs = [fr_ref[bi, rows(n), 0:256] for bi, n in grp]
        vs = [fr_ref[bi, rows(n), 512:768] for bi, n in grp]
        qks, inter_f, inter_b = [], [], []
        for (bi, n), q in zip(grp, qs):
            kb = fr_ref[bi, rows(n), 256:512].astype(BF16)
            qstack = jnp.concatenate([jnp.where(lane_w == h, q, 0.0) for h in range(N_HEADS)], axis=0)
            qks.append(_dot_nt(qstack.astype(BF16), kb))
            qb = q.astype(BF16)
            inter_f.append(_dot(qb, stf_s[slot(bi, n)]))
            inter_b.append(_dot(qb, stb_s[slot(bi, n)]))
        outs = []
        for qk, v, i_f, i_b in zip(qks, vs, inter_f, inter_b):
            att = jnp.concatenate([qk[rows(h)] * dm_s[h] for h in range(N_HEADS)], axis=1)
            vstack = jnp.concatenate([jnp.where(lane_w == h, v, 0.0) for h in range(N_HEADS)], axis=0)
            outs.append(_dot(att.astype(BF16), vstack.astype(BF16)) + i_f * dec_s[0] + i_b * dec_s[2])
        ds = [o - mu for o, mu in zip(outs, seg_mean(outs))]
        var = seg_mean([d * d for d in ds])
        for (bi, n), d, vr in zip(grp, ds, var):
            gate = fr_ref[bi, rows(n), 768:1024]
            y = d * lax.rsqrt(vr + NORM_EPS) * gain
            o_ref[bi, rows(n), :] = (gate * jax.nn.sigmoid(gate) * y).astype(BF16)


def _ret(fr, s0f, s0b, wts, layer, want_state, states=None):
    b, t, _ = fr.shape
    nc = t // CHUNK
    bb = max(1, RET_CHUNKS_PER_STEP // nc)
    shared = s0f.shape[0] != b
    s_spec = (pl.BlockSpec((1, BRANCH_W, BRANCH_W), lambda i: (0, 0, 0)) if shared
              else pl.BlockSpec((bb, BRANCH_W, BRANCH_W), lambda i: (i, 0, 0)))
    smem = pl.BlockSpec(memory_space=pltpu.SMEM)
    out_specs = [pl.BlockSpec((bb, t, BRANCH_W), lambda i: (i, 0, 0))]
    out_shape = [jax.ShapeDtypeStruct((b, t, BRANCH_W), BF16)]
    in_specs = [pl.BlockSpec((bb, t, 1024), lambda i: (i, 0, 0)), s_spec, s_spec, smem, smem,
                pl.BlockSpec((1, 1, BRANCH_W), lambda i: (layer, 0, 0)),
                pl.BlockSpec((BRANCH_W, BRANCH_W), lambda i: (0, 0))]
    args = [fr, s0f, s0b, wts["decf"], wts["decb"], wts["gn"], wts["bd"]]
    aliases = {}
    if want_state:
        st_spec = pl.BlockSpec((bb, None, N_HEADS, HEAD_DIM, HEAD_DIM), lambda i: (i, layer, 0, 0, 0))
        out_specs += [st_spec, st_spec]
        out_shape += [jax.ShapeDtypeStruct((b, DEPTH, N_HEADS, HEAD_DIM, HEAD_DIM), F32)] * 2
        if states is not None:
            aliases = {len(args) + i: 1 + i for i in range(2)}
            in_specs += [pl.BlockSpec(memory_space=pl.ANY)] * 2
            args += list(states)
    sq = (BRANCH_W, BRANCH_W)
    return pl.pallas_call(
        functools.partial(_ret_kernel, t=t, bb=bb, s0_stride=0 if shared else 1, layer=layer,
                          want_state=want_state, n_alias=len(aliases)),
        grid=(b // bb,),
        in_specs=in_specs, out_specs=out_specs, out_shape=out_shape, input_output_aliases=aliases,
        scratch_shapes=[pltpu.VMEM((N_HEADS, CHUNK, CHUNK), F32), pltpu.VMEM((4, CHUNK, BRANCH_W), F32),
                        pltpu.VMEM((2,) + sq, F32),
                        pltpu.VMEM((bb * nc,) + sq, F32), pltpu.VMEM((bb * nc,) + sq, F32),
                        pltpu.VMEM((bb * nc,) + sq, BF16), pltpu.VMEM((bb * nc,) + sq, BF16),
                        pltpu.VMEM((bb,) + sq, F32), pltpu.VMEM((bb,) + sq, F32)],
        compiler_params=pltpu.CompilerParams(dimension_semantics=("arbitrary",),
                                             vmem_limit_bytes=VMEM_LIMIT),
        name="retention",
    )(*args)


def _post_kernel(*refs, n_x, n_out, ctx_tiles):
    x_refs, refs = refs[:n_x], refs[n_x:]
    (mod_ref, oatt_c, oatt_l, oret_c, oret_l, wg_ref, wb_ref, wo_ref, g1_ref, b1_ref,
     wup_ref, wdn_ref, g2_ref, b2_ref) = refs[:14]
    o_refs = refs[14:]
    is_ctx = pl.program_id(0) < ctx_tiles
    pick = lambda c_ref, l_ref, r, cols: jnp.where(is_ctx, c_ref[r, cols], l_ref[r, cols])
    every = slice(None)
    if n_x == 1:
        x_rows = lambda r: x_refs[0][r, :]
    else:
        x_rows = lambda r: pick(x_refs[0], x_refs[1], r, every)
    sh1, sc1, g1, sh2, sc2, g2 = [mod_ref[0, i:i + 1, :] for i in range(6)]
    subs = _sub_tiles(x_refs[0].shape[0])
    ff_half = D_FF // 2
    xs = [x_rows(r) for r in subs]

    def gated(r, x):
        h = (x * (1.0 + sc1) + sh1).astype(BF16)
        branches = (pick(oatt_c, oatt_l, r, slice(0, 256)), pick(oret_c, oret_l, r, every),
                    pick(oatt_c, oatt_l, r, slice(256, 512)), pick(oatt_c, oatt_l, r, slice(512, 768)))
        tsum = None
        for i, o in enumerate(branches):
            gate = jax.nn.sigmoid(_dot_nt(h, wg_ref[0, i * D_MODEL:(i + 1) * D_MODEL, :]))
            term = gate * _dot(o, wb_ref[0, i])
            tsum = term if tsum is None else tsum + term
        return tsum

    sums = [gated(r, x) for r, x in zip(subs, xs)]
    ys = [_dot(s.astype(BF16), wo_ref[0]) for s in sums]
    x1s = [_layernorm(ALPHA * x + g1 * y, g1_ref[0], b1_ref[0]) for x, y in zip(xs, ys)]
    hs = [(x1 * (1.0 + sc2) + sh2).astype(BF16) for x1 in x1s]
    us = [[_dot(h, wup_ref[0, :, c * ff_half:(c + 1) * ff_half]) for c in range(2)] for h in hs]
    fs = []
    for u2 in us:
        f = None
        for c, u in enumerate(u2):
            u = jnp.maximum(u, 0.0)
            part = _dot((u * u).astype(BF16), wdn_ref[0, c * ff_half:(c + 1) * ff_half, :])
            f = part if f is None else f + part
        fs.append(f)
    outs = [_layernorm(ALPHA * x1 + g2 * f, g2_ref[0], b2_ref[0]) for x1, f in zip(x1s, fs)]
    if n_out == 1:
        for r, o in zip(subs, outs):
            o_refs[0][r, :] = o
    else:
        @pl.when(is_ctx)
        def _():
            for r, o in zip(subs, outs):
                o_refs[0][r, :] = o

        @pl.when(jnp.logical_not(is_ctx))
        def _():
            for r, o in zip(subs, outs):
                o_refs[1][r, :] = o


def _post(xs, mod, oatts, orets, wts, layer, tm, latent_len, split_out):
    n_ctx, n_lat = oatts[0].shape[0], oatts[1].shape[0]
    ctx_tiles, lat_tiles = n_ctx // tm, n_lat // tm
    per = latent_len // tm
    all_tok = lambda w: pl.BlockSpec((tm, w), lambda t: (t, 0))
    ctx_tok = lambda w: pl.BlockSpec((tm, w), lambda t: (jnp.minimum(t, ctx_tiles - 1), 0))
    lat_tok = lambda w: pl.BlockSpec((tm, w), lambda t: (jnp.maximum(t - ctx_tiles, 0), 0))
    pair = lambda w: [ctx_tok(w), lat_tok(w)]
    vec = _layer_spec(layer, 1, D_MODEL)
    mod_spec = pl.BlockSpec((1, 6, D_MODEL),
                            lambda t: (jnp.where(t < ctx_tiles, 0, 1 + (t - ctx_tiles) // per), 0, 0))
    x_specs = [all_tok(D_MODEL)] if len(xs) == 1 else pair(D_MODEL)
    if split_out:
        out_specs = pair(D_MODEL)
        out_shape = [jax.ShapeDtypeStruct((n_ctx, D_MODEL), F32), jax.ShapeDtypeStruct((n_lat, D_MODEL), F32)]
    else:
        out_specs = [all_tok(D_MODEL)]
        out_shape = [jax.ShapeDtypeStruct((n_ctx + n_lat, D_MODEL), F32)]
    return pl.pallas_call(
        functools.partial(_post_kernel, n_x=len(xs), n_out=len(out_shape), ctx_tiles=ctx_tiles),
        grid=(ctx_tiles + lat_tiles,),
        in_specs=x_specs + [mod_spec] + pair(768) + pair(BRANCH_W) + [
            _layer_spec(layer, 4 * D_MODEL, D_MODEL), _layer_spec(layer, 4, BRANCH_W, D_MODEL),
            _layer_spec(layer, D_MODEL, D_MODEL), vec, vec,
            _layer_spec(layer, D_MODEL, D_FF), _layer_spec(layer, D_FF, D_MODEL), vec, vec],
        out_specs=out_specs, out_shape=out_shape,
        compiler_params=pltpu.CompilerParams(dimension_semantics=("arbitrary",), vmem_limit_bytes=VMEM_LIMIT),
        name="merge_mlp",
    )(*xs, mod, *oatts, *orets, wts["wg"], wts["wb"], wts["wo"], wts["ln1g"], wts["ln1b"],
      wts["wup"], wts["wdn"], wts["ln2g"], wts["ln2b"])


def _win_kernel(w_ref, wmix_ref, wg_ref):
    o_ret = 256 + 128 + MLA_ROPE
    o_wq, o_wk, o_gq, o_gk, o_gate = o_ret + 1024, o_ret + 1280, o_ret + 1536, o_ret + 1792, o_ret + 2048

    def put(dst, src, n):
        wmix_ref[0, dst:dst + n, :] = w_ref[0, src:src + n, :].astype(BF16)

    def put_swapped(dst, src):
        for i, h in enumerate((0, 2, 1, 3)):
            put(dst + i * HEAD_DIM, src + h * HEAD_DIM, HEAD_DIM)

    put(P_QLAT, 0, o_ret)
    pad = P_RET - o_ret
    wmix_ref[0, o_ret:P_RET, :] = jnp.zeros((pad, w_ref.shape[2]), BF16)
    put(P_RET, o_ret, o_wq - o_ret)
    put_swapped(P_WQ, o_wq)
    put(P_WK, o_wk, o_gq - o_wk)
    put_swapped(P_GQ, o_gq)
    put(P_GK, o_gk, o_gate - o_gk)
    wg_ref[0] = w_ref[0, o_gate:, :].astype(BF16)


def _prep_win(w_in):
    w_t = jnp.swapaxes(w_in, 1, 2)
    n_in = w_t.shape[1]
    kb = D_MODEL // 2
    return pl.pallas_call(
        _win_kernel,
        grid=(DEPTH, D_MODEL // kb),
        in_specs=[pl.BlockSpec((1, n_in, kb), lambda l, k: (l, 0, k))],
        out_specs=[pl.BlockSpec((1, P_END, kb), lambda l, k: (l, 0, k)),
                   pl.BlockSpec((1, 4 * D_MODEL, kb), lambda l, k: (l, 0, k))],
        out_shape=[jax.ShapeDtypeStruct((DEPTH, P_END, D_MODEL), BF16),
                   jax.ShapeDtypeStruct((DEPTH, 4 * D_MODEL, D_MODEL), BF16)],
        compiler_params=pltpu.CompilerParams(dimension_semantics=("arbitrary", "arbitrary"),
                                             vmem_limit_bytes=VMEM_LIMIT),
        name="prep_w_in",
    )(w_t)


def _prep_weights(w_in, mla_q_norm, mla_w_uq, mla_kv_norm, mla_w_uk, mla_w_uv, ret_decay_fwd, ret_decay_bwd,
                  ret_gn_gain, win_sink, gqa_q_norm, gqa_k_norm, w_branch, w_o, ln1_g, ln1_b, w_up, w_down,
                  ln2_g, ln2_b):
    wmix, wg = _prep_win(w_in)
    uq = mla_w_uq.reshape(DEPTH, 256, N_HEADS, HEAD_DIM + MLA_ROPE)
    half = MLA_ROPE // 2
    uq_sw = jnp.concatenate([jnp.zeros_like(uq[..., :HEAD_DIM]), uq[..., HEAD_DIM + half:],
                             uq[..., HEAD_DIM:HEAD_DIM + half]], axis=-1)
    lane_pad = ((0, 0), (0, 0), (0, 0), (0, LANES - HEAD_DIM - MLA_ROPE))
    wuq = jnp.concatenate([jnp.pad(w, lane_pad).reshape(DEPTH, 256, N_HEADS * LANES) for w in (uq, uq_sw)],
                          axis=-1).astype(BF16)
    uk = mla_w_uk.reshape(DEPTH, MLA_KV_RANK, N_HEADS, HEAD_DIM).transpose(0, 2, 1, 3)
    top = jnp.pad(uk, ((0, 0), (0, 0), (0, 0), (0, LANES - HEAD_DIM)))
    eye = np.zeros((LANES, LANES), np.float32)
    eye[np.arange(MLA_ROPE), HEAD_DIM + np.arange(MLA_ROPE)] = 1.0
    wka = jnp.concatenate([top, jnp.broadcast_to(eye, (DEPTH, N_HEADS, LANES, LANES))], axis=2).astype(BF16)
    wuv = jnp.pad(mla_w_uv, ((0, 0), (0, 256 - MLA_KV_RANK), (0, 0))).astype(BF16)
    bd = (np.arange(BRANCH_W)[:, None] // HEAD_DIM == np.arange(BRANCH_W)[None, :] // HEAD_DIM).astype(np.float32)
    wb_cd = jnp.swapaxes(w_branch[:, 2:].reshape(DEPTH, 2, 2, 2, HEAD_DIM, D_MODEL), 2, 3
                         ).reshape(DEPTH, 2, BRANCH_W, D_MODEL)
    w_branch = jnp.concatenate([w_branch[:, :2], wb_cd], axis=1)
    row = lambda a: a[:, None, :]
    return dict(
        wmix=wmix, wg=wg, wuq=wuq, wka=wka, wuv=wuv, bd=jnp.asarray(bd, BF16),
        qn=row(mla_q_norm), kvn=row(mla_kv_norm),
        gqn=row(jnp.tile(gqa_q_norm, (1, N_HEADS))), gkn=row(jnp.tile(gqa_k_norm, (1, 2))),
        decf=ret_decay_fwd, decb=ret_decay_bwd, gn=row(ret_gn_gain), sink=win_sink,
        wb=w_branch.astype(BF16), wo=w_o.astype(BF16), ln1g=row(ln1_g), ln1b=row(ln1_b),
        wup=w_up.astype(BF16), wdn=w_down.astype(BF16), ln2g=row(ln2_g), ln2b=row(ln2_b))


def _axial_tables(t, rot_dim):
    rows = t // GRID_W
    row = np.repeat(np.arange(rows, dtype=np.float32), GRID_W)
    col = (np.arange(t) % GRID_W).astype(np.float32)
    n_freq = rot_dim // 4
    inv = np.power(np.float32(ROPE_BASE), -np.arange(n_freq, dtype=np.float32) / np.float32(n_freq))
    ang = np.concatenate([row[:, None] * inv, col[:, None] * inv], axis=-1).astype(np.float32)
    return np.cos(ang), np.sin(ang)


def _rope_tables(t):
    ca, sa = _axial_tables(t, MLA_ROPE)
    ch, sh = _axial_tables(t, HEAD_DIM)
    one = lambda n: np.ones((t, n), np.float32)
    zero = lambda n: np.zeros((t, n), np.float32)
    cq = np.concatenate([one(HEAD_DIM), ca, ca, one(32)], axis=1)
    sq = np.concatenate([zero(HEAD_DIM), -sa, sa, zero(32)], axis=1)
    ck = np.concatenate([ca, ca, one(96)], axis=1)
    sk = np.concatenate([-sa, sa, zero(96)], axis=1)
    chh = np.concatenate([ch, ch, ch, ch], axis=1)
    shh = np.concatenate([-sh, sh, -sh, sh], axis=1)
    return tuple(jnp.asarray(a, F32) for a in (cq, sq, ck, sk, chh, shh))


def _block_diag(s):
    b = s.shape[0]
    same = np.eye(N_HEADS, dtype=bool)[None, :, None, :, None]
    return jnp.where(same, s[:, :, :, None, :], 0.0).reshape(b, BRANCH_W, BRANCH_W)


def kernel(x_prompt, x_sample, cache_mla_ckv, cache_mla_kpe, cache_win_k, cache_win_v, cache_gqa_k, cache_gqa_v,
           state_ret_fwd, state_ret_bwd, c, c_ctx, w_ada, b_ada, w_in, mla_q_norm, mla_w_uq, mla_kv_norm, mla_w_uk,
           mla_w_uv, ret_decay_fwd, ret_decay_bwd, ret_gn_gain, win_sink, gqa_q_norm, gqa_k_norm, w_branch, w_o,
           ln1_g, ln1_b, w_up, w_down, ln2_g, ln2_b):
    batch, seq, _ = x_prompt.shape
    dec_b, dec_t, _ = x_sample.shape
    past = cache_mla_ckv.shape[2]

    cond8 = jnp.concatenate([c_ctx[None], c, jnp.zeros((8 - 1 - dec_b, D_MODEL), F32)], axis=0)
    mod = _ada(cond8, w_ada, b_ada).reshape(DEPTH, 8, 6, D_MODEL)
    rope = _rope_tables(dec_t)
    caches = (cache_mla_ckv,
              jnp.pad(cache_mla_kpe, ((0, 0), (0, 0), (0, 0), (0, LANES - MLA_ROPE))),
              cache_win_k.reshape(dec_b, DEPTH, past, LANES), cache_win_v.reshape(dec_b, DEPTH, past, LANES),
              cache_gqa_k.reshape(dec_b, DEPTH, past, LANES), cache_gqa_v.reshape(dec_b, DEPTH, past, LANES))

    wts = _prep_weights(w_in, mla_q_norm, mla_w_uq, mla_kv_norm, mla_w_uk, mla_w_uv, ret_decay_fwd, ret_decay_bwd,
                        ret_gn_gain, win_sink, gqa_q_norm, gqa_k_norm, w_branch, w_o, ln1_g, ln1_b, w_up, w_down,
                        ln2_g, ln2_b)
    zero_state = jnp.zeros((1, BRANCH_W, BRANCH_W), F32)
    n_ctx, n_lat = batch * seq, dec_b * dec_t
    per_b = lambda a: a.reshape(batch, seq, a.shape[-1])
    per_d = lambda a: a.reshape(dec_b, dec_t, a.shape[-1])
    flat = lambda a: a.reshape(-1, a.shape[-1])
    x = [x_prompt.reshape(n_ctx, D_MODEL), x_sample.reshape(n_lat, D_MODEL)]
    new_caches = None
    new_states = None
    for l in range(DEPTH):
        x_ctx, x_lat = (x[0], 0), ((x[1], 0) if len(x) == 2 else (x[0], n_ctx))
        fq, fkv, fr, *new_caches = _proj(*x_ctx, n_ctx, mod[l], wts, l, None, TM_PROJ, None, seq, new_caches)
        oatt_c = _attn(per_b(fq), per_b(fkv), None, wts, l, seq)
        oret_c, *new_states = _ret(per_b(fr), zero_state, zero_state, wts, l, True, new_states)
        fq, fkv, fr = _proj(*x_lat, n_lat, mod[l], wts, l, rope, TM_PROJ, dec_t)
        oatt_l = _attn(per_d(fq), per_d(fkv), caches, wts, l, TQ_LATENT)
        (oret_l,) = _ret(per_d(fr), _block_diag(state_ret_fwd[:, l]), _block_diag(state_ret_bwd[:, l]), wts, l, False)
        x = _post(x, mod[l], (flat(oatt_c), flat(oatt_l)), (flat(oret_c), flat(oret_l)), wts, l, TM_POST, dec_t,
                  split_out=l == DEPTH - 1)

    ckv, kpe, *kv = new_caches
    kv = [jnp.transpose(a.reshape(batch, DEPTH, 2, HEAD_DIM, seq), (0, 1, 4, 2, 3)) for a in kv]
    return (per_b(x[0]), per_d(x[1]), ckv, jnp.swapaxes(kpe, 2, 3), *kv, *new_states)
```

```python
import functools

import jax
import jax.numpy as jnp
import numpy as np
from jax import lax
from jax.experimental import pallas as pl
from jax.experimental.pallas import tpu as pltpu

D_MODEL = 1024
DEPTH = 2
GRID_W = 64
CHUNK = 128
WINDOW = 128
ROPE_BASE = 10000.0
NORM_EPS = 1e-6
NEG_INF = -1e30
HEAD_DIM = 64
N_HEADS = 4
BRANCH_W = 256
MLA_ROPE = 32
MLA_KV_RANK = 128
LOG2E = 1.4426950408889634
MLA_SCALE = (HEAD_DIM + MLA_ROPE) ** -0.5 * LOG2E
ATT_SCALE = HEAD_DIM ** -0.5 * LOG2E
D_FF = 4 * D_MODEL
ALPHA = (2.0 * DEPTH) ** 0.25
LANES = 128

P_QLAT, P_KVLAT, P_KPE, P_RET, P_WQ, P_WK, P_WV, P_GQ, P_GK, P_GV, P_END = (
    0, 256, 384, 512, 1536, 1792, 1920, 2048, 2304, 2432, 2560)
FQ_W = 1024
FKV_W = 768
VMEM_LIMIT = 60 * 1024 * 1024
TM_PROJ = 1024
TM_POST = 512
SUB_ROWS = 256
TQ_LATENT = 256
ATTN_LOOKAHEAD = 2
MXU_SUM_MIN_KEYS = 1024
RET_GROUP = 4
RET_CHUNKS_PER_STEP = 8

F32 = jnp.float32
BF16 = jnp.bfloat16


def _dot(a, b):
    return jnp.dot(a, b, preferred_element_type=F32)


def _dot_nt(a, b):
    return lax.dot_general(a, b, (((1,), (1,)), ((), ())), preferred_element_type=F32)


def _dot_tn(a, b):
    return lax.dot_general(a, b, (((0,), (0,)), ((), ())), preferred_element_type=F32)


def _sub_tiles(tm, sub_rows=SUB_ROWS):
    n = max(1, tm // sub_rows)
    step = tm // n
    return [slice(i * step, (i + 1) * step) for i in range(n)]


def _layernorm(x, g, b):
    mu = jnp.mean(x, -1, keepdims=True)
    d = x - mu
    var = jnp.mean(d * d, -1, keepdims=True)
    return d * lax.rsqrt(var + NORM_EPS) * g + b


def _rmsnorm(x, g):
    return x * lax.rsqrt(jnp.mean(x * x, -1, keepdims=True) + NORM_EPS) * g


def _seg_sum(x, ones_bd):
    hi = x.astype(BF16)
    lo = (x - hi.astype(F32)).astype(BF16)
    return _dot(hi, ones_bd) + _dot(lo, ones_bd)


def _rope_block(x, cos, sin, half, first):
    rot = jnp.where(first, pltpu.roll(x, LANES - half, 1), pltpu.roll(x, half, 1))
    return x * cos + rot * sin


def _ada_kernel(cond_ref, w_ref, b_ref, o_ref):
    cnd = cond_ref[...]
    s = (cnd * jax.nn.sigmoid(cnd)).astype(BF16)
    o_ref[0] = _dot(s, w_ref[0].astype(BF16)) + b_ref[0]


def _ada(cond8, w_ada, b_ada):
    tn = 1024
    n = w_ada.shape[-1]
    return pl.pallas_call(
        _ada_kernel,
        grid=(DEPTH, n // tn),
        in_specs=[pl.BlockSpec((8, D_MODEL), lambda l, j: (0, 0)),
                  pl.BlockSpec((1, D_MODEL, tn), lambda l, j: (l, 0, j)),
                  pl.BlockSpec((1, 1, tn), lambda l, j: (l, 0, j))],
        out_specs=pl.BlockSpec((1, 8, tn), lambda l, j: (l, 0, j)),
        out_shape=jax.ShapeDtypeStruct((DEPTH, 8, n), F32),
        compiler_params=pltpu.CompilerParams(dimension_semantics=("arbitrary", "arbitrary")),
        name="ada_mod",
    )(cond8, w_ada, b_ada.reshape(DEPTH, 1, n))


def _store_per_seq(o_ref, val, row0, transposed):
    seq = o_ref.shape[2] if transposed else o_ref.shape[1]
    for i in range(val.shape[0] // seq):
        blk = val[i * seq:(i + 1) * seq, :]
        o_ref[row0 // seq + i] = blk.T[0:o_ref.shape[1], :] if transposed else blk


def _proj_kernel(*refs, latent, n_alias):
    refs = refs[:9] + refs[9 + n_alias:]
    x_ref, mod_ref, wmix_ref = refs[:3]
    sh1 = mod_ref[0, 0:1, :]
    sc1 = mod_ref[0, 1:2, :]
    subs = _sub_tiles(x_ref.shape[0])
    def project(r):
        return _dot_nt((x_ref[r, :] * (1.0 + sc1) + sh1).astype(BF16), wmix_ref[0])

    ps = [project(subs[0])]
    for i, r in enumerate(subs):
        if i + 1 < len(subs):
            ps.append(project(subs[i + 1]))
        _proj_post(r, ps.pop(0), refs, latent)


def _proj_post(r, p, refs, latent):
    if latent:
        (_, _, _, qn_ref, wuq_ref, kvn_ref, gqn_ref, gkn_ref, bd_ref,
         cq_ref, sq_ref, ck_ref, sk_ref, ch_ref, sh_ref, fq_ref, fkv_ref, fr_ref) = refs
    else:
        (_, _, _, qn_ref, wuq_ref, kvn_ref, gqn_ref, gkn_ref, bd_ref,
         fq_ref, fkv_ref, fr_ref, ockv_ref, okpe_ref, owk_ref, owv_ref, ogk_ref, ogv_ref) = refs
    rows = p.shape[0]
    lane = lax.broadcasted_iota(jnp.int32, (rows, LANES), 1)
    first_head = (lane % HEAD_DIM) < (HEAD_DIM // 2)

    def rope_heads(v):
        if not latent:
            return v
        cos, sin = ch_ref[r, :], sh_ref[r, :]
        blocks = [_rope_block(v[:, j:j + LANES], cos, sin, HEAD_DIM // 2, first_head)
                  for j in range(0, v.shape[1], LANES)]
        return blocks[0] if len(blocks) == 1 else jnp.concatenate(blocks, axis=1)

    qn = _rmsnorm(p[:, P_QLAT:P_KVLAT], qn_ref[0]).astype(BF16)
    if latent:
        qa2 = _dot(qn, wuq_ref[0])
        cos, sin = cq_ref[r, :], sq_ref[r, :]
        qa = jnp.concatenate([qa2[:, j:j + LANES] * cos + qa2[:, 512 + j:512 + j + LANES] * sin
                              for j in range(0, 512, LANES)], axis=1)
    else:
        qa = _dot(qn, wuq_ref[0, :, 0:512])
    fq_ref[r, 0:512] = (qa * MLA_SCALE).astype(BF16)

    ckv = _rmsnorm(p[:, P_KVLAT:P_KPE], kvn_ref[0])
    kpe = p[:, P_KPE:P_RET]
    if latent:
        first = lane < MLA_ROPE // 2
        kpe_r = _rope_block(kpe, ck_ref[r, :], sk_ref[r, :], MLA_ROPE // 2, first)
    else:
        kpe_r = kpe
        _store_per_seq(ockv_ref, ckv, r.start, False)
        _store_per_seq(okpe_ref, kpe, r.start, True)
    fkv_ref[r, 0:128] = ckv.astype(BF16)
    fkv_ref[r, 128:256] = kpe_r.astype(BF16)

    fr_ref[r, 0:256] = p[:, P_RET:P_RET + 256]
    fr_ref[r, 256:512] = p[:, P_RET + 256:P_RET + 512] * (HEAD_DIM ** -0.5)
    fr_ref[r, 512:1024] = p[:, P_RET + 512:P_WQ]

    fq_ref[r, 512:768] = (rope_heads(p[:, P_WQ:P_WK]) * ATT_SCALE).astype(BF16)
    wk = p[:, P_WK:P_WV]
    wv = p[:, P_WV:P_GQ]
    fkv_ref[r, 256:384] = rope_heads(wk).astype(BF16)
    fkv_ref[r, 384:512] = wv.astype(BF16)

    bd = bd_ref[...]
    gq = p[:, P_GQ:P_GK]
    gqn = gq * lax.rsqrt(_seg_sum(gq * gq, bd) * (1.0 / HEAD_DIM) + NORM_EPS) * gqn_ref[0]
    fq_ref[r, 768:1024] = (rope_heads(gqn) * ATT_SCALE).astype(BF16)
    gk = p[:, P_GK:P_GV]
    gkn = gk * lax.rsqrt(_seg_sum(gk * gk, bd[0:128, 0:128]) * (1.0 / HEAD_DIM) + NORM_EPS) * gkn_ref[0]
    gv = p[:, P_GV:P_END]
    fkv_ref[r, 512:640] = rope_heads(gkn).astype(BF16)
    fkv_ref[r, 640:768] = gv.astype(BF16)
    if not latent:
        _store_per_seq(owk_ref, wk, r.start, True)
        _store_per_seq(owv_ref, wv, r.start, True)
        _store_per_seq(ogk_ref, gkn, r.start, True)
        _store_per_seq(ogv_ref, gv, r.start, True)


def _layer_spec(layer, *s):
    return pl.BlockSpec((1,) + s, lambda t: (layer,) + (0,) * len(s), pipeline_mode=pl.Buffered(1))


def _mod_spec(tm, group_len):
    if group_len is None:
        return pl.BlockSpec((1, 6, D_MODEL), lambda t: (0, 0, 0))
    per = group_len // tm
    return pl.BlockSpec((1, 6, D_MODEL), lambda t: (1 + t // per, 0, 0))


def _proj(x, row0, nt, mod, wts, layer, rope, tm, group_len, ctx_seq=None, caches=None):
    latent = group_len is not None
    aliases = {}
    tile0 = row0 // tm
    tok = lambda w: pl.BlockSpec((tm, w), lambda t: (t, 0))
    in_specs = [pl.BlockSpec((tm, D_MODEL), lambda t: (t + tile0, 0)), _mod_spec(tm, group_len),
                _layer_spec(layer, P_END, D_MODEL), _layer_spec(layer, 1, 256), _layer_spec(layer, 256, 1024),
                _layer_spec(layer, 1, 128), _layer_spec(layer, 1, 256), _layer_spec(layer, 1, 128),
                pl.BlockSpec((256, 256), lambda t: (0, 0))]
    args = [x, mod, wts["wmix"], wts["qn"], wts["wuq"], wts["kvn"], wts["gqn"], wts["gkn"], wts["bd"]]
    out_specs = [tok(FQ_W), tok(FKV_W), tok(1024)]
    out_shape = [jax.ShapeDtypeStruct((nt, FQ_W), BF16), jax.ShapeDtypeStruct((nt, FKV_W), BF16),
                 jax.ShapeDtypeStruct((nt, 1024), F32)]
    if latent:
        per = group_len // tm
        in_specs += [pl.BlockSpec((tm, LANES), lambda t: (t % per, 0))] * 6
        args += list(rope)
    else:
        nb = tm // ctx_seq
        seqs = nt // ctx_seq
        sub = _sub_tiles(tm)[0]
        assert (sub.stop - sub.start) % ctx_seq == 0, "a sub-tile must hold whole context sequences"
        out_specs += [pl.BlockSpec((nb, None, ctx_seq, 128), lambda t: (t, layer, 0, 0))]
        out_shape += [jax.ShapeDtypeStruct((seqs, DEPTH, ctx_seq, 128), F32)]
        for r in (MLA_ROPE, 128, 128, 128, 128):
            out_specs.append(pl.BlockSpec((nb, None, r, ctx_seq), lambda t: (t, layer, 0, 0)))
            out_shape.append(jax.ShapeDtypeStruct((seqs, DEPTH, r, ctx_seq), F32))
        if caches is not None:
            aliases = {len(args) + i: 3 + i for i in range(len(caches))}
            in_specs += [pl.BlockSpec(memory_space=pl.ANY)] * len(caches)
            args += list(caches)
    return pl.pallas_call(
        functools.partial(_proj_kernel, latent=latent, n_alias=len(aliases)),
        grid=(nt // tm,), in_specs=in_specs, out_specs=out_specs, out_shape=out_shape,
        input_output_aliases=aliases,
        compiler_params=pltpu.CompilerParams(dimension_semantics=("arbitrary",), vmem_limit_bytes=VMEM_LIMIT),
        name="proj_latent" if latent else "proj_ctx",
    )(*args)


def _softmax_weights(parts, sink, mxu_sum):
    m = parts[0].max(-1, keepdims=True)
    for s in parts[1:]:
        m = jnp.maximum(m, s.max(-1, keepdims=True))
    if sink is not None:
        m = jnp.maximum(m, sink)
    es = [jnp.exp2(s - m) for s in parts]
    extra = None if sink is None else jnp.exp2(sink - m)
    if not mxu_sum:
        for e in es:
            extra = e.sum(-1, keepdims=True) if extra is None else extra + e.sum(-1, keepdims=True)
    return [e.astype(BF16) for e in es], extra


def _normalised(pv, extra):
    if pv.shape[1] == LANES:
        return pv * (1.0 / extra)
    den = pv[:, LANES:LANES + 1]
    if extra is not None:
        den = den + extra
    return pv[:, 0:LANES] * (1.0 / den)


def _window_start(qi, tq, t):
    return jnp.clip(qi * tq - WINDOW, 0, t - (tq + 2 * WINDOW))


def _band_bias(tq):
    span = tq + 2 * WINDOW
    r = (np.arange(2 * tq) % tq)[None, :, None]
    c = np.arange(span)[None, None, :]
    off = (np.arange(3) * WINDOW)[:, None, None]
    return jnp.asarray(np.where(np.abs(r + off - c) <= WINDOW, 0.0, NEG_INF), F32)


def _attn_kernel(*refs, t, tq, n_cache, layer):
    latent = n_cache > 0
    s_len = t + n_cache
    if latent:
        (fq_ref, fkv_ref, cckv_ref, ckpe_ref, cwk_ref, cwv_ref, cgk_ref, cgv_ref, bias_ref,
         wka_ref, wuv_ref, sink_ref, o_ref, ckpe_s, ka_s, va_s, kd_s, vd_s, vc_s, kcc_s, vcc_s) = refs
    else:
        (fq_ref, fkv_ref, wka_ref, wuv_ref, sink_ref, o_ref, ckpe_s, ka_s, va_s, kd_s, vd_s, vc_s) = refs

    qi = pl.program_id(1)

    @pl.when(qi == 0)
    def _():
        ckpe_s[0:t, :] = fkv_ref[0, :, 0:256]
        kd_s[0:t, :] = fkv_ref[0, :, 512:640]
        vd_s[0:t, 0:LANES] = fkv_ref[0, :, 640:768]
        vc_s[:, 0:LANES] = fkv_ref[0, :, 384:512]
        if latent:
            ckpe_s[t:s_len, 0:128] = cckv_ref[0, 0].astype(BF16)
            ckpe_s[t:s_len, 128:256] = ckpe_ref[0, 0].astype(BF16)
            kd_s[t:s_len, :] = cgk_ref[0, 0].astype(BF16)
            vd_s[t:s_len, 0:LANES] = cgv_ref[0, 0].astype(BF16)
            kcc_s[...] = cwk_ref[0, 0].astype(BF16)
            vcc_s[:, 0:LANES] = cwv_ref[0, 0].astype(BF16)
            vcc_s[:, LANES:] = jnp.ones((n_cache, LANES), BF16)
        vd_s[:, LANES:] = jnp.ones((s_len, LANES), BF16)
        vc_s[:, LANES:] = jnp.ones((t, LANES), BF16)
        ck = ckpe_s[...]
        va = _dot(ck, wuv_ref[0]).astype(BF16)
        for blk in range(2):
            va_s[blk, :, 0:LANES] = va[:, blk * LANES:(blk + 1) * LANES]
            va_s[blk, :, LANES:] = jnp.ones((s_len, LANES), BF16)
        for h in range(N_HEADS):
            ka_s[h] = _dot(ck, wka_ref[0, h]).astype(BF16)

    lane = lax.broadcasted_iota(jnp.int32, (tq, LANES), 1)
    half = [lane < HEAD_DIM, lane >= HEAD_DIM]

    def keep(x, j):
        return jnp.where(half[j], x, 0.0)

    def stacked_q(col, j):
        blks = [fq_ref[0, :, col + g * LANES:col + (g + 1) * LANES] for g in range(2)]
        return jnp.concatenate([keep(b.astype(F32), j).astype(BF16) for b in blks], axis=0)

    if latent:
        span = tq + 2 * WINDOW
        start = pl.multiple_of(_window_start(qi, tq, t), LANES)
    row1 = lax.broadcasted_iota(jnp.int32, (2 * tq, 1), 0)
    mxu_sum = s_len >= MXU_SUM_MIN_KEYS
    vw = 2 * LANES if mxu_sum else LANES

    acc_a = [jnp.zeros((tq, LANES), F32) for _ in range(2)]
    acc_c = [jnp.zeros((tq, LANES), F32) for _ in range(2)]
    acc_d = [jnp.zeros((tq, LANES), F32) for _ in range(2)]

    def a_scores(h):
        return [_dot_nt(fq_ref[0, :, h * LANES:(h + 1) * LANES], ka_s[h])]

    def a_finish(h, ps, sink_term):
        blk = h // 2
        acc_a[blk] = acc_a[blk] + keep(_normalised(_dot(ps[0], va_s[blk, :, 0:vw]), sink_term), h % 2)

    def c_scores(j):
        qs = stacked_q(512, j)
        if latent:
            return [_dot_nt(qs, fkv_ref[0, pl.ds(start, span), 256:384]) + bias_ref[0], _dot_nt(qs, kcc_s[...])]
        return [_dot_nt(qs, fkv_ref[0, :, 256:384])]

    def c_finish(j, ps, sink_term):
        if latent:
            pv = _dot(ps[0], vc_s[pl.ds(start, span), 0:vw]) + _dot(ps[1], vcc_s[:, 0:vw])
        else:
            pv = _dot(ps[0], vc_s[:, 0:vw])
        pv = _normalised(pv, sink_term)
        for g in range(2):
            acc_c[g] = acc_c[g] + keep(pv[g * tq:(g + 1) * tq], j)

    def d_scores(j):
        return [_dot_nt(stacked_q(768, j), kd_s[...])]

    def d_finish(j, ps, sink_term):
        pv = _normalised(_dot(ps[0], vd_s[:, 0:vw]), sink_term)
        for g in range(2):
            acc_d[g] = acc_d[g] + keep(pv[g * tq:(g + 1) * tq], j)

    def c_sink(j):
        return jnp.where(row1 < tq, sink_ref[layer, 2 * j], sink_ref[layer, 2 * j + 1]) * LOG2E

    jobs = [(functools.partial(a_scores, h), functools.partial(a_finish, h), None) for h in range(N_HEADS)]
    jobs += [(functools.partial(d_scores, j), functools.partial(d_finish, j), None) for j in range(2)]
    jobs += [(functools.partial(c_scores, j), functools.partial(c_finish, j), functools.partial(c_sink, j))
             for j in range(2)]
    ahead = ATTN_LOOKAHEAD if mxu_sum else len(jobs)
    pending = [job[0]() for job in jobs[:ahead]]
    for i, (_, finish, sink) in enumerate(jobs):
        if i + ahead < len(jobs):
            pending.append(jobs[i + ahead][0]())
        finish(*_softmax_weights(pending.pop(0), None if sink is None else sink(), mxu_sum))

    for g in range(2):
        o_ref[0, :, g * LANES:(g + 1) * LANES] = acc_a[g].astype(BF16)
        o_ref[0, :, 256 + g * LANES:256 + (g + 1) * LANES] = acc_c[g].astype(BF16)
        o_ref[0, :, 512 + g * LANES:512 + (g + 1) * LANES] = acc_d[g].astype(BF16)


def _attn(fq, fkv, caches, wts, layer, tq):
    b, t, _ = fq.shape
    latent = caches is not None
    n_cache = caches[0].shape[2] if latent else 0
    s_len = t + n_cache
    in_specs = [pl.BlockSpec((1, tq, FQ_W), lambda i, q: (i, q, 0)),
                pl.BlockSpec((1, t, FKV_W), lambda i, q: (i, 0, 0))]
    args = [fq, fkv]
    if latent:
        in_specs += [pl.BlockSpec((1, 1, n_cache, LANES), lambda i, q: (i, layer, 0, 0))] * 6
        args += list(caches)
        span = tq + 2 * WINDOW
        assert t >= span and WINDOW <= tq
        in_specs.append(pl.BlockSpec((1, 2 * tq, span),
                                     lambda i, q: ((q * tq - _window_start(q, tq, t)) // WINDOW, 0, 0)))
        args.append(_band_bias(tq))
    in_specs += [pl.BlockSpec((1, N_HEADS, 256, 128), lambda i, q: (layer, 0, 0, 0)),
                 pl.BlockSpec((1, 256, 256), lambda i, q: (layer, 0, 0)),
                 pl.BlockSpec(memory_space=pltpu.SMEM)]
    args += [wts["wka"], wts["wuv"], wts["sink"]]
    scratch = [pltpu.VMEM((s_len, 256), BF16), pltpu.VMEM((N_HEADS, s_len, 128), BF16),
               pltpu.VMEM((2, s_len, 256), BF16), pltpu.VMEM((s_len, 128), BF16), pltpu.VMEM((s_len, 256), BF16),
               pltpu.VMEM((t, 256), BF16)]
    if latent:
        scratch += [pltpu.VMEM((n_cache, 128), BF16), pltpu.VMEM((n_cache, 256), BF16)]
    return pl.pallas_call(
        functools.partial(_attn_kernel, t=t, tq=tq, n_cache=n_cache, layer=layer),
        grid=(b, t // tq), in_specs=in_specs,
        out_specs=pl.BlockSpec((1, tq, 768), lambda i, q: (i, q, 0)),
        out_shape=jax.ShapeDtypeStruct((b, t, 768), BF16),
        scratch_shapes=scratch,
        compiler_params=pltpu.CompilerParams(dimension_semantics=("arbitrary", "arbitrary"),
                                             vmem_limit_bytes=VMEM_LIMIT),
        name="attn_latent" if latent else "attn_ctx",
    )(*args)


def _ret_kernel(*refs, t, bb, s0_stride, layer, want_state, n_alias):
    refs = refs[:7] + refs[7 + n_alias:]
    if want_state:
        (fr_ref, s0f_ref, s0b_ref, decf_ref, decb_ref, gain_ref, bd_ref, o_ref, sf_ref, sb_ref,
         dm_s, dec_s, cd_s, kvf_s, kvb_s, stf_s, stb_s, sf_s, sb_s) = refs
    else:
        (fr_ref, s0f_ref, s0b_ref, decf_ref, decb_ref, gain_ref, bd_ref, o_ref,
         dm_s, dec_s, cd_s, kvf_s, kvb_s, stf_s, stb_s, sf_s, sb_s) = refs
    nc = t // CHUNK
    w = BRANCH_W
    lane_w = lax.broadcasted_iota(jnp.int32, (CHUNK, w), 1) // HEAD_DIM

    @pl.when(pl.program_id(0) == 0)
    def _():
        row_w = lax.broadcasted_iota(jnp.int32, (CHUNK, w), 0).astype(F32)

        def lane_decay(dec_ref):
            v = jnp.zeros((CHUNK, w), F32)
            for h in range(N_HEADS):
                v = jnp.where(lane_w == h, dec_ref[layer, h], v)
            return jax.nn.log_sigmoid(v)

        lgf = lane_decay(decf_ref)
        lgb = lane_decay(decb_ref)
        dec_s[0] = jnp.exp((row_w + 1.0) * lgf)
        dec_s[1] = jnp.exp((CHUNK - 1.0 - row_w) * lgf)
        dec_s[2] = jnp.exp((CHUNK - row_w) * lgb)
        dec_s[3] = jnp.exp(row_w * lgb)
        cd_s[0] = jnp.concatenate([jnp.exp(CHUNK * lgf)] * (w // CHUNK), axis=0)
        cd_s[1] = jnp.concatenate([jnp.exp(CHUNK * lgb)] * (w // CHUNK), axis=0)
        ii = lax.broadcasted_iota(jnp.int32, (CHUNK, CHUNK), 0).astype(F32)
        jj = lax.broadcasted_iota(jnp.int32, (CHUNK, CHUNK), 1).astype(F32)
        diff = ii - jj
        for h in range(N_HEADS):
            lf = jax.nn.log_sigmoid(jnp.full((CHUNK, CHUNK), decf_ref[layer, h], F32))
            lb = jax.nn.log_sigmoid(jnp.full((CHUNK, CHUNK), decb_ref[layer, h], F32))
            d_f = jnp.where(diff >= 0, jnp.exp(jnp.maximum(diff, 0.0) * lf), 0.0)
            d_b = jnp.where(diff < 0, jnp.exp(jnp.maximum(-diff, 0.0) * lb), 0.0)
            dm_s[h] = d_f + d_b

    r2 = lax.broadcasted_iota(jnp.int32, (w, w), 0) // HEAD_DIM
    c2 = lax.broadcasted_iota(jnp.int32, (w, w), 1) // HEAD_DIM
    diag = r2 == c2

    jobs = [(bi, n) for bi in range(bb) for n in range(nc)]
    groups = [jobs[i:i + RET_GROUP] for i in range(0, len(jobs), RET_GROUP)]
    rows = lambda n: slice(n * CHUNK, (n + 1) * CHUNK)
    slot = lambda bi, n: bi * nc + n

    for grp in groups:
        kvs = []
        for bi, n in grp:
            k = fr_ref[bi, rows(n), 256:512]
            vb = fr_ref[bi, rows(n), 512:768].astype(BF16)
            kk = jnp.concatenate([k * dec_s[1], k * dec_s[3]], axis=1).astype(BF16)
            kvs.append(_dot_tn(kk, vb))
        for (bi, n), kv in zip(grp, kvs):
            kvf_s[slot(bi, n)] = jnp.where(diag, kv[0:w], 0.0)
            kvb_s[slot(bi, n)] = jnp.where(diag, kv[w:2 * w], 0.0)

    for bi in range(bb):
        sf_s[bi] = s0f_ref[bi * s0_stride]
        sb_s[bi] = s0b_ref[bi * s0_stride]
    for i in range(nc):
        m = nc - 1 - i
        for bi in range(bb):
            sf = sf_s[bi]
            stf_s[slot(bi, i)] = sf.astype(BF16)
            sf_s[bi] = sf * cd_s[0] + kvf_s[slot(bi, i)]
            sb = sb_s[bi]
            stb_s[slot(bi, m)] = sb.astype(BF16)
            sb_s[bi] = sb * cd_s[1] + kvb_s[slot(bi, m)]
    if want_state:
        for bi in range(bb):
            for h in range(N_HEADS):
                sl = slice(h * HEAD_DIM, (h + 1) * HEAD_DIM)
                sf_ref[bi, h] = sf_s[bi, sl, sl]
                sb_ref[bi, h] = sb_s[bi, sl, sl]

    gain = gain_ref[0]
    bd2 = jnp.concatenate([bd_ref[...], bd_ref[...]], axis=0)

    def seg_mean(xs):
        cat = []
        for x in xs:
            hi = x.astype(BF16)
            cat.append(jnp.concatenate([hi, (x - hi.astype(F32)).astype(BF16)], axis=1))
        return [_dot(c, bd2) * (1.0 / HEAD_DIM) for c in cat]

    for grp in groups:
        qs = [fr_ref[bi, rows(n), 0:256] for bi, n in grp]
        vs = [fr_ref[bi, rows(n), 512:768] for bi, n in grp]
        qks, inter_f, inter_b = [], [], []
        for (bi, n), q in zip(grp, qs):
            kb = fr_ref[bi, rows(n), 256:512].astype(BF16)
            qstack = jnp.concatenate([jnp.where(lane_w == h, q, 0.0) for h in range(N_HEADS)], axis=0)
            qks.append(_dot_nt(qstack.astype(BF16), kb))
            qb = q.astype(BF16)
            inter_f.append(_dot(qb, stf_s[slot(bi, n)]))
            inter_b.append(_dot(qb, stb_s[slot(bi, n)]))
        outs = []
        for qk, v, i_f, i_b in zip(qks, vs, inter_f, inter_b):
            att = jnp.concatenate([qk[rows(h)] * dm_s[h] for h in range(N_HEADS)], axis=1)
            vstack = jnp.concatenate([jnp.where(lane_w == h, v, 0.0) for h in range(N_HEADS)], axis=0)
            outs.append(_dot(att.astype(BF16), vstack.astype(BF16)) + i_f * dec_s[0] + i_b * dec_s[2])
        ds = [o - mu for o, mu in zip(outs, seg_mean(outs))]
        var = seg_mean([d * d for d in ds])
        for (bi, n), d, vr in zip(grp, ds, var):
            gate = fr_ref[bi, rows(n), 768:1024]
            y = d * lax.rsqrt(vr + NORM_EPS) * gain
            o_ref[bi, rows(n), :] = (gate * jax.nn.sigmoid(gate) * y).astype(BF16)


def _ret(fr, s0f, s0b, wts, layer, want_state, states=None):
    b, t, _ = fr.shape
    nc = t // CHUNK
    bb = max(1, RET_CHUNKS_PER_STEP // nc)
    shared = s0f.shape[0] != b
    s_spec = (pl.BlockSpec((1, BRANCH_W, BRANCH_W), lambda i: (0, 0, 0)) if shared
              else pl.BlockSpec((bb, BRANCH_W, BRANCH_W), lambda i: (i, 0, 0)))
    smem = pl.BlockSpec(memory_space=pltpu.SMEM)
    out_specs = [pl.BlockSpec((bb, t, BRANCH_W), lambda i: (i, 0, 0))]
    out_shape = [jax.ShapeDtypeStruct((b, t, BRANCH_W), BF16)]
    in_specs = [pl.BlockSpec((bb, t, 1024), lambda i: (i, 0, 0)), s_spec, s_spec, smem, smem,
                pl.BlockSpec((1, 1, BRANCH_W), lambda i: (layer, 0, 0)),
                pl.BlockSpec((BRANCH_W, BRANCH_W), lambda i: (0, 0))]
    args = [fr, s0f, s0b, wts["decf"], wts["decb"], wts["gn"], wts["bd"]]
    aliases = {}
    if want_state:
        st_spec = pl.BlockSpec((bb, None, N_HEADS, HEAD_DIM, HEAD_DIM), lambda i: (i, layer, 0, 0, 0))
        out_specs += [st_spec, st_spec]
        out_shape += [jax.ShapeDtypeStruct((b, DEPTH, N_HEADS, HEAD_DIM, HEAD_DIM), F32)] * 2
        if states is not None:
            aliases = {len(args) + i: 1 + i for i in range(2)}
            in_specs += [pl.BlockSpec(memory_space=pl.ANY)] * 2
            args += list(states)
    sq = (BRANCH_W, BRANCH_W)
    return pl.pallas_call(
        functools.partial(_ret_kernel, t=t, bb=bb, s0_stride=0 if shared else 1, layer=layer,
                          want_state=want_state, n_alias=len(aliases)),
        grid=(b // bb,),
        in_specs=in_specs, out_specs=out_specs, out_shape=out_shape, input_output_aliases=aliases,
        scratch_shapes=[pltpu.VMEM((N_HEADS, CHUNK, CHUNK), F32), pltpu.VMEM((4, CHUNK, BRANCH_W), F32),
                        pltpu.VMEM((2,) + sq, F32),
                        pltpu.VMEM((bb * nc,) + sq, F32), pltpu.VMEM((bb * nc,) + sq, F32),
                        pltpu.VMEM((bb * nc,) + sq, BF16), pltpu.VMEM((bb * nc,) + sq, BF16),
                        pltpu.VMEM((bb,) + sq, F32), pltpu.VMEM((bb,) + sq, F32)],
        compiler_params=pltpu.CompilerParams(dimension_semantics=("arbitrary",),
                                             vmem_limit_bytes=VMEM_LIMIT),
        name="retention",
    )(*args)


def _post_kernel(*refs, n_x, n_out, ctx_tiles):
    x_refs, refs = refs[:n_x], refs[n_x:]
    (mod_ref, oatt_c, oatt_l, oret_c, oret_l, wg_ref, wb_ref, wo_ref, g1_ref, b1_ref,
     wup_ref, wdn_ref, g2_ref, b2_ref) = refs[:14]
    o_refs = refs[14:]
    is_ctx = pl.program_id(0) < ctx_tiles
    pick = lambda c_ref, l_ref, r, cols: jnp.where(is_ctx, c_ref[r, cols], l_ref[r, cols])
    every = slice(None)
    if n_x == 1:
        x_rows = lambda r: x_refs[0][r, :]
    else:
        x_rows = lambda r: pick(x_refs[0], x_refs[1], r, every)
    sh1, sc1, g1, sh2, sc2, g2 = [mod_ref[0, i:i + 1, :] for i in range(6)]
    subs = _sub_tiles(x_refs[0].shape[0])
    ff_half = D_FF // 2
    xs = [x_rows(r) for r in subs]

    def gated(r, x):
        h = (x * (1.0 + sc1) + sh1).astype(BF16)
        branches = (pick(oatt_c, oatt_l, r, slice(0, 256)), pick(oret_c, oret_l, r, every),
                    pick(oatt_c, oatt_l, r, slice(256, 512)), pick(oatt_c, oatt_l, r, slice(512, 768)))
        tsum = None
        for i, o in enumerate(branches):
            gate = jax.nn.sigmoid(_dot_nt(h, wg_ref[0, i * D_MODEL:(i + 1) * D_MODEL, :]))
            term = gate * _dot(o, wb_ref[0, i])
            tsum = term if tsum is None else tsum + term
        return tsum

    sums = [gated(r, x) for r, x in zip(subs, xs)]
    ys = [_dot(s.astype(BF16), wo_ref[0]) for s in sums]
    x1s = [_layernorm(ALPHA * x + g1 * y, g1_ref[0], b1_ref[0]) for x, y in zip(xs, ys)]
    hs = [(x1 * (1.0 + sc2) + sh2).astype(BF16) for x1 in x1s]
    us = [[_dot(h, wup_ref[0, :, c * ff_half:(c + 1) * ff_half]) for c in range(2)] for h in hs]
    fs = []
    for u2 in us:
        f = None
        for c, u in enumerate(u2):
            u = jnp.maximum(u, 0.0)
            part = _dot((u * u).astype(BF16), wdn_ref[0, c * ff_half:(c + 1) * ff_half, :])
            f = part if f is None else f + part
        fs.append(f)
    outs = [_layernorm(ALPHA * x1 + g2 * f, g2_ref[0], b2_ref[0]) for x1, f in zip(x1s, fs)]
    if n_out == 1:
        for r, o in zip(subs, outs):
            o_refs[0][r, :] = o
    else:
        @pl.when(is_ctx)
        def _():
            for r, o in zip(subs, outs):
                o_refs[0][r, :] = o

        @pl.when(jnp.logical_not(is_ctx))
        def _():
            for r, o in zip(subs, outs):
                o_refs[1][r, :] = o


def _post(xs, mod, oatts, orets, wts, layer, tm, latent_len, split_out):
    n_ctx, n_lat = oatts[0].shape[0], oatts[1].shape[0]
    ctx_tiles, lat_tiles = n_ctx // tm, n_lat // tm
    per = latent_len // tm
    all_tok = lambda w: pl.BlockSpec((tm, w), lambda t: (t, 0))
    ctx_tok = lambda w: pl.BlockSpec((tm, w), lambda t: (jnp.minimum(t, ctx_tiles - 1), 0))
    lat_tok = lambda w: pl.BlockSpec((tm, w), lambda t: (jnp.maximum(t - ctx_tiles, 0), 0))
    pair = lambda w: [ctx_tok(w), lat_tok(w)]
    vec = _layer_spec(layer, 1, D_MODEL)
    mod_spec = pl.BlockSpec((1, 6, D_MODEL),
                            lambda t: (jnp.where(t < ctx_tiles, 0, 1 + (t - ctx_tiles) // per), 0, 0))
    x_specs = [all_tok(D_MODEL)] if len(xs) == 1 else pair(D_MODEL)
    if split_out:
        out_specs = pair(D_MODEL)
        out_shape = [jax.ShapeDtypeStruct((n_ctx, D_MODEL), F32), jax.ShapeDtypeStruct((n_lat, D_MODEL), F32)]
    else:
        out_specs = [all_tok(D_MODEL)]
        out_shape = [jax.ShapeDtypeStruct((n_ctx + n_lat, D_MODEL), F32)]
    return pl.pallas_call(
        functools.partial(_post_kernel, n_x=len(xs), n_out=len(out_shape), ctx_tiles=ctx_tiles),
        grid=(ctx_tiles + lat_tiles,),
        in_specs=x_specs + [mod_spec] + pair(768) + pair(BRANCH_W) + [
            _layer_spec(layer, 4 * D_MODEL, D_MODEL), _layer_spec(layer, 4, BRANCH_W, D_MODEL),
            _layer_spec(layer, D_MODEL, D_MODEL), vec, vec,
            _layer_spec(layer, D_MODEL, D_FF), _layer_spec(layer, D_FF, D_MODEL), vec, vec],
        out_specs=out_specs, out_shape=out_shape,
        compiler_params=pltpu.CompilerParams(dimension_semantics=("arbitrary",), vmem_limit_bytes=VMEM_LIMIT),
        name="merge_mlp",
    )(*xs, mod, *oatts, *orets, wts["wg"], wts["wb"], wts["wo"], wts["ln1g"], wts["ln1b"],
      wts["wup"], wts["wdn"], wts["ln2g"], wts["ln2b"])


def _win_kernel(w_ref, wmix_ref, wg_ref):
    o_ret = 256 + 128 + MLA_ROPE
    o_wq, o_wk, o_gq, o_gk, o_gate = o_ret + 1024, o_ret + 1280, o_ret + 1536, o_ret + 1792, o_ret + 2048

    def put(dst, src, n):
        wmix_ref[0, dst:dst + n, :] = w_ref[0, src:src + n, :].astype(BF16)

    def put_swapped(dst, src):
        for i, h in enumerate((0, 2, 1, 3)):
            put(dst + i * HEAD_DIM, src + h * HEAD_DIM, HEAD_DIM)

    put(P_QLAT, 0, o_ret)
    pad = P_RET - o_ret
    wmix_ref[0, o_ret:P_RET, :] = jnp.zeros((pad, w_ref.shape[2]), BF16)
    put(P_RET, o_ret, o_wq - o_ret)
    put_swapped(P_WQ, o_wq)
    put(P_WK, o_wk, o_gq - o_wk)
    put_swapped(P_GQ, o_gq)
    put(P_GK, o_gk, o_gate - o_gk)
    wg_ref[0] = w_ref[0, o_gate:, :].astype(BF16)


def _prep_win(w_in):
    w_t = jnp.swapaxes(w_in, 1, 2)
    n_in = w_t.shape[1]
    kb = D_MODEL // 2
    return pl.pallas_call(
        _win_kernel,
        grid=(DEPTH, D_MODEL // kb),
        in_specs=[pl.BlockSpec((1, n_in, kb), lambda l, k: (l, 0, k))],
        out_specs=[pl.BlockSpec((1, P_END, kb), lambda l, k: (l, 0, k)),
                   pl.BlockSpec((1, 4 * D_MODEL, kb), lambda l, k: (l, 0, k))],
        out_shape=[jax.ShapeDtypeStruct((DEPTH, P_END, D_MODEL), BF16),
                   jax.ShapeDtypeStruct((DEPTH, 4 * D_MODEL, D_MODEL), BF16)],
        compiler_params=pltpu.CompilerParams(dimension_semantics=("arbitrary", "arbitrary"),
                                             vmem_limit_bytes=VMEM_LIMIT),
        name="prep_w_in",
    )(w_t)


def _prep_weights(w_in, mla_q_norm, mla_w_uq, mla_kv_norm, mla_w_uk, mla_w_uv, ret_decay_fwd, ret_decay_bwd,
                  ret_gn_gain, win_sink, gqa_q_norm, gqa_k_norm, w_branch, w_o, ln1_g, ln1_b, w_up, w_down,
                  ln2_g, ln2_b):
    wmix, wg = _prep_win(w_in)
    uq = mla_w_uq.reshape(DEPTH, 256, N_HEADS, HEAD_DIM + MLA_ROPE)
    half = MLA_ROPE // 2
    uq_sw = jnp.concatenate([jnp.zeros_like(uq[..., :HEAD_DIM]), uq[..., HEAD_DIM + half:],
                             uq[..., HEAD_DIM:HEAD_DIM + half]], axis=-1)
    lane_pad = ((0, 0), (0, 0), (0, 0), (0, LANES - HEAD_DIM - MLA_ROPE))
    wuq = jnp.concatenate([jnp.pad(w, lane_pad).reshape(DEPTH, 256, N_HEADS * LANES) for w in (uq, uq_sw)],
                          axis=-1).astype(BF16)
    uk = mla_w_uk.reshape(DEPTH, MLA_KV_RANK, N_HEADS, HEAD_DIM).transpose(0, 2, 1, 3)
    top = jnp.pad(uk, ((0, 0), (0, 0), (0, 0), (0, LANES - HEAD_DIM)))
    eye = np.zeros((LANES, LANES), np.float32)
    eye[np.arange(MLA_ROPE), HEAD_DIM + np.arange(MLA_ROPE)] = 1.0
    wka = jnp.concatenate([top, jnp.broadcast_to(eye, (DEPTH, N_HEADS, LANES, LANES))], axis=2).astype(BF16)
    wuv = jnp.pad(mla_w_uv, ((0, 0), (0, 256 - MLA_KV_RANK), (0, 0))).astype(BF16)
    bd = (np.arange(BRANCH_W)[:, None] // HEAD_DIM == np.arange(BRANCH_W)[None, :] // HEAD_DIM).astype(np.float32)
    wb_cd = jnp.swapaxes(w_branch[:, 2:].reshape(DEPTH, 2, 2, 2, HEAD_DIM, D_MODEL), 2, 3
                         ).reshape(DEPTH, 2, BRANCH_W, D_MODEL)
    w_branch = jnp.concatenate([w_branch[:, :2], wb_cd], axis=1)
    row = lambda a: a[:, None, :]
    return dict(
        wmix=wmix, wg=wg, wuq=wuq, wka=wka, wuv=wuv, bd=jnp.asarray(bd, BF16),
        qn=row(mla_q_norm), kvn=row(mla_kv_norm),
        gqn=row(jnp.tile(gqa_q_norm, (1, N_HEADS))), gkn=row(jnp.tile(gqa_k_norm, (1, 2))),
        decf=ret_decay_fwd, decb=ret_decay_bwd, gn=row(ret_gn_gain), sink=win_sink,
        wb=w_branch.astype(BF16), wo=w_o.astype(BF16), ln1g=row(ln1_g), ln1b=row(ln1_b),
        wup=w_up.astype(BF16), wdn=w_down.astype(BF16), ln2g=row(ln2_g), ln2b=row(ln2_b))


def _axial_tables(t, rot_dim):
    rows = t // GRID_W
    row = np.repeat(np.arange(rows, dtype=np.float32), GRID_W)
    col = (np.arange(t) % GRID_W).astype(np.float32)
    n_freq = rot_dim // 4
    inv = np.power(np.float32(ROPE_BASE), -np.arange(n_freq, dtype=np.float32) / np.float32(n_freq))
    ang = np.concatenate([row[:, None] * inv, col[:, None] * inv], axis=-1).astype(np.float32)
    return np.cos(ang), np.sin(ang)


def _rope_tables(t):
    ca, sa = _axial_tables(t, MLA_ROPE)
    ch, sh = _axial_tables(t, HEAD_DIM)
    one = lambda n: np.ones((t, n), np.float32)
    zero = lambda n: np.zeros((t, n), np.float32)
    cq = np.concatenate([one(HEAD_DIM), ca, ca, one(32)], axis=1)
    sq = np.concatenate([zero(HEAD_DIM), -sa, sa, zero(32)], axis=1)
    ck = np.concatenate([ca, ca, one(96)], axis=1)
    sk = np.concatenate([-sa, sa, zero(96)], axis=1)
    chh = np.concatenate([ch, ch, ch, ch], axis=1)
    shh = np.concatenate([-sh, sh, -sh, sh], axis=1)
    return tuple(jnp.asarray(a, F32) for a in (cq, sq, ck, sk, chh, shh))


def _block_diag(s):
    b = s.shape[0]
    same = np.eye(N_HEADS, dtype=bool)[None, :, None, :, None]
    return jnp.where(same, s[:, :, :, None, :], 0.0).reshape(b, BRANCH_W, BRANCH_W)


def kernel(x_prompt, x_sample, cache_mla_ckv, cache_mla_kpe, cache_win_k, cache_win_v, cache_gqa_k, cache_gqa_v,
           state_ret_fwd, state_ret_bwd, c, c_ctx, w_ada, b_ada, w_in, mla_q_norm, mla_w_uq, mla_kv_norm, mla_w_uk,
           mla_w_uv, ret_decay_fwd, ret_decay_bwd, ret_gn_gain, win_sink, gqa_q_norm, gqa_k_norm, w_branch, w_o,
           ln1_g, ln1_b, w_up, w_down, ln2_g, ln2_b):
    batch, seq, _ = x_prompt.shape
    dec_b, dec_t, _ = x_sample.shape
    past = cache_mla_ckv.shape[2]

    cond8 = jnp.concatenate([c_ctx[None], c, jnp.zeros((8 - 1 - dec_b, D_MODEL), F32)], axis=0)
    mod = _ada(cond8, w_ada, b_ada).reshape(DEPTH, 8, 6, D_MODEL)
    rope = _rope_tables(dec_t)
    caches = (cache_mla_ckv,
              jnp.pad(cache_mla_kpe, ((0, 0), (0, 0), (0, 0), (0, LANES - MLA_ROPE))),
              cache_win_k.reshape(dec_b, DEPTH, past, LANES), cache_win_v.reshape(dec_b, DEPTH, past, LANES),
              cache_gqa_k.reshape(dec_b, DEPTH, past, LANES), cache_gqa_v.reshape(dec_b, DEPTH, past, LANES))

    wts = _prep_weights(w_in, mla_q_norm, mla_w_uq, mla_kv_norm, mla_w_uk, mla_w_uv, ret_decay_fwd, ret_decay_bwd,
                        ret_gn_gain, win_sink, gqa_q_norm, gqa_k_norm, w_branch, w_o, ln1_g, ln1_b, w_up, w_down,
                        ln2_g, ln2_b)
    zero_state = jnp.zeros((1, BRANCH_W, BRANCH_W), F32)
    n_ctx, n_lat = batch * seq, dec_b * dec_t
    per_b = lambda a: a.reshape(batch, seq, a.shape[-1])
    per_d = lambda a: a.reshape(dec_b, dec_t, a.shape[-1])
    flat = lambda a: a.reshape(-1, a.shape[-1])
    x = [x_prompt.reshape(n_ctx, D_MODEL), x_sample.reshape(n_lat, D_MODEL)]
    new_caches = None
    new_states = None
    for l in range(DEPTH):
        x_ctx, x_lat = (x[0], 0), ((x[1], 0) if len(x) == 2 else (x[0], n_ctx))
        fq, fkv, fr, *new_caches = _proj(*x_ctx, n_ctx, mod[l], wts, l, None, TM_PROJ, None, seq, new_caches)
        oatt_c = _attn(per_b(fq), per_b(fkv), None, wts, l, seq)
        oret_c, *new_states = _ret(per_b(fr), zero_state, zero_state, wts, l, True, new_states)
        fq, fkv, fr = _proj(*x_lat, n_lat, mod[l], wts, l, rope, TM_PROJ, dec_t)
        oatt_l = _attn(per_d(fq), per_d(fkv), caches, wts, l, TQ_LATENT)
        (oret_l,) = _ret(per_d(fr), _block_diag(state_ret_fwd[:, l]), _block_diag(state_ret_bwd[:, l]), wts, l, False)
        x = _post(x, mod[l], (flat(oatt_c), flat(oatt_l)), (flat(oret_c), flat(oret_l)), wts, l, TM_POST, dec_t,
                  split_out=l == DEPTH - 1)

    ckv, kpe, *kv = new_caches
    kv = [jnp.transpose(a.reshape(batch, DEPTH, 2, HEAD_DIM, seq), (0, 1, 4, 2, 3)) for a in kv]
    return (per_b(x[0]), per_d(x[1]), ckv, jnp.swapaxes(kpe, 2, 3), *kv, *new_states)
```

```python
import functools

import jax
import jax.numpy as jnp
import numpy as np
from jax import lax
from jax.experimental import pallas as pl
from jax.experimental.pallas import tpu as pltpu

D_MODEL = 1024
DEPTH = 2
GRID_W = 64
CHUNK = 128
WINDOW = 128
ROPE_BASE = 10000.0
NORM_EPS = 1e-6
NEG_INF = -1e30
HEAD_DIM = 64
N_HEADS = 4
BRANCH_W = 256
MLA_ROPE = 32
MLA_KV_RANK = 128
LOG2E = 1.4426950408889634
MLA_SCALE = (HEAD_DIM + MLA_ROPE) ** -0.5 * LOG2E
ATT_SCALE = HEAD_DIM ** -0.5 * LOG2E
D_FF = 4 * D_MODEL
ALPHA = (2.0 * DEPTH) ** 0.25
LANES = 128

P_QLAT, P_KVLAT, P_KPE, P_RET, P_WQ, P_WK, P_WV, P_GQ, P_GK, P_GV, P_END = (
    0, 256, 384, 512, 1536, 1792, 1920, 2048, 2304, 2432, 2560)
FQ_W = 1024
FQ_A, FQ_C, FQ_D = slice(0, 512), slice(512, 768), slice(768, 1024)
FKV_W = 768
FKV_CKV, FKV_KPE, FKV_WK, FKV_WV, FKV_GK, FKV_GV = (slice(c, c + 128) for c in range(0, FKV_W, 128))
FR_W = 1024
FR_Q, FR_K, FR_V, FR_G = (slice(c, c + 256) for c in range(0, FR_W, 256))
OATT_W = 768
OATT_A, OATT_C, OATT_D = (slice(c, c + 256) for c in range(0, OATT_W, 256))
VMEM_LIMIT = 60 * 1024 * 1024
TM_PROJ = 1024
TM_POST = 512
SUB_ROWS = 256
TQ_LATENT = 256
ATTN_LOOKAHEAD = 2
MXU_SUM_MIN_KEYS = 1024
RET_GROUP = 4
RET_CHUNKS_PER_STEP = 8

F32 = jnp.float32
BF16 = jnp.bfloat16


def _dot(a, b):
    return jnp.dot(a, b, preferred_element_type=F32)


def _dot_nt(a, b):
    return lax.dot_general(a, b, (((1,), (1,)), ((), ())), preferred_element_type=F32)


def _dot_tn(a, b):
    return lax.dot_general(a, b, (((0,), (0,)), ((), ())), preferred_element_type=F32)


def _sub_tiles(tm, sub_rows=SUB_ROWS):
    n = max(1, tm // sub_rows)
    step = tm // n
    return [slice(i * step, (i + 1) * step) for i in range(n)]


def _layernorm(x, g, b):
    mu = jnp.mean(x, -1, keepdims=True)
    d = x - mu
    var = jnp.mean(d * d, -1, keepdims=True)
    return d * lax.rsqrt(var + NORM_EPS) * g + b


def _rmsnorm(x, g):
    return x * lax.rsqrt(jnp.mean(x * x, -1, keepdims=True) + NORM_EPS) * g


def _seg_sum(x, ones_bd):
    hi = x.astype(BF16)
    lo = (x - hi.astype(F32)).astype(BF16)
    return _dot(hi, ones_bd) + _dot(lo, ones_bd)


def _rope_block(x, cos, sin, half, first):
    rot = jnp.where(first, pltpu.roll(x, LANES - half, 1), pltpu.roll(x, half, 1))
    return x * cos + rot * sin


def _ada_kernel(cond_ref, w_ref, b_ref, o_ref):
    cnd = cond_ref[...]
    s = (cnd * jax.nn.sigmoid(cnd)).astype(BF16)
    o_ref[0] = _dot(s, w_ref[0].astype(BF16)) + b_ref[0]


def _ada(cond8, w_ada, b_ada):
    tn = 2048
    n = w_ada.shape[-1]
    return pl.pallas_call(
        _ada_kernel,
        grid=(DEPTH, n // tn),
        in_specs=[pl.BlockSpec((8, D_MODEL), lambda l, j: (0, 0)),
                  pl.BlockSpec((1, D_MODEL, tn), lambda l, j: (l, 0, j)),
                  pl.BlockSpec((1, 1, tn), lambda l, j: (l, 0, j))],
        out_specs=pl.BlockSpec((1, 8, tn), lambda l, j: (l, 0, j)),
        out_shape=jax.ShapeDtypeStruct((DEPTH, 8, n), F32),
        compiler_params=pltpu.CompilerParams(dimension_semantics=("arbitrary", "arbitrary")),
        name="ada_mod",
    )(cond8, w_ada, b_ada.reshape(DEPTH, 1, n))


def _store_per_seq(o_ref, val, row0, transposed):
    seq = o_ref.shape[2] if transposed else o_ref.shape[1]
    for i in range(val.shape[0] // seq):
        blk = val[i * seq:(i + 1) * seq, :]
        o_ref[row0 // seq + i] = blk.T[0:o_ref.shape[1], :] if transposed else blk


def _proj_kernel(*refs, latent, n_alias):
    refs = refs[:9] + refs[9 + n_alias:]
    x_ref, mod_ref, wmix_ref = refs[:3]
    sh1 = mod_ref[0, 0:1, :]
    sc1 = mod_ref[0, 1:2, :]
    subs = _sub_tiles(x_ref.shape[0])
    def project(r):
        return _dot_nt((x_ref[r, :] * (1.0 + sc1) + sh1).astype(BF16), wmix_ref[0])

    ps = [project(subs[0])]
    for i, r in enumerate(subs):
        if i + 1 < len(subs):
            ps.append(project(subs[i + 1]))
        _proj_post(r, ps.pop(0), refs, latent)


def _proj_post(r, p, refs, latent):
    if latent:
        (_, _, _, qn_ref, wuq_ref, kvn_ref, gqn_ref, gkn_ref, bd_ref,
         cq_ref, sq_ref, ck_ref, sk_ref, ch_ref, sh_ref, fq_ref, fkv_ref, fr_ref) = refs
    else:
        (_, _, _, qn_ref, wuq_ref, kvn_ref, gqn_ref, gkn_ref, bd_ref,
         fq_ref, fkv_ref, fr_ref, ockv_ref, okpe_ref, owk_ref, owv_ref, ogk_ref, ogv_ref) = refs
    rows = p.shape[0]
    lane = lax.broadcasted_iota(jnp.int32, (rows, LANES), 1)
    first_head = (lane % HEAD_DIM) < (HEAD_DIM // 2)

    def rope_heads(v):
        if not latent:
            return v
        cos, sin = ch_ref[r, :], sh_ref[r, :]
        blocks = [_rope_block(v[:, j:j + LANES], cos, sin, HEAD_DIM // 2, first_head)
                  for j in range(0, v.shape[1], LANES)]
        return blocks[0] if len(blocks) == 1 else jnp.concatenate(blocks, axis=1)

    qn = _rmsnorm(p[:, P_QLAT:P_KVLAT], qn_ref[0]).astype(BF16)
    if latent:
        qa2 = _dot(qn, wuq_ref[0])
        cos, sin = cq_ref[r, :], sq_ref[r, :]
        qa = jnp.concatenate([qa2[:, j:j + LANES] * cos + qa2[:, 512 + j:512 + j + LANES] * sin
                              for j in range(0, 512, LANES)], axis=1)
    else:
        qa = _dot(qn, wuq_ref[0, :, 0:512])
    fq_ref[r, FQ_A] = (qa * MLA_SCALE).astype(BF16)

    ckv = _rmsnorm(p[:, P_KVLAT:P_KPE], kvn_ref[0])
    kpe = p[:, P_KPE:P_RET]
    if latent:
        first = lane < MLA_ROPE // 2
        kpe_r = _rope_block(kpe, ck_ref[r, :], sk_ref[r, :], MLA_ROPE // 2, first)
    else:
        kpe_r = kpe
        _store_per_seq(ockv_ref, ckv, r.start, False)
        _store_per_seq(okpe_ref, kpe, r.start, True)
    fkv_ref[r, FKV_CKV] = ckv.astype(BF16)
    fkv_ref[r, FKV_KPE] = kpe_r.astype(BF16)

    fr_ref[r, FR_Q] = p[:, P_RET:P_RET + 256]
    fr_ref[r, FR_K] = p[:, P_RET + 256:P_RET + 512] * (HEAD_DIM ** -0.5)
    fr_ref[r, FR_V.start:FR_G.stop] = p[:, P_RET + 512:P_WQ]

    fq_ref[r, FQ_C] = (rope_heads(p[:, P_WQ:P_WK]) * ATT_SCALE).astype(BF16)
    wk = p[:, P_WK:P_WV]
    wv = p[:, P_WV:P_GQ]
    fkv_ref[r, FKV_WK] = rope_heads(wk).astype(BF16)
    fkv_ref[r, FKV_WV] = wv.astype(BF16)

    bd = bd_ref[...]
    gq = p[:, P_GQ:P_GK]
    gqn = gq * lax.rsqrt(_seg_sum(gq * gq, bd) * (1.0 / HEAD_DIM) + NORM_EPS) * gqn_ref[0]
    fq_ref[r, FQ_D] = (rope_heads(gqn) * ATT_SCALE).astype(BF16)
    gk = p[:, P_GK:P_GV]
    gkn = gk * lax.rsqrt(_seg_sum(gk * gk, bd[0:128, 0:128]) * (1.0 / HEAD_DIM) + NORM_EPS) * gkn_ref[0]
    gv = p[:, P_GV:P_END]
    fkv_ref[r, FKV_GK] = rope_heads(gkn).astype(BF16)
    fkv_ref[r, FKV_GV] = gv.astype(BF16)
    if not latent:
        _store_per_seq(owk_ref, wk, r.start, True)
        _store_per_seq(owv_ref, wv, r.start, True)
        _store_per_seq(ogk_ref, gkn, r.start, True)
        _store_per_seq(ogv_ref, gv, r.start, True)


def _layer_spec(layer, *s):
    return pl.BlockSpec((1,) + s, lambda t: (layer,) + (0,) * len(s), pipeline_mode=pl.Buffered(1))


def _mod_spec(tm, group_len):
    if group_len is None:
        return pl.BlockSpec((1, 6, D_MODEL), lambda t: (0, 0, 0))
    per = group_len // tm
    return pl.BlockSpec((1, 6, D_MODEL), lambda t: (1 + t // per, 0, 0))


def _proj(x, row0, nt, mod, wts, layer, rope, tm, group_len, ctx_seq=None, caches=None):
    latent = group_len is not None
    aliases = {}
    tile0 = row0 // tm
    tok = lambda w: pl.BlockSpec((tm, w), lambda t: (t, 0))
    in_specs = [pl.BlockSpec((tm, D_MODEL), lambda t: (t + tile0, 0)), _mod_spec(tm, group_len),
                _layer_spec(layer, P_END, D_MODEL), _layer_spec(layer, 1, 256), _layer_spec(layer, 256, 1024),
                _layer_spec(layer, 1, 128), _layer_spec(layer, 1, 256), _layer_spec(layer, 1, 128),
                pl.BlockSpec((256, 256), lambda t: (0, 0))]
    args = [x, mod, wts["wmix"], wts["qn"], wts["wuq"], wts["kvn"], wts["gqn"], wts["gkn"], wts["bd"]]
    out_specs = [tok(FQ_W), tok(FKV_W), tok(FR_W)]
    out_shape = [jax.ShapeDtypeStruct((nt, FQ_W), BF16), jax.ShapeDtypeStruct((nt, FKV_W), BF16),
                 jax.ShapeDtypeStruct((nt, FR_W), F32)]
    if latent:
        per = group_len // tm
        in_specs += [pl.BlockSpec((tm, LANES), lambda t: (t % per, 0))] * 6
        args += list(rope)
    else:
        nb = tm // ctx_seq
        seqs = nt // ctx_seq
        sub = _sub_tiles(tm)[0]
        assert (sub.stop - sub.start) % ctx_seq == 0, "a sub-tile must hold whole context sequences"
        out_specs += [pl.BlockSpec((nb, None, ctx_seq, 128), lambda t: (t, layer, 0, 0))]
        out_shape += [jax.ShapeDtypeStruct((seqs, DEPTH, ctx_seq, 128), F32)]
        for r in (MLA_ROPE, 128, 128, 128, 128):
            out_specs.append(pl.BlockSpec((nb, None, r, ctx_seq), lambda t: (t, layer, 0, 0)))
            out_shape.append(jax.ShapeDtypeStruct((seqs, DEPTH, r, ctx_seq), F32))
        if caches is not None:
            aliases = {len(args) + i: 3 + i for i in range(len(caches))}
            in_specs += [pl.BlockSpec(memory_space=pl.ANY)] * len(caches)
            args += list(caches)
    return pl.pallas_call(
        functools.partial(_proj_kernel, latent=latent, n_alias=len(aliases)),
        grid=(nt // tm,), in_specs=in_specs, out_specs=out_specs, out_shape=out_shape,
        input_output_aliases=aliases,
        compiler_params=pltpu.CompilerParams(dimension_semantics=("arbitrary",), vmem_limit_bytes=VMEM_LIMIT),
        name="proj_latent" if latent else "proj_ctx",
    )(*args)


def _softmax_weights(parts, sink, mxu_sum):
    m = parts[0].max(-1, keepdims=True)
    for s in parts[1:]:
        m = jnp.maximum(m, s.max(-1, keepdims=True))
    if sink is not None:
        m = jnp.maximum(m, sink)
    es = [jnp.exp2(s - m) for s in parts]
    extra = None if sink is None else jnp.exp2(sink - m)
    if not mxu_sum:
        for e in es:
            extra = e.sum(-1, keepdims=True) if extra is None else extra + e.sum(-1, keepdims=True)
    return [e.astype(BF16) for e in es], extra


def _normalised(pv, extra):
    if pv.shape[1] == LANES:
        return pv * (1.0 / extra)
    den = pv[:, LANES:LANES + 1]
    if extra is not None:
        den = den + extra
    return pv[:, 0:LANES] * (1.0 / den)


def _window_start(qi, tq, t):
    return jnp.clip(qi * tq - WINDOW, 0, t - (tq + 2 * WINDOW))


def _band_bias(tq):
    span = tq + 2 * WINDOW
    r = (np.arange(2 * tq) % tq)[None, :, None]
    c = np.arange(span)[None, None, :]
    off = (np.arange(3) * WINDOW)[:, None, None]
    return jnp.asarray(np.where(np.abs(r + off - c) <= WINDOW, 0.0, NEG_INF), F32)


def _attn_kernel(*refs, t, tq, n_cache, layer):
    latent = n_cache > 0
    s_len = t + n_cache
    if latent:
        (fq_ref, fkv_ref, cckv_ref, ckpe_ref, cwk_ref, cwv_ref, cgk_ref, cgv_ref, bias_ref,
         wka_ref, wuv_ref, sink_ref, o_ref, ckpe_s, ka_s, va_s, kd_s, vd_s, vc_s, kcc_s, vcc_s) = refs
    else:
        (fq_ref, fkv_ref, wka_ref, wuv_ref, sink_ref, o_ref, ckpe_s, ka_s, va_s, kd_s, vd_s, vc_s) = refs

    qi = pl.program_id(1)

    @pl.when(qi == 0)
    def _():
        ckpe_s[0:t, :] = fkv_ref[0, :, FKV_CKV.start:FKV_KPE.stop]
        kd_s[0:t, :] = fkv_ref[0, :, FKV_GK]
        vd_s[0:t, 0:LANES] = fkv_ref[0, :, FKV_GV]
        vc_s[:, 0:LANES] = fkv_ref[0, :, FKV_WV]
        if latent:
            ckpe_s[t:s_len, 0:128] = cckv_ref[0, 0].astype(BF16)
            ckpe_s[t:s_len, 128:256] = ckpe_ref[0, 0].astype(BF16)
            kd_s[t:s_len, :] = cgk_ref[0, 0].astype(BF16)
            vd_s[t:s_len, 0:LANES] = cgv_ref[0, 0].astype(BF16)
            kcc_s[...] = cwk_ref[0, 0].astype(BF16)
            vcc_s[:, 0:LANES] = cwv_ref[0, 0].astype(BF16)
            vcc_s[:, LANES:] = jnp.ones((n_cache, LANES), BF16)
        vd_s[:, LANES:] = jnp.ones((s_len, LANES), BF16)
        vc_s[:, LANES:] = jnp.ones((t, LANES), BF16)
        ck = ckpe_s[...]
        va = _dot(ck, wuv_ref[0]).astype(BF16)
        for blk in range(2):
            va_s[blk, :, 0:LANES] = va[:, blk * LANES:(blk + 1) * LANES]
            va_s[blk, :, LANES:] = jnp.ones((s_len, LANES), BF16)
        for h in range(N_HEADS):
            ka_s[h] = _dot(ck, wka_ref[0, h]).astype(BF16)

    lane = lax.broadcasted_iota(jnp.int32, (tq, LANES), 1)
    half = [lane < HEAD_DIM, lane >= HEAD_DIM]

    def keep(x, j):
        return jnp.where(half[j], x, 0.0)

    def stacked_q(col, j):
        blks = [fq_ref[0, :, col + g * LANES:col + (g + 1) * LANES] for g in range(2)]
        return jnp.concatenate([keep(b.astype(F32), j).astype(BF16) for b in blks], axis=0)

    if latent:
        span = tq + 2 * WINDOW
        start = pl.multiple_of(_window_start(qi, tq, t), LANES)
    row1 = lax.broadcasted_iota(jnp.int32, (2 * tq, 1), 0)
    mxu_sum = s_len >= MXU_SUM_MIN_KEYS
    vw = 2 * LANES if mxu_sum else LANES

    acc_a = [jnp.zeros((tq, LANES), F32) for _ in range(2)]
    acc_c = [jnp.zeros((tq, LANES), F32) for _ in range(2)]
    acc_d = [jnp.zeros((tq, LANES), F32) for _ in range(2)]

    def a_scores(h):
        return [_dot_nt(fq_ref[0, :, h * LANES:(h + 1) * LANES], ka_s[h])]

    def a_finish(h, ps, sink_term):
        blk = h // 2
        acc_a[blk] = acc_a[blk] + keep(_normalised(_dot(ps[0], va_s[blk, :, 0:vw]), sink_term), h % 2)

    def c_scores(j):
        qs = stacked_q(FQ_C.start, j)
        if latent:
            return [_dot_nt(qs, fkv_ref[0, pl.ds(start, span), FKV_WK]) + bias_ref[0], _dot_nt(qs, kcc_s[...])]
        return [_dot_nt(qs, fkv_ref[0, :, FKV_WK])]

    def c_finish(j, ps, sink_term):
        if latent:
            pv = _dot(ps[0], vc_s[pl.ds(start, span), 0:vw]) + _dot(ps[1], vcc_s[:, 0:vw])
        else:
            pv = _dot(ps[0], vc_s[:, 0:vw])
        pv = _normalised(pv, sink_term)
        for g in range(2):
            acc_c[g] = acc_c[g] + keep(pv[g * tq:(g + 1) * tq], j)

    def d_scores(j):
        return [_dot_nt(stacked_q(FQ_D.start, j), kd_s[...])]

    def d_finish(j, ps, sink_term):
        pv = _normalised(_dot(ps[0], vd_s[:, 0:vw]), sink_term)
        for g in range(2):
            acc_d[g] = acc_d[g] + keep(pv[g * tq:(g + 1) * tq], j)

    def c_sink(j):
        return jnp.where(row1 < tq, sink_ref[layer, 2 * j], sink_ref[layer, 2 * j + 1]) * LOG2E

    jobs = [(functools.partial(a_scores, h), functools.partial(a_finish, h), None) for h in range(N_HEADS)]
    jobs += [(functools.partial(d_scores, j), functools.partial(d_finish, j), None) for j in range(2)]
    jobs += [(functools.partial(c_scores, j), functools.partial(c_finish, j), functools.partial(c_sink, j))
             for j in range(2)]
    ahead = ATTN_LOOKAHEAD if mxu_sum else len(jobs)
    pending = [job[0]() for job in jobs[:ahead]]
    for i, (_, finish, sink) in enumerate(jobs):
        if i + ahead < len(jobs):
            pending.append(jobs[i + ahead][0]())
        finish(*_softmax_weights(pending.pop(0), None if sink is None else sink(), mxu_sum))

    for cols, acc in ((OATT_A, acc_a), (OATT_C, acc_c), (OATT_D, acc_d)):
        for g in range(2):
            o_ref[0, :, cols.start + g * LANES:cols.start + (g + 1) * LANES] = acc[g].astype(BF16)


def _attn(fq, fkv, caches, wts, layer, tq):
    b, t, _ = fq.shape
    latent = caches is not None
    n_cache = caches[0].shape[2] if latent else 0
    s_len = t + n_cache
    in_specs = [pl.BlockSpec((1, tq, FQ_W), lambda i, q: (i, q, 0)),
                pl.BlockSpec((1, t, FKV_W), lambda i, q: (i, 0, 0))]
    args = [fq, fkv]
    if latent:
        in_specs += [pl.BlockSpec((1, 1, n_cache, LANES), lambda i, q: (i, layer, 0, 0))] * 6
        args += list(caches)
        span = tq + 2 * WINDOW
        assert t >= span and WINDOW <= tq
        in_specs.append(pl.BlockSpec((1, 2 * tq, span),
                                     lambda i, q: ((q * tq - _window_start(q, tq, t)) // WINDOW, 0, 0)))
        args.append(_band_bias(tq))
    in_specs += [pl.BlockSpec((1, N_HEADS, 256, 128), lambda i, q: (layer, 0, 0, 0)),
                 pl.BlockSpec((1, 256, 256), lambda i, q: (layer, 0, 0)),
                 pl.BlockSpec(memory_space=pltpu.SMEM)]
    args += [wts["wka"], wts["wuv"], wts["sink"]]
    scratch = [pltpu.VMEM((s_len, 256), BF16), pltpu.VMEM((N_HEADS, s_len, 128), BF16),
               pltpu.VMEM((2, s_len, 256), BF16), pltpu.VMEM((s_len, 128), BF16), pltpu.VMEM((s_len, 256), BF16),
               pltpu.VMEM((t, 256), BF16)]
    if latent:
        scratch += [pltpu.VMEM((n_cache, 128), BF16), pltpu.VMEM((n_cache, 256), BF16)]
    return pl.pallas_call(
        functools.partial(_attn_kernel, t=t, tq=tq, n_cache=n_cache, layer=layer),
        grid=(b, t // tq), in_specs=in_specs,
        out_specs=pl.BlockSpec((1, tq, OATT_W), lambda i, q: (i, q, 0)),
        out_shape=jax.ShapeDtypeStruct((b, t, OATT_W), BF16),
        scratch_shapes=scratch,
        compiler_params=pltpu.CompilerParams(dimension_semantics=("arbitrary", "arbitrary"),
                                             vmem_limit_bytes=VMEM_LIMIT),
        name="attn_latent" if latent else "attn_ctx",
    )(*args)


def _ret_kernel(*refs, t, bb, s0_stride, layer, want_state, n_alias):
    refs = refs[:7] + refs[7 + n_alias:]
    if want_state:
        (fr_ref, s0f_ref, s0b_ref, decf_ref, decb_ref, gain_ref, bd_ref, o_ref, sf_ref, sb_ref,
         dm_s, dec_s, cd_s, kvf_s, kvb_s, stf_s, stb_s, sf_s, sb_s) = refs
    else:
        (fr_ref, s0f_ref, s0b_ref, decf_ref, decb_ref, gain_ref, bd_ref, o_ref,
         dm_s, dec_s, cd_s, kvf_s, kvb_s, stf_s, stb_s, sf_s, sb_s) = refs
    nc = t // CHUNK
    w = BRANCH_W
    lane_w = lax.broadcasted_iota(jnp.int32, (CHUNK, w), 1) // HEAD_DIM

    @pl.when(pl.program_id(0) == 0)
    def _():
        row_w = lax.broadcasted_iota(jnp.int32, (CHUNK, w), 0).astype(F32)

        def lane_decay(dec_ref):
            v = jnp.zeros((CHUNK, w), F32)
            for h in range(N_HEADS):
                v = jnp.where(lane_w == h, dec_ref[layer, h], v)
            return jax.nn.log_sigmoid(v)

        lgf = lane_decay(decf_ref)
        lgb = lane_decay(decb_ref)
        dec_s[0] = jnp.exp((row_w + 1.0) * lgf)
        dec_s[1] = jnp.exp((CHUNK - 1.0 - row_w) * lgf)
        dec_s[2] = jnp.exp((CHUNK - row_w) * lgb)
        dec_s[3] = jnp.exp(row_w * lgb)
        cd_s[0] = jnp.concatenate([jnp.exp(CHUNK * lgf)] * (w // CHUNK), axis=0)
        cd_s[1] = jnp.concatenate([jnp.exp(CHUNK * lgb)] * (w // CHUNK), axis=0)
        ii = lax.broadcasted_iota(jnp.int32, (CHUNK, CHUNK), 0).astype(F32)
        jj = lax.broadcasted_iota(jnp.int32, (CHUNK, CHUNK), 1).astype(F32)
        diff = ii - jj
        for h in range(N_HEADS):
            lf = jax.nn.log_sigmoid(jnp.full((CHUNK, CHUNK), decf_ref[layer, h], F32))
            lb = jax.nn.log_sigmoid(jnp.full((CHUNK, CHUNK), decb_ref[layer, h], F32))
            d_f = jnp.where(diff >= 0, jnp.exp(jnp.maximum(diff, 0.0) * lf), 0.0)
            d_b = jnp.where(diff < 0, jnp.exp(jnp.maximum(-diff, 0.0) * lb), 0.0)
            dm_s[h] = d_f + d_b

    r2 = lax.broadcasted_iota(jnp.int32, (w, w), 0) // HEAD_DIM
    c2 = lax.broadcasted_iota(jnp.int32, (w, w), 1) // HEAD_DIM
    diag = r2 == c2

    jobs = [(bi, n) for bi in range(bb) for n in range(nc)]
    groups = [jobs[i:i + RET_GROUP] for i in range(0, len(jobs), RET_GROUP)]
    rows = lambda n: slice(n * CHUNK, (n + 1) * CHUNK)
    slot = lambda bi, n: bi * nc + n

    for grp in groups:
        kvs = []
        for bi, n in grp:
            k = fr_ref[bi, rows(n), FR_K]
            vb = fr_ref[bi, rows(n), FR_V].astype(BF16)
            kk = jnp.concatenate([k * dec_s[1], k * dec_s[3]], axis=1).astype(BF16)
            kvs.append(_dot_tn(kk, vb))
        for (bi, n), kv in zip(grp, kvs):
            kvf_s[slot(bi, n)] = jnp.where(diag, kv[0:w], 0.0)
            kvb_s[slot(bi, n)] = jnp.where(diag, kv[w:2 * w], 0.0)

    for bi in range(bb):
        sf_s[bi] = s0f_ref[bi * s0_stride]
        sb_s[bi] = s0b_ref[bi * s0_stride]
    for i in range(nc):
        m = nc - 1 - i
        for bi in range(bb):
            sf = sf_s[bi]
            stf_s[slot(bi, i)] = sf.astype(BF16)
            sf_s[bi] = sf * cd_s[0] + kvf_s[slot(bi, i)]
            sb = sb_s[bi]
            stb_s[slot(bi, m)] = sb.astype(BF16)
            sb_s[bi] = sb * cd_s[1] + kvb_s[slot(bi, m)]
    if want_state:
        for bi in range(bb):
            for h in range(N_HEADS):
                sl = slice(h * HEAD_DIM, (h + 1) * HEAD_DIM)
                sf_ref[bi, h] = sf_s[bi, sl, sl]
                sb_ref[bi, h] = sb_s[bi, sl, sl]

    gain = gain_ref[0]
    bd2 = jnp.concatenate([bd_ref[...], bd_ref[...]], axis=0)

    def seg_mean(xs):
        cat = []
        for x in xs:
            hi = x.astype(BF16)
            cat.append(jnp.concatenate([hi, (x - hi.astype(F32)).astype(BF16)], axis=1))
        return [_dot(c, bd2) * (1.0 / HEAD_DIM) for c in cat]

    for grp in groups:
        qs = [fr_ref[bi, rows(n), FR_Q] for bi, n in grp]
        vs = [fr_ref[bi, rows(n), FR_V] for bi, n in grp]
        qks, inter_f, inter_b = [], [], []
        for (bi, n), q in zip(grp, qs):
            kb = fr_ref[bi, rows(n), FR_K].astype(BF16)
            qstack = jnp.concatenate([jnp.where(lane_w == h, q, 0.0) for h in range(N_HEADS)], axis=0)
            qks.append(_dot_nt(qstack.astype(BF16), kb))
            qb = q.astype(BF16)
            inter_f.append(_dot(qb, stf_s[slot(bi, n)]))
            inter_b.append(_dot(qb, stb_s[slot(bi, n)]))
        outs = []
        for qk, v, i_f, i_b in zip(qks, vs, inter_f, inter_b):
            att = jnp.concatenate([qk[rows(h)] * dm_s[h] for h in range(N_HEADS)], axis=1)
            vstack = jnp.concatenate([jnp.where(lane_w == h, v, 0.0) for h in range(N_HEADS)], axis=0)
            outs.append(_dot(att.astype(BF16), vstack.astype(BF16)) + i_f * dec_s[0] + i_b * dec_s[2])
        ds = [o - mu for o, mu in zip(outs, seg_mean(outs))]
        var = seg_mean([d * d for d in ds])
        for (bi, n), d, vr in zip(grp, ds, var):
            gate = fr_ref[bi, rows(n), FR_G]
            y = d * lax.rsqrt(vr + NORM_EPS) * gain
            o_ref[bi, rows(n), :] = (gate * jax.nn.sigmoid(gate) * y).astype(BF16)


def _ret(fr, s0f, s0b, wts, layer, want_state, states=None):
    b, t, _ = fr.shape
    nc = t // CHUNK
    bb = max(1, RET_CHUNKS_PER_STEP // nc)
    shared = s0f.shape[0] != b
    s_spec = (pl.BlockSpec((1, BRANCH_W, BRANCH_W), lambda i: (0, 0, 0)) if shared
              else pl.BlockSpec((bb, BRANCH_W, BRANCH_W), lambda i: (i, 0, 0)))
    smem = pl.BlockSpec(memory_space=pltpu.SMEM)
    out_specs = [pl.BlockSpec((bb, t, BRANCH_W), lambda i: (i, 0, 0))]
    out_shape = [jax.ShapeDtypeStruct((b, t, BRANCH_W), BF16)]
    in_specs = [pl.BlockSpec((bb, t, FR_W), lambda i: (i, 0, 0)), s_spec, s_spec, smem, smem,
                pl.BlockSpec((1, 1, BRANCH_W), lambda i: (layer, 0, 0)),
                pl.BlockSpec((BRANCH_W, BRANCH_W), lambda i: (0, 0))]
    args = [fr, s0f, s0b, wts["decf"], wts["decb"], wts["gn"], wts["bd"]]
    aliases = {}
    if want_state:
        st_spec = pl.BlockSpec((bb, None, N_HEADS, HEAD_DIM, HEAD_DIM), lambda i: (i, layer, 0, 0, 0))
        out_specs += [st_spec, st_spec]
        out_shape += [jax.ShapeDtypeStruct((b, DEPTH, N_HEADS, HEAD_DIM, HEAD_DIM), F32)] * 2
        if states is not None:
            aliases = {len(args) + i: 1 + i for i in range(2)}
            in_specs += [pl.BlockSpec(memory_space=pl.ANY)] * 2
            args += list(states)
    sq = (BRANCH_W, BRANCH_W)
    return pl.pallas_call(
        functools.partial(_ret_kernel, t=t, bb=bb, s0_stride=0 if shared else 1, layer=layer,
                          want_state=want_state, n_alias=len(aliases)),
        grid=(b // bb,),
        in_specs=in_specs, out_specs=out_specs, out_shape=out_shape, input_output_aliases=aliases,
        scratch_shapes=[pltpu.VMEM((N_HEADS, CHUNK, CHUNK), F32), pltpu.VMEM((4, CHUNK, BRANCH_W), F32),
                        pltpu.VMEM((2,) + sq, F32),
                        pltpu.VMEM((bb * nc,) + sq, F32), pltpu.VMEM((bb * nc,) + sq, F32),
                        pltpu.VMEM((bb * nc,) + sq, BF16), pltpu.VMEM((bb * nc,) + sq, BF16),
                        pltpu.VMEM((bb,) + sq, F32), pltpu.VMEM((bb,) + sq, F32)],
        compiler_params=pltpu.CompilerParams(dimension_semantics=("arbitrary",),
                                             vmem_limit_bytes=VMEM_LIMIT),
        name="retention",
    )(*args)


def _post_kernel(*refs, n_x, n_out, ctx_tiles):
    x_refs, refs = refs[:n_x], refs[n_x:]
    (mod_ref, oatt_c, oatt_l, oret_c, oret_l, wg_ref, wb_ref, wo_ref, g1_ref, b1_ref,
     wup_ref, wdn_ref, g2_ref, b2_ref) = refs[:14]
    o_refs = refs[14:]
    is_ctx = pl.program_id(0) < ctx_tiles
    pick = lambda c_ref, l_ref, r, cols: jnp.where(is_ctx, c_ref[r, cols], l_ref[r, cols])
    every = slice(None)
    if n_x == 1:
        x_rows = lambda r: x_refs[0][r, :]
    else:
        x_rows = lambda r: pick(x_refs[0], x_refs[1], r, every)
    sh1, sc1, g1, sh2, sc2, g2 = [mod_ref[0, i:i + 1, :] for i in range(6)]
    subs = _sub_tiles(x_refs[0].shape[0])
    ff_half = D_FF // 2
    xs = [x_rows(r) for r in subs]

    def gated(r, x):
        h = (x * (1.0 + sc1) + sh1).astype(BF16)
        branches = (pick(oatt_c, oatt_l, r, OATT_A), pick(oret_c, oret_l, r, every),
                    pick(oatt_c, oatt_l, r, OATT_C), pick(oatt_c, oatt_l, r, OATT_D))
        tsum = None
        for i, o in enumerate(branches):
            gate = jax.nn.sigmoid(_dot_nt(h, wg_ref[0, i * D_MODEL:(i + 1) * D_MODEL, :]))
            term = gate * _dot(o, wb_ref[0, i])
            tsum = term if tsum is None else tsum + term
        return tsum

    sums = [gated(r, x) for r, x in zip(subs, xs)]
    ys = [_dot(s.astype(BF16), wo_ref[0]) for s in sums]
    x1s = [_layernorm(ALPHA * x + g1 * y, g1_ref[0], b1_ref[0]) for x, y in zip(xs, ys)]
    hs = [(x1 * (1.0 + sc2) + sh2).astype(BF16) for x1 in x1s]
    us = [[_dot(h, wup_ref[0, :, c * ff_half:(c + 1) * ff_half]) for c in range(2)] for h in hs]
    fs = []
    for u2 in us:
        f = None
        for c, u in enumerate(u2):
            u = jnp.maximum(u, 0.0)
            part = _dot((u * u).astype(BF16), wdn_ref[0, c * ff_half:(c + 1) * ff_half, :])
            f = part if f is None else f + part
        fs.append(f)
    outs = [_layernorm(ALPHA * x1 + g2 * f, g2_ref[0], b2_ref[0]) for x1, f in zip(x1s, fs)]
    if n_out == 1:
        for r, o in zip(subs, outs):
            o_refs[0][r, :] = o
    else:
        @pl.when(is_ctx)
        def _():
            for r, o in zip(subs, outs):
                o_refs[0][r, :] = o

        @pl.when(jnp.logical_not(is_ctx))
        def _():
            for r, o in zip(subs, outs):
                o_refs[1][r, :] = o


def _post(xs, mod, oatts, orets, wts, layer, tm, latent_len, split_out):
    n_ctx, n_lat = oatts[0].shape[0], oatts[1].shape[0]
    ctx_tiles, lat_tiles = n_ctx // tm, n_lat // tm
    per = latent_len // tm
    all_tok = lambda w: pl.BlockSpec((tm, w), lambda t: (t, 0))
    ctx_tok = lambda w: pl.BlockSpec((tm, w), lambda t: (jnp.minimum(t, ctx_tiles - 1), 0))
    lat_tok = lambda w: pl.BlockSpec((tm, w), lambda t: (jnp.maximum(t - ctx_tiles, 0), 0))
    pair = lambda w: [ctx_tok(w), lat_tok(w)]
    vec = _layer_spec(layer, 1, D_MODEL)
    mod_spec = pl.BlockSpec((1, 6, D_MODEL),
                            lambda t: (jnp.where(t < ctx_tiles, 0, 1 + (t - ctx_tiles) // per), 0, 0))
    x_specs = [all_tok(D_MODEL)] if len(xs) == 1 else pair(D_MODEL)
    if split_out:
        out_specs = pair(D_MODEL)
        out_shape = [jax.ShapeDtypeStruct((n_ctx, D_MODEL), F32), jax.ShapeDtypeStruct((n_lat, D_MODEL), F32)]
    else:
        out_specs = [all_tok(D_MODEL)]
        out_shape = [jax.ShapeDtypeStruct((n_ctx + n_lat, D_MODEL), F32)]
    return pl.pallas_call(
        functools.partial(_post_kernel, n_x=len(xs), n_out=len(out_shape), ctx_tiles=ctx_tiles),
        grid=(ctx_tiles + lat_tiles,),
        in_specs=x_specs + [mod_spec] + pair(OATT_W) + pair(BRANCH_W) + [
            _layer_spec(layer, 4 * D_MODEL, D_MODEL), _layer_spec(layer, 4, BRANCH_W, D_MODEL),
            _layer_spec(layer, D_MODEL, D_MODEL), vec, vec,
            _layer_spec(layer, D_MODEL, D_FF), _layer_spec(layer, D_FF, D_MODEL), vec, vec],
        out_specs=out_specs, out_shape=out_shape,
        compiler_params=pltpu.CompilerParams(dimension_semantics=("arbitrary",), vmem_limit_bytes=VMEM_LIMIT),
        name="merge_mlp",
    )(*xs, mod, *oatts, *orets, wts["wg"], wts["wb"], wts["wo"], wts["ln1g"], wts["ln1b"],
      wts["wup"], wts["wdn"], wts["ln2g"], wts["ln2b"])


def _win_kernel(w_ref, wmix_ref, wg_ref):
    o_ret = 256 + 128 + MLA_ROPE
    o_wq, o_wk, o_gq, o_gk, o_gate = o_ret + 1024, o_ret + 1280, o_ret + 1536, o_ret + 1792, o_ret + 2048

    def put(dst, src, n):
        wmix_ref[0, dst:dst + n, :] = w_ref[0, src:src + n, :].astype(BF16)

    def put_swapped(dst, src):
        for i, h in enumerate((0, 2, 1, 3)):
            put(dst + i * HEAD_DIM, src + h * HEAD_DIM, HEAD_DIM)

    put(P_QLAT, 0, o_ret)
    pad = P_RET - o_ret
    wmix_ref[0, o_ret:P_RET, :] = jnp.zeros((pad, w_ref.shape[2]), BF16)
    put(P_RET, o_ret, o_wq - o_ret)
    put_swapped(P_WQ, o_wq)
    put(P_WK, o_wk, o_gq - o_wk)
    put_swapped(P_GQ, o_gq)
    put(P_GK, o_gk, o_gate - o_gk)
    wg_ref[0] = w_ref[0, o_gate:, :].astype(BF16)


def _prep_win(w_in):
    w_t = jnp.swapaxes(w_in, 1, 2)
    n_in = w_t.shape[1]
    kb = D_MODEL // 2
    return pl.pallas_call(
        _win_kernel,
        grid=(DEPTH, D_MODEL // kb),
        in_specs=[pl.BlockSpec((1, n_in, kb), lambda l, k: (l, 0, k))],
        out_specs=[pl.BlockSpec((1, P_END, kb), lambda l, k: (l, 0, k)),
                   pl.BlockSpec((1, 4 * D_MODEL, kb), lambda l, k: (l, 0, k))],
        out_shape=[jax.ShapeDtypeStruct((DEPTH, P_END, D_MODEL), BF16),
                   jax.ShapeDtypeStruct((DEPTH, 4 * D_MODEL, D_MODEL), BF16)],
        compiler_params=pltpu.CompilerParams(dimension_semantics=("arbitrary", "arbitrary"),
                                             vmem_limit_bytes=VMEM_LIMIT),
        name="prep_w_in",
    )(w_t)


def _prep_weights(w_in, mla_q_norm, mla_w_uq, mla_kv_norm, mla_w_uk, mla_w_uv, ret_decay_fwd, ret_decay_bwd,
                  ret_gn_gain, win_sink, gqa_q_norm, gqa_k_norm, w_branch, w_o, ln1_g, ln1_b, w_up, w_down,
                  ln2_g, ln2_b):
    wmix, wg = _prep_win(w_in)
    uq = mla_w_uq.reshape(DEPTH, 256, N_HEADS, HEAD_DIM + MLA_ROPE)
    half = MLA_ROPE // 2
    uq_sw = jnp.concatenate([jnp.zeros_like(uq[..., :HEAD_DIM]), uq[..., HEAD_DIM + half:],
                             uq[..., HEAD_DIM:HEAD_DIM + half]], axis=-1)
    lane_pad = ((0, 0), (0, 0), (0, 0), (0, LANES - HEAD_DIM - MLA_ROPE))
    wuq = jnp.concatenate([jnp.pad(w, lane_pad).reshape(DEPTH, 256, N_HEADS * LANES) for w in (uq, uq_sw)],
                          axis=-1).astype(BF16)
    uk = mla_w_uk.reshape(DEPTH, MLA_KV_RANK, N_HEADS, HEAD_DIM).transpose(0, 2, 1, 3)
    top = jnp.pad(uk, ((0, 0), (0, 0), (0, 0), (0, LANES - HEAD_DIM)))
    eye = np.zeros((LANES, LANES), np.float32)
    eye[np.arange(MLA_ROPE), HEAD_DIM + np.arange(MLA_ROPE)] = 1.0
    wka = jnp.concatenate([top, jnp.broadcast_to(eye, (DEPTH, N_HEADS, LANES, LANES))], axis=2).astype(BF16)
    wuv = jnp.pad(mla_w_uv, ((0, 0), (0, 256 - MLA_KV_RANK), (0, 0))).astype(BF16)
    bd = (np.arange(BRANCH_W)[:, None] // HEAD_DIM == np.arange(BRANCH_W)[None, :] // HEAD_DIM).astype(np.float32)
    wb_cd = jnp.swapaxes(w_branch[:, 2:].reshape(DEPTH, 2, 2, 2, HEAD_DIM, D_MODEL), 2, 3
                         ).reshape(DEPTH, 2, BRANCH_W, D_MODEL)
    w_branch = jnp.concatenate([w_branch[:, :2], wb_cd], axis=1)
    row = lambda a: a[:, None, :]
    return dict(
        wmix=wmix, wg=wg, wuq=wuq, wka=wka, wuv=wuv, bd=jnp.asarray(bd, BF16),
        qn=row(mla_q_norm), kvn=row(mla_kv_norm),
        gqn=row(jnp.tile(gqa_q_norm, (1, N_HEADS))), gkn=row(jnp.tile(gqa_k_norm, (1, 2))),
        decf=ret_decay_fwd, decb=ret_decay_bwd, gn=row(ret_gn_gain), sink=win_sink,
        wb=w_branch.astype(BF16), wo=w_o.astype(BF16), ln1g=row(ln1_g), ln1b=row(ln1_b),
        wup=w_up.astype(BF16), wdn=w_down.astype(BF16), ln2g=row(ln2_g), ln2b=row(ln2_b))


def _axial_tables(t, rot_dim):
    rows = t // GRID_W
    row = np.repeat(np.arange(rows, dtype=np.float32), GRID_W)
    col = (np.arange(t) % GRID_W).astype(np.float32)
    n_freq = rot_dim // 4
    inv = np.power(np.float32(ROPE_BASE), -np.arange(n_freq, dtype=np.float32) / np.float32(n_freq))
    ang = np.concatenate([row[:, None] * inv, col[:, None] * inv], axis=-1).astype(np.float32)
    return np.cos(ang), np.sin(ang)


def _rope_tables(t):
    ca, sa = _axial_tables(t, MLA_ROPE)
    ch, sh = _axial_tables(t, HEAD_DIM)
    one = lambda n: np.ones((t, n), np.float32)
    zero = lambda n: np.zeros((t, n), np.float32)
    cq = np.concatenate([one(HEAD_DIM), ca, ca, one(32)], axis=1)
    sq = np.concatenate([zero(HEAD_DIM), -sa, sa, zero(32)], axis=1)
    ck = np.concatenate([ca, ca, one(96)], axis=1)
    sk = np.concatenate([-sa, sa, zero(96)], axis=1)
    chh = np.concatenate([ch, ch, ch, ch], axis=1)
    shh = np.concatenate([-sh, sh, -sh, sh], axis=1)
    return tuple(jnp.asarray(a, F32) for a in (cq, sq, ck, sk, chh, shh))


def _block_diag(s):
    b = s.shape[0]
    same = np.eye(N_HEADS, dtype=bool)[None, :, None, :, None]
    return jnp.where(same, s[:, :, :, None, :], 0.0).reshape(b, BRANCH_W, BRANCH_W)


def kernel(x_prompt, x_sample, cache_mla_ckv, cache_mla_kpe, cache_win_k, cache_win_v, cache_gqa_k, cache_gqa_v,
           state_ret_fwd, state_ret_bwd, c, c_ctx, w_ada, b_ada, w_in, mla_q_norm, mla_w_uq, mla_kv_norm, mla_w_uk,
           mla_w_uv, ret_decay_fwd, ret_decay_bwd, ret_gn_gain, win_sink, gqa_q_norm, gqa_k_norm, w_branch, w_o,
           ln1_g, ln1_b, w_up, w_down, ln2_g, ln2_b):
    batch, seq, _ = x_prompt.shape
    dec_b, dec_t, _ = x_sample.shape
    past = cache_mla_ckv.shape[2]

    cond8 = jnp.concatenate([c_ctx[None], c, jnp.zeros((8 - 1 - dec_b, D_MODEL), F32)], axis=0)
    mod = _ada(cond8, w_ada, b_ada).reshape(DEPTH, 8, 6, D_MODEL)
    rope = _rope_tables(dec_t)
    caches = (cache_mla_ckv,
              jnp.pad(cache_mla_kpe, ((0, 0), (0, 0), (0, 0), (0, LANES - MLA_ROPE))),
              cache_win_k.reshape(dec_b, DEPTH, past, LANES), cache_win_v.reshape(dec_b, DEPTH, past, LANES),
              cache_gqa_k.reshape(dec_b, DEPTH, past, LANES), cache_gqa_v.reshape(dec_b, DEPTH, past, LANES))

    wts = _prep_weights(w_in, mla_q_norm, mla_w_uq, mla_kv_norm, mla_w_uk, mla_w_uv, ret_decay_fwd, ret_decay_bwd,
                        ret_gn_gain, win_sink, gqa_q_norm, gqa_k_norm, w_branch, w_o, ln1_g, ln1_b, w_up, w_down,
                        ln2_g, ln2_b)
    zero_state = jnp.zeros((1, BRANCH_W, BRANCH_W), F32)
    n_ctx, n_lat = batch * seq, dec_b * dec_t
    per_b = lambda a: a.reshape(batch, seq, a.shape[-1])
    per_d = lambda a: a.reshape(dec_b, dec_t, a.shape[-1])
    flat = lambda a: a.reshape(-1, a.shape[-1])
    x = [x_prompt.reshape(n_ctx, D_MODEL), x_sample.reshape(n_lat, D_MODEL)]
    new_caches = None
    new_states = None
    for l in range(DEPTH):
        x_ctx, x_lat = (x[0], 0), ((x[1], 0) if len(x) == 2 else (x[0], n_ctx))
        fq, fkv, fr, *new_caches = _proj(*x_ctx, n_ctx, mod[l], wts, l, None, TM_PROJ, None, seq, new_caches)
        oatt_c = _attn(per_b(fq), per_b(fkv), None, wts, l, seq)
        oret_c, *new_states = _ret(per_b(fr), zero_state, zero_state, wts, l, True, new_states)
        fq, fkv, fr = _proj(*x_lat, n_lat, mod[l], wts, l, rope, TM_PROJ, dec_t)
        oatt_l = _attn(per_d(fq), per_d(fkv), caches, wts, l, TQ_LATENT)
        (oret_l,) = _ret(per_d(fr), _block_diag(state_ret_fwd[:, l]), _block_diag(state_ret_bwd[:, l]), wts, l, False)
        x = _post(x, mod[l], (flat(oatt_c), flat(oatt_l)), (flat(oret_c), flat(oret_l)), wts, l, TM_POST, dec_t,
                  split_out=l == DEPTH - 1)

    ckv, kpe, *kv = new_caches
    kv = [jnp.transpose(a.reshape(batch, DEPTH, 2, HEAD_DIM, seq), (0, 1, 4, 2, 3)) for a in kv]
    return (per_b(x[0]), per_d(x[1]), ckv, jnp.swapaxes(kpe, 2, 3), *kv, *new_states)
```

```python
import functools

import jax
import jax.numpy as jnp
import numpy as np
from jax import lax
from jax.experimental import pallas as pl
from jax.experimental.pallas import tpu as pltpu

D_MODEL = 1024
DEPTH = 2
GRID_W = 64
CHUNK = 128
WINDOW = 128
ROPE_BASE = 10000.0
NORM_EPS = 1e-6
NEG_INF = -1e30
HEAD_DIM = 64
N_HEADS = 4
BRANCH_W = 256
MLA_ROPE = 32
MLA_KV_RANK = 128
LOG2E = 1.4426950408889634
MLA_SCALE = (HEAD_DIM + MLA_ROPE) ** -0.5 * LOG2E
ATT_SCALE = HEAD_DIM ** -0.5 * LOG2E
D_FF = 4 * D_MODEL
ALPHA = (2.0 * DEPTH) ** 0.25
LANES = 128

P_QLAT, P_KVLAT, P_KPE, P_RET, P_WQ, P_WK, P_WV, P_GQ, P_GK, P_GV, P_END = (
    0, 256, 384, 512, 1536, 1792, 1920, 2048, 2304, 2432, 2560)
FQ_W = 1024
FQ_A, FQ_C, FQ_D = slice(0, 512), slice(512, 768), slice(768, 1024)
FKV_W = 768
FKV_CKV, FKV_KPE, FKV_WK, FKV_WV, FKV_GK, FKV_GV = (slice(c, c + 128) for c in range(0, FKV_W, 128))
FR_W = 1024
FR_Q, FR_K, FR_V, FR_G = (slice(c, c + 256) for c in range(0, FR_W, 256))
OATT_W = 768
OATT_A, OATT_C, OATT_D = (slice(c, c + 256) for c in range(0, OATT_W, 256))
VMEM_PER_CORE = 64 * 1024 * 1024
VMEM_LIMIT = VMEM_PER_CORE * 3 // 4
VMEM_LIMIT_MERGE_MLP = VMEM_PER_CORE * 15 // 16
TM_PROJ = 1024
TM_POST = 512
SUB_ROWS = 256
TQ_LATENT = 256
ATTN_LOOKAHEAD = 2
MXU_SUM_MIN_KEYS = 1024
RET_GROUP = 4
RET_CHUNKS_PER_STEP = 8

F32 = jnp.float32
BF16 = jnp.bfloat16


def _dot(a, b):
    return jnp.dot(a, b, preferred_element_type=F32)


def _dot_nt(a, b):
    return lax.dot_general(a, b, (((1,), (1,)), ((), ())), preferred_element_type=F32)


def _dot_tn(a, b):
    return lax.dot_general(a, b, (((0,), (0,)), ((), ())), preferred_element_type=F32)


def _sub_tiles(tm, sub_rows=SUB_ROWS):
    n = max(1, tm // sub_rows)
    step = tm // n
    return [slice(i * step, (i + 1) * step) for i in range(n)]


def _layernorm(x, g, b):
    mu = jnp.mean(x, -1, keepdims=True)
    d = x - mu
    var = jnp.mean(d * d, -1, keepdims=True)
    return d * lax.rsqrt(var + NORM_EPS) * g + b


def _rmsnorm(x, g):
    return x * lax.rsqrt(jnp.mean(x * x, -1, keepdims=True) + NORM_EPS) * g


def _seg_sum(x, ones_bd):
    hi = x.astype(BF16)
    lo = (x - hi.astype(F32)).astype(BF16)
    return _dot(hi, ones_bd) + _dot(lo, ones_bd)


def _rope_block(x, cos, sin, half, first):
    rot = jnp.where(first, pltpu.roll(x, LANES - half, 1), pltpu.roll(x, half, 1))
    return x * cos + rot * sin


def _ada_kernel(cond_ref, w_ref, b_ref, o_ref):
    cnd = cond_ref[...]
    s = (cnd * jax.nn.sigmoid(cnd)).astype(BF16)
    o_ref[0] = _dot(s, w_ref[0].astype(BF16)) + b_ref[0]


def _ada(cond8, w_ada, b_ada):
    tn = 2048
    n = w_ada.shape[-1]
    return pl.pallas_call(
        _ada_kernel,
        grid=(DEPTH, n // tn),
        in_specs=[pl.BlockSpec((8, D_MODEL), lambda l, j: (0, 0)),
                  pl.BlockSpec((1, D_MODEL, tn), lambda l, j: (l, 0, j)),
                  pl.BlockSpec((1, 1, tn), lambda l, j: (l, 0, j))],
        out_specs=pl.BlockSpec((1, 8, tn), lambda l, j: (l, 0, j)),
        out_shape=jax.ShapeDtypeStruct((DEPTH, 8, n), F32),
        compiler_params=pltpu.CompilerParams(dimension_semantics=("arbitrary", "arbitrary")),
        name="ada_mod",
    )(cond8, w_ada, b_ada.reshape(DEPTH, 1, n))


def _store_per_seq(o_ref, val, row0, transposed):
    seq = o_ref.shape[2] if transposed else o_ref.shape[1]
    for i in range(val.shape[0] // seq):
        blk = val[i * seq:(i + 1) * seq, :]
        o_ref[row0 // seq + i] = blk.T[0:o_ref.shape[1], :] if transposed else blk


def _proj_kernel(*refs, latent, n_alias):
    refs = refs[:9] + refs[9 + n_alias:]
    x_ref, mod_ref, wmix_ref = refs[:3]
    sh1 = mod_ref[0, 0:1, :]
    sc1 = mod_ref[0, 1:2, :]
    subs = _sub_tiles(x_ref.shape[0])
    def project(r):
        return _dot_nt((x_ref[r, :] * (1.0 + sc1) + sh1).astype(BF16), wmix_ref[0])

    ps = [project(subs[0])]
    for i, r in enumerate(subs):
        if i + 1 < len(subs):
            ps.append(project(subs[i + 1]))
        _proj_post(r, ps.pop(0), refs, latent)


def _proj_post(r, p, refs, latent):
    if latent:
        (_, _, _, qn_ref, wuq_ref, kvn_ref, gqn_ref, gkn_ref, bd_ref,
         cq_ref, sq_ref, ck_ref, sk_ref, ch_ref, sh_ref, fq_ref, fkv_ref, fr_ref) = refs
    else:
        (_, _, _, qn_ref, wuq_ref, kvn_ref, gqn_ref, gkn_ref, bd_ref,
         fq_ref, fkv_ref, fr_ref, ockv_ref, okpe_ref, owk_ref, owv_ref, ogk_ref, ogv_ref) = refs
    rows = p.shape[0]
    lane = lax.broadcasted_iota(jnp.int32, (rows, LANES), 1)
    first_head = (lane % HEAD_DIM) < (HEAD_DIM // 2)

    def rope_heads(v):
        if not latent:
            return v
        cos, sin = ch_ref[r, :], sh_ref[r, :]
        blocks = [_rope_block(v[:, j:j + LANES], cos, sin, HEAD_DIM // 2, first_head)
                  for j in range(0, v.shape[1], LANES)]
        return blocks[0] if len(blocks) == 1 else jnp.concatenate(blocks, axis=1)

    qn = _rmsnorm(p[:, P_QLAT:P_KVLAT], qn_ref[0]).astype(BF16)
    if latent:
        qa2 = _dot(qn, wuq_ref[0])
        cos, sin = cq_ref[r, :], sq_ref[r, :]
        qa = jnp.concatenate([qa2[:, j:j + LANES] * cos + qa2[:, 512 + j:512 + j + LANES] * sin
                              for j in range(0, 512, LANES)], axis=1)
    else:
        qa = _dot(qn, wuq_ref[0, :, 0:512])
    fq_ref[r, FQ_A] = (qa * MLA_SCALE).astype(BF16)

    ckv = _rmsnorm(p[:, P_KVLAT:P_KPE], kvn_ref[0])
    kpe = p[:, P_KPE:P_RET]
    if latent:
        first = lane < MLA_ROPE // 2
        kpe_r = _rope_block(kpe, ck_ref[r, :], sk_ref[r, :], MLA_ROPE // 2, first)
    else:
        kpe_r = kpe
        _store_per_seq(ockv_ref, ckv, r.start, False)
        _store_per_seq(okpe_ref, kpe, r.start, True)
    fkv_ref[r, FKV_CKV] = ckv.astype(BF16)
    fkv_ref[r, FKV_KPE] = kpe_r.astype(BF16)

    fr_ref[r, FR_Q] = p[:, P_RET:P_RET + 256]
    fr_ref[r, FR_K] = p[:, P_RET + 256:P_RET + 512] * (HEAD_DIM ** -0.5)
    fr_ref[r, FR_V.start:FR_G.stop] = p[:, P_RET + 512:P_WQ]

    fq_ref[r, FQ_C] = (rope_heads(p[:, P_WQ:P_WK]) * ATT_SCALE).astype(BF16)
    wk = p[:, P_WK:P_WV]
    wv = p[:, P_WV:P_GQ]
    fkv_ref[r, FKV_WK] = rope_heads(wk).astype(BF16)
    fkv_ref[r, FKV_WV] = wv.astype(BF16)

    bd = bd_ref[...]
    gq = p[:, P_GQ:P_GK]
    gqn = gq * lax.rsqrt(_seg_sum(gq * gq, bd) * (1.0 / HEAD_DIM) + NORM_EPS) * gqn_ref[0]
    fq_ref[r, FQ_D] = (rope_heads(gqn) * ATT_SCALE).astype(BF16)
    gk = p[:, P_GK:P_GV]
    gkn = gk * lax.rsqrt(_seg_sum(gk * gk, bd[0:128, 0:128]) * (1.0 / HEAD_DIM) + NORM_EPS) * gkn_ref[0]
    gv = p[:, P_GV:P_END]
    fkv_ref[r, FKV_GK] = rope_heads(gkn).astype(BF16)
    fkv_ref[r, FKV_GV] = gv.astype(BF16)
    if not latent:
        _store_per_seq(owk_ref, wk, r.start, True)
        _store_per_seq(owv_ref, wv, r.start, True)
        _store_per_seq(ogk_ref, gkn, r.start, True)
        _store_per_seq(ogv_ref, gv, r.start, True)


def _layer_spec(layer, *s):
    return pl.BlockSpec((1,) + s, lambda t: (layer,) + (0,) * len(s), pipeline_mode=pl.Buffered(1))


def _mod_spec(tm, group_len):
    if group_len is None:
        return pl.BlockSpec((1, 6, D_MODEL), lambda t: (0, 0, 0))
    per = group_len // tm
    return pl.BlockSpec((1, 6, D_MODEL), lambda t: (1 + t // per, 0, 0))


def _proj(x, row0, nt, mod, wts, layer, rope, tm, group_len, ctx_seq=None, caches=None):
    latent = group_len is not None
    aliases = {}
    tile0 = row0 // tm
    tok = lambda w: pl.BlockSpec((tm, w), lambda t: (t, 0))
    in_specs = [pl.BlockSpec((tm, D_MODEL), lambda t: (t + tile0, 0)), _mod_spec(tm, group_len),
                _layer_spec(layer, P_END, D_MODEL), _layer_spec(layer, 1, 256), _layer_spec(layer, 256, 1024),
                _layer_spec(layer, 1, 128), _layer_spec(layer, 1, 256), _layer_spec(layer, 1, 128),
                pl.BlockSpec((256, 256), lambda t: (0, 0))]
    args = [x, mod, wts["wmix"], wts["qn"], wts["wuq"], wts["kvn"], wts["gqn"], wts["gkn"], wts["bd"]]
    out_specs = [tok(FQ_W), tok(FKV_W), tok(FR_W)]
    out_shape = [jax.ShapeDtypeStruct((nt, FQ_W), BF16), jax.ShapeDtypeStruct((nt, FKV_W), BF16),
                 jax.ShapeDtypeStruct((nt, FR_W), F32)]
    if latent:
        per = group_len // tm
        in_specs += [pl.BlockSpec((tm, LANES), lambda t: (t % per, 0))] * 6
        args += list(rope)
    else:
        nb = tm // ctx_seq
        seqs = nt // ctx_seq
        sub = _sub_tiles(tm)[0]
        assert (sub.stop - sub.start) % ctx_seq == 0, "a sub-tile must hold whole context sequences"
        out_specs += [pl.BlockSpec((nb, None, ctx_seq, 128), lambda t: (t, layer, 0, 0))]
        out_shape += [jax.ShapeDtypeStruct((seqs, DEPTH, ctx_seq, 128), F32)]
        for r in (MLA_ROPE, 128, 128, 128, 128):
            out_specs.append(pl.BlockSpec((nb, None, r, ctx_seq), lambda t: (t, layer, 0, 0)))
            out_shape.append(jax.ShapeDtypeStruct((seqs, DEPTH, r, ctx_seq), F32))
        if caches is not None:
            aliases = {len(args) + i: 3 + i for i in range(len(caches))}
            in_specs += [pl.BlockSpec(memory_space=pl.ANY)] * len(caches)
            args += list(caches)
    return pl.pallas_call(
        functools.partial(_proj_kernel, latent=latent, n_alias=len(aliases)),
        grid=(nt // tm,), in_specs=in_specs, out_specs=out_specs, out_shape=out_shape,
        input_output_aliases=aliases,
        compiler_params=pltpu.CompilerParams(dimension_semantics=("arbitrary",), vmem_limit_bytes=VMEM_LIMIT),
        name="proj_latent" if latent else "proj_ctx",
    )(*args)


def _softmax_weights(parts, sink, mxu_sum):
    m = parts[0].max(-1, keepdims=True)
    for s in parts[1:]:
        m = jnp.maximum(m, s.max(-1, keepdims=True))
    if sink is not None:
        m = jnp.maximum(m, sink)
    es = [jnp.exp2(s - m) for s in parts]
    extra = None if sink is None else jnp.exp2(sink - m)
    if not mxu_sum:
        for e in es:
            extra = e.sum(-1, keepdims=True) if extra is None else extra + e.sum(-1, keepdims=True)
    return [e.astype(BF16) for e in es], extra


def _normalised(pv, extra):
    if pv.shape[1] == LANES:
        return pv * (1.0 / extra)
    den = pv[:, LANES:LANES + 1]
    if extra is not None:
        den = den + extra
    return pv[:, 0:LANES] * (1.0 / den)


def _window_start(qi, tq, t):
    return jnp.clip(qi * tq - WINDOW, 0, t - (tq + 2 * WINDOW))


def _band_bias(tq):
    span = tq + 2 * WINDOW
    r = (np.arange(2 * tq) % tq)[None, :, None]
    c = np.arange(span)[None, None, :]
    off = (np.arange(3) * WINDOW)[:, None, None]
    return jnp.asarray(np.where(np.abs(r + off - c) <= WINDOW, 0.0, NEG_INF), F32)


def _attn_kernel(*refs, t, tq, n_cache, layer):
    latent = n_cache > 0
    s_len = t + n_cache
    if latent:
        (fq_ref, fkv_ref, cckv_ref, ckpe_ref, cwk_ref, cwv_ref, cgk_ref, cgv_ref, bias_ref,
         wka_ref, wuv_ref, sink_ref, o_ref, ckpe_s, ka_s, va_s, kd_s, vd_s, vc_s, kcc_s, vcc_s) = refs
    else:
        (fq_ref, fkv_ref, wka_ref, wuv_ref, sink_ref, o_ref, ckpe_s, ka_s, va_s, kd_s, vd_s, vc_s) = refs

    qi = pl.program_id(1)

    @pl.when(qi == 0)
    def _():
        ckpe_s[0:t, :] = fkv_ref[0, :, FKV_CKV.start:FKV_KPE.stop]
        kd_s[0:t, :] = fkv_ref[0, :, FKV_GK]
        vd_s[0:t, 0:LANES] = fkv_ref[0, :, FKV_GV]
        vc_s[:, 0:LANES] = fkv_ref[0, :, FKV_WV]
        if latent:
            ckpe_s[t:s_len, 0:128] = cckv_ref[0, 0].astype(BF16)
            ckpe_s[t:s_len, 128:256] = ckpe_ref[0, 0].astype(BF16)
            kd_s[t:s_len, :] = cgk_ref[0, 0].astype(BF16)
            vd_s[t:s_len, 0:LANES] = cgv_ref[0, 0].astype(BF16)
            kcc_s[...] = cwk_ref[0, 0].astype(BF16)
            vcc_s[:, 0:LANES] = cwv_ref[0, 0].astype(BF16)
            vcc_s[:, LANES:] = jnp.ones((n_cache, LANES), BF16)
        vd_s[:, LANES:] = jnp.ones((s_len, LANES), BF16)
        vc_s[:, LANES:] = jnp.ones((t, LANES), BF16)
        ck = ckpe_s[...]
        va = _dot(ck, wuv_ref[0]).astype(BF16)
        for blk in range(2):
            va_s[blk, :, 0:LANES] = va[:, blk * LANES:(blk + 1) * LANES]
            va_s[blk, :, LANES:] = jnp.ones((s_len, LANES), BF16)
        for h in range(N_HEADS):
            ka_s[h] = _dot(ck, wka_ref[0, h]).astype(BF16)

    lane = lax.broadcasted_iota(jnp.int32, (tq, LANES), 1)
    half = [lane < HEAD_DIM, lane >= HEAD_DIM]

    def keep(x, j):
        return jnp.where(half[j], x, 0.0)

    def stacked_q(col, j):
        blks = [fq_ref[0, :, col + g * LANES:col + (g + 1) * LANES] for g in range(2)]
        return jnp.concatenate([keep(b.astype(F32), j).astype(BF16) for b in blks], axis=0)

    if latent:
        span = tq + 2 * WINDOW
        start = pl.multiple_of(_window_start(qi, tq, t), LANES)
    row1 = lax.broadcasted_iota(jnp.int32, (2 * tq, 1), 0)
    mxu_sum = s_len >= MXU_SUM_MIN_KEYS
    vw = 2 * LANES if mxu_sum else LANES

    acc_a = [jnp.zeros((tq, LANES), F32) for _ in range(2)]
    acc_c = [jnp.zeros((tq, LANES), F32) for _ in range(2)]
    acc_d = [jnp.zeros((tq, LANES), F32) for _ in range(2)]

    def a_scores(h):
        return [_dot_nt(fq_ref[0, :, h * LANES:(h + 1) * LANES], ka_s[h])]

    def a_finish(h, ps, sink_term):
        blk = h // 2
        acc_a[blk] = acc_a[blk] + keep(_normalised(_dot(ps[0], va_s[blk, :, 0:vw]), sink_term), h % 2)

    def c_scores(j):
        qs = stacked_q(FQ_C.start, j)
        if latent:
            return [_dot_nt(qs, fkv_ref[0, pl.ds(start, span), FKV_WK]) + bias_ref[0], _dot_nt(qs, kcc_s[...])]
        return [_dot_nt(qs, fkv_ref[0, :, FKV_WK])]

    def c_finish(j, ps, sink_term):
        if latent:
            pv = _dot(ps[0], vc_s[pl.ds(start, span), 0:vw]) + _dot(ps[1], vcc_s[:, 0:vw])
        else:
            pv = _dot(ps[0], vc_s[:, 0:vw])
        pv = _normalised(pv, sink_term)
        for g in range(2):
            acc_c[g] = acc_c[g] + keep(pv[g * tq:(g + 1) * tq], j)

    def d_scores(j):
        return [_dot_nt(stacked_q(FQ_D.start, j), kd_s[...])]

    def d_finish(j, ps, sink_term):
        pv = _normalised(_dot(ps[0], vd_s[:, 0:vw]), sink_term)
        for g in range(2):
            acc_d[g] = acc_d[g] + keep(pv[g * tq:(g + 1) * tq], j)

    def c_sink(j):
        return jnp.where(row1 < tq, sink_ref[layer, 2 * j], sink_ref[layer, 2 * j + 1]) * LOG2E

    c_jobs = [(functools.partial(c_scores, j), functools.partial(c_finish, j), functools.partial(c_sink, j))
              for j in range(2)]
    jobs = [(functools.partial(a_scores, h), functools.partial(a_finish, h), None) for h in range(N_HEADS)]
    jobs += [(functools.partial(d_scores, j), functools.partial(d_finish, j), None) for j in range(2)]
    jobs = c_jobs[:1] + jobs + c_jobs[1:]
    ahead = ATTN_LOOKAHEAD if mxu_sum else len(jobs)
    pending = [job[0]() for job in jobs[:ahead]]
    for i, (_, finish, sink) in enumerate(jobs):
        if i + ahead < len(jobs):
            pending.append(jobs[i + ahead][0]())
        finish(*_softmax_weights(pending.pop(0), None if sink is None else sink(), mxu_sum))

    for cols, acc in ((OATT_A, acc_a), (OATT_C, acc_c), (OATT_D, acc_d)):
        for g in range(2):
            o_ref[0, :, cols.start + g * LANES:cols.start + (g + 1) * LANES] = acc[g].astype(BF16)


def _attn(fq, fkv, caches, wts, layer, tq):
    b, t, _ = fq.shape
    latent = caches is not None
    n_cache = caches[0].shape[2] if latent else 0
    s_len = t + n_cache
    in_specs = [pl.BlockSpec((1, tq, FQ_W), lambda i, q: (i, q, 0)),
                pl.BlockSpec((1, t, FKV_W), lambda i, q: (i, 0, 0))]
    args = [fq, fkv]
    if latent:
        in_specs += [pl.BlockSpec((1, 1, n_cache, LANES), lambda i, q: (i, layer, 0, 0))] * 6
        args += list(caches)
        span = tq + 2 * WINDOW
        assert t >= span and WINDOW <= tq
        in_specs.append(pl.BlockSpec((1, 2 * tq, span),
                                     lambda i, q: ((q * tq - _window_start(q, tq, t)) // WINDOW, 0, 0)))
        args.append(_band_bias(tq))
    in_specs += [pl.BlockSpec((1, N_HEADS, 256, 128), lambda i, q: (layer, 0, 0, 0)),
                 pl.BlockSpec((1, 256, 256), lambda i, q: (layer, 0, 0)),
                 pl.BlockSpec(memory_space=pltpu.SMEM)]
    args += [wts["wka"], wts["wuv"], wts["sink"]]
    scratch = [pltpu.VMEM((s_len, 256), BF16), pltpu.VMEM((N_HEADS, s_len, 128), BF16),
               pltpu.VMEM((2, s_len, 256), BF16), pltpu.VMEM((s_len, 128), BF16), pltpu.VMEM((s_len, 256), BF16),
               pltpu.VMEM((t, 256), BF16)]
    if latent:
        scratch += [pltpu.VMEM((n_cache, 128), BF16), pltpu.VMEM((n_cache, 256), BF16)]
    return pl.pallas_call(
        functools.partial(_attn_kernel, t=t, tq=tq, n_cache=n_cache, layer=layer),
        grid=(b, t // tq), in_specs=in_specs,
        out_specs=pl.BlockSpec((1, tq, OATT_W), lambda i, q: (i, q, 0)),
        out_shape=jax.ShapeDtypeStruct((b, t, OATT_W), BF16),
        scratch_shapes=scratch,
        compiler_params=pltpu.CompilerParams(dimension_semantics=("arbitrary", "arbitrary"),
                                             vmem_limit_bytes=VMEM_LIMIT),
        name="attn_latent" if latent else "attn_ctx",
    )(*args)


def _ret_kernel(*refs, t, bb, s0_stride, layer, want_state, n_alias):
    refs = refs[:7] + refs[7 + n_alias:]
    if want_state:
        (fr_ref, s0f_ref, s0b_ref, decf_ref, decb_ref, gain_ref, bd_ref, o_ref, sf_ref, sb_ref,
         dm_s, dec_s, cd_s, kvf_s, kvb_s, stf_s, stb_s, sf_s, sb_s) = refs
    else:
        (fr_ref, s0f_ref, s0b_ref, decf_ref, decb_ref, gain_ref, bd_ref, o_ref,
         dm_s, dec_s, cd_s, kvf_s, kvb_s, stf_s, stb_s, sf_s, sb_s) = refs
    nc = t // CHUNK
    w = BRANCH_W
    lane_w = lax.broadcasted_iota(jnp.int32, (CHUNK, w), 1) // HEAD_DIM

    @pl.when(pl.program_id(0) == 0)
    def _():
        row_w = lax.broadcasted_iota(jnp.int32, (CHUNK, w), 0).astype(F32)

        def lane_decay(dec_ref):
            v = jnp.zeros((CHUNK, w), F32)
            for h in range(N_HEADS):
                v = jnp.where(lane_w == h, dec_ref[layer, h], v)
            return jax.nn.log_sigmoid(v)

        lgf = lane_decay(decf_ref)
        lgb = lane_decay(decb_ref)
        dec_s[0] = jnp.exp((row_w + 1.0) * lgf)
        dec_s[1] = jnp.exp((CHUNK - 1.0 - row_w) * lgf)
        dec_s[2] = jnp.exp((CHUNK - row_w) * lgb)
        dec_s[3] = jnp.exp(row_w * lgb)
        cd_s[0] = jnp.concatenate([jnp.exp(CHUNK * lgf)] * (w // CHUNK), axis=0)
        cd_s[1] = jnp.concatenate([jnp.exp(CHUNK * lgb)] * (w // CHUNK), axis=0)
        ii = lax.broadcasted_iota(jnp.int32, (CHUNK, CHUNK), 0).astype(F32)
        jj = lax.broadcasted_iota(jnp.int32, (CHUNK, CHUNK), 1).astype(F32)
        diff = ii - jj
        for h in range(N_HEADS):
            lf = jax.nn.log_sigmoid(jnp.full((CHUNK, CHUNK), decf_ref[layer, h], F32))
            lb = jax.nn.log_sigmoid(jnp.full((CHUNK, CHUNK), decb_ref[layer, h], F32))
            d_f = jnp.where(diff >= 0, jnp.exp(jnp.maximum(diff, 0.0) * lf), 0.0)
            d_b = jnp.where(diff < 0, jnp.exp(jnp.maximum(-diff, 0.0) * lb), 0.0)
            dm_s[h] = d_f + d_b

    r2 = lax.broadcasted_iota(jnp.int32, (w, w), 0) // HEAD_DIM
    c2 = lax.broadcasted_iota(jnp.int32, (w, w), 1) // HEAD_DIM
    diag = r2 == c2

    jobs = [(bi, n) for bi in range(bb) for n in range(nc)]
    groups = [jobs[i:i + RET_GROUP] for i in range(0, len(jobs), RET_GROUP)]
    rows = lambda n: slice(n * CHUNK, (n + 1) * CHUNK)
    slot = lambda bi, n: bi * nc + n

    for grp in groups:
        kvs = []
        for bi, n in grp:
            k = fr_ref[bi, rows(n), FR_K]
            vb = fr_ref[bi, rows(n), FR_V].astype(BF16)
            kk = jnp.concatenate([k * dec_s[1], k * dec_s[3]], axis=1).astype(BF16)
            kvs.append(_dot_tn(kk, vb))
        for (bi, n), kv in zip(grp, kvs):
            kvf_s[slot(bi, n)] = jnp.where(diag, kv[0:w], 0.0)
            kvb_s[slot(bi, n)] = jnp.where(diag, kv[w:2 * w], 0.0)

    for bi in range(bb):
        sf_s[bi] = s0f_ref[bi * s0_stride]
        sb_s[bi] = s0b_ref[bi * s0_stride]
    for i in range(nc):
        m = nc - 1 - i
        for bi in range(bb):
            sf = sf_s[bi]
            stf_s[slot(bi, i)] = sf.astype(BF16)
            sf_s[bi] = sf * cd_s[0] + kvf_s[slot(bi, i)]
            sb = sb_s[bi]
            stb_s[slot(bi, m)] = sb.astype(BF16)
            sb_s[bi] = sb * cd_s[1] + kvb_s[slot(bi, m)]
    if want_state:
        for bi in range(bb):
            for h in range(N_HEADS):
                sl = slice(h * HEAD_DIM, (h + 1) * HEAD_DIM)
                sf_ref[bi, h] = sf_s[bi, sl, sl]
                sb_ref[bi, h] = sb_s[bi, sl, sl]

    gain = gain_ref[0]
    bd2 = jnp.concatenate([bd_ref[...], bd_ref[...]], axis=0)

    def seg_mean(xs):
        cat = []
        for x in xs:
            hi = x.astype(BF16)
            cat.append(jnp.concatenate([hi, (x - hi.astype(F32)).astype(BF16)], axis=1))
        return [_dot(c, bd2) * (1.0 / HEAD_DIM) for c in cat]

    for grp in groups:
        qs = [fr_ref[bi, rows(n), FR_Q] for bi, n in grp]
        vs = [fr_ref[bi, rows(n), FR_V] for bi, n in grp]
        qks, inter_f, inter_b = [], [], []
        for (bi, n), q in zip(grp, qs):
            kb = fr_ref[bi, rows(n), FR_K].astype(BF16)
            qstack = jnp.concatenate([jnp.where(lane_w == h, q, 0.0) for h in range(N_HEADS)], axis=0)
            qks.append(_dot_nt(qstack.astype(BF16), kb))
            qb = q.astype(BF16)
            inter_f.append(_dot(qb, stf_s[slot(bi, n)]))
            inter_b.append(_dot(qb, stb_s[slot(bi, n)]))
        outs = []
        for qk, v, i_f, i_b in zip(qks, vs, inter_f, inter_b):
            att = jnp.concatenate([qk[rows(h)] * dm_s[h] for h in range(N_HEADS)], axis=1)
            vstack = jnp.concatenate([jnp.where(lane_w == h, v, 0.0) for h in range(N_HEADS)], axis=0)
            outs.append(_dot(att.astype(BF16), vstack.astype(BF16)) + i_f * dec_s[0] + i_b * dec_s[2])
        ds = [o - mu for o, mu in zip(outs, seg_mean(outs))]
        var = seg_mean([d * d for d in ds])
        for (bi, n), d, vr in zip(grp, ds, var):
            gate = fr_ref[bi, rows(n), FR_G]
            y = d * lax.rsqrt(vr + NORM_EPS) * gain
            o_ref[bi, rows(n), :] = (gate * jax.nn.sigmoid(gate) * y).astype(BF16)


def _ret(fr, s0f, s0b, wts, layer, want_state, states=None):
    b, t, _ = fr.shape
    nc = t // CHUNK
    bb = max(1, RET_CHUNKS_PER_STEP // nc)
    shared = s0f.shape[0] != b
    s_spec = (pl.BlockSpec((1, BRANCH_W, BRANCH_W), lambda i: (0, 0, 0)) if shared
              else pl.BlockSpec((bb, BRANCH_W, BRANCH_W), lambda i: (i, 0, 0)))
    smem = pl.BlockSpec(memory_space=pltpu.SMEM)
    out_specs = [pl.BlockSpec((bb, t, BRANCH_W), lambda i: (i, 0, 0))]
    out_shape = [jax.ShapeDtypeStruct((b, t, BRANCH_W), BF16)]
    in_specs = [pl.BlockSpec((bb, t, FR_W), lambda i: (i, 0, 0)), s_spec, s_spec, smem, smem,
                pl.BlockSpec((1, 1, BRANCH_W), lambda i: (layer, 0, 0)),
                pl.BlockSpec((BRANCH_W, BRANCH_W), lambda i: (0, 0))]
    args = [fr, s0f, s0b, wts["decf"], wts["decb"], wts["gn"], wts["bd"]]
    aliases = {}
    if want_state:
        st_spec = pl.BlockSpec((bb, None, N_HEADS, HEAD_DIM, HEAD_DIM), lambda i: (i, layer, 0, 0, 0))
        out_specs += [st_spec, st_spec]
        out_shape += [jax.ShapeDtypeStruct((b, DEPTH, N_HEADS, HEAD_DIM, HEAD_DIM), F32)] * 2
        if states is not None:
            aliases = {len(args) + i: 1 + i for i in range(2)}
            in_specs += [pl.BlockSpec(memory_space=pl.ANY)] * 2
            args += list(states)
    sq = (BRANCH_W, BRANCH_W)
    return pl.pallas_call(
        functools.partial(_ret_kernel, t=t, bb=bb, s0_stride=0 if shared else 1, layer=layer,
                          want_state=want_state, n_alias=len(aliases)),
        grid=(b // bb,),
        in_specs=in_specs, out_specs=out_specs, out_shape=out_shape, input_output_aliases=aliases,
        scratch_shapes=[pltpu.VMEM((N_HEADS, CHUNK, CHUNK), F32), pltpu.VMEM((4, CHUNK, BRANCH_W), F32),
                        pltpu.VMEM((2,) + sq, F32),
                        pltpu.VMEM((bb * nc,) + sq, F32), pltpu.VMEM((bb * nc,) + sq, F32),
                        pltpu.VMEM((bb * nc,) + sq, BF16), pltpu.VMEM((bb * nc,) + sq, BF16),
                        pltpu.VMEM((bb,) + sq, F32), pltpu.VMEM((bb,) + sq, F32)],
        compiler_params=pltpu.CompilerParams(dimension_semantics=("arbitrary",),
                                             vmem_limit_bytes=VMEM_LIMIT),
        name="retention",
    )(*args)


def _post_kernel(*refs, n_x, n_out, ctx_tiles):
    x_refs, refs = refs[:n_x], refs[n_x:]
    (mod_ref, oatt_c, oatt_l, oret_c, oret_l, wg_ref, wb_ref, wo_ref, g1_ref, b1_ref,
     wup_ref, wdn_ref, g2_ref, b2_ref) = refs[:14]
    o_refs = refs[14:]
    is_ctx = pl.program_id(0) < ctx_tiles
    pick = lambda c_ref, l_ref, r, cols: jnp.where(is_ctx, c_ref[r, cols], l_ref[r, cols])
    every = slice(None)
    if n_x == 1:
        x_rows = lambda r: x_refs[0][r, :]
    else:
        x_rows = lambda r: pick(x_refs[0], x_refs[1], r, every)
    sh1, sc1, g1, sh2, sc2, g2 = [mod_ref[0, i:i + 1, :] for i in range(6)]
    subs = _sub_tiles(x_refs[0].shape[0])
    ff_half = D_FF // 2
    xs = [x_rows(r) for r in subs]

    def gated(r, x):
        h = (x * (1.0 + sc1) + sh1).astype(BF16)
        branches = (pick(oatt_c, oatt_l, r, OATT_A), pick(oret_c, oret_l, r, every),
                    pick(oatt_c, oatt_l, r, OATT_C), pick(oatt_c, oatt_l, r, OATT_D))
        tsum = None
        for i, o in enumerate(branches):
            gate = jax.nn.sigmoid(_dot_nt(h, wg_ref[0, i * D_MODEL:(i + 1) * D_MODEL, :]))
            term = gate * _dot(o, wb_ref[0, i])
            tsum = term if tsum is None else tsum + term
        return tsum

    sums = [gated(r, x) for r, x in zip(subs, xs)]
    ys = [_dot(s.astype(BF16), wo_ref[0]) for s in sums]
    x1s = [_layernorm(ALPHA * x + g1 * y, g1_ref[0], b1_ref[0]) for x, y in zip(xs, ys)]
    hs = [(x1 * (1.0 + sc2) + sh2).astype(BF16) for x1 in x1s]
    us = [[_dot(h, wup_ref[0, :, c * ff_half:(c + 1) * ff_half]) for c in range(2)] for h in hs]
    fs = []
    for u2 in us:
        f = None
        for c, u in enumerate(u2):
            u = jnp.maximum(u, 0.0)
            part = _dot((u * u).astype(BF16), wdn_ref[0, c * ff_half:(c + 1) * ff_half, :])
            f = part if f is None else f + part
        fs.append(f)
    outs = [_layernorm(ALPHA * x1 + g2 * f, g2_ref[0], b2_ref[0]) for x1, f in zip(x1s, fs)]
    if n_out == 1:
        for r, o in zip(subs, outs):
            o_refs[0][r, :] = o
    else:
        @pl.when(is_ctx)
        def _():
            for r, o in zip(subs, outs):
                o_refs[0][r, :] = o

        @pl.when(jnp.logical_not(is_ctx))
        def _():
            for r, o in zip(subs, outs):
                o_refs[1][r, :] = o


def _post(xs, mod, oatts, orets, wts, layer, tm, latent_len, split_out):
    n_ctx, n_lat = oatts[0].shape[0], oatts[1].shape[0]
    ctx_tiles, lat_tiles = n_ctx // tm, n_lat // tm
    per = latent_len // tm
    all_tok = lambda w: pl.BlockSpec((tm, w), lambda t: (t, 0))
    ctx_tok = lambda w: pl.BlockSpec((tm, w), lambda t: (jnp.minimum(t, ctx_tiles - 1), 0))
    lat_tok = lambda w: pl.BlockSpec((tm, w), lambda t: (jnp.maximum(t - ctx_tiles, 0), 0))
    pair = lambda w: [ctx_tok(w), lat_tok(w)]
    vec = _layer_spec(layer, 1, D_MODEL)
    mod_spec = pl.BlockSpec((1, 6, D_MODEL),
                            lambda t: (jnp.where(t < ctx_tiles, 0, 1 + (t - ctx_tiles) // per), 0, 0))
    x_specs = [all_tok(D_MODEL)] if len(xs) == 1 else pair(D_MODEL)
    if split_out:
        out_specs = pair(D_MODEL)
        out_shape = [jax.ShapeDtypeStruct((n_ctx, D_MODEL), F32), jax.ShapeDtypeStruct((n_lat, D_MODEL), F32)]
    else:
        out_specs = [all_tok(D_MODEL)]
        out_shape = [jax.ShapeDtypeStruct((n_ctx + n_lat, D_MODEL), F32)]
    return pl.pallas_call(
        functools.partial(_post_kernel, n_x=len(xs), n_out=len(out_shape), ctx_tiles=ctx_tiles),
        grid=(ctx_tiles + lat_tiles,),
        in_specs=x_specs + [mod_spec] + pair(OATT_W) + pair(BRANCH_W) + [
            _layer_spec(layer, 4 * D_MODEL, D_MODEL), _layer_spec(layer, 4, BRANCH_W, D_MODEL),
            _layer_spec(layer, D_MODEL, D_MODEL), vec, vec,
            _layer_spec(layer, D_MODEL, D_FF), _layer_spec(layer, D_FF, D_MODEL), vec, vec],
        out_specs=out_specs, out_shape=out_shape,
        compiler_params=pltpu.CompilerParams(dimension_semantics=("arbitrary",),
                                             vmem_limit_bytes=VMEM_LIMIT_MERGE_MLP),
        name="merge_mlp",
    )(*xs, mod, *oatts, *orets, wts["wg"], wts["wb"], wts["wo"], wts["ln1g"], wts["ln1b"],
      wts["wup"], wts["wdn"], wts["ln2g"], wts["ln2b"])


def _win_kernel(w_ref, wmix_ref, wg_ref):
    o_ret = 256 + 128 + MLA_ROPE
    o_wq, o_wk, o_gq, o_gk, o_gate = o_ret + 1024, o_ret + 1280, o_ret + 1536, o_ret + 1792, o_ret + 2048

    def put(dst, src, n):
        wmix_ref[0, dst:dst + n, :] = w_ref[0, src:src + n, :].astype(BF16)

    def put_swapped(dst, src):
        for i, h in enumerate((0, 2, 1, 3)):
            put(dst + i * HEAD_DIM, src + h * HEAD_DIM, HEAD_DIM)

    put(P_QLAT, 0, o_ret)
    pad = P_RET - o_ret
    wmix_ref[0, o_ret:P_RET, :] = jnp.zeros((pad, w_ref.shape[2]), BF16)
    put(P_RET, o_ret, o_wq - o_ret)
    put_swapped(P_WQ, o_wq)
    put(P_WK, o_wk, o_gq - o_wk)
    put_swapped(P_GQ, o_gq)
    put(P_GK, o_gk, o_gate - o_gk)
    wg_ref[0] = w_ref[0, o_gate:, :].astype(BF16)


def _prep_win(w_in):
    w_t = jnp.swapaxes(w_in, 1, 2)
    n_in = w_t.shape[1]
    kb = D_MODEL // 2
    return pl.pallas_call(
        _win_kernel,
        grid=(DEPTH, D_MODEL // kb),
        in_specs=[pl.BlockSpec((1, n_in, kb), lambda l, k: (l, 0, k))],
        out_specs=[pl.BlockSpec((1, P_END, kb), lambda l, k: (l, 0, k)),
                   pl.BlockSpec((1, 4 * D_MODEL, kb), lambda l, k: (l, 0, k))],
        out_shape=[jax.ShapeDtypeStruct((DEPTH, P_END, D_MODEL), BF16),
                   jax.ShapeDtypeStruct((DEPTH, 4 * D_MODEL, D_MODEL), BF16)],
        compiler_params=pltpu.CompilerParams(dimension_semantics=("arbitrary", "arbitrary"),
                                             vmem_limit_bytes=VMEM_LIMIT),
        name="prep_w_in",
    )(w_t)


def _prep_weights(w_in, mla_q_norm, mla_w_uq, mla_kv_norm, mla_w_uk, mla_w_uv, ret_decay_fwd, ret_decay_bwd,
                  ret_gn_gain, win_sink, gqa_q_norm, gqa_k_norm, w_branch, w_o, ln1_g, ln1_b, w_up, w_down,
                  ln2_g, ln2_b):
    wmix, wg = _prep_win(w_in)
    uq = mla_w_uq.reshape(DEPTH, 256, N_HEADS, HEAD_DIM + MLA_ROPE)
    half = MLA_ROPE // 2
    uq_sw = jnp.concatenate([jnp.zeros_like(uq[..., :HEAD_DIM]), uq[..., HEAD_DIM + half:],
                             uq[..., HEAD_DIM:HEAD_DIM + half]], axis=-1)
    lane_pad = ((0, 0), (0, 0), (0, 0), (0, LANES - HEAD_DIM - MLA_ROPE))
    wuq = jnp.concatenate([jnp.pad(w, lane_pad).reshape(DEPTH, 256, N_HEADS * LANES) for w in (uq, uq_sw)],
                          axis=-1).astype(BF16)
    uk = mla_w_uk.reshape(DEPTH, MLA_KV_RANK, N_HEADS, HEAD_DIM).transpose(0, 2, 1, 3)
    top = jnp.pad(uk, ((0, 0), (0, 0), (0, 0), (0, LANES - HEAD_DIM)))
    eye = np.zeros((LANES, LANES), np.float32)
    eye[np.arange(MLA_ROPE), HEAD_DIM + np.arange(MLA_ROPE)] = 1.0
    wka = jnp.concatenate([top, jnp.broadcast_to(eye, (DEPTH, N_HEADS, LANES, LANES))], axis=2).astype(BF16)
    wuv = jnp.pad(mla_w_uv, ((0, 0), (0, 256 - MLA_KV_RANK), (0, 0))).astype(BF16)
    bd = (np.arange(BRANCH_W)[:, None] // HEAD_DIM == np.arange(BRANCH_W)[None, :] // HEAD_DIM).astype(np.float32)
    wb_cd = jnp.swapaxes(w_branch[:, 2:].reshape(DEPTH, 2, 2, 2, HEAD_DIM, D_MODEL), 2, 3
                         ).reshape(DEPTH, 2, BRANCH_W, D_MODEL)
    w_branch = jnp.concatenate([w_branch[:, :2], wb_cd], axis=1)
    row = lambda a: a[:, None, :]
    return dict(
        wmix=wmix, wg=wg, wuq=wuq, wka=wka, wuv=wuv, bd=jnp.asarray(bd, BF16),
        qn=row(mla_q_norm), kvn=row(mla_kv_norm),
        gqn=row(jnp.tile(gqa_q_norm, (1, N_HEADS))), gkn=row(jnp.tile(gqa_k_norm, (1, 2))),
        decf=ret_decay_fwd, decb=ret_decay_bwd, gn=row(ret_gn_gain), sink=win_sink,
        wb=w_branch.astype(BF16), wo=w_o.astype(BF16), ln1g=row(ln1_g), ln1b=row(ln1_b),
        wup=w_up.astype(BF16), wdn=w_down.astype(BF16), ln2g=row(ln2_g), ln2b=row(ln2_b))


def _axial_tables(t, rot_dim):
    rows = t // GRID_W
    row = np.repeat(np.arange(rows, dtype=np.float32), GRID_W)
    col = (np.arange(t) % GRID_W).astype(np.float32)
    n_freq = rot_dim // 4
    inv = np.power(np.float32(ROPE_BASE), -np.arange(n_freq, dtype=np.float32) / np.float32(n_freq))
    ang = np.concatenate([row[:, None] * inv, col[:, None] * inv], axis=-1).astype(np.float32)
    return np.cos(ang), np.sin(ang)


def _rope_tables(t):
    ca, sa = _axial_tables(t, MLA_ROPE)
    ch, sh = _axial_tables(t, HEAD_DIM)
    one = lambda n: np.ones((t, n), np.float32)
    zero = lambda n: np.zeros((t, n), np.float32)
    cq = np.concatenate([one(HEAD_DIM), ca, ca, one(32)], axis=1)
    sq = np.concatenate([zero(HEAD_DIM), -sa, sa, zero(32)], axis=1)
    ck = np.concatenate([ca, ca, one(96)], axis=1)
    sk = np.concatenate([-sa, sa, zero(96)], axis=1)
    chh = np.concatenate([ch, ch, ch, ch], axis=1)
    shh = np.concatenate([-sh, sh, -sh, sh], axis=1)
    return tuple(jnp.asarray(a, F32) for a in (cq, sq, ck, sk, chh, shh))


def _block_diag(s):
    b = s.shape[0]
    same = np.eye(N_HEADS, dtype=bool)[None, :, None, :, None]
    return jnp.where(same, s[:, :, :, None, :], 0.0).reshape(b, BRANCH_W, BRANCH_W)


def kernel(x_prompt, x_sample, cache_mla_ckv, cache_mla_kpe, cache_win_k, cache_win_v, cache_gqa_k, cache_gqa_v,
           state_ret_fwd, state_ret_bwd, c, c_ctx, w_ada, b_ada, w_in, mla_q_norm, mla_w_uq, mla_kv_norm, mla_w_uk,
           mla_w_uv, ret_decay_fwd, ret_decay_bwd, ret_gn_gain, win_sink, gqa_q_norm, gqa_k_norm, w_branch, w_o,
           ln1_g, ln1_b, w_up, w_down, ln2_g, ln2_b):
    batch, seq, _ = x_prompt.shape
    dec_b, dec_t, _ = x_sample.shape
    past = cache_mla_ckv.shape[2]

    cond8 = jnp.concatenate([c_ctx[None], c, jnp.zeros((8 - 1 - dec_b, D_MODEL), F32)], axis=0)
    mod = _ada(cond8, w_ada, b_ada).reshape(DEPTH, 8, 6, D_MODEL)
    rope = _rope_tables(dec_t)
    caches = (cache_mla_ckv,
              jnp.pad(cache_mla_kpe, ((0, 0), (0, 0), (0, 0), (0, LANES - MLA_ROPE))),
              cache_win_k.reshape(dec_b, DEPTH, past, LANES), cache_win_v.reshape(dec_b, DEPTH, past, LANES),
              cache_gqa_k.reshape(dec_b, DEPTH, past, LANES), cache_gqa_v.reshape(dec_b, DEPTH, past, LANES))

    wts = _prep_weights(w_in, mla_q_norm, mla_w_uq, mla_kv_norm, mla_w_uk, mla_w_uv, ret_decay_fwd, ret_decay_bwd,
                        ret_gn_gain, win_sink, gqa_q_norm, gqa_k_norm, w_branch, w_o, ln1_g, ln1_b, w_up, w_down,
                        ln2_g, ln2_b)
    zero_state = jnp.zeros((1, BRANCH_W, BRANCH_W), F32)
    n_ctx, n_lat = batch * seq, dec_b * dec_t
    per_b = lambda a: a.reshape(batch, seq, a.shape[-1])
    per_d = lambda a: a.reshape(dec_b, dec_t, a.shape[-1])
    flat = lambda a: a.reshape(-1, a.shape[-1])
    x = [x_prompt.reshape(n_ctx, D_MODEL), x_sample.reshape(n_lat, D_MODEL)]
    new_caches = None
    new_states = None
    for l in range(DEPTH):
        x_ctx, x_lat = (x[0], 0), ((x[1], 0) if len(x) == 2 else (x[0], n_ctx))
        fq, fkv, fr, *new_caches = _proj(*x_ctx, n_ctx, mod[l], wts, l, None, TM_PROJ, None, seq, new_caches)
        oatt_c = _attn(per_b(fq), per_b(fkv), None, wts, l, seq)
        oret_c, *new_states = _ret(per_b(fr), zero_state, zero_state, wts, l, True, new_states)
        fq, fkv, fr = _proj(*x_lat, n_lat, mod[l], wts, l, rope, TM_PROJ, dec_t)
        oatt_l = _attn(per_d(fq), per_d(fkv), caches, wts, l, TQ_LATENT)
        (oret_l,) = _ret(per_d(fr), _block_diag(state_ret_fwd[:, l]), _block_diag(state_ret_bwd[:, l]), wts, l, False)
        x = _post(x, mod[l], (flat(oatt_c), flat(oatt_l)), (flat(oret_c), flat(oret_l)), wts, l, TM_POST, dec_t,
                  split_out=l == DEPTH - 1)

    ckv, kpe, *kv = new_caches
    kv = [jnp.transpose(a.reshape(batch, DEPTH, 2, HEAD_DIM, seq), (0, 1, 4, 2, 3)) for a in kv]
    return (per_b(x[0]), per_d(x[1]), ckv, jnp.swapaxes(kpe, 2, 3), *kv, *new_states)
```

```python
import functools

import jax
import jax.numpy as jnp
import numpy as np
from jax import lax
from jax.experimental import pallas as pl
from jax.experimental.pallas import tpu as pltpu

D_MODEL = 1024
DEPTH = 2
GRID_W = 64
CHUNK = 128
WINDOW = 128
ROPE_BASE = 10000.0
NORM_EPS = 1e-6
NEG_INF = -1e30
HEAD_DIM = 64
N_HEADS = 4
BRANCH_W = 256
MLA_ROPE = 32
MLA_KV_RANK = 128
LOG2E = 1.4426950408889634
MLA_SCALE = (HEAD_DIM + MLA_ROPE) ** -0.5 * LOG2E
ATT_SCALE = HEAD_DIM ** -0.5 * LOG2E
D_FF = 4 * D_MODEL
ALPHA = (2.0 * DEPTH) ** 0.25
LANES = 128

P_QLAT, P_KVLAT, P_KPE, P_RET, P_WQ, P_WK, P_WV, P_GQ, P_GK, P_GV, P_END = (
    0, 256, 384, 512, 1536, 1792, 1920, 2048, 2304, 2432, 2560)
FQ_W = 1024
FQ_A, FQ_C, FQ_D = slice(0, 512), slice(512, 768), slice(768, 1024)
FKV_W = 768
FKV_CKV, FKV_KPE, FKV_WK, FKV_WV, FKV_GK, FKV_GV = (slice(c, c + 128) for c in range(0, FKV_W, 128))
FR_W = 1024
FR_Q, FR_K, FR_V, FR_G = (slice(c, c + 256) for c in range(0, FR_W, 256))
OATT_W = 768
OATT_A, OATT_C, OATT_D = (slice(c, c + 256) for c in range(0, OATT_W, 256))
VMEM_PER_CORE = 64 * 1024 * 1024
VMEM_LIMIT = VMEM_PER_CORE * 3 // 4
VMEM_LIMIT_MERGE_MLP = VMEM_PER_CORE * 15 // 16
TM_PROJ = 1024
TM_POST = 512
SUB_ROWS = 256
TQ_LATENT = 256
ATTN_LOOKAHEAD = 2
ATTN_CTX_SEQS_PER_STEP = 2
MXU_SUM_MIN_KEYS = 1024
RET_GROUP = 4
RET_CHUNKS_PER_STEP = 8

F32 = jnp.float32
BF16 = jnp.bfloat16


def _dot(a, b):
    return jnp.dot(a, b, preferred_element_type=F32)


def _dot_nt(a, b):
    return lax.dot_general(a, b, (((1,), (1,)), ((), ())), preferred_element_type=F32)


def _dot_tn(a, b):
    return lax.dot_general(a, b, (((0,), (0,)), ((), ())), preferred_element_type=F32)


def _sub_tiles(tm, sub_rows=SUB_ROWS):
    n = max(1, tm // sub_rows)
    step = tm // n
    return [slice(i * step, (i + 1) * step) for i in range(n)]


def _layernorm(x, g, b):
    mu = jnp.mean(x, -1, keepdims=True)
    d = x - mu
    var = jnp.mean(d * d, -1, keepdims=True)
    return d * lax.rsqrt(var + NORM_EPS) * g + b


def _rmsnorm(x, g):
    return x * lax.rsqrt(jnp.mean(x * x, -1, keepdims=True) + NORM_EPS) * g


def _seg_sum(x, ones_bd):
    hi = x.astype(BF16)
    lo = (x - hi.astype(F32)).astype(BF16)
    return _dot(hi, ones_bd) + _dot(lo, ones_bd)


def _rope_block(x, cos, sin, half, first):
    rot = jnp.where(first, pltpu.roll(x, LANES - half, 1), pltpu.roll(x, half, 1))
    return x * cos + rot * sin


def _ada_kernel(cond_ref, w_ref, b_ref, o_ref):
    cnd = cond_ref[...]
    s = (cnd * jax.nn.sigmoid(cnd)).astype(BF16)
    o_ref[0] = _dot(s, w_ref[0].astype(BF16)) + b_ref[0]


def _ada(cond8, w_ada, b_ada):
    tn = 2048
    n = w_ada.shape[-1]
    return pl.pallas_call(
        _ada_kernel,
        grid=(DEPTH, n // tn),
        in_specs=[pl.BlockSpec((8, D_MODEL), lambda l, j: (0, 0)),
                  pl.BlockSpec((1, D_MODEL, tn), lambda l, j: (l, 0, j)),
                  pl.BlockSpec((1, 1, tn), lambda l, j: (l, 0, j))],
        out_specs=pl.BlockSpec((1, 8, tn), lambda l, j: (l, 0, j)),
        out_shape=jax.ShapeDtypeStruct((DEPTH, 8, n), F32),
        compiler_params=pltpu.CompilerParams(dimension_semantics=("arbitrary", "arbitrary")),
        name="ada_mod",
    )(cond8, w_ada, b_ada.reshape(DEPTH, 1, n))


def _store_per_seq(o_ref, val, row0, transposed):
    seq = o_ref.shape[2] if transposed else o_ref.shape[1]
    for i in range(val.shape[0] // seq):
        blk = val[i * seq:(i + 1) * seq, :]
        o_ref[row0 // seq + i] = blk.T[0:o_ref.shape[1], :] if transposed else blk


def _proj_kernel(*refs, latent, n_alias):
    refs = refs[:9] + refs[9 + n_alias:]
    x_ref, mod_ref, wmix_ref = refs[:3]
    sh1 = mod_ref[0, 0:1, :]
    sc1 = mod_ref[0, 1:2, :]
    subs = _sub_tiles(x_ref.shape[0])
    def project(r):
        return _dot_nt((x_ref[r, :] * (1.0 + sc1) + sh1).astype(BF16), wmix_ref[0])

    ps = [project(subs[0])]
    for i, r in enumerate(subs):
        if i + 1 < len(subs):
            ps.append(project(subs[i + 1]))
        _proj_post(r, ps.pop(0), refs, latent)


def _proj_post(r, p, refs, latent):
    if latent:
        (_, _, _, qn_ref, wuq_ref, kvn_ref, gqn_ref, gkn_ref, bd_ref,
         cq_ref, sq_ref, ck_ref, sk_ref, ch_ref, sh_ref, fq_ref, fkv_ref, fr_ref) = refs
    else:
        (_, _, _, qn_ref, wuq_ref, kvn_ref, gqn_ref, gkn_ref, bd_ref,
         fq_ref, fkv_ref, fr_ref, ockv_ref, okpe_ref, owk_ref, owv_ref, ogk_ref, ogv_ref) = refs
    rows = p.shape[0]
    lane = lax.broadcasted_iota(jnp.int32, (rows, LANES), 1)
    first_head = (lane % HEAD_DIM) < (HEAD_DIM // 2)

    def rope_heads(v):
        if not latent:
            return v
        cos, sin = ch_ref[r, :], sh_ref[r, :]
        blocks = [_rope_block(v[:, j:j + LANES], cos, sin, HEAD_DIM // 2, first_head)
                  for j in range(0, v.shape[1], LANES)]
        return blocks[0] if len(blocks) == 1 else jnp.concatenate(blocks, axis=1)

    qn = _rmsnorm(p[:, P_QLAT:P_KVLAT], qn_ref[0]).astype(BF16)
    if latent:
        qa2 = _dot(qn, wuq_ref[0])
        cos, sin = cq_ref[r, :], sq_ref[r, :]
        qa = jnp.concatenate([qa2[:, j:j + LANES] * cos + qa2[:, 512 + j:512 + j + LANES] * sin
                              for j in range(0, 512, LANES)], axis=1)
    else:
        qa = _dot(qn, wuq_ref[0, :, 0:512])
    fq_ref[r, FQ_A] = (qa * MLA_SCALE).astype(BF16)

    ckv = _rmsnorm(p[:, P_KVLAT:P_KPE], kvn_ref[0])
    kpe = p[:, P_KPE:P_RET]
    if latent:
        first = lane < MLA_ROPE // 2
        kpe_r = _rope_block(kpe, ck_ref[r, :], sk_ref[r, :], MLA_ROPE // 2, first)
    else:
        kpe_r = kpe
        _store_per_seq(ockv_ref, ckv, r.start, False)
        _store_per_seq(okpe_ref, kpe, r.start, True)
    fkv_ref[r, FKV_CKV] = ckv.astype(BF16)
    fkv_ref[r, FKV_KPE] = kpe_r.astype(BF16)

    fr_ref[r, FR_Q] = p[:, P_RET:P_RET + 256]
    fr_ref[r, FR_K] = p[:, P_RET + 256:P_RET + 512] * (HEAD_DIM ** -0.5)
    fr_ref[r, FR_V.start:FR_G.stop] = p[:, P_RET + 512:P_WQ]

    fq_ref[r, FQ_C] = (rope_heads(p[:, P_WQ:P_WK]) * ATT_SCALE).astype(BF16)
    wk = p[:, P_WK:P_WV]
    wv = p[:, P_WV:P_GQ]
    fkv_ref[r, FKV_WK] = rope_heads(wk).astype(BF16)
    fkv_ref[r, FKV_WV] = wv.astype(BF16)

    bd = bd_ref[...]
    gq = p[:, P_GQ:P_GK]
    gqn = gq * lax.rsqrt(_seg_sum(gq * gq, bd) * (1.0 / HEAD_DIM) + NORM_EPS) * gqn_ref[0]
    fq_ref[r, FQ_D] = (rope_heads(gqn) * ATT_SCALE).astype(BF16)
    gk = p[:, P_GK:P_GV]
    gkn = gk * lax.rsqrt(_seg_sum(gk * gk, bd[0:128, 0:128]) * (1.0 / HEAD_DIM) + NORM_EPS) * gkn_ref[0]
    gv = p[:, P_GV:P_END]
    fkv_ref[r, FKV_GK] = rope_heads(gkn).astype(BF16)
    fkv_ref[r, FKV_GV] = gv.astype(BF16)
    if not latent:
        _store_per_seq(owk_ref, wk, r.start, True)
        _store_per_seq(owv_ref, wv, r.start, True)
        _store_per_seq(ogk_ref, gkn, r.start, True)
        _store_per_seq(ogv_ref, gv, r.start, True)


def _layer_spec(layer, *s):
    return pl.BlockSpec((1,) + s, lambda t: (layer,) + (0,) * len(s), pipeline_mode=pl.Buffered(1))


def _mod_spec(tm, group_len):
    if group_len is None:
        return pl.BlockSpec((1, 6, D_MODEL), lambda t: (0, 0, 0))
    per = group_len // tm
    return pl.BlockSpec((1, 6, D_MODEL), lambda t: (1 + t // per, 0, 0))


def _proj(x, row0, nt, mod, wts, layer, rope, tm, group_len, ctx_seq=None, caches=None):
    latent = group_len is not None
    aliases = {}
    tile0 = row0 // tm
    tok = lambda w: pl.BlockSpec((tm, w), lambda t: (t, 0))
    in_specs = [pl.BlockSpec((tm, D_MODEL), lambda t: (t + tile0, 0)), _mod_spec(tm, group_len),
                _layer_spec(layer, P_END, D_MODEL), _layer_spec(layer, 1, 256), _layer_spec(layer, 256, 1024),
                _layer_spec(layer, 1, 128), _layer_spec(layer, 1, 256), _layer_spec(layer, 1, 128),
                pl.BlockSpec((256, 256), lambda t: (0, 0))]
    args = [x, mod, wts["wmix"], wts["qn"], wts["wuq"], wts["kvn"], wts["gqn"], wts["gkn"], wts["bd"]]
    out_specs = [tok(FQ_W), tok(FKV_W), tok(FR_W)]
    out_shape = [jax.ShapeDtypeStruct((nt, FQ_W), BF16), jax.ShapeDtypeStruct((nt, FKV_W), BF16),
                 jax.ShapeDtypeStruct((nt, FR_W), F32)]
    if latent:
        per = group_len // tm
        in_specs += [pl.BlockSpec((tm, LANES), lambda t: (t % per, 0))] * 6
        args += list(rope)
    else:
        nb = tm // ctx_seq
        seqs = nt // ctx_seq
        sub = _sub_tiles(tm)[0]
        assert (sub.stop - sub.start) % ctx_seq == 0, "a sub-tile must hold whole context sequences"
        out_specs += [pl.BlockSpec((nb, None, ctx_seq, 128), lambda t: (t, layer, 0, 0))]
        out_shape += [jax.ShapeDtypeStruct((seqs, DEPTH, ctx_seq, 128), F32)]
        for r in (MLA_ROPE, 128, 128, 128, 128):
            out_specs.append(pl.BlockSpec((nb, None, r, ctx_seq), lambda t: (t, layer, 0, 0)))
            out_shape.append(jax.ShapeDtypeStruct((seqs, DEPTH, r, ctx_seq), F32))
        if caches is not None:
            aliases = {len(args) + i: 3 + i for i in range(len(caches))}
            in_specs += [pl.BlockSpec(memory_space=pl.ANY)] * len(caches)
            args += list(caches)
    return pl.pallas_call(
        functools.partial(_proj_kernel, latent=latent, n_alias=len(aliases)),
        grid=(nt // tm,), in_specs=in_specs, out_specs=out_specs, out_shape=out_shape,
        input_output_aliases=aliases,
        compiler_params=pltpu.CompilerParams(dimension_semantics=("arbitrary",), vmem_limit_bytes=VMEM_LIMIT),
        name="proj_latent" if latent else "proj_ctx",
    )(*args)


def _softmax_weights(parts, sink, mxu_sum):
    m = parts[0].max(-1, keepdims=True)
    for s in parts[1:]:
        m = jnp.maximum(m, s.max(-1, keepdims=True))
    if sink is not None:
        m = jnp.maximum(m, sink)
    es = [jnp.exp2(s - m) for s in parts]
    extra = None if sink is None else jnp.exp2(sink - m)
    if not mxu_sum:
        for e in es:
            extra = e.sum(-1, keepdims=True) if extra is None else extra + e.sum(-1, keepdims=True)
    return [e.astype(BF16) for e in es], extra


def _normalised(pv, extra):
    if pv.shape[1] == LANES:
        return pv * (1.0 / extra)
    den = pv[:, LANES:LANES + 1]
    if extra is not None:
        den = den + extra
    return pv[:, 0:LANES] * (1.0 / den)


def _window_start(qi, tq, t):
    return jnp.clip(qi * tq - WINDOW, 0, t - (tq + 2 * WINDOW))


def _band_bias(tq):
    span = tq + 2 * WINDOW
    r = (np.arange(2 * tq) % tq)[None, :, None]
    c = np.arange(span)[None, None, :]
    off = (np.arange(3) * WINDOW)[:, None, None]
    return jnp.asarray(np.where(np.abs(r + off - c) <= WINDOW, 0.0, NEG_INF), F32)


def _attn_kernel(*refs, t, tq, n_cache, layer):
    latent = n_cache > 0
    s_len = t + n_cache
    if latent:
        (fq_ref, fkv_ref, cckv_ref, ckpe_ref, cwk_ref, cwv_ref, cgk_ref, cgv_ref, bias_ref,
         wka_ref, wuv_ref, sink_ref, o_ref, ckpe_s, ka_s, va_s, kd_s, vd_s, vc_s, kcc_s, vcc_s) = refs
    else:
        (fq_ref, fkv_ref, wka_ref, wuv_ref, sink_ref, o_ref, ckpe_s, ka_s, va_s, kd_s, vd_s, vc_s) = refs

    qi = pl.program_id(1)
    bb = fq_ref.shape[0]
    assert bb == 1 or not latent

    @pl.when(qi == 0)
    def _():
        for bi in range(bb):
            ckpe_s[bi, 0:t, :] = fkv_ref[bi, :, FKV_CKV.start:FKV_KPE.stop]
            kd_s[bi, 0:t, :] = fkv_ref[bi, :, FKV_GK]
            vd_s[bi, 0:t, 0:LANES] = fkv_ref[bi, :, FKV_GV]
            vc_s[bi, :, 0:LANES] = fkv_ref[bi, :, FKV_WV]
            if latent:
                ckpe_s[bi, t:s_len, 0:128] = cckv_ref[0, 0].astype(BF16)
                ckpe_s[bi, t:s_len, 128:256] = ckpe_ref[0, 0].astype(BF16)
                kd_s[bi, t:s_len, :] = cgk_ref[0, 0].astype(BF16)
                vd_s[bi, t:s_len, 0:LANES] = cgv_ref[0, 0].astype(BF16)
                kcc_s[...] = cwk_ref[0, 0].astype(BF16)
                vcc_s[:, 0:LANES] = cwv_ref[0, 0].astype(BF16)
                vcc_s[:, LANES:] = jnp.ones((n_cache, LANES), BF16)
            vd_s[bi, :, LANES:] = jnp.ones((s_len, LANES), BF16)
            vc_s[bi, :, LANES:] = jnp.ones((t, LANES), BF16)
            ck = ckpe_s[bi]
            va = _dot(ck, wuv_ref[0]).astype(BF16)
            for blk in range(2):
                va_s[bi, blk, :, 0:LANES] = va[:, blk * LANES:(blk + 1) * LANES]
                va_s[bi, blk, :, LANES:] = jnp.ones((s_len, LANES), BF16)
            for h in range(N_HEADS):
                ka_s[bi, h] = _dot(ck, wka_ref[0, h]).astype(BF16)

    lane = lax.broadcasted_iota(jnp.int32, (tq, LANES), 1)
    half = [lane < HEAD_DIM, lane >= HEAD_DIM]

    def keep(x, j):
        return jnp.where(half[j], x, 0.0)

    if latent:
        span = tq + 2 * WINDOW
        start = pl.multiple_of(_window_start(qi, tq, t), LANES)
    row1 = lax.broadcasted_iota(jnp.int32, (2 * tq, 1), 0)
    mxu_sum = s_len >= MXU_SUM_MIN_KEYS
    vw = 2 * LANES if mxu_sum else LANES

    def element_jobs(bi):
        acc_a = [jnp.zeros((tq, LANES), F32) for _ in range(2)]
        acc_c = [jnp.zeros((tq, LANES), F32) for _ in range(2)]
        acc_d = [jnp.zeros((tq, LANES), F32) for _ in range(2)]

        def stacked_q(col, j):
            blks = [fq_ref[bi, :, col + g * LANES:col + (g + 1) * LANES] for g in range(2)]
            return jnp.concatenate([keep(b.astype(F32), j).astype(BF16) for b in blks], axis=0)

        def a_scores(h):
            return [_dot_nt(fq_ref[bi, :, h * LANES:(h + 1) * LANES], ka_s[bi, h])]

        def a_finish(h, ps, sink_term):
            blk = h // 2
            acc_a[blk] = acc_a[blk] + keep(_normalised(_dot(ps[0], va_s[bi, blk, :, 0:vw]), sink_term), h % 2)

        def c_scores(j):
            qs = stacked_q(FQ_C.start, j)
            if latent:
                return [_dot_nt(qs, fkv_ref[bi, pl.ds(start, span), FKV_WK]) + bias_ref[0],
                        _dot_nt(qs, kcc_s[...])]
            return [_dot_nt(qs, fkv_ref[bi, :, FKV_WK])]

        def c_finish(j, ps, sink_term):
            if latent:
                pv = _dot(ps[0], vc_s[bi, pl.ds(start, span), 0:vw]) + _dot(ps[1], vcc_s[:, 0:vw])
            else:
                pv = _dot(ps[0], vc_s[bi, :, 0:vw])
            pv = _normalised(pv, sink_term)
            for g in range(2):
                acc_c[g] = acc_c[g] + keep(pv[g * tq:(g + 1) * tq], j)

        def d_scores(j):
            return [_dot_nt(stacked_q(FQ_D.start, j), kd_s[bi])]

        def d_finish(j, ps, sink_term):
            pv = _normalised(_dot(ps[0], vd_s[bi, :, 0:vw]), sink_term)
            for g in range(2):
                acc_d[g] = acc_d[g] + keep(pv[g * tq:(g + 1) * tq], j)

        def c_sink(j):
            return jnp.where(row1 < tq, sink_ref[layer, 2 * j], sink_ref[layer, 2 * j + 1]) * LOG2E

        c_jobs = [(functools.partial(c_scores, j), functools.partial(c_finish, j), functools.partial(c_sink, j))
                  for j in range(2)]
        jobs = [(functools.partial(a_scores, h), functools.partial(a_finish, h), None) for h in range(N_HEADS)]
        jobs += [(functools.partial(d_scores, j), functools.partial(d_finish, j), None) for j in range(2)]
        return c_jobs[:1] + jobs + c_jobs[1:], (acc_a, acc_c, acc_d)

    per_element = [element_jobs(bi) for bi in range(bb)]
    jobs = [job for elem_jobs, _ in per_element for job in elem_jobs]
    ahead = ATTN_LOOKAHEAD if mxu_sum else len(jobs)
    pending = [job[0]() for job in jobs[:ahead]]
    for i, (_, finish, sink) in enumerate(jobs):
        if i + ahead < len(jobs):
            pending.append(jobs[i + ahead][0]())
        finish(*_softmax_weights(pending.pop(0), None if sink is None else sink(), mxu_sum))

    for bi, (_, accs) in enumerate(per_element):
        for cols, acc in zip((OATT_A, OATT_C, OATT_D), accs):
            for g in range(2):
                o_ref[bi, :, cols.start + g * LANES:cols.start + (g + 1) * LANES] = acc[g].astype(BF16)


def _attn(fq, fkv, caches, wts, layer, tq):
    b, t, _ = fq.shape
    latent = caches is not None
    n_cache = caches[0].shape[2] if latent else 0
    s_len = t + n_cache
    bb = 1 if latent else ATTN_CTX_SEQS_PER_STEP
    in_specs = [pl.BlockSpec((bb, tq, FQ_W), lambda i, q: (i, q, 0)),
                pl.BlockSpec((bb, t, FKV_W), lambda i, q: (i, 0, 0))]
    args = [fq, fkv]
    if latent:
        in_specs += [pl.BlockSpec((1, 1, n_cache, LANES), lambda i, q: (i, layer, 0, 0))] * 6
        args += list(caches)
        span = tq + 2 * WINDOW
        assert t >= span and WINDOW <= tq
        in_specs.append(pl.BlockSpec((1, 2 * tq, span),
                                     lambda i, q: ((q * tq - _window_start(q, tq, t)) // WINDOW, 0, 0)))
        args.append(_band_bias(tq))
    in_specs += [pl.BlockSpec((1, N_HEADS, 256, 128), lambda i, q: (layer, 0, 0, 0)),
                 pl.BlockSpec((1, 256, 256), lambda i, q: (layer, 0, 0)),
                 pl.BlockSpec(memory_space=pltpu.SMEM)]
    args += [wts["wka"], wts["wuv"], wts["sink"]]
    scratch = [pltpu.VMEM((bb, s_len, 256), BF16), pltpu.VMEM((bb, N_HEADS, s_len, 128), BF16),
               pltpu.VMEM((bb, 2, s_len, 256), BF16), pltpu.VMEM((bb, s_len, 128), BF16),
               pltpu.VMEM((bb, s_len, 256), BF16), pltpu.VMEM((bb, t, 256), BF16)]
    if latent:
        scratch += [pltpu.VMEM((n_cache, 128), BF16), pltpu.VMEM((n_cache, 256), BF16)]
    return pl.pallas_call(
        functools.partial(_attn_kernel, t=t, tq=tq, n_cache=n_cache, layer=layer),
        grid=(b // bb, t // tq), in_specs=in_specs,
        out_specs=pl.BlockSpec((bb, tq, OATT_W), lambda i, q: (i, q, 0)),
        out_shape=jax.ShapeDtypeStruct((b, t, OATT_W), BF16),
        scratch_shapes=scratch,
        compiler_params=pltpu.CompilerParams(dimension_semantics=("arbitrary", "arbitrary"),
                                             vmem_limit_bytes=VMEM_LIMIT),
        name="attn_latent" if latent else "attn_ctx",
    )(*args)


def _ret_kernel(*refs, t, bb, s0_stride, layer, want_state, n_alias):
    refs = refs[:7] + refs[7 + n_alias:]
    if want_state:
        (fr_ref, s0f_ref, s0b_ref, decf_ref, decb_ref, gain_ref, bd_ref, o_ref, sf_ref, sb_ref,
         dm_s, dec_s, cd_s, kvf_s, kvb_s, stf_s, stb_s, sf_s, sb_s) = refs
    else:
        (fr_ref, s0f_ref, s0b_ref, decf_ref, decb_ref, gain_ref, bd_ref, o_ref,
         dm_s, dec_s, cd_s, kvf_s, kvb_s, stf_s, stb_s, sf_s, sb_s) = refs
    nc = t // CHUNK
    w = BRANCH_W
    lane_w = lax.broadcasted_iota(jnp.int32, (CHUNK, w), 1) // HEAD_DIM

    @pl.when(pl.program_id(0) == 0)
    def _():
        row_w = lax.broadcasted_iota(jnp.int32, (CHUNK, w), 0).astype(F32)

        def lane_decay(dec_ref):
            v = jnp.zeros((CHUNK, w), F32)
            for h in range(N_HEADS):
                v = jnp.where(lane_w == h, dec_ref[layer, h], v)
            return jax.nn.log_sigmoid(v)

        lgf = lane_decay(decf_ref)
        lgb = lane_decay(decb_ref)
        dec_s[0] = jnp.exp((row_w + 1.0) * lgf)
        dec_s[1] = jnp.exp((CHUNK - 1.0 - row_w) * lgf)
        dec_s[2] = jnp.exp((CHUNK - row_w) * lgb)
        dec_s[3] = jnp.exp(row_w * lgb)
        cd_s[0] = jnp.concatenate([jnp.exp(CHUNK * lgf)] * (w // CHUNK), axis=0)
        cd_s[1] = jnp.concatenate([jnp.exp(CHUNK * lgb)] * (w // CHUNK), axis=0)
        ii = lax.broadcasted_iota(jnp.int32, (CHUNK, CHUNK), 0).astype(F32)
        jj = lax.broadcasted_iota(jnp.int32, (CHUNK, CHUNK), 1).astype(F32)
        diff = ii - jj
        for h in range(N_HEADS):
            lf = jax.nn.log_sigmoid(jnp.full((CHUNK, CHUNK), decf_ref[layer, h], F32))
            lb = jax.nn.log_sigmoid(jnp.full((CHUNK, CHUNK), decb_ref[layer, h], F32))
            d_f = jnp.where(diff >= 0, jnp.exp(jnp.maximum(diff, 0.0) * lf), 0.0)
            d_b = jnp.where(diff < 0, jnp.exp(jnp.maximum(-diff, 0.0) * lb), 0.0)
            dm_s[h] = d_f + d_b

    r2 = lax.broadcasted_iota(jnp.int32, (w, w), 0) // HEAD_DIM
    c2 = lax.broadcasted_iota(jnp.int32, (w, w), 1) // HEAD_DIM
    diag = r2 == c2

    jobs = [(bi, n) for bi in range(bb) for n in range(nc)]
    groups = [jobs[i:i + RET_GROUP] for i in range(0, len(jobs), RET_GROUP)]
    rows = lambda n: slice(n * CHUNK, (n + 1) * CHUNK)
    slot = lambda bi, n: bi * nc + n

    for grp in groups:
        kvs = []
        for bi, n in grp:
            k = fr_ref[bi, rows(n), FR_K]
            vb = fr_ref[bi, rows(n), FR_V].astype(BF16)
            kk = jnp.concatenate([k * dec_s[1], k * dec_s[3]], axis=1).astype(BF16)
            kvs.append(_dot_tn(kk, vb))
        for (bi, n), kv in zip(grp, kvs):
            kvf_s[slot(bi, n)] = jnp.where(diag, kv[0:w], 0.0)
            kvb_s[slot(bi, n)] = jnp.where(diag, kv[w:2 * w], 0.0)

    for bi in range(bb):
        sf_s[bi] = s0f_ref[bi * s0_stride]
        sb_s[bi] = s0b_ref[bi * s0_stride]
    for i in range(nc):
        m = nc - 1 - i
        for bi in range(bb):
            sf = sf_s[bi]
            stf_s[slot(bi, i)] = sf.astype(BF16)
            sf_s[bi] = sf * cd_s[0] + kvf_s[slot(bi, i)]
            sb = sb_s[bi]
            stb_s[slot(bi, m)] = sb.astype(BF16)
            sb_s[bi] = sb * cd_s[1] + kvb_s[slot(bi, m)]
    if want_state:
        for bi in range(bb):
            for h in range(N_HEADS):
                sl = slice(h * HEAD_DIM, (h + 1) * HEAD_DIM)
                sf_ref[bi, h] = sf_s[bi, sl, sl]
                sb_ref[bi, h] = sb_s[bi, sl, sl]

    gain = gain_ref[0]
    bd2 = jnp.concatenate([bd_ref[...], bd_ref[...]], axis=0)

    def seg_mean(xs):
        cat = []
        for x in xs:
            hi = x.astype(BF16)
            cat.append(jnp.concatenate([hi, (x - hi.astype(F32)).astype(BF16)], axis=1))
        return [_dot(c, bd2) * (1.0 / HEAD_DIM) for c in cat]

    for grp in groups:
        qs = [fr_ref[bi, rows(n), FR_Q] for bi, n in grp]
        vs = [fr_ref[bi, rows(n), FR_V] for bi, n in grp]
        qks, inter_f, inter_b = [], [], []
        for (bi, n), q in zip(grp, qs):
            kb = fr_ref[bi, rows(n), FR_K].astype(BF16)
            qstack = jnp.concatenate([jnp.where(lane_w == h, q, 0.0) for h in range(N_HEADS)], axis=0)
            qks.append(_dot_nt(qstack.astype(BF16), kb))
            qb = q.astype(BF16)
            inter_f.append(_dot(qb, stf_s[slot(bi, n)]))
            inter_b.append(_dot(qb, stb_s[slot(bi, n)]))
        outs = []
        for qk, v, i_f, i_b in zip(qks, vs, inter_f, inter_b):
            att = jnp.concatenate([qk[rows(h)] * dm_s[h] for h in range(N_HEADS)], axis=1)
            vstack = jnp.concatenate([jnp.where(lane_w == h, v, 0.0) for h in range(N_HEADS)], axis=0)
            outs.append(_dot(att.astype(BF16), vstack.astype(BF16)) + i_f * dec_s[0] + i_b * dec_s[2])
        ds = [o - mu for o, mu in zip(outs, seg_mean(outs))]
        var = seg_mean([d * d for d in ds])
        for (bi, n), d, vr in zip(grp, ds, var):
            gate = fr_ref[bi, rows(n), FR_G]
            y = d * lax.rsqrt(vr + NORM_EPS) * gain
            o_ref[bi, rows(n), :] = (gate * jax.nn.sigmoid(gate) * y).astype(BF16)


def _ret(fr, s0f, s0b, wts, layer, want_state, states=None):
    b, t, _ = fr.shape
    nc = t // CHUNK
    bb = max(1, RET_CHUNKS_PER_STEP // nc)
    shared = s0f.shape[0] != b
    s_spec = (pl.BlockSpec((1, BRANCH_W, BRANCH_W), lambda i: (0, 0, 0)) if shared
              else pl.BlockSpec((bb, BRANCH_W, BRANCH_W), lambda i: (i, 0, 0)))
    smem = pl.BlockSpec(memory_space=pltpu.SMEM)
    out_specs = [pl.BlockSpec((bb, t, BRANCH_W), lambda i: (i, 0, 0))]
    out_shape = [jax.ShapeDtypeStruct((b, t, BRANCH_W), BF16)]
    in_specs = [pl.BlockSpec((bb, t, FR_W), lambda i: (i, 0, 0)), s_spec, s_spec, smem, smem,
                pl.BlockSpec((1, 1, BRANCH_W), lambda i: (layer, 0, 0)),
                pl.BlockSpec((BRANCH_W, BRANCH_W), lambda i: (0, 0))]
    args = [fr, s0f, s0b, wts["decf"], wts["decb"], wts["gn"], wts["bd"]]
    aliases = {}
    if want_state:
        st_spec = pl.BlockSpec((bb, None, N_HEADS, HEAD_DIM, HEAD_DIM), lambda i: (i, layer, 0, 0, 0))
        out_specs += [st_spec, st_spec]
        out_shape += [jax.ShapeDtypeStruct((b, DEPTH, N_HEADS, HEAD_DIM, HEAD_DIM), F32)] * 2
        if states is not None:
            aliases = {len(args) + i: 1 + i for i in range(2)}
            in_specs += [pl.BlockSpec(memory_space=pl.ANY)] * 2
            args += list(states)
    sq = (BRANCH_W, BRANCH_W)
    return pl.pallas_call(
        functools.partial(_ret_kernel, t=t, bb=bb, s0_stride=0 if shared else 1, layer=layer,
                          want_state=want_state, n_alias=len(aliases)),
        grid=(b // bb,),
        in_specs=in_specs, out_specs=out_specs, out_shape=out_shape, input_output_aliases=aliases,
        scratch_shapes=[pltpu.VMEM((N_HEADS, CHUNK, CHUNK), F32), pltpu.VMEM((4, CHUNK, BRANCH_W), F32),
                        pltpu.VMEM((2,) + sq, F32),
                        pltpu.VMEM((bb * nc,) + sq, F32), pltpu.VMEM((bb * nc,) + sq, F32),
                        pltpu.VMEM((bb * nc,) + sq, BF16), pltpu.VMEM((bb * nc,) + sq, BF16),
                        pltpu.VMEM((bb,) + sq, F32), pltpu.VMEM((bb,) + sq, F32)],
        compiler_params=pltpu.CompilerParams(dimension_semantics=("arbitrary",),
                                             vmem_limit_bytes=VMEM_LIMIT),
        name="retention",
    )(*args)


def _post_kernel(*refs, n_x, n_out, ctx_tiles):
    x_refs, refs = refs[:n_x], refs[n_x:]
    (mod_ref, oatt_c, oatt_l, oret_c, oret_l, wg_ref, wb_ref, wo_ref, g1_ref, b1_ref,
     wup_ref, wdn_ref, g2_ref, b2_ref) = refs[:14]
    o_refs = refs[14:]
    is_ctx = pl.program_id(0) < ctx_tiles
    pick = lambda c_ref, l_ref, r, cols: jnp.where(is_ctx, c_ref[r, cols], l_ref[r, cols])
    every = slice(None)
    if n_x == 1:
        x_rows = lambda r: x_refs[0][r, :]
    else:
        x_rows = lambda r: pick(x_refs[0], x_refs[1], r, every)
    sh1, sc1, g1, sh2, sc2, g2 = [mod_ref[0, i:i + 1, :] for i in range(6)]
    subs = _sub_tiles(x_refs[0].shape[0])
    ff_half = D_FF // 2
    xs = [x_rows(r) for r in subs]

    def gated(r, x):
        h = (x * (1.0 + sc1) + sh1).astype(BF16)
        branches = (pick(oatt_c, oatt_l, r, OATT_A), pick(oret_c, oret_l, r, every),
                    pick(oatt_c, oatt_l, r, OATT_C), pick(oatt_c, oatt_l, r, OATT_D))
        tsum = None
        for i, o in enumerate(branches):
            gate = jax.nn.sigmoid(_dot_nt(h, wg_ref[0, i * D_MODEL:(i + 1) * D_MODEL, :]))
            term = gate * _dot(o, wb_ref[0, i])
            tsum = term if tsum is None else tsum + term
        return tsum

    sums = [gated(r, x) for r, x in zip(subs, xs)]
    ys = [_dot(s.astype(BF16), wo_ref[0]) for s in sums]
    x1s = [_layernorm(ALPHA * x + g1 * y, g1_ref[0], b1_ref[0]) for x, y in zip(xs, ys)]
    hs = [(x1 * (1.0 + sc2) + sh2).astype(BF16) for x1 in x1s]
    us = [[_dot(h, wup_ref[0, :, c * ff_half:(c + 1) * ff_half]) for c in range(2)] for h in hs]
    fs = []
    for u2 in us:
        f = None
        for c, u in enumerate(u2):
            u = jnp.maximum(u, 0.0)
            part = _dot((u * u).astype(BF16), wdn_ref[0, c * ff_half:(c + 1) * ff_half, :])
            f = part if f is None else f + part
        fs.append(f)
    outs = [_layernorm(ALPHA * x1 + g2 * f, g2_ref[0], b2_ref[0]) for x1, f in zip(x1s, fs)]
    if n_out == 1:
        for r, o in zip(subs, outs):
            o_refs[0][r, :] = o
    else:
        @pl.when(is_ctx)
        def _():
            for r, o in zip(subs, outs):
                o_refs[0][r, :] = o

        @pl.when(jnp.logical_not(is_ctx))
        def _():
            for r, o in zip(subs, outs):
                o_refs[1][r, :] = o


def _post(xs, mod, oatts, orets, wts, layer, tm, latent_len, split_out):
    n_ctx, n_lat = oatts[0].shape[0], oatts[1].shape[0]
    ctx_tiles, lat_tiles = n_ctx // tm, n_lat // tm
    per = latent_len // tm
    all_tok = lambda w: pl.BlockSpec((tm, w), lambda t: (t, 0))
    ctx_tok = lambda w: pl.BlockSpec((tm, w), lambda t: (jnp.minimum(t, ctx_tiles - 1), 0))
    lat_tok = lambda w: pl.BlockSpec((tm, w), lambda t: (jnp.maximum(t - ctx_tiles, 0), 0))
    pair = lambda w: [ctx_tok(w), lat_tok(w)]
    vec = _layer_spec(layer, 1, D_MODEL)
    mod_spec = pl.BlockSpec((1, 6, D_MODEL),
                            lambda t: (jnp.where(t < ctx_tiles, 0, 1 + (t - ctx_tiles) // per), 0, 0))
    x_specs = [all_tok(D_MODEL)] if len(xs) == 1 else pair(D_MODEL)
    if split_out:
        out_specs = pair(D_MODEL)
        out_shape = [jax.ShapeDtypeStruct((n_ctx, D_MODEL), F32), jax.ShapeDtypeStruct((n_lat, D_MODEL), F32)]
    else:
        out_specs = [all_tok(D_MODEL)]
        out_shape = [jax.ShapeDtypeStruct((n_ctx + n_lat, D_MODEL), F32)]
    return pl.pallas_call(
        functools.partial(_post_kernel, n_x=len(xs), n_out=len(out_shape), ctx_tiles=ctx_tiles),
        grid=(ctx_tiles + lat_tiles,),
        in_specs=x_specs + [mod_spec] + pair(OATT_W) + pair(BRANCH_W) + [
            _layer_spec(layer, 4 * D_MODEL, D_MODEL), _layer_spec(layer, 4, BRANCH_W, D_MODEL),
            _layer_spec(layer, D_MODEL, D_MODEL), vec, vec,
            _layer_spec(layer, D_MODEL, D_FF), _layer_spec(layer, D_FF, D_MODEL), vec, vec],
        out_specs=out_specs, out_shape=out_shape,
        compiler_params=pltpu.CompilerParams(dimension_semantics=("arbitrary",),
                                             vmem_limit_bytes=VMEM_LIMIT_MERGE_MLP),
        name="merge_mlp",
    )(*xs, mod, *oatts, *orets, wts["wg"], wts["wb"], wts["wo"], wts["ln1g"], wts["ln1b"],
      wts["wup"], wts["wdn"], wts["ln2g"], wts["ln2b"])


def _win_kernel(w_ref, wmix_ref, wg_ref):
    o_ret = 256 + 128 + MLA_ROPE
    o_wq, o_wk, o_gq, o_gk, o_gate = o_ret + 1024, o_ret + 1280, o_ret + 1536, o_ret + 1792, o_ret + 2048

    def put(dst, src, n):
        wmix_ref[0, dst:dst + n, :] = w_ref[0, src:src + n, :].astype(BF16)

    def put_swapped(dst, src):
        for i, h in enumerate((0, 2, 1, 3)):
            put(dst + i * HEAD_DIM, src + h * HEAD_DIM, HEAD_DIM)

    put(P_QLAT, 0, o_ret)
    pad = P_RET - o_ret
    wmix_ref[0, o_ret:P_RET, :] = jnp.zeros((pad, w_ref.shape[2]), BF16)
    put(P_RET, o_ret, o_wq - o_ret)
    put_swapped(P_WQ, o_wq)
    put(P_WK, o_wk, o_gq - o_wk)
    put_swapped(P_GQ, o_gq)
    put(P_GK, o_gk, o_gate - o_gk)
    wg_ref[0] = w_ref[0, o_gate:, :].astype(BF16)


def _prep_win(w_in):
    w_t = jnp.swapaxes(w_in, 1, 2)
    n_in = w_t.shape[1]
    kb = D_MODEL // 2
    return pl.pallas_call(
        _win_kernel,
        grid=(DEPTH, D_MODEL // kb),
        in_specs=[pl.BlockSpec((1, n_in, kb), lambda l, k: (l, 0, k))],
        out_specs=[pl.BlockSpec((1, P_END, kb), lambda l, k: (l, 0, k)),
                   pl.BlockSpec((1, 4 * D_MODEL, kb), lambda l, k: (l, 0, k))],
        out_shape=[jax.ShapeDtypeStruct((DEPTH, P_END, D_MODEL), BF16),
                   jax.ShapeDtypeStruct((DEPTH, 4 * D_MODEL, D_MODEL), BF16)],
        compiler_params=pltpu.CompilerParams(dimension_semantics=("arbitrary", "arbitrary"),
                                             vmem_limit_bytes=VMEM_LIMIT),
        name="prep_w_in",
    )(w_t)


def _prep_weights(w_in, mla_q_norm, mla_w_uq, mla_kv_norm, mla_w_uk, mla_w_uv, ret_decay_fwd, ret_decay_bwd,
                  ret_gn_gain, win_sink, gqa_q_norm, gqa_k_norm, w_branch, w_o, ln1_g, ln1_b, w_up, w_down,
                  ln2_g, ln2_b):
    wmix, wg = _prep_win(w_in)
    uq = mla_w_uq.reshape(DEPTH, 256, N_HEADS, HEAD_DIM + MLA_ROPE)
    half = MLA_ROPE // 2
    uq_sw = jnp.concatenate([jnp.zeros_like(uq[..., :HEAD_DIM]), uq[..., HEAD_DIM + half:],
                             uq[..., HEAD_DIM:HEAD_DIM + half]], axis=-1)
    lane_pad = ((0, 0), (0, 0), (0, 0), (0, LANES - HEAD_DIM - MLA_ROPE))
    wuq = jnp.concatenate([jnp.pad(w, lane_pad).reshape(DEPTH, 256, N_HEADS * LANES) for w in (uq, uq_sw)],
                          axis=-1).astype(BF16)
    uk = mla_w_uk.reshape(DEPTH, MLA_KV_RANK, N_HEADS, HEAD_DIM).transpose(0, 2, 1, 3)
    top = jnp.pad(uk, ((0, 0), (0, 0), (0, 0), (0, LANES - HEAD_DIM)))
    eye = np.zeros((LANES, LANES), np.float32)
    eye[np.arange(MLA_ROPE), HEAD_DIM + np.arange(MLA_ROPE)] = 1.0
    wka = jnp.concatenate([top, jnp.broadcast_to(eye, (DEPTH, N_HEADS, LANES, LANES))], axis=2).astype(BF16)
    wuv = jnp.pad(mla_w_uv, ((0, 0), (0, 256 - MLA_KV_RANK), (0, 0))).astype(BF16)
    bd = (np.arange(BRANCH_W)[:, None] // HEAD_DIM == np.arange(BRANCH_W)[None, :] // HEAD_DIM).astype(np.float32)
    wb_cd = jnp.swapaxes(w_branch[:, 2:].reshape(DEPTH, 2, 2, 2, HEAD_DIM, D_MODEL), 2, 3
                         ).reshape(DEPTH, 2, BRANCH_W, D_MODEL)
    w_branch = jnp.concatenate([w_branch[:, :2], wb_cd], axis=1)
    row = lambda a: a[:, None, :]
    return dict(
        wmix=wmix, wg=wg, wuq=wuq, wka=wka, wuv=wuv, bd=jnp.asarray(bd, BF16),
        qn=row(mla_q_norm), kvn=row(mla_kv_norm),
        gqn=row(jnp.tile(gqa_q_norm, (1, N_HEADS))), gkn=row(jnp.tile(gqa_k_norm, (1, 2))),
        decf=ret_decay_fwd, decb=ret_decay_bwd, gn=row(ret_gn_gain), sink=win_sink,
        wb=w_branch.astype(BF16), wo=w_o.astype(BF16), ln1g=row(ln1_g), ln1b=row(ln1_b),
        wup=w_up.astype(BF16), wdn=w_down.astype(BF16), ln2g=row(ln2_g), ln2b=row(ln2_b))


def _axial_tables(t, rot_dim):
    rows = t // GRID_W
    row = np.repeat(np.arange(rows, dtype=np.float32), GRID_W)
    col = (np.arange(t) % GRID_W).astype(np.float32)
    n_freq = rot_dim // 4
    inv = np.power(np.float32(ROPE_BASE), -np.arange(n_freq, dtype=np.float32) / np.float32(n_freq))
    ang = np.concatenate([row[:, None] * inv, col[:, None] * inv], axis=-1).astype(np.float32)
    return np.cos(ang), np.sin(ang)


def _rope_tables(t):
    ca, sa = _axial_tables(t, MLA_ROPE)
    ch, sh = _axial_tables(t, HEAD_DIM)
    one = lambda n: np.ones((t, n), np.float32)
    zero = lambda n: np.zeros((t, n), np.float32)
    cq = np.concatenate([one(HEAD_DIM), ca, ca, one(32)], axis=1)
    sq = np.concatenate([zero(HEAD_DIM), -sa, sa, zero(32)], axis=1)
    ck = np.concatenate([ca, ca, one(96)], axis=1)
    sk = np.concatenate([-sa, sa, zero(96)], axis=1)
    chh = np.concatenate([ch, ch, ch, ch], axis=1)
    shh = np.concatenate([-sh, sh, -sh, sh], axis=1)
    return tuple(jnp.asarray(a, F32) for a in (cq, sq, ck, sk, chh, shh))


def _block_diag(s):
    b = s.shape[0]
    same = np.eye(N_HEADS, dtype=bool)[None, :, None, :, None]
    return jnp.where(same, s[:, :, :, None, :], 0.0).reshape(b, BRANCH_W, BRANCH_W)


def kernel(x_prompt, x_sample, cache_mla_ckv, cache_mla_kpe, cache_win_k, cache_win_v, cache_gqa_k, cache_gqa_v,
           state_ret_fwd, state_ret_bwd, c, c_ctx, w_ada, b_ada, w_in, mla_q_norm, mla_w_uq, mla_kv_norm, mla_w_uk,
           mla_w_uv, ret_decay_fwd, ret_decay_bwd, ret_gn_gain, win_sink, gqa_q_norm, gqa_k_norm, w_branch, w_o,
           ln1_g, ln1_b, w_up, w_down, ln2_g, ln2_b):
    batch, seq, _ = x_prompt.shape
    dec_b, dec_t, _ = x_sample.shape
    past = cache_mla_ckv.shape[2]

    cond8 = jnp.concatenate([c_ctx[None], c, jnp.zeros((8 - 1 - dec_b, D_MODEL), F32)], axis=0)
    mod = _ada(cond8, w_ada, b_ada).reshape(DEPTH, 8, 6, D_MODEL)
    rope = _rope_tables(dec_t)
    caches = (cache_mla_ckv,
              jnp.pad(cache_mla_kpe, ((0, 0), (0, 0), (0, 0), (0, LANES - MLA_ROPE))),
              cache_win_k.reshape(dec_b, DEPTH, past, LANES), cache_win_v.reshape(dec_b, DEPTH, past, LANES),
              cache_gqa_k.reshape(dec_b, DEPTH, past, LANES), cache_gqa_v.reshape(dec_b, DEPTH, past, LANES))

    wts = _prep_weights(w_in, mla_q_norm, mla_w_uq, mla_kv_norm, mla_w_uk, mla_w_uv, ret_decay_fwd, ret_decay_bwd,
                        ret_gn_gain, win_sink, gqa_q_norm, gqa_k_norm, w_branch, w_o, ln1_g, ln1_b, w_up, w_down,
                        ln2_g, ln2_b)
    zero_state = jnp.zeros((1, BRANCH_W, BRANCH_W), F32)
    n_ctx, n_lat = batch * seq, dec_b * dec_t
    per_b = lambda a: a.reshape(batch, seq, a.shape[-1])
    per_d = lambda a: a.reshape(dec_b, dec_t, a.shape[-1])
    flat = lambda a: a.reshape(-1, a.shape[-1])
    x = [x_prompt.reshape(n_ctx, D_MODEL), x_sample.reshape(n_lat, D_MODEL)]
    new_caches = None
    new_states = None
    for l in range(DEPTH):
        x_ctx, x_lat = (x[0], 0), ((x[1], 0) if len(x) == 2 else (x[0], n_ctx))
        fq, fkv, fr, *new_caches = _proj(*x_ctx, n_ctx, mod[l], wts, l, None, TM_PROJ, None, seq, new_caches)
        oatt_c = _attn(per_b(fq), per_b(fkv), None, wts, l, seq)
        oret_c, *new_states = _ret(per_b(fr), zero_state, zero_state, wts, l, True, new_states)
        fq, fkv, fr = _proj(*x_lat, n_lat, mod[l], wts, l, rope, TM_PROJ, dec_t)
        oatt_l = _attn(per_d(fq), per_d(fkv), caches, wts, l, TQ_LATENT)
        (oret_l,) = _ret(per_d(fr), _block_diag(state_ret_fwd[:, l]), _block_diag(state_ret_bwd[:, l]), wts, l, False)
        x = _post(x, mod[l], (flat(oatt_c), flat(oatt_l)), (flat(oret_c), flat(oret_l)), wts, l, TM_POST, dec_t,
                  split_out=l == DEPTH - 1)

    ckv, kpe, *kv = new_caches
    kv = [jnp.transpose(a.reshape(batch, DEPTH, 2, HEAD_DIM, seq), (0, 1, 4, 2, 3)) for a in kv]
    return (per_b(x[0]), per_d(x[1]), ckv, jnp.swapaxes(kpe, 2, 3), *kv, *new_states)
```
